```python
import math
import jax, jax.numpy as jnp
from jax import lax
import numpy as np

D_MODEL = 2048
BATCH = 2
SEQ = 8192
DEPTH = 1
DEC_BATCH = 32
DEC_SEQ = 64
PAST_LEN = 1024

CHUNK = 64
Q_BLOCK = 128
R_HEADS = 4
R_DK = D_MODEL // 16
R_DV = 2 * R_DK
D_HEADS = 8
D_HD = D_MODEL // 32
D_DV = 2 * D_HD
N_MEM = 256
M_HEADS = 4
M_HD = D_MODEL // 16
N_EXPERTS = 32
TOP_K = 4
D_FF = D_MODEL
SWIGLU_ALPHA = 1.702
SWIGLU_LIMIT = 7.0
EXPERT_BLOCK = 128
LN_EPS = 1e-5
RMS_EPS = 1e-5
ROPE_BASE = 10000.0
DEEPNORM_ALPHA = (2.0 * DEPTH) ** 0.25
DEEPNORM_BETA = (8.0 * DEPTH) ** -0.25
IN_WIDTHS = (R_HEADS * R_DK, R_HEADS * R_DK, R_HEADS * R_DV, R_HEADS * R_DV,
             D_HEADS * 2 * D_HD, D_HEADS * 2 * D_HD, D_HEADS * D_DV, 2 * D_MODEL)
N_IN = sum(IN_WIDTHS)

kernel_name = 'hybrid_retention_diffattn_moe_stream_step'


def layer_norm(x, g, b):
    xf = x.astype(jnp.float32)
    mu = jnp.mean(xf, -1, keepdims=True)
    var = jnp.mean(jnp.square(xf - mu), -1, keepdims=True)
    y = (xf - mu) * lax.rsqrt(var + LN_EPS) * g.astype(jnp.float32) + b.astype(jnp.float32)
    return y.astype(x.dtype)


def rms_norm(x, w=None):
    xf = x.astype(jnp.float32)
    y = xf * lax.rsqrt(jnp.mean(jnp.square(xf), -1, keepdims=True) + RMS_EPS)
    if w is not None:
        y = y * w.astype(jnp.float32)
    return y.astype(x.dtype)


def rotary(x, pos):
    inv = 1.0 / (ROPE_BASE ** jnp.linspace(0.0, 1.0, R_DK // 2, dtype=jnp.float32))
    ang = pos.astype(jnp.float32)[:, None] * inv[None, :]
    cos = jnp.cos(ang)[None, :, None, :].astype(x.dtype)
    sin = jnp.sin(ang)[None, :, None, :].astype(x.dtype)
    x1, x2 = jnp.split(x, 2, axis=-1)
    return jnp.concatenate([x1 * cos - x2 * sin, x2 * cos + x1 * sin], axis=-1)


def retention_block(S, q, k, v):
    C = q.shape[2]
    log_g = jnp.log1p(-jnp.power(2.0, -5.0 - jnp.arange(R_HEADS, dtype=jnp.float32)))
    i = jnp.arange(C, dtype=jnp.float32)
    rel = i[:, None] - i[None, :]
    dmask = jnp.where(rel >= 0, jnp.exp(log_g[:, None, None] * jnp.maximum(rel, 0.0)), 0.0).astype(q.dtype)
    q_decay = jnp.exp(log_g[:, None] * (i + 1.0)).astype(q.dtype)
    k_decay = jnp.exp(log_g[:, None] * (C - 1.0 - i)).astype(q.dtype)
    block_decay = jnp.exp(log_g * C).astype(q.dtype)
    scores = jnp.einsum('bhqd,bhkd->bhqk', q, k) * dmask
    o = (jnp.einsum('bhqk,bhke->bhqe', scores, v)
         + jnp.einsum('bhqd,bhde->bhqe', q * q_decay[..., None], S))
    S_next = (block_decay[:, None, None] * S
              + jnp.einsum('bhkd,bhke->bhde', k * k_decay[..., None], v))
    return o, S_next


def retention_branch(q, k, v, g, pos, S0):
    B, L = q.shape[:2]
    q = rotary(q, pos)
    k = rotary(k, pos) * (R_DK ** -0.5)
    C = min(L, CHUNK)
    n = L // C

    def blocks(t):
        return t.reshape(B, n, C, R_HEADS, t.shape[-1]).transpose(1, 0, 3, 2, 4)

    def step(S, qkv):
        o, S_next = retention_block(S, *qkv)
        return S_next, o

    S_final, o = lax.scan(step, S0, (blocks(q), blocks(k), blocks(v)))
    o = o.transpose(1, 0, 3, 2, 4).reshape(B, L, R_HEADS, R_DV)
    y = jax.nn.silu(g) * rms_norm(o)
    return y.reshape(B, L, R_HEADS * R_DV), S_final


def diff_attention(q, k, v, lam, mask):
    q1, q2 = jnp.split(q, 2, axis=-1)
    k1, k2 = jnp.split(k, 2, axis=-1)
    scale = D_HD ** -0.5

    def probs(qa, ka):
        s = jnp.einsum('bqhd,bkhd->bhqk', qa, ka).astype(jnp.float32) * scale
        if mask is not None:
            s = jnp.where(mask, s, -1e30)
        return jax.nn.softmax(s, axis=-1)

    a = probs(q1, k1) - lam * probs(q2, k2)
    return jnp.einsum('bhqk,bkhe->bqhe', a.astype(v.dtype), v)


def diff_attention_prompt(q, k, v, lam):
    B, S = q.shape[:2]
    nb = S // Q_BLOCK
    qb = q.reshape(B, nb, Q_BLOCK, D_HEADS, 2 * D_HD).transpose(1, 0, 2, 3, 4)
    key_chunk = jnp.arange(S) // CHUNK

    def one(args):
        qblk, b = args
        q_chunk = (b * Q_BLOCK + jnp.arange(Q_BLOCK)) // CHUNK
        mask = key_chunk[None, :] <= q_chunk[:, None]
        return diff_attention(qblk, k, v, lam, mask)

    o = lax.map(one, (qb, jnp.arange(nb)))
    return o.transpose(1, 0, 2, 3, 4).reshape(B, S, D_HEADS, D_DV)


def memory_attention(h, mem_k, mem_v, w_q, w_o):
    B, L, _ = h.shape
    q = (h @ w_q).reshape(B, L, M_HEADS, M_HD)
    s = jnp.einsum('blhd,bmhd->bhlm', q, mem_k).astype(jnp.float32) * (M_HD ** -0.5)
    p = jax.nn.softmax(s, axis=-1)
    o = jnp.einsum('bhlm,bmhd->blhd', p.astype(mem_v.dtype), mem_v).reshape(B, L, M_HEADS * M_HD)
    return o @ w_o


def moe(h, router_w, router_b, w1, b1, w2, b2):
    B, L, D = h.shape
    x = h.reshape(B * L, D)
    T = x.shape[0]
    TK = T * TOP_K
    logits = (x @ router_w).astype(jnp.float32) + router_b.astype(jnp.float32)
    top_val, top_idx = lax.top_k(logits, TOP_K)
    gate = jax.nn.softmax(top_val, axis=-1)
    flat_e = top_idx.reshape(-1)
    flat_tok = jnp.arange(TK, dtype=jnp.int32) // TOP_K
    flat_w = gate.reshape(-1)
    order = jnp.argsort(flat_e)
    se = flat_e[order]
    counts = jnp.zeros((N_EXPERTS,), jnp.int32).at[flat_e].add(1)
    padded = (counts + EXPERT_BLOCK - 1) // EXPERT_BLOCK * EXPERT_BLOCK
    pad_end = jnp.cumsum(padded)
    pad_start = pad_end - padded
    start = jnp.cumsum(counts) - counts
    dest = pad_start[se] + jnp.arange(TK, dtype=jnp.int32) - start[se]
    n_blocks = -(-TK // EXPERT_BLOCK) + N_EXPERTS
    n_rows = n_blocks * EXPERT_BLOCK
    row_tok = jnp.full((n_rows,), T, jnp.int32).at[dest].set(flat_tok[order])
    row_w = jnp.zeros((n_rows,), jnp.float32).at[dest].set(flat_w[order])
    block_start = jnp.arange(n_blocks, dtype=jnp.int32) * EXPERT_BLOCK
    block_e = jnp.minimum(jnp.searchsorted(pad_end, block_start, side='right'), N_EXPERTS - 1)
    x_pad = jnp.concatenate([x, jnp.zeros((1, D), x.dtype)], axis=0)

    def expert_block(args):
        tok, e = args
        u = x_pad[tok] @ w1[e] + b1[e]
        u_glu, u_lin = jnp.split(u, 2, axis=-1)
        u_glu = jnp.minimum(u_glu, SWIGLU_LIMIT)
        u_lin = jnp.clip(u_lin, -SWIGLU_LIMIT, SWIGLU_LIMIT)
        a = u_glu * jax.nn.sigmoid(SWIGLU_ALPHA * u_glu) * (u_lin + 1.0)
        return a @ w2[e] + b2[e]

    rows = lax.map(expert_block, (row_tok.reshape(n_blocks, EXPERT_BLOCK), block_e))
    rows = rows.reshape(n_rows, D) * row_w[:, None].astype(x.dtype)
    y = jnp.zeros((T + 1, D), x.dtype).at[row_tok].add(rows)[:T]
    return y.reshape(B, L, D)


def trunk_layer(h, pos, ret_state, past_k, past_v, mem_k, mem_v, lambda_init, p):
    B, L, _ = h.shape
    cuts = np.cumsum(IN_WIDTHS[:-1]).tolist()
    rq, rk, rv, rg, dq, dk, dv, gates = jnp.split(h @ p['w_in'], cuts, axis=-1)
    y_ret, new_state = retention_branch(
        rq.reshape(B, L, R_HEADS, R_DK), rk.reshape(B, L, R_HEADS, R_DK),
        rv.reshape(B, L, R_HEADS, R_DV), rg.reshape(B, L, R_HEADS, R_DV), pos, ret_state)
    dq = dq.reshape(B, L, D_HEADS, 2 * D_HD)
    dk = dk.reshape(B, L, D_HEADS, 2 * D_HD)
    dv = dv.reshape(B, L, D_HEADS, D_DV)
    lv = p['diff_lambda'].astype(jnp.float32)
    lam = jnp.exp(jnp.sum(lv[0] * lv[1])) - jnp.exp(jnp.sum(lv[2] * lv[3])) + lambda_init
    if past_k is None:
        o = diff_attention_prompt(dq, dk, dv, lam)
    else:
        o = diff_attention(dq, jnp.concatenate([past_k, dk], axis=1),
                           jnp.concatenate([past_v, dv], axis=1), lam, None)
    y_diff = (rms_norm(o, p['diff_subln']) * (1.0 - lambda_init)).reshape(B, L, D_HEADS * D_DV)
    g_ret, g_diff = jnp.split(jax.nn.sigmoid(gates + p['b_gate']), 2, axis=-1)
    mixed = (g_ret * (y_ret @ p['w_proj_ret']) + g_diff * (y_diff @ p['w_proj_diff'])) @ p['w_out']
    h = layer_norm(DEEPNORM_ALPHA * h + mixed, p['ln1_g'], p['ln1_b'])
    h = layer_norm(DEEPNORM_ALPHA * h + memory_attention(h, mem_k, mem_v, p['w_mq'], p['w_mo']),
                   p['ln2_g'], p['ln2_b'])
    h = layer_norm(DEEPNORM_ALPHA * h + moe(h, p['router_w'], p['router_b'], p['w1'], p['b1'],
                                            p['w2'], p['b2']),
                   p['ln3_g'], p['ln3_b'])
    return h, new_state, dk, dv


def setup_inputs(seed: int = 0) -> dict:
    key = jax.random.key(seed)
    ks = iter(jax.random.split(key, 40))

    def nrm(shape, scale=1.0):
        return jax.random.normal(next(ks), shape, jnp.float32) * scale

    Dm = D_MODEL
    beta = DEEPNORM_BETA
    return {
        'x_prompt': nrm((BATCH, SEQ, Dm)),
        'x_sample': nrm((DEC_BATCH, DEC_SEQ, Dm)),
        'cache_diff_k': nrm((DEPTH, DEC_BATCH, PAST_LEN, D_HEADS, 2 * D_HD)),
        'cache_diff_v': nrm((DEPTH, DEC_BATCH, PAST_LEN, D_HEADS, D_DV)),
        'state_ret': nrm((DEPTH, DEC_BATCH, R_HEADS, R_DK, R_DV), 0.5),
        'cache_mem_k': nrm((DEPTH, DEC_BATCH, N_MEM, M_HEADS, M_HD)),
        'cache_mem_v': nrm((DEPTH, DEC_BATCH, N_MEM, M_HEADS, M_HD)),
        'mem_prompt': nrm((BATCH, N_MEM, Dm)),
        'ln_in_g': 1.0 + nrm((Dm,), 0.01),
        'ln_in_b': nrm((Dm,), 0.01),
        'w_in': nrm((DEPTH, Dm, N_IN), Dm ** -0.5),
        'b_gate': nrm((DEPTH, 2 * Dm), 0.01),
        'diff_lambda': nrm((DEPTH, 4, D_HD), 0.1),
        'diff_subln': 1.0 + nrm((DEPTH, D_DV), 0.01),
        'w_proj_ret': nrm((DEPTH, R_HEADS * R_DV, Dm), (R_HEADS * R_DV) ** -0.5),
        'w_proj_diff': nrm((DEPTH, D_HEADS * D_DV, Dm), (D_HEADS * D_DV) ** -0.5),
        'w_out': nrm((DEPTH, Dm, Dm), beta * Dm ** -0.5),
        'ln1_g': 1.0 + nrm((DEPTH, Dm), 0.01),
        'ln1_b': nrm((DEPTH, Dm), 0.01),
        'w_mq': nrm((DEPTH, Dm, M_HEADS * M_HD), Dm ** -0.5),
        'w_mk': nrm((DEPTH, Dm, M_HEADS * M_HD), Dm ** -0.5),
        'w_mv': nrm((DEPTH, Dm, M_HEADS * M_HD), beta * Dm ** -0.5),
        'w_mo': nrm((DEPTH, M_HEADS * M_HD, Dm), beta * (M_HEADS * M_HD) ** -0.5),
        'ln2_g': 1.0 + nrm((DEPTH, Dm), 0.01),
        'ln2_b': nrm((DEPTH, Dm), 0.01),
        'router_w': nrm((DEPTH, Dm, N_EXPERTS), Dm ** -0.5),
        'router_b': nrm((DEPTH, N_EXPERTS), 0.01),
        'w1': nrm((DEPTH, N_EXPERTS, Dm, 2 * D_FF), Dm ** -0.5),
        'b1': nrm((DEPTH, N_EXPERTS, 2 * D_FF), 0.01),
        'w2': nrm((DEPTH, N_EXPERTS, D_FF, Dm), beta * D_FF ** -0.5),
        'b2': nrm((DEPTH, N_EXPERTS, Dm), 0.01),
        'ln3_g': 1.0 + nrm((DEPTH, Dm), 0.01),
        'ln3_b': nrm((DEPTH, Dm), 0.01),
    }


def reference(x_prompt, x_sample, cache_diff_k, cache_diff_v, state_ret, cache_mem_k, cache_mem_v,
              mem_prompt, ln_in_g, ln_in_b, w_in, b_gate, diff_lambda, diff_subln, w_proj_ret,
              w_proj_diff, w_out, ln1_g, ln1_b, w_mq, w_mk, w_mv, w_mo, ln2_g, ln2_b,
              router_w, router_b, w1, b1, w2, b2, ln3_g, ln3_b):
    Bp, Lp, _ = x_prompt.shape
    Bs, Ls, _ = x_sample.shape
    past = cache_diff_k.shape[2]
    pos_p = jnp.arange(Lp, dtype=jnp.int32)
    pos_s = past + jnp.arange(Ls, dtype=jnp.int32)
    hp = layer_norm(x_prompt, ln_in_g, ln_in_b)
    hs = layer_norm(x_sample, ln_in_g, ln_in_b)
    kp_l, vp_l, sp_l, mkp_l, mvp_l, ks_l, vs_l, ss_l = [], [], [], [], [], [], [], []
    for l in range(DEPTH):
        lambda_init = 0.8 - 0.6 * math.exp(-0.3 * l)
        p = {
            'w_in': w_in[l], 'b_gate': b_gate[l], 'diff_lambda': diff_lambda[l],
            'diff_subln': diff_subln[l], 'w_proj_ret': w_proj_ret[l], 'w_proj_diff': w_proj_diff[l],
            'w_out': w_out[l], 'ln1_g': ln1_g[l], 'ln1_b': ln1_b[l], 'w_mq': w_mq[l], 'w_mo': w_mo[l],
            'ln2_g': ln2_g[l], 'ln2_b': ln2_b[l], 'router_w': router_w[l], 'router_b': router_b[l],
            'w1': w1[l], 'b1': b1[l], 'w2': w2[l], 'b2': b2[l], 'ln3_g': ln3_g[l], 'ln3_b': ln3_b[l],
        }
        mk_p = (mem_prompt @ w_mk[l]).reshape(Bp, N_MEM, M_HEADS, M_HD)
        mv_p = (mem_prompt @ w_mv[l]).reshape(Bp, N_MEM, M_HEADS, M_HD)
        S0 = jnp.zeros((Bp, R_HEADS, R_DK, R_DV), hp.dtype)
        hp, s_p, k_p, v_p = trunk_layer(hp, pos_p, S0, None, None, mk_p, mv_p, lambda_init, p)
        hs, s_s, k_s, v_s = trunk_layer(hs, pos_s, state_ret[l], cache_diff_k[l], cache_diff_v[l],
                                        cache_mem_k[l], cache_mem_v[l], lambda_init, p)
        kp_l.append(k_p); vp_l.append(v_p); sp_l.append(s_p); mkp_l.append(mk_p); mvp_l.append(mv_p)
        ks_l.append(k_s); vs_l.append(v_s); ss_l.append(s_s)
    return (hp, hs, jnp.stack(kp_l), jnp.stack(vp_l), jnp.stack(sp_l), jnp.stack(mkp_l),
            jnp.stack(mvp_l), jnp.stack(ks_l), jnp.stack(vs_l), jnp.stack(ss_l))
```

```python
import functools
import math

import jax
import jax.numpy as jnp
from jax import lax
from jax.experimental import pallas as pl
from jax.experimental.pallas import tpu as pltpu

F32 = jnp.float32
BF16 = jnp.bfloat16

CHUNK = 64
R_HEADS = 4
D_HEADS = 8
M_HEADS = 4
N_EXPERTS = 32
TOP_K = 4
SWIGLU_ALPHA = 1.702
SWIGLU_LIMIT = 7.0
LN_EPS = 1e-5
RMS_EPS = 1e-5
ROPE_BASE = 10000.0

LANES = 128
VMEM_LIMIT = 56 * 1024 * 1024

IN_TM = 512
IN_TN = 1024
RET_CHUNK = 256
ATT_BLK = 512
MIX_TM = 256
MEM_TM = 512
MOE_BM = 256
MOE_TN1 = 512
MOE_TN2 = 1024
GATHER_ROWS = 256
COMBINE_TM = 128


def _cparams(sem):
    return pltpu.CompilerParams(dimension_semantics=sem, vmem_limit_bytes=VMEM_LIMIT)


def _layer_norm(x, g, b):
    mu = jnp.mean(x, axis=-1, keepdims=True)
    xc = x - mu
    var = jnp.mean(xc * xc, axis=-1, keepdims=True)
    return xc * lax.rsqrt(var + LN_EPS) * g + b


def _dot(a, b):
    return jnp.dot(a, b, preferred_element_type=F32)


def _dot_nt(a, b):
    return lax.dot_general(a, b, (((1,), (1,)), ((), ())), preferred_element_type=F32)


def _dot_tn(a, b):
    return lax.dot_general(a, b, (((0,), (0,)), ((), ())), preferred_element_type=F32)


def _in_proj_kernel(x_ref, g_ref, b_ref, w_ref, p_ref, kv_ref, h_scr):
    j = pl.program_id(1)

    @pl.when(j == 0)
    def _():
        h_scr[...] = _layer_norm(x_ref[...], g_ref[...], b_ref[...]).astype(BF16)

    acc = _dot(h_scr[...], w_ref[...])
    p_ref[...] = acc.astype(BF16)

    @pl.when(j < 2)
    def _():
        kv_ref[...] = acc


def _in_proj(x, g, b, w):
    T, D = x.shape
    N = w.shape[1]
    grid = (T // IN_TM, N // IN_TN)
    return pl.pallas_call(
        _in_proj_kernel,
        grid=grid,
        in_specs=[
            pl.BlockSpec((IN_TM, D), lambda i, j: (i, 0)),
            pl.BlockSpec((1, D), lambda i, j: (0, 0)),
            pl.BlockSpec((1, D), lambda i, j: (0, 0)),
            pl.BlockSpec((D, IN_TN), lambda i, j: (0, j)),
        ],
        out_specs=[
            pl.BlockSpec((IN_TM, IN_TN), lambda i, j: (i, j)),
            pl.BlockSpec((IN_TM, IN_TN), lambda i, j: (i, jnp.minimum(j, 1))),
        ],
        out_shape=[
            jax.ShapeDtypeStruct((T, N), BF16),
            jax.ShapeDtypeStruct((T, 2 * IN_TN), F32),
        ],
        scratch_shapes=[pltpu.VMEM((IN_TM, D), BF16)],
        compiler_params=_cparams(("parallel", "arbitrary")),
        name="in_proj",
    )(x, g, b, w)


P_DK = 0
P_DV = 8
P_RQ = 16
P_RK = 20
P_RV = 24
P_RG = 32
P_DQ = 40
P_GATE = 48


def _retention_kernel(q_ref, k_ref, v_ref, g_ref, cos_ref, sin_ref, dm_ref, qd_ref, kd_ref,
                      bd_ref, s0_ref, y_ref, sout_ref, s_scr, *, n_chunks, dk, dv):
    c = pl.program_id(1)

    @pl.when(c == 0)
    def _():
        s_scr[...] = s0_ref[0]

    cos = cos_ref[...]
    sin = sin_ref[...]
    k_scale = dk ** -0.5
    for h in range(R_HEADS):
        q = q_ref[:, h * dk:(h + 1) * dk].astype(F32)
        k = k_ref[:, h * dk:(h + 1) * dk].astype(F32)
        v = v_ref[:, h * dv:(h + 1) * dv]
        g = g_ref[:, h * dv:(h + 1) * dv].astype(F32)
        q = q * cos + pltpu.roll(q, dk // 2, 1) * sin
        k = (k * cos + pltpu.roll(k, dk // 2, 1) * sin) * k_scale
        qb = q.astype(BF16)
        kb = k.astype(BF16)
        s = s_scr[h]
        scores = _dot_nt(qb, kb) * dm_ref[h]
        o = _dot(scores.astype(BF16), v) + _dot((q * qd_ref[h]).astype(BF16), s.astype(BF16))
        s_scr[h] = bd_ref[h] * s + _dot_tn((k * kd_ref[h]).astype(BF16), v)
        o = o * lax.rsqrt(jnp.mean(o * o, axis=-1, keepdims=True) + RMS_EPS)
        y_ref[:, h * dv:(h + 1) * dv] = (g * jax.nn.sigmoid(g) * o).astype(BF16)

    @pl.when(c == n_chunks - 1)
    def _():
        sout_ref[0] = s_scr[...]


def _retention(P, row_off, B, L, pos0, S0):
    dk, dv = 128, 256
    C = min(L, RET_CHUNK)
    nc = L // C
    ob = row_off // C
    pos = (pos0 + jnp.arange(L, dtype=jnp.int32)).astype(F32)
    inv = 1.0 / (ROPE_BASE ** jnp.linspace(0.0, 1.0, dk // 2, dtype=F32))
    ang = pos[:, None] * inv[None, :]
    cos = jnp.concatenate([jnp.cos(ang), jnp.cos(ang)], axis=-1)
    sin = jnp.concatenate([-jnp.sin(ang), jnp.sin(ang)], axis=-1)
    log_g = jnp.log1p(-jnp.power(2.0, -5.0 - jnp.arange(R_HEADS, dtype=F32)))
    i = jnp.arange(C, dtype=F32)
    rel = i[:, None] - i[None, :]
    dmask = jnp.where(rel >= 0, jnp.exp(log_g[:, None, None] * jnp.maximum(rel, 0.0)), 0.0)
    q_decay = jnp.exp(log_g[:, None] * (i + 1.0))[..., None]
    k_decay = jnp.exp(log_g[:, None] * (C - 1.0 - i))[..., None]
    b_decay = jnp.exp(log_g * C)[:, None, None]

    def rows(b, c):
        return ob + b * nc + c

    kern = functools.partial(_retention_kernel, n_chunks=nc, dk=dk, dv=dv)
    return pl.pallas_call(
        kern,
        grid=(B, nc),
        in_specs=[
            pl.BlockSpec((C, R_HEADS * dk), lambda b, c: (rows(b, c), P_RQ * LANES // (R_HEADS * dk))),
            pl.BlockSpec((C, R_HEADS * dk), lambda b, c: (rows(b, c), P_RK * LANES // (R_HEADS * dk))),
            pl.BlockSpec((C, R_HEADS * dv), lambda b, c: (rows(b, c), P_RV * LANES // (R_HEADS * dv))),
            pl.BlockSpec((C, R_HEADS * dv), lambda b, c: (rows(b, c), P_RG * LANES // (R_HEADS * dv))),
            pl.BlockSpec((C, dk), lambda b, c: (c, 0)),
            pl.BlockSpec((C, dk), lambda b, c: (c, 0)),
            pl.BlockSpec((R_HEADS, C, C), lambda b, c: (0, 0, 0)),
            pl.BlockSpec((R_HEADS, C, 1), lambda b, c: (0, 0, 0)),
            pl.BlockSpec((R_HEADS, C, 1), lambda b, c: (0, 0, 0)),
            pl.BlockSpec((R_HEADS, 1, 1), lambda b, c: (0, 0, 0)),
            pl.BlockSpec((1, R_HEADS, dk, dv), lambda b, c: (b, 0, 0, 0)),
        ],
        out_specs=[
            pl.BlockSpec((C, R_HEADS * dv), lambda b, c: (b * nc + c, 0)),
            pl.BlockSpec((1, R_HEADS, dk, dv), lambda b, c: (b, 0, 0, 0)),
        ],
        out_shape=[
            jax.ShapeDtypeStruct((B * L, R_HEADS * dv), BF16),
            jax.ShapeDtypeStruct((B, R_HEADS, dk, dv), F32),
        ],
        scratch_shapes=[pltpu.VMEM((R_HEADS, dk, dv), F32)],
        compiler_params=_cparams(("parallel", "arbitrary")),
        name="retention",
    )(P, P, P, P, cos, sin, dmask, q_decay, k_decay, b_decay, S0)


def _diff_lambda(lam_ref, lambda_init):
    lv = lam_ref[...]
    a = jnp.sum(lv[0:1] * lv[1:2], axis=-1, keepdims=True)
    b = jnp.sum(lv[2:3] * lv[3:4], axis=-1, keepdims=True)
    return jnp.exp(a) - jnp.exp(b) + lambda_init


def _stack_maps(q, hd):
    lane = lax.broadcasted_iota(jnp.int32, q.shape, 1)
    zero = jnp.zeros_like(q)
    return jnp.concatenate([jnp.where(lane < hd, q, zero), jnp.where(lane < hd, zero, q)], axis=0)


def _diff_finish(acc, l, lam, subln, lambda_init, tq):
    o = acc[:tq] / l[:tq] - lam * (acc[tq:] / l[tq:])
    o = o * lax.rsqrt(jnp.mean(o * o, axis=-1, keepdims=True) + RMS_EPS)
    return (o * subln * (1.0 - lambda_init)).astype(BF16)


def _attn_prompt_kernel(q_ref, k_ref, v_ref, lam_ref, sub_ref, y_ref, m_scr, l_scr, acc_scr,
                        *, blk, hd, lambda_init):
    qi = pl.program_id(2)
    qs = _stack_maps(q_ref[...] * (hd ** -0.5), hd)

    m_scr[...] = jnp.full_like(m_scr, -jnp.inf)
    l_scr[...] = jnp.zeros_like(l_scr)
    acc_scr[...] = jnp.zeros_like(acc_scr)

    def step(ki, masked):
        off = pl.multiple_of(ki * blk, blk)
        kb = k_ref[pl.ds(off, blk), :]
        vb = v_ref[pl.ds(off, blk), :]
        s = _dot_nt(qs, kb)
        if masked:
            row = lax.broadcasted_iota(jnp.int32, s.shape, 0)
            col = lax.broadcasted_iota(jnp.int32, s.shape, 1)
            qrow = jnp.where(row >= blk, row - blk, row)
            shift = CHUNK.bit_length() - 1
            s = jnp.where((col >> shift) <= (qrow >> shift), s, -1e30)
        m_prev = m_scr[...]
        m_new = jnp.maximum(m_prev, jnp.max(s, axis=-1, keepdims=True))
        alpha = jnp.exp(m_prev - m_new)
        p = jnp.exp(s - m_new)
        l_scr[...] = alpha * l_scr[...] + jnp.sum(p, axis=-1, keepdims=True)
        acc_scr[...] = alpha * acc_scr[...] + _dot(p.astype(BF16), vb)
        m_scr[...] = m_new

    def body(ki, carry):
        step(ki, False)
        return carry

    lax.fori_loop(0, qi, body, 0)
    step(qi, True)

    lam = _diff_lambda(lam_ref, lambda_init)
    y_ref[...] = _diff_finish(acc_scr[...], l_scr[...], lam, sub_ref[...], lambda_init, blk)


def _attn_prompt(P, B, S, lam_p, subln, lambda_init):
    hd = 64
    blk = min(ATT_BLK, S)
    nq = S // blk
    kern = functools.partial(_attn_prompt_kernel, blk=blk, hd=hd, lambda_init=lambda_init)
    return pl.pallas_call(
        kern,
        grid=(B, D_HEADS, nq),
        in_specs=[
            pl.BlockSpec((blk, 2 * hd), lambda b, h, i: (b * nq + i, P_DQ + h)),
            pl.BlockSpec((S, 2 * hd), lambda b, h, i: (b, P_DK + h)),
            pl.BlockSpec((S, 2 * hd), lambda b, h, i: (b, P_DV + h)),
            pl.BlockSpec((4, hd), lambda b, h, i: (0, 0)),
            pl.BlockSpec((1, 2 * hd), lambda b, h, i: (0, 0)),
        ],
        out_specs=pl.BlockSpec((blk, 2 * hd), lambda b, h, i: (b * nq + i, h)),
        out_shape=jax.ShapeDtypeStruct((B * S, D_HEADS * 2 * hd), BF16),
        scratch_shapes=[
            pltpu.VMEM((2 * blk, 1), F32),
            pltpu.VMEM((2 * blk, 1), F32),
            pltpu.VMEM((2 * blk, 2 * hd), F32),
        ],
        compiler_params=_cparams(("parallel", "parallel", "arbitrary")),
        name="attn_prompt",
    )(P, P, P, lam_p, subln)


def _attn_sample_kernel(q_ref, kn_ref, vn_ref, kc_ref, vc_ref, lam_ref, sub_ref, y_ref,
                        *, tq, hd, lambda_init):
    qs = _stack_maps(q_ref[...] * (hd ** -0.5), hd)
    kc = kc_ref[...].astype(BF16)
    vc = vc_ref[...].astype(BF16)
    s_c = _dot_nt(qs, kc)
    s_n = _dot_nt(qs, kn_ref[...])
    m = jnp.maximum(jnp.max(s_c, axis=-1, keepdims=True), jnp.max(s_n, axis=-1, keepdims=True))
    p_c = jnp.exp(s_c - m)
    p_n = jnp.exp(s_n - m)
    l = jnp.sum(p_c, axis=-1, keepdims=True) + jnp.sum(p_n, axis=-1, keepdims=True)
    acc = _dot(p_c.astype(BF16), vc) + _dot(p_n.astype(BF16), vn_ref[...])
    lam = _diff_lambda(lam_ref, lambda_init)
    y_ref[...] = _diff_finish(acc, l, lam, sub_ref[...], lambda_init, tq)


def _attn_sample(P, row_off, B, L, cache_k, cache_v, lam_p, subln, lambda_init):
    hd = 64
    past = cache_k.shape[0] // B
    ob = row_off // L
    kern = functools.partial(_attn_sample_kernel, tq=L, hd=hd, lambda_init=lambda_init)
    return pl.pallas_call(
        kern,
        grid=(B, D_HEADS),
        in_specs=[
            pl.BlockSpec((L, 2 * hd), lambda b, h: (ob + b, P_DQ + h)),
            pl.BlockSpec((L, 2 * hd), lambda b, h: (ob + b, P_DK + h)),
            pl.BlockSpec((L, 2 * hd), lambda b, h: (ob + b, P_DV + h)),
            pl.BlockSpec((past, 2 * hd), lambda b, h: (b, h)),
            pl.BlockSpec((past, 2 * hd), lambda b, h: (b, h)),
            pl.BlockSpec((4, hd), lambda b, h: (0, 0)),
            pl.BlockSpec((1, 2 * hd), lambda b, h: (0, 0)),
        ],
        out_specs=pl.BlockSpec((L, 2 * hd), lambda b, h: (b, h)),
        out_shape=jax.ShapeDtypeStruct((B * L, D_HEADS * 2 * hd), BF16),
        compiler_params=_cparams(("parallel", "parallel")),
        name="attn_sample",
    )(P, P, P, cache_k, cache_v, lam_p, subln)


def _mix_kernel(x_ref, yr_ref, yd_ref, gr_ref, gd_ref, lig_ref, lib_ref, bg_ref, wpr_ref, wpd_ref,
                wo_ref, g1_ref, b1_ref, h1_ref, *, alpha):
    d = x_ref.shape[1]
    h0 = _layer_norm(x_ref[...], lig_ref[...], lib_ref[...])
    g_ret = jax.nn.sigmoid(gr_ref[...].astype(F32) + bg_ref[:, :d])
    g_diff = jax.nn.sigmoid(gd_ref[...].astype(F32) + bg_ref[:, d:])
    merged = g_ret * _dot(yr_ref[...], wpr_ref[...]) + g_diff * _dot(yd_ref[...], wpd_ref[...])
    mixed = _dot(merged.astype(BF16), wo_ref[...])
    h1_ref[...] = _layer_norm(alpha * h0 + mixed, g1_ref[...], b1_ref[...])


def _mix(x, y_ret, y_diff, P, ln_in_g, ln_in_b, b_gate, wpr, wpd, wo, g1, b1, alpha):
    T, D = x.shape
    tm = MIX_TM
    gcol = P_GATE * LANES // D
    const = lambda i: (0, 0)
    single = pl.Buffered(1)
    kern = functools.partial(_mix_kernel, alpha=alpha)
    return pl.pallas_call(
        kern,
        grid=(T // tm,),
        in_specs=[
            pl.BlockSpec((tm, D), lambda i: (i, 0)),
            pl.BlockSpec((tm, y_ret.shape[1]), lambda i: (i, 0)),
            pl.BlockSpec((tm, y_diff.shape[1]), lambda i: (i, 0)),
            pl.BlockSpec((tm, D), lambda i: (i, gcol)),
            pl.BlockSpec((tm, D), lambda i: (i, gcol + 1)),
            pl.BlockSpec((1, D), const),
            pl.BlockSpec((1, D), const),
            pl.BlockSpec((1, 2 * D), const),
            pl.BlockSpec(wpr.shape, const, pipeline_mode=single),
            pl.BlockSpec(wpd.shape, const, pipeline_mode=single),
            pl.BlockSpec(wo.shape, const, pipeline_mode=single),
            pl.BlockSpec((1, D), const),
            pl.BlockSpec((1, D), const),
        ],
        out_specs=pl.BlockSpec((tm, D), lambda i: (i, 0)),
        out_shape=jax.ShapeDtypeStruct((T, D), F32),
        compiler_params=_cparams(("parallel",)),
        name="mix",
    )(x, y_ret, y_diff, P, P, ln_in_g, ln_in_b, b_gate, wpr, wpd, wo, g1, b1)


def _mem_kernel(h1_ref, mk_ref, mv_ref, wq_ref, wo_ref, g2_ref, b2_ref, rw_ref, rb_ref,
                h2_ref, route_ref, *, n_sub, alpha, hd):
    tm = h1_ref.shape[0]
    seg = tm // n_sub
    h1 = h1_ref[...]
    q = (_dot(h1.astype(BF16), wq_ref[...]) * (hd ** -0.5)).astype(BF16)
    outs = []
    for s in range(n_sub):
        mk = mk_ref[s].astype(BF16)
        mv = mv_ref[s].astype(BF16)
        heads = []
        for h in range(M_HEADS):
            qh = q[s * seg:(s + 1) * seg, h * hd:(h + 1) * hd]
            sc = _dot_nt(qh, mk[:, h * hd:(h + 1) * hd])
            sc = sc - jnp.max(sc, axis=-1, keepdims=True)
            p = jnp.exp(sc)
            p = p / jnp.sum(p, axis=-1, keepdims=True)
            heads.append(_dot(p.astype(BF16), mv[:, h * hd:(h + 1) * hd]))
        outs.append(jnp.concatenate(heads, axis=-1))
    o = outs[0] if n_sub == 1 else jnp.concatenate(outs, axis=0)
    h2 = _layer_norm(alpha * h1 + _dot(o.astype(BF16), wo_ref[...]), g2_ref[...], b2_ref[...])
    h2_ref[...] = h2
    h2b = h2.astype(BF16)

    h_lo = (h2 - h2b.astype(F32)).astype(BF16)
    rw = rw_ref[...]
    rw_hi = rw.astype(BF16)
    rw_lo = (rw - rw_hi.astype(F32)).astype(BF16)
    logits = _dot(h2b, rw_hi) + _dot(h2b, rw_lo) + _dot(h_lo, rw_hi) + rb_ref[...]

    n_e = logits.shape[1]
    eidx = lax.broadcasted_iota(jnp.int32, logits.shape, 1).astype(F32)
    lane = lax.broadcasted_iota(jnp.int32, (tm, LANES), 1)
    route = jnp.zeros((tm, LANES), F32)
    work = logits
    vals = []
    for k in range(TOP_K):
        mx = jnp.max(work, axis=-1, keepdims=True)
        sel = jnp.min(jnp.where(work == mx, eidx, float(n_e)), axis=-1, keepdims=True)
        work = jnp.where(eidx == sel, -jnp.inf, work)
        vals.append(mx)
        route = jnp.where(lane == TOP_K + k, sel, route)
    ex = [jnp.exp(v - vals[0]) for v in vals]
    den = ex[0] + ex[1] + ex[2] + ex[3]
    for k in range(TOP_K):
        route = jnp.where(lane == k, ex[k] / den, route)
    route_ref[...] = route


def _mem_attn(h1, row_off, n_rows, mem_k, mem_v, L, wq, wo, g2, b2, rw, rb, alpha):
    D = h1.shape[1]
    hd = 128
    tm = MEM_TM
    n_sub = max(1, tm // L)
    per_b = max(1, L // tm)
    ob = row_off // tm
    n_mem = mem_k.shape[1]
    const = lambda i: (0, 0)
    single = pl.Buffered(1)
    kern = functools.partial(_mem_kernel, n_sub=n_sub, alpha=alpha, hd=hd)
    return pl.pallas_call(
        kern,
        grid=(n_rows // tm,),
        in_specs=[
            pl.BlockSpec((tm, D), lambda i: (ob + i, 0)),
            pl.BlockSpec((n_sub, n_mem, M_HEADS * hd), lambda i: (i // per_b, 0, 0)),
            pl.BlockSpec((n_sub, n_mem, M_HEADS * hd), lambda i: (i // per_b, 0, 0)),
            pl.BlockSpec(wq.shape, const, pipeline_mode=single),
            pl.BlockSpec(wo.shape, const, pipeline_mode=single),
            pl.BlockSpec((1, D), const),
            pl.BlockSpec((1, D), const),
            pl.BlockSpec(rw.shape, const, pipeline_mode=single),
            pl.BlockSpec((1, rw.shape[1]), const),
        ],
        out_specs=[
            pl.BlockSpec((tm, D), lambda i: (i, 0)),
            pl.BlockSpec((tm, LANES), lambda i: (i, 0)),
        ],
        out_shape=[
            jax.ShapeDtypeStruct((n_rows, D), F32),
            jax.ShapeDtypeStruct((n_rows, LANES), F32),
        ],
        compiler_params=_cparams(("parallel",)),
        name="mem_attn",
    )(h1, mem_k, mem_v, wq, wo, g2, b2, rw, rb)


def _mem_kv_kernel(x_ref, w_ref, o_ref):
    o_ref[...] = _dot(x_ref[...].astype(BF16), w_ref[...])


def _mem_kv(mem, w):
    R, D = mem.shape
    N = w.shape[1]
    tm = 256
    return pl.pallas_call(
        _mem_kv_kernel,
        grid=(R // tm,),
        in_specs=[pl.BlockSpec((tm, D), lambda i: (i, 0)), pl.BlockSpec((D, N), lambda i: (0, 0))],
        out_specs=pl.BlockSpec((tm, N), lambda i: (i, 0)),
        out_shape=jax.ShapeDtypeStruct((R, N), F32),
        compiler_params=_cparams(("parallel",)),
        name="mem_kv",
    )(mem, w)


def _row_copy(src_hbm, dst, sem, src_row, dst_row):
    return pltpu.make_async_copy(src_hbm.at[pl.ds(src_row, 1)], dst.at[pl.ds(dst_row, 1)], sem)


def _gather_kernel(tok_ref, h_hbm, o_ref, buf, sem, *, rows, n_steps):
    i = pl.program_id(0)

    def issue(step, slot):
        def body(r, carry):
            _row_copy(h_hbm, buf.at[slot], sem.at[slot], tok_ref[step * rows + r], r).start()
            return carry
        lax.fori_loop(0, rows, body, 0, unroll=8)

    def wait(slot):
        def body(r, carry):
            _row_copy(h_hbm, buf.at[slot], sem.at[slot], 0, r).wait()
            return carry
        lax.fori_loop(0, rows, body, 0, unroll=8)

    slot = i % 2

    @pl.when(i == 0)
    def _():
        issue(0, 0)

    @pl.when(i + 1 < n_steps)
    def _():
        issue(i + 1, 1 - slot)

    wait(slot)
    o_ref[...] = buf[slot].astype(BF16)


def _moe_gather(h2, row_tok):
    n_rows = row_tok.shape[0]
    D = h2.shape[1]
    rows = GATHER_ROWS
    n_steps = n_rows // rows
    kern = functools.partial(_gather_kernel, rows=rows, n_steps=n_steps)
    return pl.pallas_call(
        kern,
        grid_spec=pltpu.PrefetchScalarGridSpec(
            num_scalar_prefetch=1,
            grid=(n_steps,),
            in_specs=[pl.BlockSpec(memory_space=pl.ANY)],
            out_specs=pl.BlockSpec((rows, D), lambda i, tok: (i, 0)),
            scratch_shapes=[pltpu.VMEM((2, rows, D), F32), pltpu.SemaphoreType.DMA((2,))],
        ),
        out_shape=jax.ShapeDtypeStruct((n_rows, D), BF16),
        compiler_params=_cparams(("arbitrary",)),
        name="moe_gather",
    )(row_tok, h2)


def _expert_changed(be_ref, m):
    prev = be_ref[jnp.maximum(m - 1, 0)]
    return jnp.logical_or(m == 0, be_ref[m] != prev)


def _moe_up_kernel(be_ref, nu_ref, x_ref, wg_ref, wl_ref, bg_ref, bl_ref, a_ref, wg_scr, wl_scr):
    m = pl.program_id(1)

    @pl.when(_expert_changed(be_ref, m))
    def _():
        wg_scr[...] = wg_ref[0].astype(BF16)
        wl_scr[...] = wl_ref[0].astype(BF16)

    @pl.when(m < nu_ref[0])
    def _():
        x = x_ref[...]
        u_glu = jnp.minimum(_dot(x, wg_scr[...]) + bg_ref[0], SWIGLU_LIMIT)
        u_lin = jnp.clip(_dot(x, wl_scr[...]) + bl_ref[0], -SWIGLU_LIMIT, SWIGLU_LIMIT)
        a = u_glu * jax.nn.sigmoid(SWIGLU_ALPHA * u_glu) * (u_lin + 1.0)
        a_ref[...] = a.astype(BF16)

    @pl.when(m >= nu_ref[0])
    def _():
        a_ref[...] = jnp.zeros_like(a_ref)


def _moe_up(xs, w1, b1, block_e, n_used):
    n_rows, D = xs.shape
    F = w1.shape[2] // 2
    bm, tn = MOE_BM, MOE_TN1
    nb = n_rows // bm
    nf = F // tn
    return pl.pallas_call(
        _moe_up_kernel,
        grid_spec=pltpu.PrefetchScalarGridSpec(
            num_scalar_prefetch=2,
            grid=(nf, nb),
            in_specs=[
                pl.BlockSpec((bm, D), lambda f, m, be, nu: (jnp.minimum(m, nu[0] - 1), 0)),
                pl.BlockSpec((1, D, tn), lambda f, m, be, nu: (be[m], 0, f)),
                pl.BlockSpec((1, D, tn), lambda f, m, be, nu: (be[m], 0, nf + f)),
                pl.BlockSpec((1, 1, tn), lambda f, m, be, nu: (be[m], 0, f)),
                pl.BlockSpec((1, 1, tn), lambda f, m, be, nu: (be[m], 0, nf + f)),
            ],
            out_specs=pl.BlockSpec((bm, tn), lambda f, m, be, nu: (m, f)),
            scratch_shapes=[pltpu.VMEM((D, tn), BF16), pltpu.VMEM((D, tn), BF16)],
        ),
        out_shape=jax.ShapeDtypeStruct((n_rows, F), BF16),
        compiler_params=_cparams(("arbitrary", "arbitrary")),
        name="moe_up",
    )(block_e, n_used, xs, w1, w1, b1, b1)


def _moe_down_kernel(be_ref, nu_ref, a_ref, w_ref, b_ref, rw_ref, y_ref, w_scr):
    m = pl.program_id(1)

    @pl.when(_expert_changed(be_ref, m))
    def _():
        w_scr[...] = w_ref[0].astype(BF16)

    @pl.when(m < nu_ref[0])
    def _():
        y = (_dot(a_ref[...], w_scr[...]) + b_ref[0]) * rw_ref[...]
        y_ref[...] = _pack_halves(y)

    @pl.when(m >= nu_ref[0])
    def _():
        y_ref[...] = jnp.zeros_like(y_ref)


def _pack_halves(y):
    half = y.shape[1] // 2
    bits = lax.bitcast_convert_type(y.astype(BF16).astype(F32), jnp.uint32)
    return bits[:, :half] | (bits[:, half:] >> 16)


def _unpack_halves(w):
    hi = lax.bitcast_convert_type(w & jnp.uint32(0xFFFF0000), F32)
    lo = lax.bitcast_convert_type(w << 16, F32)
    return jnp.concatenate([hi, lo], axis=-1)


def _moe_down(a, w2, b2, row_w, block_e, n_used):
    n_rows, F = a.shape
    D = w2.shape[2]
    bm, tn = MOE_BM, MOE_TN2
    nb = n_rows // bm
    nd = D // tn
    return pl.pallas_call(
        _moe_down_kernel,
        grid_spec=pltpu.PrefetchScalarGridSpec(
            num_scalar_prefetch=2,
            grid=(nd, nb),
            in_specs=[
                pl.BlockSpec((bm, F), lambda d, m, be, nu: (jnp.minimum(m, nu[0] - 1), 0)),
                pl.BlockSpec((1, F, tn), lambda d, m, be, nu: (be[m], 0, d)),
                pl.BlockSpec((1, 1, tn), lambda d, m, be, nu: (be[m], 0, d)),
                pl.BlockSpec((bm, 1), lambda d, m, be, nu: (m, 0)),
            ],
            out_specs=pl.BlockSpec((bm, tn // 2), lambda d, m, be, nu: (m, d)),
            scratch_shapes=[pltpu.VMEM((F, tn), BF16)],
        ),
        out_shape=jax.ShapeDtypeStruct((n_rows, D // 2), jnp.uint32),
        compiler_params=_cparams(("arbitrary", "arbitrary")),
        name="moe_down",
    )(block_e, n_used, a, w2, b2, row_w)


def _combine_kernel(dest_ref, h2_ref, g_ref, b_ref, ys_hbm, op_ref, os_ref, buf, sem,
                    *, tm, n_steps, n_prompt, alpha):
    i = pl.program_id(0)

    def issue(step, slot):
        def body(t, carry):
            for k in range(TOP_K):
                row = dest_ref[(step * tm + t) * TOP_K + k]
                _row_copy(ys_hbm, buf.at[slot, k], sem.at[slot], row, t).start()
            return carry
        lax.fori_loop(0, tm, body, 0, unroll=2)

    def wait(slot):
        def body(t, carry):
            for k in range(TOP_K):
                _row_copy(ys_hbm, buf.at[slot, k], sem.at[slot], 0, t).wait()
            return carry
        lax.fori_loop(0, tm, body, 0, unroll=2)

    slot = i % 2

    @pl.when(i == 0)
    def _():
        issue(0, 0)

    @pl.when(i + 1 < n_steps)
    def _():
        issue(i + 1, 1 - slot)

    wait(slot)
    half = MOE_TN2 // 2
    parts = []
    for d in range(buf.shape[-1] // half):
        yd = _unpack_halves(buf[slot, 0, :, d * half:(d + 1) * half])
        for k in range(1, TOP_K):
            yd = yd + _unpack_halves(buf[slot, k, :, d * half:(d + 1) * half])
        parts.append(yd)
    y = jnp.concatenate(parts, axis=-1)
    out = _layer_norm(alpha * h2_ref[...] + y, g_ref[...], b_ref[...])

    @pl.when(i < n_prompt)
    def _():
        op_ref[...] = out

    @pl.when(i >= n_prompt)
    def _():
        os_ref[...] = out


def _moe_combine(ys, dest, h2, g3, b3, t_prompt, alpha):
    T, D = h2.shape
    tm = COMBINE_TM
    n_steps = T // tm
    n_prompt = t_prompt // tm
    kern = functools.partial(_combine_kernel, tm=tm, n_steps=n_steps, n_prompt=n_prompt, alpha=alpha)
    return pl.pallas_call(
        kern,
        grid_spec=pltpu.PrefetchScalarGridSpec(
            num_scalar_prefetch=1,
            grid=(n_steps,),
            in_specs=[
                pl.BlockSpec((tm, D), lambda i, d: (i, 0)),
                pl.BlockSpec((1, D), lambda i, d: (0, 0)),
                pl.BlockSpec((1, D), lambda i, d: (0, 0)),
                pl.BlockSpec(memory_space=pl.ANY),
            ],
            out_specs=[
                pl.BlockSpec((tm, D), lambda i, d: (jnp.minimum(i, n_prompt - 1), 0)),
                pl.BlockSpec((tm, D), lambda i, d: (jnp.maximum(i - n_prompt, 0), 0)),
            ],
            scratch_shapes=[pltpu.VMEM((2, TOP_K, tm, D // 2), jnp.uint32), pltpu.SemaphoreType.DMA((2,))],
        ),
        out_shape=[
            jax.ShapeDtypeStruct((t_prompt, D), F32),
            jax.ShapeDtypeStruct((T - t_prompt, D), F32),
        ],
        compiler_params=_cparams(("arbitrary",)),
        name="moe_combine",
    )(dest, h2, g3, b3, ys)


def _moe_plan(route, bm):
    T = route.shape[0]
    gate = route[:, :TOP_K]
    top_idx = route[:, TOP_K:2 * TOP_K].astype(jnp.int32)
    onehot = (top_idx[:, :, None] == jnp.arange(N_EXPERTS, dtype=jnp.int32)[None, None, :])
    cnt = jnp.sum(onehot.astype(jnp.int32), axis=1)
    incl = jnp.cumsum(cnt, axis=0)
    counts = incl[-1]
    rank = jnp.take_along_axis(incl - cnt, top_idx, axis=1)
    padded = (counts + bm - 1) // bm * bm
    pad_end = jnp.cumsum(padded)
    pad_start = pad_end - padded
    dest = (pad_start[top_idx] + rank).reshape(-1)
    n_blocks = -(-(T * TOP_K) // bm) + N_EXPERTS
    n_rows = n_blocks * bm
    flat_tok = jnp.arange(T * TOP_K, dtype=jnp.int32) // TOP_K
    row_tok = jnp.zeros((n_rows,), jnp.int32).at[dest].set(flat_tok, unique_indices=True)
    row_w = jnp.zeros((n_rows,), F32).at[dest].set(gate.reshape(-1), unique_indices=True)
    block_start = jnp.arange(n_blocks, dtype=jnp.int32) * bm
    block_e = jnp.minimum(jnp.searchsorted(pad_end, block_start, side='right'),
                          N_EXPERTS - 1).astype(jnp.int32)
    n_used = (pad_end[-1] // bm).astype(jnp.int32).reshape(1)
    return dest.astype(jnp.int32), row_tok, row_w.reshape(n_rows, 1), block_e, n_used


def kernel(x_prompt, x_sample, cache_diff_k, cache_diff_v, state_ret, cache_mem_k, cache_mem_v, mem_prompt, ln_in_g, ln_in_b, w_in, b_gate, diff_lambda, diff_subln, w_proj_ret, w_proj_diff, w_out, ln1_g, ln1_b, w_mq, w_mk, w_mv, w_mo, ln2_g, ln2_b, router_w, router_b, w1, b1, w2, b2, ln3_g, ln3_b):
    Bp, Lp, D = x_prompt.shape
    Bs, Ls, _ = x_sample.shape
    depth = w_in.shape[0]
    assert depth == 1
    past = cache_diff_k.shape[2]
    n_mem = mem_prompt.shape[1]
    Tp, Ts = Bp * Lp, Bs * Ls
    alpha = (2.0 * depth) ** 0.25
    lambda_init = 0.8 - 0.6 * math.exp(-0.3 * 0)
    row = lambda v: v.reshape(1, -1)

    x = jnp.concatenate([x_prompt.reshape(Tp, D), x_sample.reshape(Ts, D)], axis=0)
    wi = w_in[0]
    w_perm = jnp.concatenate([wi[:, 4096:6144], wi[:, :4096], wi[:, 6144:]], axis=1).astype(BF16)

    P, KV = _in_proj(x, row(ln_in_g), row(ln_in_b), w_perm)

    zeros_state = jnp.zeros((Bp,) + state_ret.shape[2:], F32)
    yr_p, s_p = _retention(P, 0, Bp, Lp, 0, zeros_state)
    yr_s, s_s = _retention(P, Tp, Bs, Ls, past, state_ret[0])

    yd_p = _attn_prompt(P, Bp, Lp, diff_lambda[0], row(diff_subln[0]), lambda_init)
    ck = cache_diff_k[0].reshape(Bs * past, -1)
    cv = cache_diff_v[0].reshape(Bs * past, -1)
    yd_s = _attn_sample(P, Tp, Bs, Ls, ck, cv, diff_lambda[0], row(diff_subln[0]), lambda_init)

    y_ret = jnp.concatenate([yr_p, yr_s], axis=0)
    y_diff = jnp.concatenate([yd_p, yd_s], axis=0)
    h1 = _mix(x, y_ret, y_diff, P, row(ln_in_g), row(ln_in_b), row(b_gate[0]),
              w_proj_ret[0].astype(BF16), w_proj_diff[0].astype(BF16), w_out[0].astype(BF16),
              row(ln1_g[0]), row(ln1_b[0]), alpha)

    w_mkv = jnp.concatenate([w_mk[0], w_mv[0]], axis=1).astype(BF16)
    mkv = _mem_kv(mem_prompt.reshape(Bp * n_mem, D), w_mkv)
    hm = w_mk.shape[2]
    mk_p = mkv[:, :hm].reshape(Bp, n_mem, hm)
    mv_p = mkv[:, hm:].reshape(Bp, n_mem, hm)
    mem_args = (w_mq[0].astype(BF16), w_mo[0].astype(BF16), row(ln2_g[0]), row(ln2_b[0]),
                router_w[0], row(router_b[0]), alpha)
    h2_p, rt_p = _mem_attn(h1, 0, Tp, mk_p, mv_p, Lp, *mem_args)
    h2_s, rt_s = _mem_attn(h1, Tp, Ts, cache_mem_k[0].reshape(Bs, n_mem, hm),
                                  cache_mem_v[0].reshape(Bs, n_mem, hm), Ls, *mem_args)
    h2 = jnp.concatenate([h2_p, h2_s], axis=0)
    route = jnp.concatenate([rt_p, rt_s], axis=0)

    dest, row_tok, row_w, block_e, n_used = _moe_plan(route, MOE_BM)
    xs = _moe_gather(h2, row_tok)
    act = _moe_up(xs, w1[0], b1[0][:, None, :], block_e, n_used)
    ys = _moe_down(act, w2[0], b2[0][:, None, :], row_w, block_e, n_used)
    out_p, out_s = _moe_combine(ys, dest, h2, row(ln3_g[0]), row(ln3_b[0]), Tp, alpha)

    hk = D_HEADS * 128
    dk_all, dv_all = KV[:, :hk], KV[:, hk:]
    return (
        out_p.reshape(Bp, Lp, D),
        out_s.reshape(Bs, Ls, D),
        dk_all[:Tp].reshape(1, Bp, Lp, D_HEADS, 128),
        dv_all[:Tp].reshape(1, Bp, Lp, D_HEADS, 128),
        s_p[None],
        mk_p.reshape(1, Bp, n_mem, M_HEADS, hm // M_HEADS),
        mv_p.reshape(1, Bp, n_mem, M_HEADS, hm // M_HEADS),
        dk_all[Tp:].reshape(1, Bs, Ls, D_HEADS, 128),
        dv_all[Tp:].reshape(1, Bs, Ls, D_HEADS, 128),
        s_s[None],
    )
```

```python
import functools
import math

import jax
import jax.numpy as jnp
from jax import lax
from jax.experimental import pallas as pl
from jax.experimental.pallas import tpu as pltpu

F32 = jnp.float32
BF16 = jnp.bfloat16

CHUNK = 64
R_HEADS = 4
D_HEADS = 8
M_HEADS = 4
N_EXPERTS = 32
TOP_K = 4
SWIGLU_ALPHA = 1.702
SWIGLU_LIMIT = 7.0
LN_EPS = 1e-5
RMS_EPS = 1e-5
ROPE_BASE = 10000.0

LANES = 128
VMEM_LIMIT = 56 * 1024 * 1024

IN_TM = 512
IN_TN = 2048
RET_CHUNK = 256
ATT_BLK = 512
ATT_UNROLL = 4
MIX_TM = 256
MEM_TM = 512
MOE_BM = 256
MOE_TN1 = 1024
MOE_TN2 = 2048
GATHER_ROWS = 256
COMBINE_TM = 128


def _cparams(sem):
    return pltpu.CompilerParams(dimension_semantics=sem, vmem_limit_bytes=VMEM_LIMIT)


def _layer_norm(x, g, b):
    mu = jnp.mean(x, axis=-1, keepdims=True)
    xc = x - mu
    var = jnp.mean(xc * xc, axis=-1, keepdims=True)
    return xc * lax.rsqrt(var + LN_EPS) * g + b


def _dot(a, b):
    return jnp.dot(a, b, preferred_element_type=F32)


def _dot_nt(a, b):
    return lax.dot_general(a, b, (((1,), (1,)), ((), ())), preferred_element_type=F32)


def _dot_tn(a, b):
    return lax.dot_general(a, b, (((0,), (0,)), ((), ())), preferred_element_type=F32)


def _lo(i, n):
    return jnp.minimum(i, n - 1)


def _hi(i, n):
    return jnp.maximum(i - n, 0)


def _in_proj_kernel(xp_ref, xs_ref, g_ref, b_ref, w_ref, p_ref, dkp_ref, dvp_ref, dks_ref, dvs_ref,
                    h_scr, *, n_p):
    i = pl.program_id(0)
    j = pl.program_id(1)
    half = dkp_ref.shape[1]

    @pl.when(jnp.logical_and(j == 0, i < n_p))
    def _():
        h_scr[...] = _layer_norm(xp_ref[...], g_ref[...], b_ref[...]).astype(BF16)

    @pl.when(jnp.logical_and(j == 0, i >= n_p))
    def _():
        h_scr[...] = _layer_norm(xs_ref[...], g_ref[...], b_ref[...]).astype(BF16)

    acc = _dot(h_scr[...], w_ref[...])
    p_ref[...] = acc.astype(BF16)

    @pl.when(jnp.logical_and(j == 0, i < n_p))
    def _():
        dkp_ref[...] = acc[:, :half]
        dvp_ref[...] = acc[:, half:]

    @pl.when(jnp.logical_and(j == 0, i >= n_p))
    def _():
        dks_ref[...] = acc[:, :half]
        dvs_ref[...] = acc[:, half:]


def _in_proj(xp, xs, g, b, w, kv_blk):
    Tp, D = xp.shape
    Ts = xs.shape[0]
    N = w.shape[1]
    n_p, n_s = Tp // IN_TM, Ts // IN_TM
    half = IN_TN // 2

    def col(j):
        return jnp.where(j == 0, kv_blk, jnp.where(j <= kv_blk, j - 1, j))

    single = pl.Buffered(1)
    kv_p = pl.BlockSpec((IN_TM, half), lambda i, j: (_lo(i, n_p), 0), pipeline_mode=single)
    kv_s = pl.BlockSpec((IN_TM, half), lambda i, j: (_hi(i, n_p), 0), pipeline_mode=single)
    return pl.pallas_call(
        functools.partial(_in_proj_kernel, n_p=n_p),
        grid=(n_p + n_s, N // IN_TN),
        in_specs=[
            pl.BlockSpec((IN_TM, D), lambda i, j: (_lo(i, n_p), 0), pipeline_mode=single),
            pl.BlockSpec((IN_TM, D), lambda i, j: (_hi(i, n_p), 0), pipeline_mode=single),
            pl.BlockSpec((1, D), lambda i, j: (0, 0)),
            pl.BlockSpec((1, D), lambda i, j: (0, 0)),
            pl.BlockSpec((D, IN_TN), lambda i, j: (0, col(j))),
        ],
        out_specs=[pl.BlockSpec((IN_TM, IN_TN), lambda i, j: (i, j)), kv_p, kv_p, kv_s, kv_s],
        out_shape=[
            jax.ShapeDtypeStruct((Tp + Ts, N), BF16),
            jax.ShapeDtypeStruct((Tp, half), F32),
            jax.ShapeDtypeStruct((Tp, half), F32),
            jax.ShapeDtypeStruct((Ts, half), F32),
            jax.ShapeDtypeStruct((Ts, half), F32),
        ],
        scratch_shapes=[pltpu.VMEM((IN_TM, D), BF16)],
        compiler_params=_cparams(("arbitrary", "arbitrary")),
        name="in_proj",
    )(xp, xs, g, b, w)


P_DK = 0
P_DV = 8
P_RQ = 16
P_RK = 20
P_RV = 24
P_RG = 32
P_DQ = 40
P_GATE = 48


def _retention_kernel(q_ref, k_ref, v_ref, g_ref, cos_ref, sin_ref, dm_ref, qd_ref, kd_ref,
                      bd_ref, s0_ref, y_ref, sout_ref, s_scr, *, n_chunks, dk, dv):
    c = pl.program_id(1)

    @pl.when(c == 0)
    def _():
        s_scr[...] = s0_ref[0]

    cos = cos_ref[...]
    sin = sin_ref[...]
    k_scale = dk ** -0.5
    for h in range(R_HEADS):
        q = q_ref[:, h * dk:(h + 1) * dk].astype(F32)
        k = k_ref[:, h * dk:(h + 1) * dk].astype(F32)
        v = v_ref[:, h * dv:(h + 1) * dv]
        g = g_ref[:, h * dv:(h + 1) * dv].astype(F32)
        q = q * cos + pltpu.roll(q, dk // 2, 1) * sin
        k = (k * cos + pltpu.roll(k, dk // 2, 1) * sin) * k_scale
        qb = q.astype(BF16)
        kb = k.astype(BF16)
        s = s_scr[h]
        scores = _dot_nt(qb, kb) * dm_ref[h]
        o = _dot(scores.astype(BF16), v) + _dot((q * qd_ref[h]).astype(BF16), s.astype(BF16))
        s_scr[h] = bd_ref[h] * s + _dot_tn((k * kd_ref[h]).astype(BF16), v)
        o = o * lax.rsqrt(jnp.mean(o * o, axis=-1, keepdims=True) + RMS_EPS)
        y_ref[:, h * dv:(h + 1) * dv] = (g * jax.nn.sigmoid(g) * o).astype(BF16)

    @pl.when(c == n_chunks - 1)
    def _():
        sout_ref[0] = s_scr[...]


def _retention(P, row_off, B, L, pos0, S0):
    dk, dv = 128, 256
    C = min(L, RET_CHUNK)
    nc = L // C
    ob = row_off // C
    pos = (pos0 + jnp.arange(L, dtype=jnp.int32)).astype(F32)
    inv = 1.0 / (ROPE_BASE ** jnp.linspace(0.0, 1.0, dk // 2, dtype=F32))
    ang = pos[:, None] * inv[None, :]
    cos = jnp.concatenate([jnp.cos(ang), jnp.cos(ang)], axis=-1)
    sin = jnp.concatenate([-jnp.sin(ang), jnp.sin(ang)], axis=-1)
    log_g = jnp.log1p(-jnp.power(2.0, -5.0 - jnp.arange(R_HEADS, dtype=F32)))
    i = jnp.arange(C, dtype=F32)
    rel = i[:, None] - i[None, :]
    dmask = jnp.where(rel >= 0, jnp.exp(log_g[:, None, None] * jnp.maximum(rel, 0.0)), 0.0)
    q_decay = jnp.exp(log_g[:, None] * (i + 1.0))[..., None]
    k_decay = jnp.exp(log_g[:, None] * (C - 1.0 - i))[..., None]
    b_decay = jnp.exp(log_g * C)[:, None, None]

    def rows(b, c):
        return ob + b * nc + c

    kern = functools.partial(_retention_kernel, n_chunks=nc, dk=dk, dv=dv)
    return pl.pallas_call(
        kern,
        grid=(B, nc),
        in_specs=[
            pl.BlockSpec((C, R_HEADS * dk), lambda b, c: (rows(b, c), P_RQ * LANES // (R_HEADS * dk))),
            pl.BlockSpec((C, R_HEADS * dk), lambda b, c: (rows(b, c), P_RK * LANES // (R_HEADS * dk))),
            pl.BlockSpec((C, R_HEADS * dv), lambda b, c: (rows(b, c), P_RV * LANES // (R_HEADS * dv))),
            pl.BlockSpec((C, R_HEADS * dv), lambda b, c: (rows(b, c), P_RG * LANES // (R_HEADS * dv))),
            pl.BlockSpec((C, dk), lambda b, c: (c, 0)),
            pl.BlockSpec((C, dk), lambda b, c: (c, 0)),
            pl.BlockSpec((R_HEADS, C, C), lambda b, c: (0, 0, 0)),
            pl.BlockSpec((R_HEADS, C, 1), lambda b, c: (0, 0, 0)),
            pl.BlockSpec((R_HEADS, C, 1), lambda b, c: (0, 0, 0)),
            pl.BlockSpec((R_HEADS, 1, 1), lambda b, c: (0, 0, 0)),
            pl.BlockSpec((1, R_HEADS, dk, dv), lambda b, c: (b, 0, 0, 0)),
        ],
        out_specs=[
            pl.BlockSpec((C, R_HEADS * dv), lambda b, c: (b * nc + c, 0)),
            pl.BlockSpec((1, R_HEADS, dk, dv), lambda b, c: (b, 0, 0, 0)),
        ],
        out_shape=[
            jax.ShapeDtypeStruct((B * L, R_HEADS * dv), BF16),
            jax.ShapeDtypeStruct((B, R_HEADS, dk, dv), F32),
        ],
        scratch_shapes=[pltpu.VMEM((R_HEADS, dk, dv), F32)],
        compiler_params=_cparams(("parallel", "arbitrary")),
        name="retention",
    )(P, P, P, P, cos, sin, dmask, q_decay, k_decay, b_decay, S0)


def _diff_lambda(lam_ref, lambda_init):
    lv = lam_ref[...]
    a = jnp.sum(lv[0:1] * lv[1:2], axis=-1, keepdims=True)
    b = jnp.sum(lv[2:3] * lv[3:4], axis=-1, keepdims=True)
    return jnp.exp(a) - jnp.exp(b) + lambda_init


def _lane_tile(x, n):
    return jnp.concatenate([x] * n, axis=1)


def _stack_maps(q, hd):
    lane = lax.broadcasted_iota(jnp.int32, q.shape, 1)
    zero = jnp.zeros_like(q)
    return jnp.concatenate([jnp.where(lane < hd, q, zero), jnp.where(lane < hd, zero, q)], axis=0)


def _diff_finish(acc, l, lam, subln, lambda_init, tq):
    o = acc[:tq] / l[:tq] - lam * (acc[tq:] / l[tq:])
    o = o * lax.rsqrt(jnp.mean(o * o, axis=-1, keepdims=True) + RMS_EPS)
    return (o * subln * (1.0 - lambda_init)).astype(BF16)


def _attn_prompt_kernel(q_ref, k_ref, v_ref, lam_ref, sub_ref, y_ref, qs_scr, ve_scr, m_scr, acc_scr,
                        *, blk, hd, lambda_init):
    qi = pl.program_id(2)
    dv = v_ref.shape[1]
    seq = v_ref.shape[0]

    @pl.when(qi == 0)
    def _():
        ve_scr[:, :dv] = v_ref[...]
        ve_scr[:, dv:] = jnp.ones((seq, dv), BF16)

    qs_scr[...] = _stack_maps(q_ref[...] * (hd ** -0.5), hd)
    m_scr[...] = jnp.full_like(m_scr, -jnp.inf)
    acc_scr[...] = jnp.zeros_like(acc_scr)

    def step(ki, masked):
        off = pl.multiple_of(ki * blk, blk)
        s = _dot_nt(qs_scr[...], k_ref[pl.ds(off, blk), :])
        if masked:
            row = lax.broadcasted_iota(jnp.int32, s.shape, 0)
            col = lax.broadcasted_iota(jnp.int32, s.shape, 1)
            qrow = jnp.where(row >= blk, row - blk, row)
            shift = CHUNK.bit_length() - 1
            s = jnp.where((col >> shift) <= (qrow >> shift), s, -1e30)
        m_prev = m_scr[...]
        m_new = jnp.maximum(m_prev, jnp.max(s, axis=-1, keepdims=True))
        alpha = jnp.exp(m_prev - m_new)
        p = jnp.exp(s - _lane_tile(m_new, blk // LANES))
        pv = _dot(p.astype(BF16), ve_scr[pl.ds(off, blk), :])
        acc_scr[...] = _lane_tile(alpha, 2 * dv // LANES) * acc_scr[...] + pv
        m_scr[...] = m_new

    def body(j, carry):
        for u in range(ATT_UNROLL):
            step(ATT_UNROLL * j + u, False)
        return carry

    n_main = qi // ATT_UNROLL
    lax.fori_loop(0, n_main, body, 0)
    done = n_main * ATT_UNROLL
    width = ATT_UNROLL // 2
    while width >= 1:
        @pl.when(((qi - done) // width) % 2 == 1)
        def _(width=width):
            base = qi - (qi - done) % (2 * width)
            for u in range(width):
                step(base + u, False)
        width //= 2
    step(qi, True)

    lam = _diff_lambda(lam_ref, lambda_init)
    acc = acc_scr[...]
    o = acc[:blk, :dv] / acc[:blk, dv:] - lam * (acc[blk:, :dv] / acc[blk:, dv:])
    o = o * lax.rsqrt(jnp.mean(o * o, axis=-1, keepdims=True) + RMS_EPS)
    y_ref[...] = (o * sub_ref[...] * (1.0 - lambda_init)).astype(BF16)


def _attn_prompt(P, B, S, lam_p, subln, lambda_init):
    hd = 64
    blk = min(ATT_BLK, S)
    nq = S // blk
    kern = functools.partial(_attn_prompt_kernel, blk=blk, hd=hd, lambda_init=lambda_init)
    return pl.pallas_call(
        kern,
        grid=(B, D_HEADS, nq),
        in_specs=[
            pl.BlockSpec((blk, 2 * hd), lambda b, h, i: (b * nq + i, P_DQ + h)),
            pl.BlockSpec((S, 2 * hd), lambda b, h, i: (b, P_DK + h)),
            pl.BlockSpec((S, 2 * hd), lambda b, h, i: (b, P_DV + h)),
            pl.BlockSpec((4, hd), lambda b, h, i: (0, 0)),
            pl.BlockSpec((1, 2 * hd), lambda b, h, i: (0, 0)),
        ],
        out_specs=pl.BlockSpec((blk, 2 * hd), lambda b, h, i: (b * nq + i, h)),
        out_shape=jax.ShapeDtypeStruct((B * S, D_HEADS * 2 * hd), BF16),
        scratch_shapes=[
            pltpu.VMEM((2 * blk, 2 * hd), BF16),
            pltpu.VMEM((S, 4 * hd), BF16),
            pltpu.VMEM((2 * blk, LANES), F32),
            pltpu.VMEM((2 * blk, 4 * hd), F32),
        ],
        compiler_params=_cparams(("parallel", "parallel", "arbitrary")),
        name="attn_prompt",
    )(P, P, P, lam_p, subln)


def _attn_sample_kernel(q_ref, kn_ref, vn_ref, kc_ref, vc_ref, lam_ref, sub_ref, y_ref,
                        *, tq, hd, lambda_init):
    qs = _stack_maps(q_ref[...] * (hd ** -0.5), hd)
    kc = kc_ref[...].astype(BF16)
    vc = vc_ref[...].astype(BF16)
    s_c = _dot_nt(qs, kc)
    s_n = _dot_nt(qs, kn_ref[...])
    m = jnp.maximum(jnp.max(s_c, axis=-1, keepdims=True), jnp.max(s_n, axis=-1, keepdims=True))
    p_c = jnp.exp(s_c - m)
    p_n = jnp.exp(s_n - m)
    l = jnp.sum(p_c, axis=-1, keepdims=True) + jnp.sum(p_n, axis=-1, keepdims=True)
    acc = _dot(p_c.astype(BF16), vc) + _dot(p_n.astype(BF16), vn_ref[...])
    lam = _diff_lambda(lam_ref, lambda_init)
    y_ref[...] = _diff_finish(acc, l, lam, sub_ref[...], lambda_init, tq)


def _attn_sample(P, row_off, B, L, cache_k, cache_v, lam_p, subln, lambda_init):
    hd = 64
    past = cache_k.shape[0] // B
    ob = row_off // L
    kern = functools.partial(_attn_sample_kernel, tq=L, hd=hd, lambda_init=lambda_init)
    return pl.pallas_call(
        kern,
        grid=(B, D_HEADS),
        in_specs=[
            pl.BlockSpec((L, 2 * hd), lambda b, h: (ob + b, P_DQ + h)),
            pl.BlockSpec((L, 2 * hd), lambda b, h: (ob + b, P_DK + h)),
            pl.BlockSpec((L, 2 * hd), lambda b, h: (ob + b, P_DV + h)),
            pl.BlockSpec((past, 2 * hd), lambda b, h: (b, h)),
            pl.BlockSpec((past, 2 * hd), lambda b, h: (b, h)),
            pl.BlockSpec((4, hd), lambda b, h: (0, 0)),
            pl.BlockSpec((1, 2 * hd), lambda b, h: (0, 0)),
        ],
        out_specs=pl.BlockSpec((L, 2 * hd), lambda b, h: (b, h)),
        out_shape=jax.ShapeDtypeStruct((B * L, D_HEADS * 2 * hd), BF16),
        compiler_params=_cparams(("parallel", "parallel")),
        name="attn_sample",
    )(P, P, P, cache_k, cache_v, lam_p, subln)


def _mix_kernel(xp_ref, xs_ref, yrp_ref, yrs_ref, ydp_ref, yds_ref, gr_ref, gd_ref, lig_ref, lib_ref,
                bg_ref, wpr_ref, wpd_ref, wo_ref, g1_ref, b1_ref, h1_ref, *, alpha, n_p):
    i = pl.program_id(0)
    d = xp_ref.shape[1]

    def compute(x_ref, yr_ref, yd_ref):
        h0 = _layer_norm(x_ref[...], lig_ref[...], lib_ref[...])
        g_ret = jax.nn.sigmoid(gr_ref[...].astype(F32) + bg_ref[:, :d])
        g_diff = jax.nn.sigmoid(gd_ref[...].astype(F32) + bg_ref[:, d:])
        merged = g_ret * _dot(yr_ref[...], wpr_ref[...]) + g_diff * _dot(yd_ref[...], wpd_ref[...])
        mixed = _dot(merged.astype(BF16), wo_ref[...])
        h1_ref[...] = _layer_norm(alpha * h0 + mixed, g1_ref[...], b1_ref[...])

    @pl.when(i < n_p)
    def _():
        compute(xp_ref, yrp_ref, ydp_ref)

    @pl.when(i >= n_p)
    def _():
        compute(xs_ref, yrs_ref, yds_ref)


def _mix(xp, xs, yr_p, yr_s, yd_p, yd_s, P, ln_in_g, ln_in_b, b_gate, wpr, wpd, wo, g1, b1, alpha):
    Tp, D = xp.shape
    Ts = xs.shape[0]
    tm = MIX_TM
    n_p, n_s = Tp // tm, Ts // tm
    gcol = P_GATE * LANES // D
    const = lambda i: (0, 0)
    lo = lambda i: (_lo(i, n_p), 0)
    hi = lambda i: (_hi(i, n_p), 0)
    single = pl.Buffered(1)
    kern = functools.partial(_mix_kernel, alpha=alpha, n_p=n_p)
    return pl.pallas_call(
        kern,
        grid=(n_p + n_s,),
        in_specs=[
            pl.BlockSpec((tm, D), lo),
            pl.BlockSpec((tm, D), hi),
            pl.BlockSpec((tm, yr_p.shape[1]), lo),
            pl.BlockSpec((tm, yr_s.shape[1]), hi),
            pl.BlockSpec((tm, yd_p.shape[1]), lo),
            pl.BlockSpec((tm, yd_s.shape[1]), hi),
            pl.BlockSpec((tm, D), lambda i: (i, gcol)),
            pl.BlockSpec((tm, D), lambda i: (i, gcol + 1)),
            pl.BlockSpec((1, D), const),
            pl.BlockSpec((1, D), const),
            pl.BlockSpec((1, 2 * D), const),
            pl.BlockSpec(wpr.shape, const, pipeline_mode=single),
            pl.BlockSpec(wpd.shape, const, pipeline_mode=single),
            pl.BlockSpec(wo.shape, const, pipeline_mode=single),
            pl.BlockSpec((1, D), const),
            pl.BlockSpec((1, D), const),
        ],
        out_specs=pl.BlockSpec((tm, D), lambda i: (i, 0)),
        out_shape=jax.ShapeDtypeStruct((Tp + Ts, D), F32),
        compiler_params=_cparams(("arbitrary",)),
        name="mix",
    )(xp, xs, yr_p, yr_s, yd_p, yd_s, P, P, ln_in_g, ln_in_b, b_gate, wpr, wpd, wo, g1, b1)


def _mem_kernel(h1_ref, mkp_ref, mvp_ref, mks_ref, mvs_ref, wq_ref, wo_ref, g2_ref, b2_ref, rw_ref,
                rb_ref, h2_ref, route_ref, counts_ref, q_scr, o_scr, cnt_scr, *, n_p, alpha, hd):
    i = pl.program_id(0)
    tm = h1_ref.shape[0]
    h1 = h1_ref[...]
    q_scr[...] = (_dot(h1.astype(BF16), wq_ref[...]) * (hd ** -0.5)).astype(BF16)

    def attend(mk_ref, mv_ref):
        n_sub = mk_ref.shape[0]
        seg = tm // n_sub
        for s in range(n_sub):
            mk = mk_ref[s].astype(BF16)
            mv = mv_ref[s].astype(BF16)
            for h in range(M_HEADS):
                qh = q_scr[s * seg:(s + 1) * seg, h * hd:(h + 1) * hd]
                sc = _dot_nt(qh, mk[:, h * hd:(h + 1) * hd])
                sc = sc - jnp.max(sc, axis=-1, keepdims=True)
                p = jnp.exp(sc)
                p = p / jnp.sum(p, axis=-1, keepdims=True)
                o_scr[s * seg:(s + 1) * seg, h * hd:(h + 1) * hd] = _dot(
                    p.astype(BF16), mv[:, h * hd:(h + 1) * hd]).astype(BF16)

    @pl.when(i < n_p)
    def _():
        attend(mkp_ref, mvp_ref)

    @pl.when(i >= n_p)
    def _():
        attend(mks_ref, mvs_ref)

    h2 = _layer_norm(alpha * h1 + _dot(o_scr[...], wo_ref[...]), g2_ref[...], b2_ref[...])
    h2_ref[...] = h2
    h2b = h2.astype(BF16)

    h_lo = (h2 - h2b.astype(F32)).astype(BF16)
    rw = rw_ref[...]
    rw_hi = rw.astype(BF16)
    rw_lo = (rw - rw_hi.astype(F32)).astype(BF16)
    logits = _dot(h2b, rw_hi) + _dot(h2b, rw_lo) + _dot(h_lo, rw_hi) + rb_ref[...]

    n_e = logits.shape[1]
    eidx = lax.broadcasted_iota(jnp.int32, logits.shape, 1).astype(F32)
    lane = lax.broadcasted_iota(jnp.int32, (tm, LANES), 1)
    route = jnp.zeros((tm, LANES), F32)
    work = logits
    vals = []
    sels = []
    for k in range(TOP_K):
        mx = jnp.max(work, axis=-1, keepdims=True)
        sel = jnp.min(jnp.where(work == mx, eidx, float(n_e)), axis=-1, keepdims=True)
        work = jnp.where(eidx == sel, -jnp.inf, work)
        vals.append(mx)
        sels.append(sel)
        route = jnp.where(lane == TOP_K + k, sel, route)
    ex = [jnp.exp(v - vals[0]) for v in vals]
    den = ex[0] + ex[1] + ex[2] + ex[3]
    for k in range(TOP_K):
        route = jnp.where(lane == k, ex[k] / den, route)

    @pl.when(i == 0)
    def _():
        cnt_scr[...] = jnp.zeros_like(cnt_scr)

    lane_f = lane.astype(F32)
    hit = [lane_f == sels[k] for k in range(TOP_K)]
    cnt = sum(h.astype(F32) for h in hit)
    r_io = lax.broadcasted_iota(jnp.int32, (tm, tm), 0)
    c_io = lax.broadcasted_iota(jnp.int32, (tm, tm), 1)
    ltri = jnp.where(r_io > c_io, 1.0, 0.0).astype(BF16)
    excl = _dot(ltri, cnt.astype(BF16)) + cnt_scr[...]
    for k in range(TOP_K):
        rank = jnp.sum(jnp.where(hit[k], excl, 0.0), axis=-1, keepdims=True)
        route = jnp.where(lane == 2 * TOP_K + k, rank, route)
    route_ref[...] = route
    cnt_scr[...] = cnt_scr[...] + jnp.sum(cnt, axis=0, keepdims=True)
    counts_ref[...] = cnt_scr[...]


def _mem_attn(h1, t_prompt, mem_kp, mem_vp, l_prompt, mem_ks, mem_vs, l_sample, wq, wo, g2, b2, rw, rb,
              alpha):
    T, D = h1.shape
    hd = 128
    tm = MEM_TM
    assert l_prompt % tm == 0 and tm % l_sample == 0
    n_p = t_prompt // tm
    n_s = (T - t_prompt) // tm
    per_b = l_prompt // tm
    sub_s = tm // l_sample
    n_mem = mem_kp.shape[1]
    bp = mem_kp.shape[0]
    const = lambda i: (0, 0)
    single = pl.Buffered(1)
    mem_p = pl.BlockSpec((1, n_mem, M_HEADS * hd), lambda i: (jnp.minimum(i // per_b, bp - 1), 0, 0))
    mem_s = pl.BlockSpec((sub_s, n_mem, M_HEADS * hd), lambda i: (_hi(i, n_p), 0, 0))
    kern = functools.partial(_mem_kernel, n_p=n_p, alpha=alpha, hd=hd)
    return pl.pallas_call(
        kern,
        grid=(n_p + n_s,),
        in_specs=[
            pl.BlockSpec((tm, D), lambda i: (i, 0)),
            mem_p, mem_p, mem_s, mem_s,
            pl.BlockSpec(wq.shape, const, pipeline_mode=single),
            pl.BlockSpec(wo.shape, const, pipeline_mode=single),
            pl.BlockSpec((1, D), const),
            pl.BlockSpec((1, D), const),
            pl.BlockSpec(rw.shape, const, pipeline_mode=single),
            pl.BlockSpec((1, rw.shape[1]), const),
        ],
        out_specs=[
            pl.BlockSpec((tm, D), lambda i: (i, 0)),
            pl.BlockSpec((tm, LANES), lambda i: (i, 0)),
            pl.BlockSpec((1, LANES), const),
        ],
        out_shape=[
            jax.ShapeDtypeStruct((T, D), F32),
            jax.ShapeDtypeStruct((T, LANES), F32),
            jax.ShapeDtypeStruct((1, LANES), F32),
        ],
        scratch_shapes=[
            pltpu.VMEM((tm, M_HEADS * hd), BF16),
            pltpu.VMEM((tm, M_HEADS * hd), BF16),
            pltpu.VMEM((1, LANES), F32),
        ],
        compiler_params=_cparams(("arbitrary",)),
        name="mem_attn",
    )(h1, mem_kp, mem_vp, mem_ks, mem_vs, wq, wo, g2, b2, rw, rb)


def _mem_kv_kernel(x_ref, w_ref, o_ref):
    o_ref[...] = _dot(x_ref[...].astype(BF16), w_ref[...])


def _mem_kv(mem, w):
    R, D = mem.shape
    N = w.shape[1]
    tm = 256
    return pl.pallas_call(
        _mem_kv_kernel,
        grid=(R // tm,),
        in_specs=[pl.BlockSpec((tm, D), lambda i: (i, 0)), pl.BlockSpec((D, N), lambda i: (0, 0))],
        out_specs=pl.BlockSpec((tm, N), lambda i: (i, 0)),
        out_shape=jax.ShapeDtypeStruct((R, N), F32),
        compiler_params=_cparams(("parallel",)),
        name="mem_kv",
    )(mem, w)


def _row_copy(src_hbm, dst, sem, src_row, dst_row):
    return pltpu.make_async_copy(src_hbm.at[pl.ds(src_row, 1)], dst.at[pl.ds(dst_row, 1)], sem)


def _dispatch_kernel(dest_ref, pend_ref, padded_ref, nu_ref, h_ref, xs_hbm, stage, zeros, sem, zsem,
                     *, tm, bm, n_steps, n_blocks):
    i = pl.program_id(0)
    slot = i % 2

    def zero_block(off):
        return pltpu.make_async_copy(zeros, xs_hbm.at[pl.ds(pl.multiple_of(off, bm), bm)], zsem)

    def zero_fill(act):
        for e in range(N_EXPERTS):
            @pl.when(padded_ref[e] > 0)
            def _(e=e):
                act(zero_block(pend_ref[e] - bm))
        for b in range(n_blocks - N_EXPERTS, n_blocks):
            @pl.when(b >= nu_ref[0])
            def _(b=b):
                act(zero_block(b * bm))

    @pl.when(i == 0)
    def _():
        zeros[...] = jnp.zeros_like(zeros)
        zero_fill(lambda c: c.start())
        zero_fill(lambda c: c.wait())

    def wait(s):
        for _ in range(TOP_K):
            pltpu.make_async_copy(stage.at[s], xs_hbm.at[pl.ds(0, tm)], sem.at[s]).wait()

    @pl.when(i >= 2)
    def _():
        wait(slot)

    stage[slot] = h_ref[...]

    def body(t, carry):
        for k in range(TOP_K):
            row = dest_ref[(i * tm + t) * TOP_K + k]
            pltpu.make_async_copy(stage.at[slot, pl.ds(t, 1)], xs_hbm.at[pl.ds(row, 1)],
                                  sem.at[slot]).start()
        return carry
    lax.fori_loop(0, tm, body, 0, unroll=4)

    @pl.when(i == n_steps - 1)
    def _():
        wait(slot)
        if n_steps > 1:
            wait(1 - slot)


def _moe_dispatch(h2, dest, pad_end, padded, n_used, n_rows, bm):
    T, D = h2.shape
    tm = GATHER_ROWS
    n_steps = T // tm
    kern = functools.partial(_dispatch_kernel, tm=tm, bm=bm, n_steps=n_steps, n_blocks=n_rows // bm)
    return pl.pallas_call(
        kern,
        grid_spec=pltpu.PrefetchScalarGridSpec(
            num_scalar_prefetch=4,
            grid=(n_steps,),
            in_specs=[pl.BlockSpec((tm, D), lambda i, d, pe, pd, nu: (i, 0))],
            out_specs=pl.BlockSpec(memory_space=pl.ANY),
            scratch_shapes=[
                pltpu.VMEM((2, tm, D), F32),
                pltpu.VMEM((bm, D), F32),
                pltpu.SemaphoreType.DMA((2,)),
                pltpu.SemaphoreType.DMA(()),
            ],
        ),
        out_shape=jax.ShapeDtypeStruct((n_rows, D), F32),
        compiler_params=_cparams(("arbitrary",)),
        name="moe_dispatch",
    )(dest, pad_end, padded, n_used, h2)


def _expert_changed(be_ref, m):
    prev = be_ref[jnp.maximum(m - 1, 0)]
    return jnp.logical_or(m == 0, be_ref[m] != prev)


def _moe_up_kernel(be_ref, nu_ref, x_ref, wg_ref, wl_ref, bg_ref, bl_ref, a_ref, wg_scr, wl_scr):
    m = pl.program_id(1)

    @pl.when(_expert_changed(be_ref, m))
    def _():
        wg_scr[...] = wg_ref[0].astype(BF16)
        wl_scr[...] = wl_ref[0].astype(BF16)

    @pl.when(m < nu_ref[0])
    def _():
        x = x_ref[...].astype(BF16)
        u_glu = jnp.minimum(_dot(x, wg_scr[...]) + bg_ref[0], SWIGLU_LIMIT)
        u_lin = jnp.clip(_dot(x, wl_scr[...]) + bl_ref[0], -SWIGLU_LIMIT, SWIGLU_LIMIT)
        a = u_glu * jax.nn.sigmoid(SWIGLU_ALPHA * u_glu) * (u_lin + 1.0)
        a_ref[...] = a.astype(BF16)

    @pl.when(m >= nu_ref[0])
    def _():
        a_ref[...] = jnp.zeros_like(a_ref)


def _moe_up(xs, w1, b1, block_e, n_used):
    n_rows, D = xs.shape
    F = w1.shape[2] // 2
    bm, tn = MOE_BM, MOE_TN1
    nb = n_rows // bm
    nf = F // tn
    return pl.pallas_call(
        _moe_up_kernel,
        grid_spec=pltpu.PrefetchScalarGridSpec(
            num_scalar_prefetch=2,
            grid=(nf, nb),
            in_specs=[
                pl.BlockSpec((bm, D), lambda f, m, be, nu: (jnp.minimum(m, nu[0] - 1), 0)),
                pl.BlockSpec((1, D, tn), lambda f, m, be, nu: (be[m], 0, f)),
                pl.BlockSpec((1, D, tn), lambda f, m, be, nu: (be[m], 0, nf + f)),
                pl.BlockSpec((1, 1, tn), lambda f, m, be, nu: (be[m], 0, f)),
                pl.BlockSpec((1, 1, tn), lambda f, m, be, nu: (be[m], 0, nf + f)),
            ],
            out_specs=pl.BlockSpec((bm, tn), lambda f, m, be, nu: (m, f)),
            scratch_shapes=[pltpu.VMEM((D, tn), BF16), pltpu.VMEM((D, tn), BF16)],
        ),
        out_shape=jax.ShapeDtypeStruct((n_rows, F), BF16),
        compiler_params=_cparams(("arbitrary", "arbitrary")),
        name="moe_up",
    )(block_e, n_used, xs, w1, w1, b1, b1)


def _moe_down_kernel(be_ref, nu_ref, a_ref, w_ref, b_ref, y_ref, w_scr):
    m = pl.program_id(1)

    @pl.when(_expert_changed(be_ref, m))
    def _():
        w_scr[...] = w_ref[0].astype(BF16)

    @pl.when(m < nu_ref[0])
    def _():
        y_ref[...] = _dot(a_ref[...], w_scr[...]) + b_ref[0]

    @pl.when(m >= nu_ref[0])
    def _():
        y_ref[...] = jnp.zeros_like(y_ref)


def _moe_down(a, w2, b2, block_e, n_used):
    n_rows, F = a.shape
    D = w2.shape[2]
    bm, tn = MOE_BM, MOE_TN2
    nb = n_rows // bm
    nd = D // tn
    return pl.pallas_call(
        _moe_down_kernel,
        grid_spec=pltpu.PrefetchScalarGridSpec(
            num_scalar_prefetch=2,
            grid=(nd, nb),
            in_specs=[
                pl.BlockSpec((bm, F), lambda d, m, be, nu: (jnp.minimum(m, nu[0] - 1), 0)),
                pl.BlockSpec((1, F, tn), lambda d, m, be, nu: (be[m], 0, d)),
                pl.BlockSpec((1, 1, tn), lambda d, m, be, nu: (be[m], 0, d)),
            ],
            out_specs=pl.BlockSpec((bm, tn), lambda d, m, be, nu: (m, d)),
            scratch_shapes=[pltpu.VMEM((F, tn), BF16)],
        ),
        out_shape=jax.ShapeDtypeStruct((n_rows, D), F32),
        compiler_params=_cparams(("arbitrary", "arbitrary")),
        name="moe_down",
    )(block_e, n_used, a, w2, b2)


def _combine_kernel(dest_ref, h2_ref, rt_ref, g_ref, b_ref, ys_hbm, op_ref, os_ref, buf, sem,
                    *, tm, n_steps, n_prompt, alpha):
    i = pl.program_id(0)

    def issue(step, slot):
        def body(t, carry):
            for k in range(TOP_K):
                row = dest_ref[(step * tm + t) * TOP_K + k]
                _row_copy(ys_hbm, buf.at[slot, k], sem.at[slot], row, t).start()
            return carry
        lax.fori_loop(0, tm, body, 0, unroll=2)

    def wait(slot):
        for k in range(TOP_K):
            pltpu.make_async_copy(ys_hbm.at[pl.ds(0, tm)], buf.at[slot, k], sem.at[slot]).wait()

    slot = i % 2

    @pl.when(i == 0)
    def _():
        issue(0, 0)

    @pl.when(i + 1 < n_steps)
    def _():
        issue(i + 1, 1 - slot)

    wait(slot)
    y = rt_ref[:, 0:1] * buf[slot, 0]
    for k in range(1, TOP_K):
        y = y + rt_ref[:, k:k + 1] * buf[slot, k]
    out = _layer_norm(alpha * h2_ref[...] + y, g_ref[...], b_ref[...])

    @pl.when(i < n_prompt)
    def _():
        op_ref[...] = out

    @pl.when(i >= n_prompt)
    def _():
        os_ref[...] = out


def _moe_combine(ys, dest, h2, route, g3, b3, t_prompt, alpha):
    T, D = h2.shape
    tm = COMBINE_TM
    n_steps = T // tm
    n_prompt = t_prompt // tm
    kern = functools.partial(_combine_kernel, tm=tm, n_steps=n_steps, n_prompt=n_prompt, alpha=alpha)
    return pl.pallas_call(
        kern,
        grid_spec=pltpu.PrefetchScalarGridSpec(
            num_scalar_prefetch=1,
            grid=(n_steps,),
            in_specs=[
                pl.BlockSpec((tm, D), lambda i, d: (i, 0)),
                pl.BlockSpec((tm, LANES), lambda i, d: (i, 0)),
                pl.BlockSpec((1, D), lambda i, d: (0, 0)),
                pl.BlockSpec((1, D), lambda i, d: (0, 0)),
                pl.BlockSpec(memory_space=pl.ANY),
            ],
            out_specs=[
                pl.BlockSpec((tm, D), lambda i, d: (jnp.minimum(i, n_prompt - 1), 0)),
                pl.BlockSpec((tm, D), lambda i, d: (jnp.maximum(i - n_prompt, 0), 0)),
            ],
            scratch_shapes=[pltpu.VMEM((2, TOP_K, tm, D), F32), pltpu.SemaphoreType.DMA((2,))],
        ),
        out_shape=[
            jax.ShapeDtypeStruct((t_prompt, D), F32),
            jax.ShapeDtypeStruct((T - t_prompt, D), F32),
        ],
        compiler_params=_cparams(("arbitrary",)),
        name="moe_combine",
    )(dest, h2, route, g3, b3, ys)


def _moe_plan(route, counts, bm):
    T = route.shape[0]
    top_idx = route[:, TOP_K:2 * TOP_K].astype(jnp.int32)
    rank = route[:, 2 * TOP_K:3 * TOP_K].astype(jnp.int32)
    counts = counts[0, :N_EXPERTS].astype(jnp.int32)
    padded = (counts + bm - 1) // bm * bm
    pad_end = jnp.cumsum(padded)
    pad_start = pad_end - padded
    dest = (pad_start[top_idx] + rank).reshape(-1)
    n_blocks = -(-(T * TOP_K) // bm) + N_EXPERTS
    block_start = jnp.arange(n_blocks, dtype=jnp.int32) * bm
    block_e = jnp.sum((pad_end[None, :] <= block_start[:, None]).astype(jnp.int32), axis=1)
    block_e = jnp.minimum(block_e, N_EXPERTS - 1)
    n_used = (pad_end[-1] // bm).astype(jnp.int32).reshape(1)
    return dest, pad_end, padded, block_e, n_used, n_blocks * bm


def kernel(x_prompt, x_sample, cache_diff_k, cache_diff_v, state_ret, cache_mem_k, cache_mem_v, mem_prompt, ln_in_g, ln_in_b, w_in, b_gate, diff_lambda, diff_subln, w_proj_ret, w_proj_diff, w_out, ln1_g, ln1_b, w_mq, w_mk, w_mv, w_mo, ln2_g, ln2_b, router_w, router_b, w1, b1, w2, b2, ln3_g, ln3_b):
    Bp, Lp, D = x_prompt.shape
    Bs, Ls, _ = x_sample.shape
    depth = w_in.shape[0]
    assert depth == 1
    past = cache_diff_k.shape[2]
    n_mem = mem_prompt.shape[1]
    Tp, Ts = Bp * Lp, Bs * Ls
    alpha = (2.0 * depth) ** 0.25
    lambda_init = 0.8 - 0.6 * math.exp(-0.3 * 0)
    row = lambda v: v.reshape(1, -1)

    xp = x_prompt.reshape(Tp, D)
    xs_in = x_sample.reshape(Ts, D)
    kv_blk = 4096 // IN_TN
    P, dk_p, dv_p, dk_s, dv_s = _in_proj(xp, xs_in, row(ln_in_g), row(ln_in_b), w_in[0].astype(BF16),
                                         kv_blk)

    zeros_state = jnp.zeros((Bp,) + state_ret.shape[2:], F32)
    yr_p, s_p = _retention(P, 0, Bp, Lp, 0, zeros_state)
    yr_s, s_s = _retention(P, Tp, Bs, Ls, past, state_ret[0])

    yd_p = _attn_prompt(P, Bp, Lp, diff_lambda[0], row(diff_subln[0]), lambda_init)
    ck = cache_diff_k[0].reshape(Bs * past, -1)
    cv = cache_diff_v[0].reshape(Bs * past, -1)
    yd_s = _attn_sample(P, Tp, Bs, Ls, ck, cv, diff_lambda[0], row(diff_subln[0]), lambda_init)

    h1 = _mix(xp, xs_in, yr_p, yr_s, yd_p, yd_s, P, row(ln_in_g), row(ln_in_b), row(b_gate[0]),
              w_proj_ret[0].astype(BF16), w_proj_diff[0].astype(BF16), w_out[0].astype(BF16),
              row(ln1_g[0]), row(ln1_b[0]), alpha)

    w_mkv = jnp.concatenate([w_mk[0], w_mv[0]], axis=1).astype(BF16)
    mkv = _mem_kv(mem_prompt.reshape(Bp * n_mem, D), w_mkv)
    hm = w_mk.shape[2]
    mk_p = mkv[:, :hm].reshape(Bp, n_mem, hm)
    mv_p = mkv[:, hm:].reshape(Bp, n_mem, hm)
    h2, route, counts = _mem_attn(
        h1, Tp, mk_p, mv_p, Lp, cache_mem_k[0].reshape(Bs, n_mem, hm),
        cache_mem_v[0].reshape(Bs, n_mem, hm), Ls, w_mq[0].astype(BF16), w_mo[0].astype(BF16),
        row(ln2_g[0]), row(ln2_b[0]), router_w[0], row(router_b[0]), alpha)

    dest, pad_end, padded, block_e, n_used, n_rows = _moe_plan(route, counts, MOE_BM)
    xs = _moe_dispatch(h2, dest, pad_end, padded, n_used, n_rows, MOE_BM)
    act = _moe_up(xs, w1[0], b1[0][:, None, :], block_e, n_used)
    ys = _moe_down(act, w2[0], b2[0][:, None, :], block_e, n_used)
    out_p, out_s = _moe_combine(ys, dest, h2, route, row(ln3_g[0]), row(ln3_b[0]), Tp, alpha)

    return (
        out_p.reshape(Bp, Lp, D),
        out_s.reshape(Bs, Ls, D),
        dk_p.reshape(1, Bp, Lp, D_HEADS, 128),
        dv_p.reshape(1, Bp, Lp, D_HEADS, 128),
        s_p[None],
        mk_p.reshape(1, Bp, n_mem, M_HEADS, hm // M_HEADS),
        mv_p.reshape(1, Bp, n_mem, M_HEADS, hm // M_HEADS),
        dk_s.reshape(1, Bs, Ls, D_HEADS, 128),
        dv_s.reshape(1, Bs, Ls, D_HEADS, 128),
        s_s[None],
    )
```

```python
import functools
import math

import jax
import jax.numpy as jnp
from jax import lax
from jax.experimental import pallas as pl
from jax.experimental.pallas import tpu as pltpu

F32 = jnp.float32
BF16 = jnp.bfloat16

CHUNK = 64
R_HEADS = 4
D_HEADS = 8
M_HEADS = 4
N_EXPERTS = 32
TOP_K = 4
SWIGLU_ALPHA = 1.702
SWIGLU_LIMIT = 7.0
LN_EPS = 1e-5
RMS_EPS = 1e-5
ROPE_BASE = 10000.0

LANES = 128
VMEM_LIMIT = 58 * 1024 * 1024

IN_TM = 512
IN_TN = 2048
RET_CHUNK = 256
ATT_BLK = 512
ATT_UNROLL = 4
MIX_TM = 256
MEM_TM = 512
MOE_BM = 256
MOE_TN1 = 1024
MOE_TN2 = 2048
GATHER_ROWS = 256
COMBINE_TM = 128


def _cparams(sem):
    return pltpu.CompilerParams(dimension_semantics=sem, vmem_limit_bytes=VMEM_LIMIT)


def _layer_norm(x, g, b):
    mu = jnp.mean(x, axis=-1, keepdims=True)
    xc = x - mu
    var = jnp.mean(xc * xc, axis=-1, keepdims=True)
    return xc * lax.rsqrt(var + LN_EPS) * g + b


def _dot(a, b):
    return jnp.dot(a, b, preferred_element_type=F32)


def _dot_nt(a, b):
    return lax.dot_general(a, b, (((1,), (1,)), ((), ())), preferred_element_type=F32)


def _dot_tn(a, b):
    return lax.dot_general(a, b, (((0,), (0,)), ((), ())), preferred_element_type=F32)


def _lo(i, n):
    return jnp.minimum(i, n - 1)


def _hi(i, n):
    return jnp.maximum(i - n, 0)


def _in_proj_kernel(xp_ref, xs_ref, g_ref, b_ref, w_ref, p_ref, dkp_ref, dvp_ref, dks_ref, dvs_ref,
                    h_scr, *, n_p):
    i = pl.program_id(0)
    j = pl.program_id(1)
    half = dkp_ref.shape[1]

    @pl.when(jnp.logical_and(j == 0, i < n_p))
    def _():
        h_scr[...] = _layer_norm(xp_ref[...], g_ref[...], b_ref[...]).astype(BF16)

    @pl.when(jnp.logical_and(j == 0, i >= n_p))
    def _():
        h_scr[...] = _layer_norm(xs_ref[...], g_ref[...], b_ref[...]).astype(BF16)

    acc = _dot(h_scr[...], w_ref[...])
    p_ref[...] = acc.astype(BF16)

    @pl.when(jnp.logical_and(j == 0, i < n_p))
    def _():
        dkp_ref[...] = acc[:, :half]
        dvp_ref[...] = acc[:, half:]

    @pl.when(jnp.logical_and(j == 0, i >= n_p))
    def _():
        dks_ref[...] = acc[:, :half]
        dvs_ref[...] = acc[:, half:]


def _in_proj(xp, xs, g, b, w, kv_blk):
    Tp, D = xp.shape
    Ts = xs.shape[0]
    N = w.shape[1]
    n_p, n_s = Tp // IN_TM, Ts // IN_TM
    half = IN_TN // 2

    def col(j):
        return jnp.where(j == 0, kv_blk, jnp.where(j <= kv_blk, j - 1, j))

    single = pl.Buffered(1)
    kv_p = pl.BlockSpec((IN_TM, half), lambda i, j: (_lo(i, n_p), 0))
    kv_s = pl.BlockSpec((IN_TM, half), lambda i, j: (_hi(i, n_p), 0), pipeline_mode=single)
    return pl.pallas_call(
        functools.partial(_in_proj_kernel, n_p=n_p),
        grid=(n_p + n_s, N // IN_TN),
        in_specs=[
            pl.BlockSpec((IN_TM, D), lambda i, j: (_lo(i, n_p), 0)),
            pl.BlockSpec((IN_TM, D), lambda i, j: (_hi(i, n_p), 0), pipeline_mode=single),
            pl.BlockSpec((1, D), lambda i, j: (0, 0)),
            pl.BlockSpec((1, D), lambda i, j: (0, 0)),
            pl.BlockSpec((D, IN_TN), lambda i, j: (0, col(j))),
        ],
        out_specs=[pl.BlockSpec((IN_TM, IN_TN), lambda i, j: (i, j)), kv_p, kv_p, kv_s, kv_s],
        out_shape=[
            jax.ShapeDtypeStruct((Tp + Ts, N), BF16),
            jax.ShapeDtypeStruct((Tp, half), F32),
            jax.ShapeDtypeStruct((Tp, half), F32),
            jax.ShapeDtypeStruct((Ts, half), F32),
            jax.ShapeDtypeStruct((Ts, half), F32),
        ],
        scratch_shapes=[pltpu.VMEM((IN_TM, D), BF16)],
        compiler_params=_cparams(("arbitrary", "arbitrary")),
        name="in_proj",
    )(xp, xs, g, b, w)


P_DK = 0
P_DV = 8
P_RQ = 16
P_RK = 20
P_RV = 24
P_RG = 32
P_DQ = 40
P_GATE = 48


def _retention_kernel(q_ref, k_ref, v_ref, g_ref, cos_ref, sin_ref, dm_ref, qd_ref, kd_ref,
                      bd_ref, s0_ref, y_ref, sout_ref, s_scr, *, n_chunks, dk, dv):
    c = pl.program_id(1)

    @pl.when(c == 0)
    def _():
        s_scr[...] = s0_ref[0]

    cos = cos_ref[...]
    sin = sin_ref[...]
    k_scale = dk ** -0.5
    for h in range(R_HEADS):
        q = q_ref[:, h * dk:(h + 1) * dk].astype(F32)
        k = k_ref[:, h * dk:(h + 1) * dk].astype(F32)
        v = v_ref[:, h * dv:(h + 1) * dv]
        g = g_ref[:, h * dv:(h + 1) * dv].astype(F32)
        q = q * cos + pltpu.roll(q, dk // 2, 1) * sin
        k = (k * cos + pltpu.roll(k, dk // 2, 1) * sin) * k_scale
        qb = q.astype(BF16)
        kb = k.astype(BF16)
        s = s_scr[h]
        scores = _dot_nt(qb, kb) * dm_ref[h]
        o = _dot(scores.astype(BF16), v) + _dot((q * qd_ref[h]).astype(BF16), s.astype(BF16))
        s_scr[h] = bd_ref[h] * s + _dot_tn((k * kd_ref[h]).astype(BF16), v)
        o = o * lax.rsqrt(jnp.mean(o * o, axis=-1, keepdims=True) + RMS_EPS)
        y_ref[:, h * dv:(h + 1) * dv] = (g * jax.nn.sigmoid(g) * o).astype(BF16)

    @pl.when(c == n_chunks - 1)
    def _():
        sout_ref[0] = s_scr[...]


def _retention(P, row_off, B, L, pos0, S0):
    dk, dv = 128, 256
    C = min(L, RET_CHUNK)
    nc = L // C
    ob = row_off // C
    pos = (pos0 + jnp.arange(L, dtype=jnp.int32)).astype(F32)
    inv = 1.0 / (ROPE_BASE ** jnp.linspace(0.0, 1.0, dk // 2, dtype=F32))
    ang = pos[:, None] * inv[None, :]
    cos = jnp.concatenate([jnp.cos(ang), jnp.cos(ang)], axis=-1)
    sin = jnp.concatenate([-jnp.sin(ang), jnp.sin(ang)], axis=-1)
    log_g = jnp.log1p(-jnp.power(2.0, -5.0 - jnp.arange(R_HEADS, dtype=F32)))
    i = jnp.arange(C, dtype=F32)
    rel = i[:, None] - i[None, :]
    dmask = jnp.where(rel >= 0, jnp.exp(log_g[:, None, None] * jnp.maximum(rel, 0.0)), 0.0)
    q_decay = jnp.exp(log_g[:, None] * (i + 1.0))[..., None]
    k_decay = jnp.exp(log_g[:, None] * (C - 1.0 - i))[..., None]
    b_decay = jnp.exp(log_g * C)[:, None, None]

    def rows(b, c):
        return ob + b * nc + c

    kern = functools.partial(_retention_kernel, n_chunks=nc, dk=dk, dv=dv)
    return pl.pallas_call(
        kern,
        grid=(B, nc),
        in_specs=[
            pl.BlockSpec((C, R_HEADS * dk), lambda b, c: (rows(b, c), P_RQ * LANES // (R_HEADS * dk))),
            pl.BlockSpec((C, R_HEADS * dk), lambda b, c: (rows(b, c), P_RK * LANES // (R_HEADS * dk))),
            pl.BlockSpec((C, R_HEADS * dv), lambda b, c: (rows(b, c), P_RV * LANES // (R_HEADS * dv))),
            pl.BlockSpec((C, R_HEADS * dv), lambda b, c: (rows(b, c), P_RG * LANES // (R_HEADS * dv))),
            pl.BlockSpec((C, dk), lambda b, c: (c, 0)),
            pl.BlockSpec((C, dk), lambda b, c: (c, 0)),
            pl.BlockSpec((R_HEADS, C, C), lambda b, c: (0, 0, 0)),
            pl.BlockSpec((R_HEADS, C, 1), lambda b, c: (0, 0, 0)),
            pl.BlockSpec((R_HEADS, C, 1), lambda b, c: (0, 0, 0)),
            pl.BlockSpec((R_HEADS, 1, 1), lambda b, c: (0, 0, 0)),
            pl.BlockSpec((1, R_HEADS, dk, dv), lambda b, c: (b, 0, 0, 0)),
        ],
        out_specs=[
            pl.BlockSpec((C, R_HEADS * dv), lambda b, c: (b * nc + c, 0)),
            pl.BlockSpec((1, R_HEADS, dk, dv), lambda b, c: (b, 0, 0, 0)),
        ],
        out_shape=[
            jax.ShapeDtypeStruct((B * L, R_HEADS * dv), BF16),
            jax.ShapeDtypeStruct((B, R_HEADS, dk, dv), F32),
        ],
        scratch_shapes=[pltpu.VMEM((R_HEADS, dk, dv), F32)],
        compiler_params=_cparams(("parallel", "arbitrary")),
        name="retention",
    )(P, P, P, P, cos, sin, dmask, q_decay, k_decay, b_decay, S0)


def _diff_lambda(lam_ref, lambda_init):
    lv = lam_ref[...]
    a = jnp.sum(lv[0:1] * lv[1:2], axis=-1, keepdims=True)
    b = jnp.sum(lv[2:3] * lv[3:4], axis=-1, keepdims=True)
    return jnp.exp(a) - jnp.exp(b) + lambda_init


def _lane_tile(x, n):
    return jnp.concatenate([x] * n, axis=1)


def _stack_maps(q, hd):
    lane = lax.broadcasted_iota(jnp.int32, q.shape, 1)
    zero = jnp.zeros_like(q)
    return jnp.concatenate([jnp.where(lane < hd, q, zero), jnp.where(lane < hd, zero, q)], axis=0)


def _diff_finish(acc, l, lam, subln, lambda_init, tq):
    o = acc[:tq] / l[:tq] - lam * (acc[tq:] / l[tq:])
    o = o * lax.rsqrt(jnp.mean(o * o, axis=-1, keepdims=True) + RMS_EPS)
    return (o * subln * (1.0 - lambda_init)).astype(BF16)


def _attn_prompt_kernel(q_ref, k_ref, v_ref, lam_ref, sub_ref, y_ref, qs_scr, ve_scr, m_scr, acc_scr,
                        *, blk, hd, lambda_init):
    qi = pl.program_id(2)
    dv = v_ref.shape[1]
    seq = v_ref.shape[0]

    @pl.when(qi == 0)
    def _():
        ve_scr[:, :dv] = v_ref[...]
        ve_scr[:, dv:] = jnp.ones((seq, dv), BF16)

    qs_scr[...] = _stack_maps(q_ref[...] * (hd ** -0.5), hd)
    m_scr[...] = jnp.full_like(m_scr, -jnp.inf)
    acc_scr[...] = jnp.zeros_like(acc_scr)

    def step(ki, masked):
        off = pl.multiple_of(ki * blk, blk)
        s = _dot_nt(qs_scr[...], k_ref[pl.ds(off, blk), :])
        if masked:
            row = lax.broadcasted_iota(jnp.int32, s.shape, 0)
            col = lax.broadcasted_iota(jnp.int32, s.shape, 1)
            qrow = jnp.where(row >= blk, row - blk, row)
            shift = CHUNK.bit_length() - 1
            s = jnp.where((col >> shift) <= (qrow >> shift), s, -1e30)
        m_prev = m_scr[...]
        m_new = jnp.maximum(m_prev, jnp.max(s, axis=-1, keepdims=True))
        alpha = jnp.exp(m_prev - m_new)
        p = jnp.exp(s - _lane_tile(m_new, blk // LANES))
        pv = _dot(p.astype(BF16), ve_scr[pl.ds(off, blk), :])
        acc_scr[...] = _lane_tile(alpha, 2 * dv // LANES) * acc_scr[...] + pv
        m_scr[...] = m_new

    def body(j, carry):
        for u in range(ATT_UNROLL):
            step(ATT_UNROLL * j + u, False)
        return carry

    n_main = qi // ATT_UNROLL
    lax.fori_loop(0, n_main, body, 0)
    done = n_main * ATT_UNROLL
    width = ATT_UNROLL // 2
    while width >= 1:
        @pl.when(((qi - done) // width) % 2 == 1)
        def _(width=width):
            base = qi - (qi - done) % (2 * width)
            for u in range(width):
                step(base + u, False)
        width //= 2
    step(qi, True)

    lam = _diff_lambda(lam_ref, lambda_init)
    acc = acc_scr[...]
    o = acc[:blk, :dv] / acc[:blk, dv:] - lam * (acc[blk:, :dv] / acc[blk:, dv:])
    o = o * lax.rsqrt(jnp.mean(o * o, axis=-1, keepdims=True) + RMS_EPS)
    y_ref[...] = (o * sub_ref[...] * (1.0 - lambda_init)).astype(BF16)


def _attn_prompt(P, B, S, lam_p, subln, lambda_init):
    hd = 64
    blk = min(ATT_BLK, S)
    nq = S // blk
    kern = functools.partial(_attn_prompt_kernel, blk=blk, hd=hd, lambda_init=lambda_init)
    return pl.pallas_call(
        kern,
        grid=(B, D_HEADS, nq),
        in_specs=[
            pl.BlockSpec((blk, 2 * hd), lambda b, h, i: (b * nq + i, P_DQ + h)),
            pl.BlockSpec((S, 2 * hd), lambda b, h, i: (b, P_DK + h)),
            pl.BlockSpec((S, 2 * hd), lambda b, h, i: (b, P_DV + h)),
            pl.BlockSpec((4, hd), lambda b, h, i: (0, 0)),
            pl.BlockSpec((1, 2 * hd), lambda b, h, i: (0, 0)),
        ],
        out_specs=pl.BlockSpec((blk, 2 * hd), lambda b, h, i: (b * nq + i, h)),
        out_shape=jax.ShapeDtypeStruct((B * S, D_HEADS * 2 * hd), BF16),
        scratch_shapes=[
            pltpu.VMEM((2 * blk, 2 * hd), BF16),
            pltpu.VMEM((S, 4 * hd), BF16),
            pltpu.VMEM((2 * blk, LANES), F32),
            pltpu.VMEM((2 * blk, 4 * hd), F32),
        ],
        compiler_params=_cparams(("parallel", "parallel", "arbitrary")),
        name="attn_prompt",
    )(P, P, P, lam_p, subln)


def _attn_sample_kernel(q_ref, kn_ref, vn_ref, kc_ref, vc_ref, lam_ref, sub_ref, y_ref,
                        *, tq, hd, lambda_init):
    lam = _diff_lambda(lam_ref, lambda_init)
    w = 2 * hd
    for h in range(D_HEADS):
        cols = slice(h * w, (h + 1) * w)
        qs = _stack_maps(q_ref[:, cols] * (hd ** -0.5), hd)
        kc = kc_ref[0, :, h, :].astype(BF16)
        vc = vc_ref[0, :, h, :].astype(BF16)
        kn = kn_ref[:, cols]
        vn = vn_ref[:, cols]
        s_c = _dot_nt(qs, kc)
        s_n = _dot_nt(qs, kn)
        m = jnp.maximum(jnp.max(s_c, axis=-1, keepdims=True), jnp.max(s_n, axis=-1, keepdims=True))
        p_c = jnp.exp(s_c - m)
        p_n = jnp.exp(s_n - m)
        l = jnp.sum(p_c, axis=-1, keepdims=True) + jnp.sum(p_n, axis=-1, keepdims=True)
        acc = _dot(p_c.astype(BF16), vc) + _dot(p_n.astype(BF16), vn)
        y_ref[:, cols] = _diff_finish(acc, l, lam, sub_ref[...], lambda_init, tq)


def _attn_sample(P, row_off, B, L, cache_k, cache_v, lam_p, subln, lambda_init):
    hd = 64
    past = cache_k.shape[1]
    ob = row_off // L
    width = D_HEADS * 2 * hd
    kern = functools.partial(_attn_sample_kernel, tq=L, hd=hd, lambda_init=lambda_init)
    cache_spec = pl.BlockSpec((1, past, D_HEADS, 2 * hd), lambda b: (b, 0, 0, 0))
    return pl.pallas_call(
        kern,
        grid=(B,),
        in_specs=[
            pl.BlockSpec((L, width), lambda b: (ob + b, P_DQ * LANES // width)),
            pl.BlockSpec((L, width), lambda b: (ob + b, P_DK * LANES // width)),
            pl.BlockSpec((L, width), lambda b: (ob + b, P_DV * LANES // width)),
            cache_spec,
            cache_spec,
            pl.BlockSpec((4, hd), lambda b: (0, 0)),
            pl.BlockSpec((1, 2 * hd), lambda b: (0, 0)),
        ],
        out_specs=pl.BlockSpec((L, width), lambda b: (b, 0)),
        out_shape=jax.ShapeDtypeStruct((B * L, width), BF16),
        compiler_params=_cparams(("parallel",)),
        name="attn_sample",
    )(P, P, P, cache_k, cache_v, lam_p, subln)


def _mix_kernel(xp_ref, xs_ref, yrp_ref, yrs_ref, ydp_ref, yds_ref, gr_ref, gd_ref, lig_ref, lib_ref,
                bg_ref, wpr_ref, wpd_ref, wo_ref, g1_ref, b1_ref, h1_ref, *, alpha, n_p):
    i = pl.program_id(0)
    d = xp_ref.shape[1]

    def compute(x_ref, yr_ref, yd_ref):
        h0 = _layer_norm(x_ref[...], lig_ref[...], lib_ref[...])
        g_ret = jax.nn.sigmoid(gr_ref[...].astype(F32) + bg_ref[:, :d])
        g_diff = jax.nn.sigmoid(gd_ref[...].astype(F32) + bg_ref[:, d:])
        merged = g_ret * _dot(yr_ref[...], wpr_ref[...]) + g_diff * _dot(yd_ref[...], wpd_ref[...])
        mixed = _dot(merged.astype(BF16), wo_ref[...])
        h1_ref[...] = _layer_norm(alpha * h0 + mixed, g1_ref[...], b1_ref[...])

    @pl.when(i < n_p)
    def _():
        compute(xp_ref, yrp_ref, ydp_ref)

    @pl.when(i >= n_p)
    def _():
        compute(xs_ref, yrs_ref, yds_ref)


def _mix(xp, xs, yr_p, yr_s, yd_p, yd_s, P, ln_in_g, ln_in_b, b_gate, wpr, wpd, wo, g1, b1, alpha):
    Tp, D = xp.shape
    Ts = xs.shape[0]
    tm = MIX_TM
    n_p, n_s = Tp // tm, Ts // tm
    gcol = P_GATE * LANES // D
    const = lambda i: (0, 0)
    lo = lambda i: (_lo(i, n_p), 0)
    hi = lambda i: (_hi(i, n_p), 0)
    single = pl.Buffered(1)
    kern = functools.partial(_mix_kernel, alpha=alpha, n_p=n_p)
    return pl.pallas_call(
        kern,
        grid=(n_p + n_s,),
        in_specs=[
            pl.BlockSpec((tm, D), lo),
            pl.BlockSpec((tm, D), hi),
            pl.BlockSpec((tm, yr_p.shape[1]), lo),
            pl.BlockSpec((tm, yr_s.shape[1]), hi),
            pl.BlockSpec((tm, yd_p.shape[1]), lo),
            pl.BlockSpec((tm, yd_s.shape[1]), hi),
            pl.BlockSpec((tm, D), lambda i: (i, gcol)),
            pl.BlockSpec((tm, D), lambda i: (i, gcol + 1)),
            pl.BlockSpec((1, D), const),
            pl.BlockSpec((1, D), const),
            pl.BlockSpec((1, 2 * D), const),
            pl.BlockSpec(wpr.shape, const, pipeline_mode=single),
            pl.BlockSpec(wpd.shape, const, pipeline_mode=single),
            pl.BlockSpec(wo.shape, const, pipeline_mode=single),
            pl.BlockSpec((1, D), const),
            pl.BlockSpec((1, D), const),
        ],
        out_specs=pl.BlockSpec((tm, D), lambda i: (i, 0)),
        out_shape=jax.ShapeDtypeStruct((Tp + Ts, D), F32),
        compiler_params=_cparams(("arbitrary",)),
        name="mix",
    )(xp, xs, yr_p, yr_s, yd_p, yd_s, P, P, ln_in_g, ln_in_b, b_gate, wpr, wpd, wo, g1, b1)


def _mem_kernel(h1_ref, mkp_ref, mvp_ref, mks_ref, mvs_ref, wq_ref, wo_ref, g2_ref, b2_ref, rw_ref,
                rb_ref, h2_ref, route_ref, counts_ref, q_scr, o_scr, cnt_scr, *, n_p, alpha, hd):
    i = pl.program_id(0)
    tm = h1_ref.shape[0]
    h1 = h1_ref[...]
    q_scr[...] = (_dot(h1.astype(BF16), wq_ref[...]) * (hd ** -0.5)).astype(BF16)

    def attend(mk_ref, mv_ref):
        n_sub = mk_ref.shape[0]
        seg = tm // n_sub
        for s in range(n_sub):
            for h in range(M_HEADS):
                if len(mk_ref.shape) == 4:
                    mk = mk_ref[s, :, h, :].astype(BF16)
                    mv = mv_ref[s, :, h, :].astype(BF16)
                else:
                    mk = mk_ref[s, :, h * hd:(h + 1) * hd].astype(BF16)
                    mv = mv_ref[s, :, h * hd:(h + 1) * hd].astype(BF16)
                qh = q_scr[s * seg:(s + 1) * seg, h * hd:(h + 1) * hd]
                sc = _dot_nt(qh, mk)
                sc = sc - jnp.max(sc, axis=-1, keepdims=True)
                p = jnp.exp(sc)
                p = p / jnp.sum(p, axis=-1, keepdims=True)
                o_scr[s * seg:(s + 1) * seg, h * hd:(h + 1) * hd] = _dot(
                    p.astype(BF16), mv).astype(BF16)

    @pl.when(i < n_p)
    def _():
        attend(mkp_ref, mvp_ref)

    @pl.when(i >= n_p)
    def _():
        attend(mks_ref, mvs_ref)

    h2 = _layer_norm(alpha * h1 + _dot(o_scr[...], wo_ref[...]), g2_ref[...], b2_ref[...])
    h2_ref[...] = h2
    h2b = h2.astype(BF16)

    h_lo = (h2 - h2b.astype(F32)).astype(BF16)
    rw = rw_ref[...]
    rw_hi = rw.astype(BF16)
    rw_lo = (rw - rw_hi.astype(F32)).astype(BF16)
    logits = _dot(h2b, rw_hi) + _dot(h2b, rw_lo) + _dot(h_lo, rw_hi) + rb_ref[...]

    n_e = logits.shape[1]
    eidx = lax.broadcasted_iota(jnp.int32, logits.shape, 1).astype(F32)
    lane = lax.broadcasted_iota(jnp.int32, (tm, LANES), 1)
    route = jnp.zeros((tm, LANES), F32)
    work = logits
    vals = []
    sels = []
    for k in range(TOP_K):
        mx = jnp.max(work, axis=-1, keepdims=True)
        sel = jnp.min(jnp.where(work == mx, eidx, float(n_e)), axis=-1, keepdims=True)
        work = jnp.where(eidx == sel, -jnp.inf, work)
        vals.append(mx)
        sels.append(sel)
        route = jnp.where(lane == TOP_K + k, sel, route)
    ex = [jnp.exp(v - vals[0]) for v in vals]
    den = ex[0] + ex[1] + ex[2] + ex[3]
    for k in range(TOP_K):
        route = jnp.where(lane == k, ex[k] / den, route)

    @pl.when(i == 0)
    def _():
        cnt_scr[...] = jnp.zeros_like(cnt_scr)

    lane_f = lane.astype(F32)
    hit = [lane_f == sels[k] for k in range(TOP_K)]
    cnt = sum(h.astype(F32) for h in hit)
    r_io = lax.broadcasted_iota(jnp.int32, (tm, tm), 0)
    c_io = lax.broadcasted_iota(jnp.int32, (tm, tm), 1)
    ltri = jnp.where(r_io > c_io, 1.0, 0.0).astype(BF16)
    excl = _dot(ltri, cnt.astype(BF16)) + cnt_scr[...]
    for k in range(TOP_K):
        rank = jnp.sum(jnp.where(hit[k], excl, 0.0), axis=-1, keepdims=True)
        route = jnp.where(lane == 2 * TOP_K + k, rank, route)
    route_ref[...] = route
    cnt_scr[...] = cnt_scr[...] + jnp.sum(cnt, axis=0, keepdims=True)
    counts_ref[...] = cnt_scr[...]


def _mem_attn(h1, t_prompt, mem_kp, mem_vp, l_prompt, mem_ks, mem_vs, l_sample, wq, wo, g2, b2, rw, rb,
              alpha):
    T, D = h1.shape
    hd = 128
    tm = MEM_TM
    assert l_prompt % tm == 0 and tm % l_sample == 0
    n_p = t_prompt // tm
    n_s = (T - t_prompt) // tm
    per_b = l_prompt // tm
    sub_s = tm // l_sample
    n_mem = mem_kp.shape[1]
    bp = mem_kp.shape[0]
    const = lambda i: (0, 0)
    single = pl.Buffered(1)
    mem_p = pl.BlockSpec((1, n_mem, M_HEADS * hd), lambda i: (jnp.minimum(i // per_b, bp - 1), 0, 0))
    mem_s = pl.BlockSpec((sub_s, n_mem, M_HEADS, hd), lambda i: (_hi(i, n_p), 0, 0, 0))
    kern = functools.partial(_mem_kernel, n_p=n_p, alpha=alpha, hd=hd)
    return pl.pallas_call(
        kern,
        grid=(n_p + n_s,),
        in_specs=[
            pl.BlockSpec((tm, D), lambda i: (i, 0)),
            mem_p, mem_p, mem_s, mem_s,
            pl.BlockSpec(wq.shape, const, pipeline_mode=single),
            pl.BlockSpec(wo.shape, const, pipeline_mode=single),
            pl.BlockSpec((1, D), const),
            pl.BlockSpec((1, D), const),
            pl.BlockSpec(rw.shape, const, pipeline_mode=single),
            pl.BlockSpec((1, rw.shape[1]), const),
        ],
        out_specs=[
            pl.BlockSpec((tm, D), lambda i: (i, 0)),
            pl.BlockSpec((tm, LANES), lambda i: (i, 0)),
            pl.BlockSpec((1, LANES), const),
        ],
        out_shape=[
            jax.ShapeDtypeStruct((T, D), F32),
            jax.ShapeDtypeStruct((T, LANES), F32),
            jax.ShapeDtypeStruct((1, LANES), F32),
        ],
        scratch_shapes=[
            pltpu.VMEM((tm, M_HEADS * hd), BF16),
            pltpu.VMEM((tm, M_HEADS * hd), BF16),
            pltpu.VMEM((1, LANES), F32),
        ],
        compiler_params=_cparams(("arbitrary",)),
        name="mem_attn",
    )(h1, mem_kp, mem_vp, mem_ks, mem_vs, wq, wo, g2, b2, rw, rb)


def _mem_kv_kernel(x_ref, w_ref, o_ref):
    o_ref[...] = _dot(x_ref[...].astype(BF16), w_ref[...])


def _mem_kv(mem, w):
    R, D = mem.shape
    N = w.shape[1]
    tm = 256
    return pl.pallas_call(
        _mem_kv_kernel,
        grid=(R // tm,),
        in_specs=[pl.BlockSpec((tm, D), lambda i: (i, 0)), pl.BlockSpec((D, N), lambda i: (0, 0))],
        out_specs=pl.BlockSpec((tm, N), lambda i: (i, 0)),
        out_shape=jax.ShapeDtypeStruct((R, N), F32),
        compiler_params=_cparams(("parallel",)),
        name="mem_kv",
    )(mem, w)


def _row_copy(src_hbm, dst, sem, src_row, dst_row):
    return pltpu.make_async_copy(src_hbm.at[pl.ds(src_row, 1)], dst.at[pl.ds(dst_row, 1)], sem)


def _dispatch_kernel(dest_ref, pend_ref, padded_ref, nu_ref, h_ref, xs_hbm, stage, zeros, sem, zsem,
                     *, tm, bm, n_steps, n_blocks):
    i = pl.program_id(0)
    slot = i % 2

    def zero_block(off):
        return pltpu.make_async_copy(zeros, xs_hbm.at[pl.ds(pl.multiple_of(off, bm), bm)], zsem)

    def zero_fill(act):
        for e in range(N_EXPERTS):
            @pl.when(padded_ref[e] > 0)
            def _(e=e):
                act(zero_block(pend_ref[e] - bm))
        for b in range(n_blocks - N_EXPERTS, n_blocks):
            @pl.when(b >= nu_ref[0])
            def _(b=b):
                act(zero_block(b * bm))

    @pl.when(i == 0)
    def _():
        zeros[...] = jnp.zeros_like(zeros)
        zero_fill(lambda c: c.start())
        zero_fill(lambda c: c.wait())

    def wait(s):
        for _ in range(TOP_K):
            pltpu.make_async_copy(stage.at[s], xs_hbm.at[pl.ds(0, tm)], sem.at[s]).wait()

    @pl.when(i >= 2)
    def _():
        wait(slot)

    stage[slot] = h_ref[...]

    def body(t, carry):
        for k in range(TOP_K):
            row = dest_ref[(i * tm + t) * TOP_K + k]
            pltpu.make_async_copy(stage.at[slot, pl.ds(t, 1)], xs_hbm.at[pl.ds(row, 1)],
                                  sem.at[slot]).start()
        return carry
    lax.fori_loop(0, tm, body, 0, unroll=4)

    @pl.when(i == n_steps - 1)
    def _():
        wait(slot)
        if n_steps > 1:
            wait(1 - slot)


def _moe_dispatch(h2, dest, pad_end, padded, n_used, n_rows, bm):
    T, D = h2.shape
    tm = GATHER_ROWS
    n_steps = T // tm
    kern = functools.partial(_dispatch_kernel, tm=tm, bm=bm, n_steps=n_steps, n_blocks=n_rows // bm)
    return pl.pallas_call(
        kern,
        grid_spec=pltpu.PrefetchScalarGridSpec(
            num_scalar_prefetch=4,
            grid=(n_steps,),
            in_specs=[pl.BlockSpec((tm, D), lambda i, d, pe, pd, nu: (i, 0))],
            out_specs=pl.BlockSpec(memory_space=pl.ANY),
            scratch_shapes=[
                pltpu.VMEM((2, tm, D), F32),
                pltpu.VMEM((bm, D), F32),
                pltpu.SemaphoreType.DMA((2,)),
                pltpu.SemaphoreType.DMA(()),
            ],
        ),
        out_shape=jax.ShapeDtypeStruct((n_rows, D), F32),
        compiler_params=_cparams(("arbitrary",)),
        name="moe_dispatch",
    )(dest, pad_end, padded, n_used, h2)


def _expert_weights(be_ref, first_ref, next_ref, cnt_ref, copies, cast, *, p, m, n_pass):
    @pl.when(jnp.logical_and(p == 0, m == 0))
    def _():
        cnt_ref[0] = 0

    @pl.when(first_ref[m] == 1)
    def _():
        seg = cnt_ref[0]
        slot = seg % 2

        @pl.when(seg == 0)
        def _():
            for c in copies(be_ref[0], 0, 0):
                c.start()

        for c in copies(be_ref[m], p, slot):
            c.wait()
        cast(slot)
        ne = next_ref[m]

        @pl.when(ne >= 0)
        def _():
            for c in copies(ne, p, 1 - slot):
                c.start()

        @pl.when(jnp.logical_and(ne < 0, p + 1 < n_pass))
        def _():
            for c in copies(be_ref[0], p + 1, 1 - slot):
                c.start()

        cnt_ref[0] = seg + 1


def _moe_up_kernel(be_ref, nu_ref, first_ref, next_ref, x_ref, w1_hbm, bg_ref, bl_ref, a_ref,
                   stage, wg_scr, wl_scr, sem, cnt_ref, *, n_pass):
    f = pl.program_id(0)
    m = pl.program_id(1)
    tn = wg_scr.shape[1]
    ff = w1_hbm.shape[2] // 2

    def copies(e, p, slot):
        off = pl.multiple_of(p * tn, tn)
        return [
            pltpu.make_async_copy(w1_hbm.at[e, :, pl.ds(off, tn)], stage.at[slot, 0], sem.at[slot]),
            pltpu.make_async_copy(w1_hbm.at[e, :, pl.ds(ff + off, tn)], stage.at[slot, 1], sem.at[slot]),
        ]

    def cast(slot):
        wg_scr[...] = stage[slot, 0].astype(BF16)
        wl_scr[...] = stage[slot, 1].astype(BF16)

    _expert_weights(be_ref, first_ref, next_ref, cnt_ref, copies, cast, p=f, m=m, n_pass=n_pass)

    @pl.when(m < nu_ref[0])
    def _():
        x = x_ref[...].astype(BF16)
        u_glu = jnp.minimum(_dot(x, wg_scr[...]) + bg_ref[0], SWIGLU_LIMIT)
        u_lin = jnp.clip(_dot(x, wl_scr[...]) + bl_ref[0], -SWIGLU_LIMIT, SWIGLU_LIMIT)
        a = u_glu * jax.nn.sigmoid(SWIGLU_ALPHA * u_glu) * (u_lin + 1.0)
        a_ref[...] = a.astype(BF16)

    @pl.when(m >= nu_ref[0])
    def _():
        a_ref[...] = jnp.zeros_like(a_ref)


def _moe_up(xs, w1, b1, plan):
    n_rows, D = xs.shape
    F = w1.shape[2] // 2
    bm, tn = MOE_BM, MOE_TN1
    nb = n_rows // bm
    nf = F // tn
    return pl.pallas_call(
        functools.partial(_moe_up_kernel, n_pass=nf),
        grid_spec=pltpu.PrefetchScalarGridSpec(
            num_scalar_prefetch=4,
            grid=(nf, nb),
            in_specs=[
                pl.BlockSpec((bm, D), lambda f, m, be, nu, fi, nx: (jnp.minimum(m, nu[0] - 1), 0)),
                pl.BlockSpec(memory_space=pl.ANY),
                pl.BlockSpec((1, 1, tn), lambda f, m, be, nu, fi, nx: (be[m], 0, f)),
                pl.BlockSpec((1, 1, tn), lambda f, m, be, nu, fi, nx: (be[m], 0, nf + f)),
            ],
            out_specs=pl.BlockSpec((bm, tn), lambda f, m, be, nu, fi, nx: (m, f)),
            scratch_shapes=[
                pltpu.VMEM((2, 2, D, tn), F32),
                pltpu.VMEM((D, tn), BF16),
                pltpu.VMEM((D, tn), BF16),
                pltpu.SemaphoreType.DMA((2,)),
                pltpu.SMEM((1,), jnp.int32),
            ],
        ),
        out_shape=jax.ShapeDtypeStruct((n_rows, F), BF16),
        compiler_params=_cparams(("arbitrary", "arbitrary")),
        name="moe_up",
    )(*plan, xs, w1, b1, b1)


def _moe_down_kernel(be_ref, nu_ref, first_ref, next_ref, a_ref, w2_hbm, b_ref, y_ref,
                     stage, w_scr, sem, cnt_ref, *, n_pass):
    d = pl.program_id(0)
    m = pl.program_id(1)
    tn = w_scr.shape[1]

    def copies(e, p, slot):
        off = pl.multiple_of(p * tn, tn)
        return [pltpu.make_async_copy(w2_hbm.at[e, :, pl.ds(off, tn)], stage.at[slot], sem.at[slot])]

    def cast(slot):
        w_scr[...] = stage[slot].astype(BF16)

    _expert_weights(be_ref, first_ref, next_ref, cnt_ref, copies, cast, p=d, m=m, n_pass=n_pass)

    @pl.when(m < nu_ref[0])
    def _():
        y_ref[...] = _dot(a_ref[...], w_scr[...]) + b_ref[0]

    @pl.when(m >= nu_ref[0])
    def _():
        y_ref[...] = jnp.zeros_like(y_ref)


def _moe_down(a, w2, b2, plan):
    n_rows, F = a.shape
    D = w2.shape[2]
    bm, tn = MOE_BM, MOE_TN2
    nb = n_rows // bm
    nd = D // tn
    return pl.pallas_call(
        functools.partial(_moe_down_kernel, n_pass=nd),
        grid_spec=pltpu.PrefetchScalarGridSpec(
            num_scalar_prefetch=4,
            grid=(nd, nb),
            in_specs=[
                pl.BlockSpec((bm, F), lambda d, m, be, nu, fi, nx: (jnp.minimum(m, nu[0] - 1), 0)),
                pl.BlockSpec(memory_space=pl.ANY),
                pl.BlockSpec((1, 1, tn), lambda d, m, be, nu, fi, nx: (be[m], 0, d)),
            ],
            out_specs=pl.BlockSpec((bm, tn), lambda d, m, be, nu, fi, nx: (m, d)),
            scratch_shapes=[
                pltpu.VMEM((2, F, tn), F32),
                pltpu.VMEM((F, tn), BF16),
                pltpu.SemaphoreType.DMA((2,)),
                pltpu.SMEM((1,), jnp.int32),
            ],
        ),
        out_shape=jax.ShapeDtypeStruct((n_rows, D), F32),
        compiler_params=_cparams(("arbitrary", "arbitrary")),
        name="moe_down",
    )(*plan, a, w2, b2)


def _combine_kernel(dest_ref, h2_ref, rt_ref, g_ref, b_ref, ys_hbm, op_ref, os_ref, buf, sem,
                    *, tm, n_steps, n_prompt, alpha):
    i = pl.program_id(0)

    def issue(step, slot):
        def body(t, carry):
            for k in range(TOP_K):
                row = dest_ref[(step * tm + t) * TOP_K + k]
                _row_copy(ys_hbm, buf.at[slot, k], sem.at[slot], row, t).start()
            return carry
        lax.fori_loop(0, tm, body, 0, unroll=2)

    def wait(slot):
        for k in range(TOP_K):
            pltpu.make_async_copy(ys_hbm.at[pl.ds(0, tm)], buf.at[slot, k], sem.at[slot]).wait()

    slot = i % 2

    @pl.when(i == 0)
    def _():
        issue(0, 0)

    @pl.when(i + 1 < n_steps)
    def _():
        issue(i + 1, 1 - slot)

    wait(slot)
    y = rt_ref[:, 0:1] * buf[slot, 0]
    for k in range(1, TOP_K):
        y = y + rt_ref[:, k:k + 1] * buf[slot, k]
    out = _layer_norm(alpha * h2_ref[...] + y, g_ref[...], b_ref[...])

    @pl.when(i < n_prompt)
    def _():
        op_ref[...] = out

    @pl.when(i >= n_prompt)
    def _():
        os_ref[...] = out


def _moe_combine(ys, dest, h2, route, g3, b3, t_prompt, alpha):
    T, D = h2.shape
    tm = COMBINE_TM
    n_steps = T // tm
    n_prompt = t_prompt // tm
    kern = functools.partial(_combine_kernel, tm=tm, n_steps=n_steps, n_prompt=n_prompt, alpha=alpha)
    return pl.pallas_call(
        kern,
        grid_spec=pltpu.PrefetchScalarGridSpec(
            num_scalar_prefetch=1,
            grid=(n_steps,),
            in_specs=[
                pl.BlockSpec((tm, D), lambda i, d: (i, 0)),
                pl.BlockSpec((tm, LANES), lambda i, d: (i, 0)),
                pl.BlockSpec((1, D), lambda i, d: (0, 0)),
                pl.BlockSpec((1, D), lambda i, d: (0, 0)),
                pl.BlockSpec(memory_space=pl.ANY),
            ],
            out_specs=[
                pl.BlockSpec((tm, D), lambda i, d: (jnp.minimum(i, n_prompt - 1), 0)),
                pl.BlockSpec((tm, D), lambda i, d: (jnp.maximum(i - n_prompt, 0), 0)),
            ],
            scratch_shapes=[pltpu.VMEM((2, TOP_K, tm, D), F32), pltpu.SemaphoreType.DMA((2,))],
        ),
        out_shape=[
            jax.ShapeDtypeStruct((t_prompt, D), F32),
            jax.ShapeDtypeStruct((T - t_prompt, D), F32),
        ],
        compiler_params=_cparams(("arbitrary",)),
        name="moe_combine",
    )(dest, h2, route, g3, b3, ys)


def _moe_plan(route, counts, bm):
    T = route.shape[0]
    top_idx = route[:, TOP_K:2 * TOP_K].astype(jnp.int32)
    rank = route[:, 2 * TOP_K:3 * TOP_K].astype(jnp.int32)
    counts = counts[0, :N_EXPERTS].astype(jnp.int32)
    padded = (counts + bm - 1) // bm * bm
    pad_end = jnp.cumsum(padded)
    pad_start = pad_end - padded
    dest = (pad_start[top_idx] + rank).reshape(-1)
    n_blocks = -(-(T * TOP_K) // bm) + N_EXPERTS
    block_start = jnp.arange(n_blocks, dtype=jnp.int32) * bm
    block_e = jnp.sum((pad_end[None, :] <= block_start[:, None]).astype(jnp.int32), axis=1)
    block_e = jnp.minimum(block_e, N_EXPERTS - 1)
    n_used = (pad_end[-1] // bm).astype(jnp.int32).reshape(1)
    blk = jnp.arange(n_blocks, dtype=jnp.int32)
    prev_e = jnp.concatenate([jnp.full((1,), -1, jnp.int32), block_e[:-1]])
    first = jnp.logical_and(blk < n_used[0], block_e != prev_e)
    later = jnp.logical_and(first[None, :], blk[None, :] > blk[:, None])
    nxt_blk = jnp.min(jnp.where(later, blk[None, :], n_blocks), axis=1)
    next_e = jnp.where(nxt_blk < n_blocks, block_e[jnp.minimum(nxt_blk, n_blocks - 1)], -1)
    plan = (block_e, n_used, first.astype(jnp.int32), next_e.astype(jnp.int32))
    return dest, pad_end, padded, plan, n_blocks * bm


def kernel(x_prompt, x_sample, cache_diff_k, cache_diff_v, state_ret, cache_mem_k, cache_mem_v, mem_prompt, ln_in_g, ln_in_b, w_in, b_gate, diff_lambda, diff_subln, w_proj_ret, w_proj_diff, w_out, ln1_g, ln1_b, w_mq, w_mk, w_mv, w_mo, ln2_g, ln2_b, router_w, router_b, w1, b1, w2, b2, ln3_g, ln3_b):
    Bp, Lp, D = x_prompt.shape
    Bs, Ls, _ = x_sample.shape
    depth = w_in.shape[0]
    assert depth == 1
    past = cache_diff_k.shape[2]
    n_mem = mem_prompt.shape[1]
    Tp, Ts = Bp * Lp, Bs * Ls
    alpha = (2.0 * depth) ** 0.25
    lambda_init = 0.8 - 0.6 * math.exp(-0.3 * 0)
    row = lambda v: v.reshape(1, -1)

    xp = x_prompt.reshape(Tp, D)
    xs_in = x_sample.reshape(Ts, D)
    kv_blk = 4096 // IN_TN
    P, dk_p, dv_p, dk_s, dv_s = _in_proj(xp, xs_in, row(ln_in_g), row(ln_in_b), w_in[0].astype(BF16),
                                         kv_blk)

    zeros_state = jnp.zeros((Bp,) + state_ret.shape[2:], F32)
    yr_p, s_p = _retention(P, 0, Bp, Lp, 0, zeros_state)
    yr_s, s_s = _retention(P, Tp, Bs, Ls, past, state_ret[0])

    yd_p = _attn_prompt(P, Bp, Lp, diff_lambda[0], row(diff_subln[0]), lambda_init)
    yd_s = _attn_sample(P, Tp, Bs, Ls, cache_diff_k[0], cache_diff_v[0], diff_lambda[0],
                        row(diff_subln[0]), lambda_init)

    h1 = _mix(xp, xs_in, yr_p, yr_s, yd_p, yd_s, P, row(ln_in_g), row(ln_in_b), row(b_gate[0]),
              w_proj_ret[0].astype(BF16), w_proj_diff[0].astype(BF16), w_out[0].astype(BF16),
              row(ln1_g[0]), row(ln1_b[0]), alpha)

    w_mkv = jnp.concatenate([w_mk[0], w_mv[0]], axis=1).astype(BF16)
    mkv = _mem_kv(mem_prompt.reshape(Bp * n_mem, D), w_mkv)
    hm = w_mk.shape[2]
    mk_p = mkv[:, :hm].reshape(Bp, n_mem, hm)
    mv_p = mkv[:, hm:].reshape(Bp, n_mem, hm)
    h2, route, counts = _mem_attn(
        h1, Tp, mk_p, mv_p, Lp, cache_mem_k[0], cache_mem_v[0], Ls, w_mq[0].astype(BF16),
        w_mo[0].astype(BF16),
        row(ln2_g[0]), row(ln2_b[0]), router_w[0], row(router_b[0]), alpha)

    dest, pad_end, padded, plan, n_rows = _moe_plan(route, counts, MOE_BM)
    xs = _moe_dispatch(h2, dest, pad_end, padded, plan[1], n_rows, MOE_BM)
    act = _moe_up(xs, w1[0], b1[0][:, None, :], plan)
    ys = _moe_down(act, w2[0], b2[0][:, None, :], plan)
    out_p, out_s = _moe_combine(ys, dest, h2, route, row(ln3_g[0]), row(ln3_b[0]), Tp, alpha)

    return (
        out_p.reshape(Bp, Lp, D),
        out_s.reshape(Bs, Ls, D),
        dk_p.reshape(1, Bp, Lp, D_HEADS, 128),
        dv_p.reshape(1, Bp, Lp, D_HEADS, 128),
        s_p[None],
        mk_p.reshape(1, Bp, n_mem, M_HEADS, hm // M_HEADS),
        mv_p.reshape(1, Bp, n_mem, M_HEADS, hm // M_HEADS),
        dk_s.reshape(1, Bs, Ls, D_HEADS, 128),
        dv_s.reshape(1, Bs, Ls, D_HEADS, 128),
        s_s[None],
    )
```

```python
import functools
import math

import jax
import jax.numpy as jnp
from jax import lax
from jax.experimental import pallas as pl
from jax.experimental.pallas import tpu as pltpu

F32 = jnp.float32
BF16 = jnp.bfloat16

CHUNK = 64
R_HEADS = 4
D_HEADS = 8
M_HEADS = 4
N_EXPERTS = 32
TOP_K = 4
SWIGLU_ALPHA = 1.702
SWIGLU_LIMIT = 7.0
LN_EPS = 1e-5
RMS_EPS = 1e-5
ROPE_BASE = 10000.0
LOG2E = 1.4426950408889634

LANES = 128
VMEM_LIMIT = 58 * 1024 * 1024

IN_TM = 512
IN_TN = 2048
RET_CHUNK = 256
ATT_BLK = 512
ATT_UNROLL = 8
MIX_TM = 256
MEM_TM = 512
MOE_BM = 256
MOE_TN1 = 1024
MOE_TN2 = 2048
GATHER_ROWS = 256
COMBINE_TM = 128


def _cparams(sem):
    return pltpu.CompilerParams(dimension_semantics=sem, vmem_limit_bytes=VMEM_LIMIT)


def _layer_norm(x, g, b):
    mu = jnp.mean(x, axis=-1, keepdims=True)
    xc = x - mu
    var = jnp.mean(xc * xc, axis=-1, keepdims=True)
    return xc * lax.rsqrt(var + LN_EPS) * g + b


def _dot(a, b):
    return jnp.dot(a, b, preferred_element_type=F32)


def _dot_nt(a, b):
    return lax.dot_general(a, b, (((1,), (1,)), ((), ())), preferred_element_type=F32)


def _dot_tn(a, b):
    return lax.dot_general(a, b, (((0,), (0,)), ((), ())), preferred_element_type=F32)


def _lo(i, n):
    return jnp.minimum(i, n - 1)


def _hi(i, n):
    return jnp.maximum(i - n, 0)


def _in_proj_kernel(xp_ref, xs_ref, g_ref, b_ref, w_ref, p_ref, dkp_ref, dvp_ref, dks_ref, dvs_ref,
                    h_scr, *, n_p):
    i = pl.program_id(0)
    j = pl.program_id(1)
    half = dkp_ref.shape[1]

    @pl.when(jnp.logical_and(j == 0, i < n_p))
    def _():
        h_scr[...] = _layer_norm(xp_ref[...], g_ref[...], b_ref[...]).astype(BF16)

    @pl.when(jnp.logical_and(j == 0, i >= n_p))
    def _():
        h_scr[...] = _layer_norm(xs_ref[...], g_ref[...], b_ref[...]).astype(BF16)

    acc = _dot(h_scr[...], w_ref[...])
    p_ref[...] = acc.astype(BF16)

    @pl.when(jnp.logical_and(j == 0, i < n_p))
    def _():
        dkp_ref[...] = acc[:, :half]
        dvp_ref[...] = acc[:, half:]

    @pl.when(jnp.logical_and(j == 0, i >= n_p))
    def _():
        dks_ref[...] = acc[:, :half]
        dvs_ref[...] = acc[:, half:]


def _in_proj(xp, xs, g, b, w, kv_blk):
    Tp, D = xp.shape
    Ts = xs.shape[0]
    N = w.shape[1]
    n_p, n_s = Tp // IN_TM, Ts // IN_TM
    half = IN_TN // 2

    def col(j):
        return jnp.where(j == 0, kv_blk, jnp.where(j <= kv_blk, j - 1, j))

    single = pl.Buffered(1)
    kv_p = pl.BlockSpec((IN_TM, half), lambda i, j: (_lo(i, n_p), 0))
    kv_s = pl.BlockSpec((IN_TM, half), lambda i, j: (_hi(i, n_p), 0), pipeline_mode=single)
    return pl.pallas_call(
        functools.partial(_in_proj_kernel, n_p=n_p),
        grid=(n_p + n_s, N // IN_TN),
        in_specs=[
            pl.BlockSpec((IN_TM, D), lambda i, j: (_lo(i, n_p), 0)),
            pl.BlockSpec((IN_TM, D), lambda i, j: (_hi(i, n_p), 0), pipeline_mode=single),
            pl.BlockSpec((1, D), lambda i, j: (0, 0)),
            pl.BlockSpec((1, D), lambda i, j: (0, 0)),
            pl.BlockSpec((D, IN_TN), lambda i, j: (0, col(j))),
        ],
        out_specs=[pl.BlockSpec((IN_TM, IN_TN), lambda i, j: (i, j)), kv_p, kv_p, kv_s, kv_s],
        out_shape=[
            jax.ShapeDtypeStruct((Tp + Ts, N), BF16),
            jax.ShapeDtypeStruct((Tp, half), F32),
            jax.ShapeDtypeStruct((Tp, half), F32),
            jax.ShapeDtypeStruct((Ts, half), F32),
            jax.ShapeDtypeStruct((Ts, half), F32),
        ],
        scratch_shapes=[pltpu.VMEM((IN_TM, D), BF16)],
        compiler_params=_cparams(("arbitrary", "arbitrary")),
        name="in_proj",
    )(xp, xs, g, b, w)


P_DK = 0
P_DV = 8
P_RQ = 16
P_RK = 20
P_RV = 24
P_RG = 32
P_DQ = 40
P_GATE = 48


def _retention_kernel(q_ref, k_ref, v_ref, g_ref, cos_ref, sin_ref, dm_ref, qd_ref, kd_ref,
                      bd_ref, s0_ref, y_ref, sout_ref, s_scr, *, n_chunks, dk, dv):
    c = pl.program_id(1)

    @pl.when(c == 0)
    def _():
        s_scr[...] = s0_ref[0]

    cos = cos_ref[...]
    sin = sin_ref[...]
    k_scale = dk ** -0.5
    for h in range(R_HEADS):
        q = q_ref[:, h * dk:(h + 1) * dk].astype(F32)
        k = k_ref[:, h * dk:(h + 1) * dk].astype(F32)
        v = v_ref[:, h * dv:(h + 1) * dv]
        g = g_ref[:, h * dv:(h + 1) * dv].astype(F32)
        q = q * cos + pltpu.roll(q, dk // 2, 1) * sin
        k = (k * cos + pltpu.roll(k, dk // 2, 1) * sin) * k_scale
        qb = q.astype(BF16)
        kb = k.astype(BF16)
        s = s_scr[h]
        scores = _dot_nt(qb, kb) * dm_ref[h]
        o = _dot(scores.astype(BF16), v) + _dot((q * qd_ref[h]).astype(BF16), s.astype(BF16))
        s_scr[h] = bd_ref[h] * s + _dot_tn((k * kd_ref[h]).astype(BF16), v)
        o = o * lax.rsqrt(jnp.mean(o * o, axis=-1, keepdims=True) + RMS_EPS)
        y_ref[:, h * dv:(h + 1) * dv] = (g * jax.nn.sigmoid(g) * o).astype(BF16)

    @pl.when(c == n_chunks - 1)
    def _():
        sout_ref[0] = s_scr[...]


def _retention(P, row_off, B, L, pos0, S0):
    dk, dv = 128, 256
    C = min(L, RET_CHUNK)
    nc = L // C
    ob = row_off // C
    pos = (pos0 + jnp.arange(L, dtype=jnp.int32)).astype(F32)
    inv = 1.0 / (ROPE_BASE ** jnp.linspace(0.0, 1.0, dk // 2, dtype=F32))
    ang = pos[:, None] * inv[None, :]
    cos = jnp.concatenate([jnp.cos(ang), jnp.cos(ang)], axis=-1)
    sin = jnp.concatenate([-jnp.sin(ang), jnp.sin(ang)], axis=-1)
    log_g = jnp.log1p(-jnp.power(2.0, -5.0 - jnp.arange(R_HEADS, dtype=F32)))
    i = jnp.arange(C, dtype=F32)
    rel = i[:, None] - i[None, :]
    dmask = jnp.where(rel >= 0, jnp.exp(log_g[:, None, None] * jnp.maximum(rel, 0.0)), 0.0)
    q_decay = jnp.exp(log_g[:, None] * (i + 1.0))[..., None]
    k_decay = jnp.exp(log_g[:, None] * (C - 1.0 - i))[..., None]
    b_decay = jnp.exp(log_g * C)[:, None, None]

    def rows(b, c):
        return ob + b * nc + c

    kern = functools.partial(_retention_kernel, n_chunks=nc, dk=dk, dv=dv)
    return pl.pallas_call(
        kern,
        grid=(B, nc),
        in_specs=[
            pl.BlockSpec((C, R_HEADS * dk), lambda b, c: (rows(b, c), P_RQ * LANES // (R_HEADS * dk))),
            pl.BlockSpec((C, R_HEADS * dk), lambda b, c: (rows(b, c), P_RK * LANES // (R_HEADS * dk))),
            pl.BlockSpec((C, R_HEADS * dv), lambda b, c: (rows(b, c), P_RV * LANES // (R_HEADS * dv))),
            pl.BlockSpec((C, R_HEADS * dv), lambda b, c: (rows(b, c), P_RG * LANES // (R_HEADS * dv))),
            pl.BlockSpec((C, dk), lambda b, c: (c, 0)),
            pl.BlockSpec((C, dk), lambda b, c: (c, 0)),
            pl.BlockSpec((R_HEADS, C, C), lambda b, c: (0, 0, 0)),
            pl.BlockSpec((R_HEADS, C, 1), lambda b, c: (0, 0, 0)),
            pl.BlockSpec((R_HEADS, C, 1), lambda b, c: (0, 0, 0)),
            pl.BlockSpec((R_HEADS, 1, 1), lambda b, c: (0, 0, 0)),
            pl.BlockSpec((1, R_HEADS, dk, dv), lambda b, c: (b, 0, 0, 0)),
        ],
        out_specs=[
            pl.BlockSpec((C, R_HEADS * dv), lambda b, c: (b * nc + c, 0)),
            pl.BlockSpec((1, R_HEADS, dk, dv), lambda b, c: (b, 0, 0, 0)),
        ],
        out_shape=[
            jax.ShapeDtypeStruct((B * L, R_HEADS * dv), BF16),
            jax.ShapeDtypeStruct((B, R_HEADS, dk, dv), F32),
        ],
        scratch_shapes=[pltpu.VMEM((R_HEADS, dk, dv), F32)],
        compiler_params=_cparams(("parallel", "arbitrary")),
        name="retention",
    )(P, P, P, P, cos, sin, dmask, q_decay, k_decay, b_decay, S0)


def _diff_lambda(lam_ref, lambda_init):
    lv = lam_ref[...]
    a = jnp.sum(lv[0:1] * lv[1:2], axis=-1, keepdims=True)
    b = jnp.sum(lv[2:3] * lv[3:4], axis=-1, keepdims=True)
    return jnp.exp(a) - jnp.exp(b) + lambda_init


def _lane_tile(x, n):
    return jnp.concatenate([x] * n, axis=1)


def _stack_maps(q, hd):
    lane = lax.broadcasted_iota(jnp.int32, q.shape, 1)
    zero = jnp.zeros_like(q)
    return jnp.concatenate([jnp.where(lane < hd, q, zero), jnp.where(lane < hd, zero, q)], axis=0)


def _diff_finish(acc, l, lam, subln, lambda_init, tq):
    o = acc[:tq] / l[:tq] - lam * (acc[tq:] / l[tq:])
    o = o * lax.rsqrt(jnp.mean(o * o, axis=-1, keepdims=True) + RMS_EPS)
    return (o * subln * (1.0 - lambda_init)).astype(BF16)


def _attn_prompt_kernel(q_ref, k_ref, v_ref, lam_ref, sub_ref, y_ref, qs_scr, ve_scr, m_scr, acc_scr,
                        *, blk, hd, lambda_init):
    qi = pl.program_id(2)
    dv = v_ref.shape[1]
    seq = v_ref.shape[0]

    @pl.when(qi == 0)
    def _():
        ve_scr[:, :dv] = v_ref[...]
        ve_scr[:, dv:] = jnp.ones((seq, dv), BF16)

    qs_scr[...] = _stack_maps((q_ref[...].astype(F32) * (hd ** -0.5 * LOG2E)).astype(BF16), hd)
    m_scr[...] = jnp.full_like(m_scr, -jnp.inf)
    acc_scr[...] = jnp.zeros_like(acc_scr)

    def step(ki, masked):
        off = pl.multiple_of(ki * blk, blk)
        s = _dot_nt(qs_scr[...], k_ref[pl.ds(off, blk), :])
        if masked:
            row = lax.broadcasted_iota(jnp.int32, s.shape, 0)
            col = lax.broadcasted_iota(jnp.int32, s.shape, 1)
            qrow = jnp.where(row >= blk, row - blk, row)
            shift = CHUNK.bit_length() - 1
            s = jnp.where((col >> shift) <= (qrow >> shift), s, -1e30)
        m_prev = m_scr[...]
        m_new = jnp.maximum(m_prev, jnp.max(s, axis=-1, keepdims=True))
        alpha = jnp.exp2(m_prev - m_new)
        p = jnp.exp2(s - _lane_tile(m_new, blk // LANES))
        pv = _dot(p.astype(BF16), ve_scr[pl.ds(off, blk), :])
        acc_scr[...] = _lane_tile(alpha, 2 * dv // LANES) * acc_scr[...] + pv
        m_scr[...] = m_new

    def body(j, carry):
        for u in range(ATT_UNROLL):
            step(ATT_UNROLL * j + u, False)
        return carry

    n_main = qi // ATT_UNROLL
    lax.fori_loop(0, n_main, body, 0)
    done = n_main * ATT_UNROLL
    width = ATT_UNROLL // 2
    while width >= 1:
        @pl.when(((qi - done) // width) % 2 == 1)
        def _(width=width):
            base = qi - (qi - done) % (2 * width)
            for u in range(width):
                step(base + u, False)
        width //= 2
    step(qi, True)

    lam = _diff_lambda(lam_ref, lambda_init)
    acc = acc_scr[...]
    o = acc[:blk, :dv] / acc[:blk, dv:] - lam * (acc[blk:, :dv] / acc[blk:, dv:])
    o = o * lax.rsqrt(jnp.mean(o * o, axis=-1, keepdims=True) + RMS_EPS)
    y_ref[...] = (o * sub_ref[...] * (1.0 - lambda_init)).astype(BF16)


def _attn_prompt(P, B, S, lam_p, subln, lambda_init):
    hd = 64
    blk = min(ATT_BLK, S)
    nq = S // blk
    kern = functools.partial(_attn_prompt_kernel, blk=blk, hd=hd, lambda_init=lambda_init)
    return pl.pallas_call(
        kern,
        grid=(B, D_HEADS, nq),
        in_specs=[
            pl.BlockSpec((blk, 2 * hd), lambda b, h, i: (b * nq + i, P_DQ + h)),
            pl.BlockSpec((S, 2 * hd), lambda b, h, i: (b, P_DK + h)),
            pl.BlockSpec((S, 2 * hd), lambda b, h, i: (b, P_DV + h)),
            pl.BlockSpec((4, hd), lambda b, h, i: (0, 0)),
            pl.BlockSpec((1, 2 * hd), lambda b, h, i: (0, 0)),
        ],
        out_specs=pl.BlockSpec((blk, 2 * hd), lambda b, h, i: (b * nq + i, h)),
        out_shape=jax.ShapeDtypeStruct((B * S, D_HEADS * 2 * hd), BF16),
        scratch_shapes=[
            pltpu.VMEM((2 * blk, 2 * hd), BF16),
            pltpu.VMEM((S, 4 * hd), BF16),
            pltpu.VMEM((2 * blk, LANES), F32),
            pltpu.VMEM((2 * blk, 4 * hd), F32),
        ],
        compiler_params=_cparams(("parallel", "parallel", "arbitrary")),
        name="attn_prompt",
    )(P, P, P, lam_p, subln)


def _attn_sample_kernel(q_ref, kn_ref, vn_ref, kc_ref, vc_ref, lam_ref, sub_ref, y_ref,
                        *, tq, hd, lambda_init):
    lam = _diff_lambda(lam_ref, lambda_init)
    w = 2 * hd
    for h in range(D_HEADS):
        cols = slice(h * w, (h + 1) * w)
        qs = _stack_maps(q_ref[:, cols] * (hd ** -0.5), hd)
        kc = kc_ref[0, :, h, :].astype(BF16)
        vc = vc_ref[0, :, h, :].astype(BF16)
        kn = kn_ref[:, cols]
        vn = vn_ref[:, cols]
        s_c = _dot_nt(qs, kc)
        s_n = _dot_nt(qs, kn)
        m = jnp.maximum(jnp.max(s_c, axis=-1, keepdims=True), jnp.max(s_n, axis=-1, keepdims=True))
        p_c = jnp.exp(s_c - m)
        p_n = jnp.exp(s_n - m)
        l = jnp.sum(p_c, axis=-1, keepdims=True) + jnp.sum(p_n, axis=-1, keepdims=True)
        acc = _dot(p_c.astype(BF16), vc) + _dot(p_n.astype(BF16), vn)
        y_ref[:, cols] = _diff_finish(acc, l, lam, sub_ref[...], lambda_init, tq)


def _attn_sample(P, row_off, B, L, cache_k, cache_v, lam_p, subln, lambda_init):
    hd = 64
    past = cache_k.shape[1]
    ob = row_off // L
    width = D_HEADS * 2 * hd
    kern = functools.partial(_attn_sample_kernel, tq=L, hd=hd, lambda_init=lambda_init)
    cache_spec = pl.BlockSpec((1, past, D_HEADS, 2 * hd), lambda b: (b, 0, 0, 0))
    return pl.pallas_call(
        kern,
        grid=(B,),
        in_specs=[
            pl.BlockSpec((L, width), lambda b: (ob + b, P_DQ * LANES // width)),
            pl.BlockSpec((L, width), lambda b: (ob + b, P_DK * LANES // width)),
            pl.BlockSpec((L, width), lambda b: (ob + b, P_DV * LANES // width)),
            cache_spec,
            cache_spec,
            pl.BlockSpec((4, hd), lambda b: (0, 0)),
            pl.BlockSpec((1, 2 * hd), lambda b: (0, 0)),
        ],
        out_specs=pl.BlockSpec((L, width), lambda b: (b, 0)),
        out_shape=jax.ShapeDtypeStruct((B * L, width), BF16),
        compiler_params=_cparams(("parallel",)),
        name="attn_sample",
    )(P, P, P, cache_k, cache_v, lam_p, subln)


def _mix_kernel(xp_ref, xs_ref, yrp_ref, yrs_ref, ydp_ref, yds_ref, gr_ref, gd_ref, lig_ref, lib_ref,
                bg_ref, wpr_ref, wpd_ref, wo_ref, g1_ref, b1_ref, h1_ref, *, alpha, n_p):
    i = pl.program_id(0)
    d = xp_ref.shape[1]

    def compute(x_ref, yr_ref, yd_ref):
        h0 = _layer_norm(x_ref[...], lig_ref[...], lib_ref[...])
        g_ret = jax.nn.sigmoid(gr_ref[...].astype(F32) + bg_ref[:, :d])
        g_diff = jax.nn.sigmoid(gd_ref[...].astype(F32) + bg_ref[:, d:])
        merged = g_ret * _dot(yr_ref[...], wpr_ref[...]) + g_diff * _dot(yd_ref[...], wpd_ref[...])
        mixed = _dot(merged.astype(BF16), wo_ref[...])
        h1_ref[...] = _layer_norm(alpha * h0 + mixed, g1_ref[...], b1_ref[...])

    @pl.when(i < n_p)
    def _():
        compute(xp_ref, yrp_ref, ydp_ref)

    @pl.when(i >= n_p)
    def _():
        compute(xs_ref, yrs_ref, yds_ref)


def _mix(xp, xs, yr_p, yr_s, yd_p, yd_s, P, ln_in_g, ln_in_b, b_gate, wpr, wpd, wo, g1, b1, alpha):
    Tp, D = xp.shape
    Ts = xs.shape[0]
    tm = MIX_TM
    n_p, n_s = Tp // tm, Ts // tm
    gcol = P_GATE * LANES // D
    const = lambda i: (0, 0)
    lo = lambda i: (_lo(i, n_p), 0)
    hi = lambda i: (_hi(i, n_p), 0)
    single = pl.Buffered(1)
    kern = functools.partial(_mix_kernel, alpha=alpha, n_p=n_p)
    return pl.pallas_call(
        kern,
        grid=(n_p + n_s,),
        in_specs=[
            pl.BlockSpec((tm, D), lo),
            pl.BlockSpec((tm, D), hi),
            pl.BlockSpec((tm, yr_p.shape[1]), lo),
            pl.BlockSpec((tm, yr_s.shape[1]), hi),
            pl.BlockSpec((tm, yd_p.shape[1]), lo),
            pl.BlockSpec((tm, yd_s.shape[1]), hi),
            pl.BlockSpec((tm, D), lambda i: (i, gcol)),
            pl.BlockSpec((tm, D), lambda i: (i, gcol + 1)),
            pl.BlockSpec((1, D), const),
            pl.BlockSpec((1, D), const),
            pl.BlockSpec((1, 2 * D), const),
            pl.BlockSpec(wpr.shape, const, pipeline_mode=single),
            pl.BlockSpec(wpd.shape, const, pipeline_mode=single),
            pl.BlockSpec(wo.shape, const, pipeline_mode=single),
            pl.BlockSpec((1, D), const),
            pl.BlockSpec((1, D), const),
        ],
        out_specs=pl.BlockSpec((tm, D), lambda i: (i, 0)),
        out_shape=jax.ShapeDtypeStruct((Tp + Ts, D), F32),
        compiler_params=_cparams(("arbitrary",)),
        name="mix",
    )(xp, xs, yr_p, yr_s, yd_p, yd_s, P, P, ln_in_g, ln_in_b, b_gate, wpr, wpd, wo, g1, b1)


def _mem_kernel(h1_ref, mkp_ref, mvp_ref, mks_ref, mvs_ref, wq_ref, wo_ref, g2_ref, b2_ref, rw_ref,
                rb_ref, h2_ref, route_ref, counts_ref, q_scr, o_scr, cnt_scr, *, n_p, alpha, hd):
    i = pl.program_id(0)
    tm = h1_ref.shape[0]
    h1 = h1_ref[...]
    q_scr[...] = (_dot(h1.astype(BF16), wq_ref[...]) * (hd ** -0.5)).astype(BF16)

    def attend(mk_ref, mv_ref):
        n_sub = mk_ref.shape[0]
        seg = tm // n_sub
        for s in range(n_sub):
            for h in range(M_HEADS):
                if len(mk_ref.shape) == 4:
                    mk = mk_ref[s, :, h, :].astype(BF16)
                    mv = mv_ref[s, :, h, :].astype(BF16)
                else:
                    mk = mk_ref[s, :, h * hd:(h + 1) * hd].astype(BF16)
                    mv = mv_ref[s, :, h * hd:(h + 1) * hd].astype(BF16)
                qh = q_scr[s * seg:(s + 1) * seg, h * hd:(h + 1) * hd]
                sc = _dot_nt(qh, mk)
                sc = sc - jnp.max(sc, axis=-1, keepdims=True)
                p = jnp.exp(sc)
                p = p / jnp.sum(p, axis=-1, keepdims=True)
                o_scr[s * seg:(s + 1) * seg, h * hd:(h + 1) * hd] = _dot(
                    p.astype(BF16), mv).astype(BF16)

    @pl.when(i < n_p)
    def _():
        attend(mkp_ref, mvp_ref)

    @pl.when(i >= n_p)
    def _():
        attend(mks_ref, mvs_ref)

    h2 = _layer_norm(alpha * h1 + _dot(o_scr[...], wo_ref[...]), g2_ref[...], b2_ref[...])
    h2_ref[...] = h2
    h2b = h2.astype(BF16)

    h_lo = (h2 - h2b.astype(F32)).astype(BF16)
    rw = rw_ref[...]
    rw_hi = rw.astype(BF16)
    rw_lo = (rw - rw_hi.astype(F32)).astype(BF16)
    logits = _dot(h2b, rw_hi) + _dot(h2b, rw_lo) + _dot(h_lo, rw_hi) + rb_ref[...]

    n_e = logits.shape[1]
    eidx = lax.broadcasted_iota(jnp.int32, logits.shape, 1).astype(F32)
    lane = lax.broadcasted_iota(jnp.int32, (tm, LANES), 1)
    route = jnp.zeros((tm, LANES), F32)
    work = logits
    vals = []
    sels = []
    for k in range(TOP_K):
        mx = jnp.max(work, axis=-1, keepdims=True)
        sel = jnp.min(jnp.where(work == mx, eidx, float(n_e)), axis=-1, keepdims=True)
        work = jnp.where(eidx == sel, -jnp.inf, work)
        vals.append(mx)
        sels.append(sel)
        route = jnp.where(lane == TOP_K + k, sel, route)
    ex = [jnp.exp(v - vals[0]) for v in vals]
    den = ex[0] + ex[1] + ex[2] + ex[3]
    for k in range(TOP_K):
        route = jnp.where(lane == k, ex[k] / den, route)

    @pl.when(i == 0)
    def _():
        cnt_scr[...] = jnp.zeros_like(cnt_scr)

    lane_f = lane.astype(F32)
    hit = [lane_f == sels[k] for k in range(TOP_K)]
    cnt = sum(h.astype(F32) for h in hit)
    r_io = lax.broadcasted_iota(jnp.int32, (tm, tm), 0)
    c_io = lax.broadcasted_iota(jnp.int32, (tm, tm), 1)
    ltri = jnp.where(r_io > c_io, 1.0, 0.0).astype(BF16)
    excl = _dot(ltri, cnt.astype(BF16)) + cnt_scr[...]
    for k in range(TOP_K):
        rank = jnp.sum(jnp.where(hit[k], excl, 0.0), axis=-1, keepdims=True)
        route = jnp.where(lane == 2 * TOP_K + k, rank, route)
    route_ref[...] = route
    cnt_scr[...] = cnt_scr[...] + jnp.sum(cnt, axis=0, keepdims=True)
    counts_ref[...] = cnt_scr[...]


def _mem_attn(h1, t_prompt, mem_kp, mem_vp, l_prompt, mem_ks, mem_vs, l_sample, wq, wo, g2, b2, rw, rb,
              alpha):
    T, D = h1.shape
    hd = 128
    tm = MEM_TM
    assert l_prompt % tm == 0 and tm % l_sample == 0
    n_p = t_prompt // tm
    n_s = (T - t_prompt) // tm
    per_b = l_prompt // tm
    sub_s = tm // l_sample
    n_mem = mem_kp.shape[1]
    bp = mem_kp.shape[0]
    const = lambda i: (0, 0)
    single = pl.Buffered(1)
    mem_p = pl.BlockSpec((1, n_mem, M_HEADS * hd), lambda i: (jnp.minimum(i // per_b, bp - 1), 0, 0))
    mem_s = pl.BlockSpec((sub_s, n_mem, M_HEADS, hd), lambda i: (_hi(i, n_p), 0, 0, 0))
    kern = functools.partial(_mem_kernel, n_p=n_p, alpha=alpha, hd=hd)
    return pl.pallas_call(
        kern,
        grid=(n_p + n_s,),
        in_specs=[
            pl.BlockSpec((tm, D), lambda i: (i, 0)),
            mem_p, mem_p, mem_s, mem_s,
            pl.BlockSpec(wq.shape, const, pipeline_mode=single),
            pl.BlockSpec(wo.shape, const, pipeline_mode=single),
            pl.BlockSpec((1, D), const),
            pl.BlockSpec((1, D), const),
            pl.BlockSpec(rw.shape, const, pipeline_mode=single),
            pl.BlockSpec((1, rw.shape[1]), const),
        ],
        out_specs=[
            pl.BlockSpec((tm, D), lambda i: (i, 0)),
            pl.BlockSpec((tm, LANES), lambda i: (i, 0)),
            pl.BlockSpec((1, LANES), const),
        ],
        out_shape=[
            jax.ShapeDtypeStruct((T, D), F32),
            jax.ShapeDtypeStruct((T, LANES), F32),
            jax.ShapeDtypeStruct((1, LANES), F32),
        ],
        scratch_shapes=[
            pltpu.VMEM((tm, M_HEADS * hd), BF16),
            pltpu.VMEM((tm, M_HEADS * hd), BF16),
            pltpu.VMEM((1, LANES), F32),
        ],
        compiler_params=_cparams(("arbitrary",)),
        name="mem_attn",
    )(h1, mem_kp, mem_vp, mem_ks, mem_vs, wq, wo, g2, b2, rw, rb)


def _mem_kv_kernel(x_ref, w_ref, o_ref):
    o_ref[...] = _dot(x_ref[...].astype(BF16), w_ref[...])


def _mem_kv(mem, w):
    R, D = mem.shape
    N = w.shape[1]
    tm = 256
    return pl.pallas_call(
        _mem_kv_kernel,
        grid=(R // tm,),
        in_specs=[pl.BlockSpec((tm, D), lambda i: (i, 0)), pl.BlockSpec((D, N), lambda i: (0, 0))],
        out_specs=pl.BlockSpec((tm, N), lambda i: (i, 0)),
        out_shape=jax.ShapeDtypeStruct((R, N), F32),
        compiler_params=_cparams(("parallel",)),
        name="mem_kv",
    )(mem, w)


def _row_copy(src_hbm, dst, sem, src_row, dst_row):
    return pltpu.make_async_copy(src_hbm.at[pl.ds(src_row, 1)], dst.at[pl.ds(dst_row, 1)], sem)


def _dispatch_kernel(dest_ref, pend_ref, padded_ref, nu_ref, h_ref, xs_hbm, stage, zeros, sem, zsem,
                     *, tm, bm, n_steps, n_blocks):
    i = pl.program_id(0)
    slot = i % 2

    def zero_block(off):
        return pltpu.make_async_copy(zeros, xs_hbm.at[pl.ds(pl.multiple_of(off, bm), bm)], zsem)

    def zero_fill(act):
        for e in range(N_EXPERTS):
            @pl.when(padded_ref[e] > 0)
            def _(e=e):
                act(zero_block(pend_ref[e] - bm))
        for b in range(n_blocks - N_EXPERTS, n_blocks):
            @pl.when(b >= nu_ref[0])
            def _(b=b):
                act(zero_block(b * bm))

    @pl.when(i == 0)
    def _():
        zeros[...] = jnp.zeros_like(zeros)
        zero_fill(lambda c: c.start())
        zero_fill(lambda c: c.wait())

    def wait(s):
        for _ in range(TOP_K):
            pltpu.make_async_copy(stage.at[s], xs_hbm.at[pl.ds(0, tm)], sem.at[s]).wait()

    @pl.when(i >= 2)
    def _():
        wait(slot)

    stage[slot] = h_ref[...]

    def body(t, carry):
        for k in range(TOP_K):
            row = dest_ref[(i * tm + t) * TOP_K + k]
            pltpu.make_async_copy(stage.at[slot, pl.ds(t, 1)], xs_hbm.at[pl.ds(row, 1)],
                                  sem.at[slot]).start(priority=k % 2)
        return carry
    lax.fori_loop(0, tm, body, 0, unroll=4)

    @pl.when(i == n_steps - 1)
    def _():
        wait(slot)
        if n_steps > 1:
            wait(1 - slot)


def _moe_dispatch(h2, dest, pad_end, padded, n_used, n_rows, bm):
    T, D = h2.shape
    tm = GATHER_ROWS
    n_steps = T // tm
    kern = functools.partial(_dispatch_kernel, tm=tm, bm=bm, n_steps=n_steps, n_blocks=n_rows // bm)
    return pl.pallas_call(
        kern,
        grid_spec=pltpu.PrefetchScalarGridSpec(
            num_scalar_prefetch=4,
            grid=(n_steps,),
            in_specs=[pl.BlockSpec((tm, D), lambda i, d, pe, pd, nu: (i, 0))],
            out_specs=pl.BlockSpec(memory_space=pl.ANY),
            scratch_shapes=[
                pltpu.VMEM((2, tm, D), F32),
                pltpu.VMEM((bm, D), F32),
                pltpu.SemaphoreType.DMA((2,)),
                pltpu.SemaphoreType.DMA(()),
            ],
        ),
        out_shape=jax.ShapeDtypeStruct((n_rows, D), F32),
        compiler_params=_cparams(("arbitrary",)),
        name="moe_dispatch",
    )(dest, pad_end, padded, n_used, h2)


def _expert_weights(be_ref, first_ref, next_ref, cnt_ref, copies, cast, *, p, m, n_pass):
    @pl.when(jnp.logical_and(p == 0, m == 0))
    def _():
        cnt_ref[0] = 0

    @pl.when(first_ref[m] == 1)
    def _():
        seg = cnt_ref[0]
        slot = seg % 2

        @pl.when(seg == 0)
        def _():
            for c in copies(be_ref[0], 0, 0):
                c.start()

        for c in copies(be_ref[m], p, slot):
            c.wait()
        cast(slot)
        ne = next_ref[m]

        @pl.when(ne >= 0)
        def _():
            for c in copies(ne, p, 1 - slot):
                c.start(priority=1)

        @pl.when(jnp.logical_and(ne < 0, p + 1 < n_pass))
        def _():
            for c in copies(be_ref[0], p + 1, 1 - slot):
                c.start(priority=1)

        cnt_ref[0] = seg + 1


def _moe_up_kernel(be_ref, nu_ref, first_ref, next_ref, x_ref, w1_hbm, bg_ref, bl_ref, a_ref,
                   stage, wg_scr, wl_scr, sem, cnt_ref, *, n_pass):
    f = pl.program_id(0)
    m = pl.program_id(1)
    tn = wg_scr.shape[1]
    ff = w1_hbm.shape[2] // 2

    def copies(e, p, slot):
        off = pl.multiple_of(p * tn, tn)
        return [
            pltpu.make_async_copy(w1_hbm.at[e, :, pl.ds(off, tn)], stage.at[slot, 0], sem.at[slot]),
            pltpu.make_async_copy(w1_hbm.at[e, :, pl.ds(ff + off, tn)], stage.at[slot, 1], sem.at[slot]),
        ]

    def cast(slot):
        wg_scr[...] = stage[slot, 0].astype(BF16)
        wl_scr[...] = stage[slot, 1].astype(BF16)

    _expert_weights(be_ref, first_ref, next_ref, cnt_ref, copies, cast, p=f, m=m, n_pass=n_pass)

    @pl.when(m < nu_ref[0])
    def _():
        x = x_ref[...].astype(BF16)
        u_glu = jnp.minimum(_dot(x, wg_scr[...]) + bg_ref[0], SWIGLU_LIMIT)
        u_lin = jnp.clip(_dot(x, wl_scr[...]) + bl_ref[0], -SWIGLU_LIMIT, SWIGLU_LIMIT)
        a = u_glu * jax.nn.sigmoid(SWIGLU_ALPHA * u_glu) * (u_lin + 1.0)
        a_ref[...] = a.astype(BF16)

    @pl.when(m >= nu_ref[0])
    def _():
        a_ref[...] = jnp.zeros_like(a_ref)


def _moe_up(xs, w1, b1, plan):
    n_rows, D = xs.shape
    F = w1.shape[2] // 2
    bm, tn = MOE_BM, MOE_TN1
    nb = n_rows // bm
    nf = F // tn
    return pl.pallas_call(
        functools.partial(_moe_up_kernel, n_pass=nf),
        grid_spec=pltpu.PrefetchScalarGridSpec(
            num_scalar_prefetch=4,
            grid=(nf, nb),
            in_specs=[
                pl.BlockSpec((bm, D), lambda f, m, be, nu, fi, nx: (jnp.minimum(m, nu[0] - 1), 0)),
                pl.BlockSpec(memory_space=pl.ANY),
                pl.BlockSpec((1, 1, tn), lambda f, m, be, nu, fi, nx: (be[m], 0, f)),
                pl.BlockSpec((1, 1, tn), lambda f, m, be, nu, fi, nx: (be[m], 0, nf + f)),
            ],
            out_specs=pl.BlockSpec((bm, tn), lambda f, m, be, nu, fi, nx: (m, f)),
            scratch_shapes=[
                pltpu.VMEM((2, 2, D, tn), F32),
                pltpu.VMEM((D, tn), BF16),
                pltpu.VMEM((D, tn), BF16),
                pltpu.SemaphoreType.DMA((2,)),
                pltpu.SMEM((1,), jnp.int32),
            ],
        ),
        out_shape=jax.ShapeDtypeStruct((n_rows, F), BF16),
        compiler_params=_cparams(("arbitrary", "arbitrary")),
        name="moe_up",
    )(*plan, xs, w1, b1, b1)


def _moe_down_kernel(be_ref, nu_ref, first_ref, next_ref, a_ref, w2_hbm, b_ref, y_ref,
                     stage, w_scr, sem, cnt_ref, *, n_pass):
    d = pl.program_id(0)
    m = pl.program_id(1)
    tn = w_scr.shape[1]

    def copies(e, p, slot):
        off = pl.multiple_of(p * tn, tn)
        return [pltpu.make_async_copy(w2_hbm.at[e, :, pl.ds(off, tn)], stage.at[slot], sem.at[slot])]

    def cast(slot):
        w_scr[...] = stage[slot].astype(BF16)

    _expert_weights(be_ref, first_ref, next_ref, cnt_ref, copies, cast, p=d, m=m, n_pass=n_pass)

    @pl.when(m < nu_ref[0])
    def _():
        y_ref[...] = _dot(a_ref[...], w_scr[...]) + b_ref[0]

    @pl.when(m >= nu_ref[0])
    def _():
        y_ref[...] = jnp.zeros_like(y_ref)


def _moe_down(a, w2, b2, plan):
    n_rows, F = a.shape
    D = w2.shape[2]
    bm, tn = MOE_BM, MOE_TN2
    nb = n_rows // bm
    nd = D // tn
    return pl.pallas_call(
        functools.partial(_moe_down_kernel, n_pass=nd),
        grid_spec=pltpu.PrefetchScalarGridSpec(
            num_scalar_prefetch=4,
            grid=(nd, nb),
            in_specs=[
                pl.BlockSpec((bm, F), lambda d, m, be, nu, fi, nx: (jnp.minimum(m, nu[0] - 1), 0)),
                pl.BlockSpec(memory_space=pl.ANY),
                pl.BlockSpec((1, 1, tn), lambda d, m, be, nu, fi, nx: (be[m], 0, d)),
            ],
            out_specs=pl.BlockSpec((bm, tn), lambda d, m, be, nu, fi, nx: (m, d)),
            scratch_shapes=[
                pltpu.VMEM((2, F, tn), F32),
                pltpu.VMEM((F, tn), BF16),
                pltpu.SemaphoreType.DMA((2,)),
                pltpu.SMEM((1,), jnp.int32),
            ],
        ),
        out_shape=jax.ShapeDtypeStruct((n_rows, D), F32),
        compiler_params=_cparams(("arbitrary", "arbitrary")),
        name="moe_down",
    )(*plan, a, w2, b2)


def _combine_kernel(dest_ref, h2_ref, rt_ref, g_ref, b_ref, ys_hbm, op_ref, os_ref, buf, sem,
                    *, tm, n_steps, n_prompt, alpha):
    i = pl.program_id(0)

    def issue(step, slot):
        def body(t, carry):
            for k in range(TOP_K):
                row = dest_ref[(step * tm + t) * TOP_K + k]
                _row_copy(ys_hbm, buf.at[slot, k], sem.at[slot], row, t).start(priority=k % 2)
            return carry
        lax.fori_loop(0, tm, body, 0, unroll=2)

    def wait(slot):
        for k in range(TOP_K):
            pltpu.make_async_copy(ys_hbm.at[pl.ds(0, tm)], buf.at[slot, k], sem.at[slot]).wait()

    slot = i % 2

    @pl.when(i == 0)
    def _():
        issue(0, 0)

    @pl.when(i + 1 < n_steps)
    def _():
        issue(i + 1, 1 - slot)

    wait(slot)
    y = rt_ref[:, 0:1] * buf[slot, 0]
    for k in range(1, TOP_K):
        y = y + rt_ref[:, k:k + 1] * buf[slot, k]
    out = _layer_norm(alpha * h2_ref[...] + y, g_ref[...], b_ref[...])

    @pl.when(i < n_prompt)
    def _():
        op_ref[...] = out

    @pl.when(i >= n_prompt)
    def _():
        os_ref[...] = out


def _moe_combine(ys, dest, h2, route, g3, b3, t_prompt, alpha):
    T, D = h2.shape
    tm = COMBINE_TM
    n_steps = T // tm
    n_prompt = t_prompt // tm
    kern = functools.partial(_combine_kernel, tm=tm, n_steps=n_steps, n_prompt=n_prompt, alpha=alpha)
    return pl.pallas_call(
        kern,
        grid_spec=pltpu.PrefetchScalarGridSpec(
            num_scalar_prefetch=1,
            grid=(n_steps,),
            in_specs=[
                pl.BlockSpec((tm, D), lambda i, d: (i, 0)),
                pl.BlockSpec((tm, LANES), lambda i, d: (i, 0)),
                pl.BlockSpec((1, D), lambda i, d: (0, 0)),
                pl.BlockSpec((1, D), lambda i, d: (0, 0)),
                pl.BlockSpec(memory_space=pl.ANY),
            ],
            out_specs=[
                pl.BlockSpec((tm, D), lambda i, d: (jnp.minimum(i, n_prompt - 1), 0)),
                pl.BlockSpec((tm, D), lambda i, d: (jnp.maximum(i - n_prompt, 0), 0)),
            ],
            scratch_shapes=[pltpu.VMEM((2, TOP_K, tm, D), F32), pltpu.SemaphoreType.DMA((2,))],
        ),
        out_shape=[
            jax.ShapeDtypeStruct((t_prompt, D), F32),
            jax.ShapeDtypeStruct((T - t_prompt, D), F32),
        ],
        compiler_params=_cparams(("arbitrary",)),
        name="moe_combine",
    )(dest, h2, route, g3, b3, ys)


def _moe_plan(route, counts, bm):
    T = route.shape[0]
    top_idx = route[:, TOP_K:2 * TOP_K].astype(jnp.int32)
    rank = route[:, 2 * TOP_K:3 * TOP_K].astype(jnp.int32)
    counts = counts[0, :N_EXPERTS].astype(jnp.int32)
    padded = (counts + bm - 1) // bm * bm
    pad_end = jnp.cumsum(padded)
    pad_start = pad_end - padded
    dest = (pad_start[top_idx] + rank).reshape(-1)
    n_blocks = -(-(T * TOP_K) // bm) + N_EXPERTS
    block_start = jnp.arange(n_blocks, dtype=jnp.int32) * bm
    block_e = jnp.sum((pad_end[None, :] <= block_start[:, None]).astype(jnp.int32), axis=1)
    block_e = jnp.minimum(block_e, N_EXPERTS - 1)
    n_used = (pad_end[-1] // bm).astype(jnp.int32).reshape(1)
    blk = jnp.arange(n_blocks, dtype=jnp.int32)
    prev_e = jnp.concatenate([jnp.full((1,), -1, jnp.int32), block_e[:-1]])
    first = jnp.logical_and(blk < n_used[0], block_e != prev_e)
    later = jnp.logical_and(first[None, :], blk[None, :] > blk[:, None])
    nxt_blk = jnp.min(jnp.where(later, blk[None, :], n_blocks), axis=1)
    next_e = jnp.where(nxt_blk < n_blocks, block_e[jnp.minimum(nxt_blk, n_blocks - 1)], -1)
    plan = (block_e, n_used, first.astype(jnp.int32), next_e.astype(jnp.int32))
    return dest, pad_end, padded, plan, n_blocks * bm


def kernel(x_prompt, x_sample, cache_diff_k, cache_diff_v, state_ret, cache_mem_k, cache_mem_v, mem_prompt, ln_in_g, ln_in_b, w_in, b_gate, diff_lambda, diff_subln, w_proj_ret, w_proj_diff, w_out, ln1_g, ln1_b, w_mq, w_mk, w_mv, w_mo, ln2_g, ln2_b, router_w, router_b, w1, b1, w2, b2, ln3_g, ln3_b):
    Bp, Lp, D = x_prompt.shape
    Bs, Ls, _ = x_sample.shape
    depth = w_in.shape[0]
    assert depth == 1
    past = cache_diff_k.shape[2]
    n_mem = mem_prompt.shape[1]
    Tp, Ts = Bp * Lp, Bs * Ls
    alpha = (2.0 * depth) ** 0.25
    lambda_init = 0.8 - 0.6 * math.exp(-0.3 * 0)
    row = lambda v: v.reshape(1, -1)

    xp = x_prompt.reshape(Tp, D)
    xs_in = x_sample.reshape(Ts, D)
    kv_blk = 4096 // IN_TN
    P, dk_p, dv_p, dk_s, dv_s = _in_proj(xp, xs_in, row(ln_in_g), row(ln_in_b), w_in[0].astype(BF16),
                                         kv_blk)

    zeros_state = jnp.zeros((Bp,) + state_ret.shape[2:], F32)
    yr_p, s_p = _retention(P, 0, Bp, Lp, 0, zeros_state)
    yr_s, s_s = _retention(P, Tp, Bs, Ls, past, state_ret[0])

    yd_p = _attn_prompt(P, Bp, Lp, diff_lambda[0], row(diff_subln[0]), lambda_init)
    yd_s = _attn_sample(P, Tp, Bs, Ls, cache_diff_k[0], cache_diff_v[0], diff_lambda[0],
                        row(diff_subln[0]), lambda_init)

    h1 = _mix(xp, xs_in, yr_p, yr_s, yd_p, yd_s, P, row(ln_in_g), row(ln_in_b), row(b_gate[0]),
              w_proj_ret[0].astype(BF16), w_proj_diff[0].astype(BF16), w_out[0].astype(BF16),
              row(ln1_g[0]), row(ln1_b[0]), alpha)

    w_mkv = jnp.concatenate([w_mk[0], w_mv[0]], axis=1).astype(BF16)
    mkv = _mem_kv(mem_prompt.reshape(Bp * n_mem, D), w_mkv)
    hm = w_mk.shape[2]
    mk_p = mkv[:, :hm].reshape(Bp, n_mem, hm)
    mv_p = mkv[:, hm:].reshape(Bp, n_mem, hm)
    h2, route, counts = _mem_attn(
        h1, Tp, mk_p, mv_p, Lp, cache_mem_k[0], cache_mem_v[0], Ls, w_mq[0].astype(BF16),
        w_mo[0].astype(BF16),
        row(ln2_g[0]), row(ln2_b[0]), router_w[0], row(router_b[0]), alpha)

    dest, pad_end, padded, plan, n_rows = _moe_plan(route, counts, MOE_BM)
    xs = _moe_dispatch(h2, dest, pad_end, padded, plan[1], n_rows, MOE_BM)
    act = _moe_up(xs, w1[0], b1[0][:, None, :], plan)
    ys = _moe_down(act, w2[0], b2[0][:, None, :], plan)
    out_p, out_s = _moe_combine(ys, dest, h2, route, row(ln3_g[0]), row(ln3_b[0]), Tp, alpha)

    return (
        out_p.reshape(Bp, Lp, D),
        out_s.reshape(Bs, Ls, D),
        dk_p.reshape(1, Bp, Lp, D_HEADS, 128),
        dv_p.reshape(1, Bp, Lp, D_HEADS, 128),
        s_p[None],
        mk_p.reshape(1, Bp, n_mem, M_HEADS, hm // M_HEADS),
        mv_p.reshape(1, Bp, n_mem, M_HEADS, hm // M_HEADS),
        dk_s.reshape(1, Bs, Ls, D_HEADS, 128),
        dv_s.reshape(1, Bs, Ls, D_HEADS, 128),
        s_s[None],
    )
```

```python
import functools
import math

import jax
import jax.numpy as jnp
from jax import lax
from jax.experimental import pallas as pl
from jax.experimental.pallas import tpu as pltpu

F32 = jnp.float32
BF16 = jnp.bfloat16

CHUNK = 64
R_HEADS = 4
D_HEADS = 8
M_HEADS = 4
N_EXPERTS = 32
TOP_K = 4
SWIGLU_ALPHA = 1.702
SWIGLU_LIMIT = 7.0
LN_EPS = 1e-5
RMS_EPS = 1e-5
ROPE_BASE = 10000.0
LOG2E = 1.4426950408889634

LANES = 128
VMEM_LIMIT = 58 * 1024 * 1024

IN_TM = 512
IN_TN = 2048
RET_CHUNK = 256
ATT_BLK = 512
ATT_UNROLL = 8
MIX_TM = 256
MEM_TM = 512
MOE_BM = 256
MOE_TN1 = 1024
MOE_TN2 = 2048
GATHER_ROWS = 256
COMBINE_TM = 128


def _cparams(sem):
    return pltpu.CompilerParams(dimension_semantics=sem, vmem_limit_bytes=VMEM_LIMIT)


def _layer_norm(x, g, b):
    mu = jnp.mean(x, axis=-1, keepdims=True)
    xc = x - mu
    var = jnp.mean(xc * xc, axis=-1, keepdims=True)
    return xc * lax.rsqrt(var + LN_EPS) * g + b


def _dot(a, b):
    return jnp.dot(a, b, preferred_element_type=F32)


def _dot_nt(a, b):
    return lax.dot_general(a, b, (((1,), (1,)), ((), ())), preferred_element_type=F32)


def _dot_tn(a, b):
    return lax.dot_general(a, b, (((0,), (0,)), ((), ())), preferred_element_type=F32)


def _lo(i, n):
    return jnp.minimum(i, n - 1)


def _hi(i, n):
    return jnp.maximum(i - n, 0)


def _in_proj_kernel(xp_ref, xs_ref, g_ref, b_ref, w_ref, p_ref, dkp_ref, dvp_ref, dks_ref, dvs_ref,
                    h_scr, *, n_p):
    i = pl.program_id(0)
    j = pl.program_id(1)
    half = dkp_ref.shape[1]

    @pl.when(jnp.logical_and(j == 0, i < n_p))
    def _():
        h_scr[...] = _layer_norm(xp_ref[...], g_ref[...], b_ref[...]).astype(BF16)

    @pl.when(jnp.logical_and(j == 0, i >= n_p))
    def _():
        h_scr[...] = _layer_norm(xs_ref[...], g_ref[...], b_ref[...]).astype(BF16)

    acc = _dot(h_scr[...], w_ref[...])
    p_ref[...] = acc.astype(BF16)

    @pl.when(jnp.logical_and(j == 0, i < n_p))
    def _():
        dkp_ref[...] = acc[:, :half]
        dvp_ref[...] = acc[:, half:]

    @pl.when(jnp.logical_and(j == 0, i >= n_p))
    def _():
        dks_ref[...] = acc[:, :half]
        dvs_ref[...] = acc[:, half:]


def _in_proj(xp, xs, g, b, w, kv_blk):
    Tp, D = xp.shape
    Ts = xs.shape[0]
    N = w.shape[1]
    n_p, n_s = Tp // IN_TM, Ts // IN_TM
    half = IN_TN // 2

    def col(j):
        return jnp.where(j == 0, kv_blk, jnp.where(j <= kv_blk, j - 1, j))

    single = pl.Buffered(1)
    kv_p = pl.BlockSpec((IN_TM, half), lambda i, j: (_lo(i, n_p), 0))
    kv_s = pl.BlockSpec((IN_TM, half), lambda i, j: (_hi(i, n_p), 0), pipeline_mode=single)
    return pl.pallas_call(
        functools.partial(_in_proj_kernel, n_p=n_p),
        grid=(n_p + n_s, N // IN_TN),
        in_specs=[
            pl.BlockSpec((IN_TM, D), lambda i, j: (_lo(i, n_p), 0)),
            pl.BlockSpec((IN_TM, D), lambda i, j: (_hi(i, n_p), 0), pipeline_mode=single),
            pl.BlockSpec((1, D), lambda i, j: (0, 0)),
            pl.BlockSpec((1, D), lambda i, j: (0, 0)),
            pl.BlockSpec((D, IN_TN), lambda i, j: (0, col(j))),
        ],
        out_specs=[pl.BlockSpec((IN_TM, IN_TN), lambda i, j: (i, j)), kv_p, kv_p, kv_s, kv_s],
        out_shape=[
            jax.ShapeDtypeStruct((Tp + Ts, N), BF16),
            jax.ShapeDtypeStruct((Tp, half), F32),
            jax.ShapeDtypeStruct((Tp, half), F32),
            jax.ShapeDtypeStruct((Ts, half), F32),
            jax.ShapeDtypeStruct((Ts, half), F32),
        ],
        scratch_shapes=[pltpu.VMEM((IN_TM, D), BF16)],
        compiler_params=_cparams(("arbitrary", "arbitrary")),
        name="in_proj",
    )(xp, xs, g, b, w)


P_DK = 0
P_DV = 8
P_RQ = 16
P_RK = 20
P_RV = 24
P_RG = 32
P_DQ = 40
P_GATE = 48


def _retention_kernel(q_ref, k_ref, v_ref, g_ref, cos_ref, sin_ref, dm_ref, qd_ref, kd_ref,
                      bd_ref, s0_ref, y_ref, sout_ref, s_scr, *, n_chunks, dk, dv):
    c = pl.program_id(1)

    @pl.when(c == 0)
    def _():
        s_scr[...] = s0_ref[0]

    cos = cos_ref[...]
    sin = sin_ref[...]
    k_scale = dk ** -0.5
    for h in range(R_HEADS):
        q = q_ref[:, h * dk:(h + 1) * dk].astype(F32)
        k = k_ref[:, h * dk:(h + 1) * dk].astype(F32)
        v = v_ref[:, h * dv:(h + 1) * dv]
        g = g_ref[:, h * dv:(h + 1) * dv].astype(F32)
        q = q * cos + pltpu.roll(q, dk // 2, 1) * sin
        k = (k * cos + pltpu.roll(k, dk // 2, 1) * sin) * k_scale
        qb = q.astype(BF16)
        kb = k.astype(BF16)
        s = s_scr[h]
        scores = _dot_nt(qb, kb) * dm_ref[h]
        o = _dot(scores.astype(BF16), v) + _dot((q * qd_ref[h]).astype(BF16), s.astype(BF16))
        s_scr[h] = bd_ref[h] * s + _dot_tn((k * kd_ref[h]).astype(BF16), v)
        o = o * lax.rsqrt(jnp.mean(o * o, axis=-1, keepdims=True) + RMS_EPS)
        y_ref[:, h * dv:(h + 1) * dv] = (g * jax.nn.sigmoid(g) * o).astype(BF16)

    @pl.when(c == n_chunks - 1)
    def _():
        sout_ref[0] = s_scr[...]


def _retention(P, row_off, B, L, pos0, S0):
    dk, dv = 128, 256
    C = min(L, RET_CHUNK)
    nc = L // C
    ob = row_off // C
    pos = (pos0 + jnp.arange(L, dtype=jnp.int32)).astype(F32)
    inv = 1.0 / (ROPE_BASE ** jnp.linspace(0.0, 1.0, dk // 2, dtype=F32))
    ang = pos[:, None] * inv[None, :]
    cos = jnp.concatenate([jnp.cos(ang), jnp.cos(ang)], axis=-1)
    sin = jnp.concatenate([-jnp.sin(ang), jnp.sin(ang)], axis=-1)
    log_g = jnp.log1p(-jnp.power(2.0, -5.0 - jnp.arange(R_HEADS, dtype=F32)))
    i = jnp.arange(C, dtype=F32)
    rel = i[:, None] - i[None, :]
    dmask = jnp.where(rel >= 0, jnp.exp(log_g[:, None, None] * jnp.maximum(rel, 0.0)), 0.0)
    q_decay = jnp.exp(log_g[:, None] * (i + 1.0))[..., None]
    k_decay = jnp.exp(log_g[:, None] * (C - 1.0 - i))[..., None]
    b_decay = jnp.exp(log_g * C)[:, None, None]

    def rows(b, c):
        return ob + b * nc + c

    kern = functools.partial(_retention_kernel, n_chunks=nc, dk=dk, dv=dv)
    return pl.pallas_call(
        kern,
        grid=(B, nc),
        in_specs=[
            pl.BlockSpec((C, R_HEADS * dk), lambda b, c: (rows(b, c), P_RQ * LANES // (R_HEADS * dk))),
            pl.BlockSpec((C, R_HEADS * dk), lambda b, c: (rows(b, c), P_RK * LANES // (R_HEADS * dk))),
            pl.BlockSpec((C, R_HEADS * dv), lambda b, c: (rows(b, c), P_RV * LANES // (R_HEADS * dv))),
            pl.BlockSpec((C, R_HEADS * dv), lambda b, c: (rows(b, c), P_RG * LANES // (R_HEADS * dv))),
            pl.BlockSpec((C, dk), lambda b, c: (c, 0)),
            pl.BlockSpec((C, dk), lambda b, c: (c, 0)),
            pl.BlockSpec((R_HEADS, C, C), lambda b, c: (0, 0, 0)),
            pl.BlockSpec((R_HEADS, C, 1), lambda b, c: (0, 0, 0)),
            pl.BlockSpec((R_HEADS, C, 1), lambda b, c: (0, 0, 0)),
            pl.BlockSpec((R_HEADS, 1, 1), lambda b, c: (0, 0, 0)),
            pl.BlockSpec((1, R_HEADS, dk, dv), lambda b, c: (b, 0, 0, 0)),
        ],
        out_specs=[
            pl.BlockSpec((C, R_HEADS * dv), lambda b, c: (b * nc + c, 0)),
            pl.BlockSpec((1, R_HEADS, dk, dv), lambda b, c: (b, 0, 0, 0)),
        ],
        out_shape=[
            jax.ShapeDtypeStruct((B * L, R_HEADS * dv), BF16),
            jax.ShapeDtypeStruct((B, R_HEADS, dk, dv), F32),
        ],
        scratch_shapes=[pltpu.VMEM((R_HEADS, dk, dv), F32)],
        compiler_params=_cparams(("parallel", "arbitrary")),
        name="retention",
    )(P, P, P, P, cos, sin, dmask, q_decay, k_decay, b_decay, S0)


def _diff_lambda(lam_ref, lambda_init):
    lv = lam_ref[...]
    a = jnp.sum(lv[0:1] * lv[1:2], axis=-1, keepdims=True)
    b = jnp.sum(lv[2:3] * lv[3:4], axis=-1, keepdims=True)
    return jnp.exp(a) - jnp.exp(b) + lambda_init


def _lane_tile(x, n):
    return jnp.concatenate([x] * n, axis=1)


def _stack_maps(q, hd):
    lane = lax.broadcasted_iota(jnp.int32, q.shape, 1)
    zero = jnp.zeros_like(q)
    return jnp.concatenate([jnp.where(lane < hd, q, zero), jnp.where(lane < hd, zero, q)], axis=0)


def _diff_finish(acc, l, lam, subln, lambda_init, tq):
    o = acc[:tq] / l[:tq] - lam * (acc[tq:] / l[tq:])
    o = o * lax.rsqrt(jnp.mean(o * o, axis=-1, keepdims=True) + RMS_EPS)
    return (o * subln * (1.0 - lambda_init)).astype(BF16)


def _attn_prompt_kernel(q_ref, k_ref, v_ref, lam_ref, sub_ref, y_ref, qs_scr, ve_scr, m_scr, acc_scr,
                        s_scr, *, blk, hd, lambda_init):
    qi = pl.program_id(2)
    dv = v_ref.shape[1]
    seq = v_ref.shape[0]

    @pl.when(qi == 0)
    def _():
        ve_scr[:, :dv] = v_ref[...]
        ve_scr[:, dv:] = jnp.ones((seq, dv), BF16)

    qs_scr[...] = _stack_maps((q_ref[...].astype(F32) * (hd ** -0.5 * LOG2E)).astype(BF16), hd)
    m_scr[...] = jnp.full_like(m_scr, -jnp.inf)
    acc_scr[...] = jnp.zeros_like(acc_scr)

    def scores(ki, slot):
        off = pl.multiple_of(ki * blk, blk)
        s_scr[slot] = _dot_nt(qs_scr[...], k_ref[pl.ds(off, blk), :])

    def step(ki, slot, masked):
        off = pl.multiple_of(ki * blk, blk)
        s = s_scr[slot]
        if masked:
            row = lax.broadcasted_iota(jnp.int32, s.shape, 0)
            col = lax.broadcasted_iota(jnp.int32, s.shape, 1)
            qrow = jnp.where(row >= blk, row - blk, row)
            shift = CHUNK.bit_length() - 1
            s = jnp.where((col >> shift) <= (qrow >> shift), s, -1e30)
        m_prev = m_scr[...]
        m_new = jnp.maximum(m_prev, jnp.max(s, axis=-1, keepdims=True))
        alpha = jnp.exp2(m_prev - m_new)
        p = jnp.exp2(s - _lane_tile(m_new, blk // LANES))
        pv = _dot(p.astype(BF16), ve_scr[pl.ds(off, blk), :])
        acc_scr[...] = _lane_tile(alpha, 2 * dv // LANES) * acc_scr[...] + pv
        m_scr[...] = m_new

    def run(base, width):
        for u in range(width):
            scores(base + u + 1, (u + 1) % 2)
            step(base + u, u % 2, False)

    def body(j, carry):
        run(ATT_UNROLL * j, ATT_UNROLL)
        return carry

    scores(0, 0)
    n_main = qi // ATT_UNROLL
    lax.fori_loop(0, n_main, body, 0)
    done = n_main * ATT_UNROLL
    width = ATT_UNROLL // 2
    while width >= 2:
        @pl.when(((qi - done) // width) % 2 == 1)
        def _(width=width):
            run(qi - (qi - done) % (2 * width), width)
        width //= 2

    @pl.when(qi % 2 == 1)
    def _():
        scores(qi, 1)
        step(qi - 1, 0, False)
        step(qi, 1, True)

    @pl.when(qi % 2 == 0)
    def _():
        step(qi, 0, True)

    lam = _diff_lambda(lam_ref, lambda_init)
    acc = acc_scr[...]
    o = acc[:blk, :dv] / acc[:blk, dv:] - lam * (acc[blk:, :dv] / acc[blk:, dv:])
    o = o * lax.rsqrt(jnp.mean(o * o, axis=-1, keepdims=True) + RMS_EPS)
    y_ref[...] = (o * sub_ref[...] * (1.0 - lambda_init)).astype(BF16)


def _attn_prompt(P, B, S, lam_p, subln, lambda_init):
    hd = 64
    blk = min(ATT_BLK, S)
    nq = S // blk
    kern = functools.partial(_attn_prompt_kernel, blk=blk, hd=hd, lambda_init=lambda_init)
    return pl.pallas_call(
        kern,
        grid=(B, D_HEADS, nq),
        in_specs=[
            pl.BlockSpec((blk, 2 * hd), lambda b, h, i: (b * nq + i, P_DQ + h)),
            pl.BlockSpec((S, 2 * hd), lambda b, h, i: (b, P_DK + h)),
            pl.BlockSpec((S, 2 * hd), lambda b, h, i: (b, P_DV + h)),
            pl.BlockSpec((4, hd), lambda b, h, i: (0, 0)),
            pl.BlockSpec((1, 2 * hd), lambda b, h, i: (0, 0)),
        ],
        out_specs=pl.BlockSpec((blk, 2 * hd), lambda b, h, i: (b * nq + i, h)),
        out_shape=jax.ShapeDtypeStruct((B * S, D_HEADS * 2 * hd), BF16),
        scratch_shapes=[
            pltpu.VMEM((2 * blk, 2 * hd), BF16),
            pltpu.VMEM((S, 4 * hd), BF16),
            pltpu.VMEM((2 * blk, LANES), F32),
            pltpu.VMEM((2 * blk, 4 * hd), F32),
            pltpu.VMEM((2, 2 * blk, blk), F32),
        ],
        compiler_params=_cparams(("parallel", "parallel", "arbitrary")),
        name="attn_prompt",
    )(P, P, P, lam_p, subln)


def _attn_sample_kernel(q_ref, kn_ref, vn_ref, kc_ref, vc_ref, lam_ref, sub_ref, y_ref,
                        *, tq, hd, lambda_init):
    lam = _diff_lambda(lam_ref, lambda_init)
    w = 2 * hd
    past = kc_ref.shape[1] // D_HEADS
    for h in range(D_HEADS):
        cols = slice(h * w, (h + 1) * w)
        qs = _stack_maps(q_ref[:, cols] * (hd ** -0.5), hd)
        kc = kc_ref[0, pl.ds(h, past, stride=D_HEADS), :].astype(BF16)
        vc = vc_ref[0, pl.ds(h, past, stride=D_HEADS), :].astype(BF16)
        kn = kn_ref[:, cols]
        vn = vn_ref[:, cols]
        s_c = _dot_nt(qs, kc)
        s_n = _dot_nt(qs, kn)
        m = jnp.maximum(jnp.max(s_c, axis=-1, keepdims=True), jnp.max(s_n, axis=-1, keepdims=True))
        p_c = jnp.exp(s_c - m)
        p_n = jnp.exp(s_n - m)
        l = jnp.sum(p_c, axis=-1, keepdims=True) + jnp.sum(p_n, axis=-1, keepdims=True)
        acc = _dot(p_c.astype(BF16), vc) + _dot(p_n.astype(BF16), vn)
        y_ref[:, cols] = _diff_finish(acc, l, lam, sub_ref[...], lambda_init, tq)


def _attn_sample(P, row_off, B, L, cache_k, cache_v, lam_p, subln, lambda_init):
    hd = 64
    past = cache_k.shape[1]
    ob = row_off // L
    width = D_HEADS * 2 * hd
    kern = functools.partial(_attn_sample_kernel, tq=L, hd=hd, lambda_init=lambda_init)
    cache_k = cache_k.reshape(B, past * D_HEADS, 2 * hd)
    cache_v = cache_v.reshape(B, past * D_HEADS, 2 * hd)
    cache_spec = pl.BlockSpec((1, past * D_HEADS, 2 * hd), lambda b: (b, 0, 0))
    return pl.pallas_call(
        kern,
        grid=(B,),
        in_specs=[
            pl.BlockSpec((L, width), lambda b: (ob + b, P_DQ * LANES // width)),
            pl.BlockSpec((L, width), lambda b: (ob + b, P_DK * LANES // width)),
            pl.BlockSpec((L, width), lambda b: (ob + b, P_DV * LANES // width)),
            cache_spec,
            cache_spec,
            pl.BlockSpec((4, hd), lambda b: (0, 0)),
            pl.BlockSpec((1, 2 * hd), lambda b: (0, 0)),
        ],
        out_specs=pl.BlockSpec((L, width), lambda b: (b, 0)),
        out_shape=jax.ShapeDtypeStruct((B * L, width), BF16),
        compiler_params=_cparams(("parallel",)),
        name="attn_sample",
    )(P, P, P, cache_k, cache_v, lam_p, subln)


def _mix_kernel(xp_ref, xs_ref, yrp_ref, yrs_ref, ydp_ref, yds_ref, gr_ref, gd_ref, lig_ref, lib_ref,
                bg_ref, wpr_ref, wpd_ref, wo_ref, g1_ref, b1_ref, h1_ref, *, alpha, n_p):
    i = pl.program_id(0)
    d = xp_ref.shape[1]

    def compute(x_ref, yr_ref, yd_ref):
        h0 = _layer_norm(x_ref[...], lig_ref[...], lib_ref[...])
        g_ret = jax.nn.sigmoid(gr_ref[...].astype(F32) + bg_ref[:, :d])
        g_diff = jax.nn.sigmoid(gd_ref[...].astype(F32) + bg_ref[:, d:])
        merged = g_ret * _dot(yr_ref[...], wpr_ref[...]) + g_diff * _dot(yd_ref[...], wpd_ref[...])
        mixed = _dot(merged.astype(BF16), wo_ref[...])
        h1_ref[...] = _layer_norm(alpha * h0 + mixed, g1_ref[...], b1_ref[...])

    @pl.when(i < n_p)
    def _():
        compute(xp_ref, yrp_ref, ydp_ref)

    @pl.when(i >= n_p)
    def _():
        compute(xs_ref, yrs_ref, yds_ref)


def _mix(xp, xs, yr_p, yr_s, yd_p, yd_s, P, ln_in_g, ln_in_b, b_gate, wpr, wpd, wo, g1, b1, alpha):
    Tp, D = xp.shape
    Ts = xs.shape[0]
    tm = MIX_TM
    n_p, n_s = Tp // tm, Ts // tm
    gcol = P_GATE * LANES // D
    const = lambda i: (0, 0)
    lo = lambda i: (_lo(i, n_p), 0)
    hi = lambda i: (_hi(i, n_p), 0)
    single = pl.Buffered(1)
    kern = functools.partial(_mix_kernel, alpha=alpha, n_p=n_p)
    return pl.pallas_call(
        kern,
        grid=(n_p + n_s,),
        in_specs=[
            pl.BlockSpec((tm, D), lo),
            pl.BlockSpec((tm, D), hi),
            pl.BlockSpec((tm, yr_p.shape[1]), lo),
            pl.BlockSpec((tm, yr_s.shape[1]), hi),
            pl.BlockSpec((tm, yd_p.shape[1]), lo),
            pl.BlockSpec((tm, yd_s.shape[1]), hi),
            pl.BlockSpec((tm, D), lambda i: (i, gcol)),
            pl.BlockSpec((tm, D), lambda i: (i, gcol + 1)),
            pl.BlockSpec((1, D), const),
            pl.BlockSpec((1, D), const),
            pl.BlockSpec((1, 2 * D), const),
            pl.BlockSpec(wpr.shape, const, pipeline_mode=single),
            pl.BlockSpec(wpd.shape, const, pipeline_mode=single),
            pl.BlockSpec(wo.shape, const, pipeline_mode=single),
            pl.BlockSpec((1, D), const),
            pl.BlockSpec((1, D), const),
        ],
        out_specs=pl.BlockSpec((tm, D), lambda i: (i, 0)),
        out_shape=jax.ShapeDtypeStruct((Tp + Ts, D), F32),
        compiler_params=_cparams(("arbitrary",)),
        name="mix",
    )(xp, xs, yr_p, yr_s, yd_p, yd_s, P, P, ln_in_g, ln_in_b, b_gate, wpr, wpd, wo, g1, b1)


def _mem_kernel(h1_ref, mkp_ref, mvp_ref, mks_ref, mvs_ref, wq_ref, wo_ref, g2_ref, b2_ref, rw_ref,
                rb_ref, h2_ref, route_ref, counts_ref, q_scr, o_scr, cnt_scr, *, n_p, alpha, hd):
    i = pl.program_id(0)
    tm = h1_ref.shape[0]
    h1 = h1_ref[...]
    q_scr[...] = (_dot(h1.astype(BF16), wq_ref[...]) * (hd ** -0.5)).astype(BF16)

    def attend(mk_ref, mv_ref):
        n_sub = mk_ref.shape[0]
        seg = tm // n_sub
        split_heads = mk_ref.shape[2] == hd
        for s in range(n_sub):
            for h in range(M_HEADS):
                if split_heads:
                    rows = pl.ds(h, mk_ref.shape[1] // M_HEADS, stride=M_HEADS)
                    mk = mk_ref[s, rows, :].astype(BF16)
                    mv = mv_ref[s, rows, :].astype(BF16)
                else:
                    mk = mk_ref[s, :, h * hd:(h + 1) * hd].astype(BF16)
                    mv = mv_ref[s, :, h * hd:(h + 1) * hd].astype(BF16)
                qh = q_scr[s * seg:(s + 1) * seg, h * hd:(h + 1) * hd]
                sc = _dot_nt(qh, mk)
                sc = sc - jnp.max(sc, axis=-1, keepdims=True)
                p = jnp.exp(sc)
                p = p / jnp.sum(p, axis=-1, keepdims=True)
                o_scr[s * seg:(s + 1) * seg, h * hd:(h + 1) * hd] = _dot(
                    p.astype(BF16), mv).astype(BF16)

    @pl.when(i < n_p)
    def _():
        attend(mkp_ref, mvp_ref)

    @pl.when(i >= n_p)
    def _():
        attend(mks_ref, mvs_ref)

    h2 = _layer_norm(alpha * h1 + _dot(o_scr[...], wo_ref[...]), g2_ref[...], b2_ref[...])
    h2_ref[...] = h2
    h2b = h2.astype(BF16)

    h_lo = (h2 - h2b.astype(F32)).astype(BF16)
    rw = rw_ref[...]
    rw_hi = rw.astype(BF16)
    rw_lo = (rw - rw_hi.astype(F32)).astype(BF16)
    logits = _dot(h2b, rw_hi) + _dot(h2b, rw_lo) + _dot(h_lo, rw_hi) + rb_ref[...]

    n_e = logits.shape[1]
    eidx = lax.broadcasted_iota(jnp.int32, logits.shape, 1).astype(F32)
    lane = lax.broadcasted_iota(jnp.int32, (tm, LANES), 1)
    route = jnp.zeros((tm, LANES), F32)
    work = logits
    vals = []
    sels = []
    for k in range(TOP_K):
        mx = jnp.max(work, axis=-1, keepdims=True)
        sel = jnp.min(jnp.where(work == mx, eidx, float(n_e)), axis=-1, keepdims=True)
        work = jnp.where(eidx == sel, -jnp.inf, work)
        vals.append(mx)
        sels.append(sel)
        route = jnp.where(lane == TOP_K + k, sel, route)
    ex = [jnp.exp(v - vals[0]) for v in vals]
    den = ex[0] + ex[1] + ex[2] + ex[3]
    for k in range(TOP_K):
        route = jnp.where(lane == k, ex[k] / den, route)

    @pl.when(i == 0)
    def _():
        cnt_scr[...] = jnp.zeros_like(cnt_scr)

    lane_f = lane.astype(F32)
    hit = [lane_f == sels[k] for k in range(TOP_K)]
    cnt = sum(h.astype(F32) for h in hit)
    r_io = lax.broadcasted_iota(jnp.int32, (tm, tm), 0)
    c_io = lax.broadcasted_iota(jnp.int32, (tm, tm), 1)
    ltri = jnp.where(r_io > c_io, 1.0, 0.0).astype(BF16)
    excl = _dot(ltri, cnt.astype(BF16)) + cnt_scr[...]
    for k in range(TOP_K):
        rank = jnp.sum(jnp.where(hit[k], excl, 0.0), axis=-1, keepdims=True)
        route = jnp.where(lane == 2 * TOP_K + k, rank, route)
    route_ref[...] = route
    cnt_scr[...] = cnt_scr[...] + jnp.sum(cnt, axis=0, keepdims=True)
    counts_ref[...] = cnt_scr[...]


def _mem_attn(h1, t_prompt, mem_kp, mem_vp, l_prompt, mem_ks, mem_vs, l_sample, wq, wo, g2, b2, rw, rb,
              alpha):
    T, D = h1.shape
    hd = 128
    tm = MEM_TM
    assert l_prompt % tm == 0 and tm % l_sample == 0
    n_p = t_prompt // tm
    n_s = (T - t_prompt) // tm
    per_b = l_prompt // tm
    sub_s = tm // l_sample
    n_mem = mem_kp.shape[1]
    bp = mem_kp.shape[0]
    const = lambda i: (0, 0)
    single = pl.Buffered(1)
    mem_p = pl.BlockSpec((1, n_mem, M_HEADS * hd), lambda i: (jnp.minimum(i // per_b, bp - 1), 0, 0))
    bs = mem_ks.shape[0]
    mem_ks = mem_ks.reshape(bs, n_mem * M_HEADS, hd)
    mem_vs = mem_vs.reshape(bs, n_mem * M_HEADS, hd)
    mem_s = pl.BlockSpec((sub_s, n_mem * M_HEADS, hd), lambda i: (_hi(i, n_p), 0, 0))
    kern = functools.partial(_mem_kernel, n_p=n_p, alpha=alpha, hd=hd)
    return pl.pallas_call(
        kern,
        grid=(n_p + n_s,),
        in_specs=[
            pl.BlockSpec((tm, D), lambda i: (i, 0)),
            mem_p, mem_p, mem_s, mem_s,
            pl.BlockSpec(wq.shape, const, pipeline_mode=single),
            pl.BlockSpec(wo.shape, const, pipeline_mode=single),
            pl.BlockSpec((1, D), const),
            pl.BlockSpec((1, D), const),
            pl.BlockSpec(rw.shape, const, pipeline_mode=single),
            pl.BlockSpec((1, rw.shape[1]), const),
        ],
        out_specs=[
            pl.BlockSpec((tm, D), lambda i: (i, 0)),
            pl.BlockSpec((tm, LANES), lambda i: (i, 0)),
            pl.BlockSpec((1, LANES), const),
        ],
        out_shape=[
            jax.ShapeDtypeStruct((T, D), F32),
            jax.ShapeDtypeStruct((T, LANES), F32),
            jax.ShapeDtypeStruct((1, LANES), F32),
        ],
        scratch_shapes=[
            pltpu.VMEM((tm, M_HEADS * hd), BF16),
            pltpu.VMEM((tm, M_HEADS * hd), BF16),
            pltpu.VMEM((1, LANES), F32),
        ],
        compiler_params=_cparams(("arbitrary",)),
        name="mem_attn",
    )(h1, mem_kp, mem_vp, mem_ks, mem_vs, wq, wo, g2, b2, rw, rb)


def _mem_kv_kernel(x_ref, w_ref, o_ref):
    o_ref[...] = _dot(x_ref[...].astype(BF16), w_ref[...])


def _mem_kv(mem, w):
    R, D = mem.shape
    N = w.shape[1]
    tm = 256
    return pl.pallas_call(
        _mem_kv_kernel,
        grid=(R // tm,),
        in_specs=[pl.BlockSpec((tm, D), lambda i: (i, 0)), pl.BlockSpec((D, N), lambda i: (0, 0))],
        out_specs=pl.BlockSpec((tm, N), lambda i: (i, 0)),
        out_shape=jax.ShapeDtypeStruct((R, N), F32),
        compiler_params=_cparams(("parallel",)),
        name="mem_kv",
    )(mem, w)


def _row_copy(src_hbm, dst, sem, src_row, dst_row):
    return pltpu.make_async_copy(src_hbm.at[pl.ds(src_row, 1)], dst.at[pl.ds(dst_row, 1)], sem)


def _dispatch_kernel(dest_ref, pend_ref, padded_ref, nu_ref, h_ref, xs_hbm, stage, zeros, sem, zsem,
                     *, tm, bm, n_steps, n_blocks):
    i = pl.program_id(0)
    slot = i % 2

    def zero_block(off):
        return pltpu.make_async_copy(zeros, xs_hbm.at[pl.ds(pl.multiple_of(off, bm), bm)], zsem)

    def zero_fill(act):
        for e in range(N_EXPERTS):
            @pl.when(padded_ref[e] > 0)
            def _(e=e):
                act(zero_block(pend_ref[e] - bm))
        for b in range(n_blocks - N_EXPERTS, n_blocks):
            @pl.when(b >= nu_ref[0])
            def _(b=b):
                act(zero_block(b * bm))

    @pl.when(i == 0)
    def _():
        zeros[...] = jnp.zeros_like(zeros)
        zero_fill(lambda c: c.start())
        zero_fill(lambda c: c.wait())

    def wait(s):
        for _ in range(TOP_K):
            pltpu.make_async_copy(stage.at[s], xs_hbm.at[pl.ds(0, tm)], sem.at[s]).wait()

    @pl.when(i >= 2)
    def _():
        wait(slot)

    stage[slot] = h_ref[...]

    def body(t, carry):
        for k in range(TOP_K):
            row = dest_ref[(i * tm + t) * TOP_K + k]
            pltpu.make_async_copy(stage.at[slot, pl.ds(t, 1)], xs_hbm.at[pl.ds(row, 1)],
                                  sem.at[slot]).start(priority=k % 2)
        return carry
    lax.fori_loop(0, tm, body, 0, unroll=4)

    @pl.when(i == n_steps - 1)
    def _():
        wait(slot)
        if n_steps > 1:
            wait(1 - slot)


def _moe_dispatch(h2, dest, pad_end, padded, n_used, n_rows, bm):
    T, D = h2.shape
    tm = GATHER_ROWS
    n_steps = T // tm
    kern = functools.partial(_dispatch_kernel, tm=tm, bm=bm, n_steps=n_steps, n_blocks=n_rows // bm)
    return pl.pallas_call(
        kern,
        grid_spec=pltpu.PrefetchScalarGridSpec(
            num_scalar_prefetch=4,
            grid=(n_steps,),
            in_specs=[pl.BlockSpec((tm, D), lambda i, d, pe, pd, nu: (i, 0))],
            out_specs=pl.BlockSpec(memory_space=pl.ANY),
            scratch_shapes=[
                pltpu.VMEM((2, tm, D), F32),
                pltpu.VMEM((bm, D), F32),
                pltpu.SemaphoreType.DMA((2,)),
                pltpu.SemaphoreType.DMA(()),
            ],
        ),
        out_shape=jax.ShapeDtypeStruct((n_rows, D), F32),
        compiler_params=_cparams(("arbitrary",)),
        name="moe_dispatch",
    )(dest, pad_end, padded, n_used, h2)


def _expert_weights(be_ref, first_ref, next_ref, cnt_ref, copies, cast, *, p, m, n_pass):
    @pl.when(jnp.logical_and(p == 0, m == 0))
    def _():
        cnt_ref[0] = 0

    @pl.when(first_ref[m] == 1)
    def _():
        seg = cnt_ref[0]
        slot = seg % 2

        @pl.when(seg == 0)
        def _():
            for c in copies(be_ref[0], 0, 0):
                c.start()

        for c in copies(be_ref[m], p, slot):
            c.wait()
        cast(slot)
        ne = next_ref[m]

        @pl.when(ne >= 0)
        def _():
            for c in copies(ne, p, 1 - slot):
                c.start(priority=1)

        @pl.when(jnp.logical_and(ne < 0, p + 1 < n_pass))
        def _():
            for c in copies(be_ref[0], p + 1, 1 - slot):
                c.start(priority=1)

        cnt_ref[0] = seg + 1


def _moe_up_kernel(be_ref, nu_ref, first_ref, next_ref, x_ref, w1_hbm, bg_ref, bl_ref, a_ref,
                   stage, wg_scr, wl_scr, sem, cnt_ref, *, n_pass):
    f = pl.program_id(0)
    m = pl.program_id(1)
    tn = wg_scr.shape[1]
    ff = w1_hbm.shape[2] // 2

    def copies(e, p, slot):
        off = pl.multiple_of(p * tn, tn)
        return [
            pltpu.make_async_copy(w1_hbm.at[e, :, pl.ds(off, tn)], stage.at[slot, 0], sem.at[slot]),
            pltpu.make_async_copy(w1_hbm.at[e, :, pl.ds(ff + off, tn)], stage.at[slot, 1], sem.at[slot]),
        ]

    def cast(slot):
        wg_scr[...] = stage[slot, 0].astype(BF16)
        wl_scr[...] = stage[slot, 1].astype(BF16)

    _expert_weights(be_ref, first_ref, next_ref, cnt_ref, copies, cast, p=f, m=m, n_pass=n_pass)

    @pl.when(m < nu_ref[0])
    def _():
        x = x_ref[...].astype(BF16)
        u_glu = jnp.minimum(_dot(x, wg_scr[...]) + bg_ref[0], SWIGLU_LIMIT)
        u_lin = jnp.clip(_dot(x, wl_scr[...]) + bl_ref[0], -SWIGLU_LIMIT, SWIGLU_LIMIT)
        a = u_glu * jax.nn.sigmoid(SWIGLU_ALPHA * u_glu) * (u_lin + 1.0)
        a_ref[...] = a.astype(BF16)

    @pl.when(m >= nu_ref[0])
    def _():
        a_ref[...] = jnp.zeros_like(a_ref)


def _moe_up(xs, w1, b1, plan):
    n_rows, D = xs.shape
    F = w1.shape[2] // 2
    bm, tn = MOE_BM, MOE_TN1
    nb = n_rows // bm
    nf = F // tn
    return pl.pallas_call(
        functools.partial(_moe_up_kernel, n_pass=nf),
        grid_spec=pltpu.PrefetchScalarGridSpec(
            num_scalar_prefetch=4,
            grid=(nf, nb),
            in_specs=[
                pl.BlockSpec((bm, D), lambda f, m, be, nu, fi, nx: (jnp.minimum(m, nu[0] - 1), 0)),
                pl.BlockSpec(memory_space=pl.ANY),
                pl.BlockSpec((1, 1, tn), lambda f, m, be, nu, fi, nx: (be[m], 0, f)),
                pl.BlockSpec((1, 1, tn), lambda f, m, be, nu, fi, nx: (be[m], 0, nf + f)),
            ],
            out_specs=pl.BlockSpec((bm, tn), lambda f, m, be, nu, fi, nx: (m, f)),
            scratch_shapes=[
                pltpu.VMEM((2, 2, D, tn), F32),
                pltpu.VMEM((D, tn), BF16),
                pltpu.VMEM((D, tn), BF16),
                pltpu.SemaphoreType.DMA((2,)),
                pltpu.SMEM((1,), jnp.int32),
            ],
        ),
        out_shape=jax.ShapeDtypeStruct((n_rows, F), BF16),
        compiler_params=_cparams(("arbitrary", "arbitrary")),
        name="moe_up",
    )(*plan, xs, w1, b1, b1)


def _moe_down_kernel(be_ref, nu_ref, first_ref, next_ref, a_ref, w2_hbm, b_ref, y_ref,
                     stage, w_scr, sem, cnt_ref, *, n_pass):
    d = pl.program_id(0)
    m = pl.program_id(1)
    tn = w_scr.shape[1]

    def copies(e, p, slot):
        off = pl.multiple_of(p * tn, tn)
        return [pltpu.make_async_copy(w2_hbm.at[e, :, pl.ds(off, tn)], stage.at[slot], sem.at[slot])]

    def cast(slot):
        w_scr[...] = stage[slot].astype(BF16)

    _expert_weights(be_ref, first_ref, next_ref, cnt_ref, copies, cast, p=d, m=m, n_pass=n_pass)

    @pl.when(m < nu_ref[0])
    def _():
        y_ref[...] = _dot(a_ref[...], w_scr[...]) + b_ref[0]

    @pl.when(m >= nu_ref[0])
    def _():
        y_ref[...] = jnp.zeros_like(y_ref)


def _moe_down(a, w2, b2, plan):
    n_rows, F = a.shape
    D = w2.shape[2]
    bm, tn = MOE_BM, MOE_TN2
    nb = n_rows // bm
    nd = D // tn
    return pl.pallas_call(
        functools.partial(_moe_down_kernel, n_pass=nd),
        grid_spec=pltpu.PrefetchScalarGridSpec(
            num_scalar_prefetch=4,
            grid=(nd, nb),
            in_specs=[
                pl.BlockSpec((bm, F), lambda d, m, be, nu, fi, nx: (jnp.minimum(m, nu[0] - 1), 0)),
                pl.BlockSpec(memory_space=pl.ANY),
                pl.BlockSpec((1, 1, tn), lambda d, m, be, nu, fi, nx: (be[m], 0, d)),
            ],
            out_specs=pl.BlockSpec((bm, tn), lambda d, m, be, nu, fi, nx: (m, d)),
            scratch_shapes=[
                pltpu.VMEM((2, F, tn), F32),
                pltpu.VMEM((F, tn), BF16),
                pltpu.SemaphoreType.DMA((2,)),
                pltpu.SMEM((1,), jnp.int32),
            ],
        ),
        out_shape=jax.ShapeDtypeStruct((n_rows, D), F32),
        compiler_params=_cparams(("arbitrary", "arbitrary")),
        name="moe_down",
    )(*plan, a, w2, b2)


def _combine_kernel(dest_ref, h2_ref, rt_ref, g_ref, b_ref, ys_hbm, op_ref, os_ref, buf, sem,
                    *, tm, n_steps, n_prompt, alpha):
    i = pl.program_id(0)

    def issue(step, slot):
        def body(t, carry):
            for k in range(TOP_K):
                row = dest_ref[(step * tm + t) * TOP_K + k]
                _row_copy(ys_hbm, buf.at[slot, k], sem.at[slot], row, t).start(priority=k % 2)
            return carry
        lax.fori_loop(0, tm, body, 0, unroll=2)

    def wait(slot):
        for k in range(TOP_K):
            pltpu.make_async_copy(ys_hbm.at[pl.ds(0, tm)], buf.at[slot, k], sem.at[slot]).wait()

    slot = i % 2

    @pl.when(i == 0)
    def _():
        issue(0, 0)

    @pl.when(i + 1 < n_steps)
    def _():
        issue(i + 1, 1 - slot)

    wait(slot)
    y = rt_ref[:, 0:1] * buf[slot, 0]
    for k in range(1, TOP_K):
        y = y + rt_ref[:, k:k + 1] * buf[slot, k]
    out = _layer_norm(alpha * h2_ref[...] + y, g_ref[...], b_ref[...])

    @pl.when(i < n_prompt)
    def _():
        op_ref[...] = out

    @pl.when(i >= n_prompt)
    def _():
        os_ref[...] = out


def _moe_combine(ys, dest, h2, route, g3, b3, t_prompt, alpha):
    T, D = h2.shape
    tm = COMBINE_TM
    n_steps = T // tm
    n_prompt = t_prompt // tm
    kern = functools.partial(_combine_kernel, tm=tm, n_steps=n_steps, n_prompt=n_prompt, alpha=alpha)
    return pl.pallas_call(
        kern,
        grid_spec=pltpu.PrefetchScalarGridSpec(
            num_scalar_prefetch=1,
            grid=(n_steps,),
            in_specs=[
                pl.BlockSpec((tm, D), lambda i, d: (i, 0)),
                pl.BlockSpec((tm, LANES), lambda i, d: (i, 0)),
                pl.BlockSpec((1, D), lambda i, d: (0, 0)),
                pl.BlockSpec((1, D), lambda i, d: (0, 0)),
                pl.BlockSpec(memory_space=pl.ANY),
            ],
            out_specs=[
                pl.BlockSpec((tm, D), lambda i, d: (jnp.minimum(i, n_prompt - 1), 0)),
                pl.BlockSpec((tm, D), lambda i, d: (jnp.maximum(i - n_prompt, 0), 0)),
            ],
            scratch_shapes=[pltpu.VMEM((2, TOP_K, tm, D), F32), pltpu.SemaphoreType.DMA((2,))],
        ),
        out_shape=[
            jax.ShapeDtypeStruct((t_prompt, D), F32),
            jax.ShapeDtypeStruct((T - t_prompt, D), F32),
        ],
        compiler_params=_cparams(("arbitrary",)),
        name="moe_combine",
    )(dest, h2, route, g3, b3, ys)


def _moe_plan(route, counts, bm):
    T = route.shape[0]
    top_idx = route[:, TOP_K:2 * TOP_K].astype(jnp.int32)
    rank = route[:, 2 * TOP_K:3 * TOP_K].astype(jnp.int32)
    counts = counts[0, :N_EXPERTS].astype(jnp.int32)
    padded = (counts + bm - 1) // bm * bm
    pad_end = jnp.cumsum(padded)
    pad_start = pad_end - padded
    dest = (pad_start[top_idx] + rank).reshape(-1)
    n_blocks = -(-(T * TOP_K) // bm) + N_EXPERTS
    block_start = jnp.arange(n_blocks, dtype=jnp.int32) * bm
    block_e = jnp.sum((pad_end[None, :] <= block_start[:, None]).astype(jnp.int32), axis=1)
    block_e = jnp.minimum(block_e, N_EXPERTS - 1)
    n_used = (pad_end[-1] // bm).astype(jnp.int32).reshape(1)
    blk = jnp.arange(n_blocks, dtype=jnp.int32)
    prev_e = jnp.concatenate([jnp.full((1,), -1, jnp.int32), block_e[:-1]])
    first = jnp.logical_and(blk < n_used[0], block_e != prev_e)
    later = jnp.logical_and(first[None, :], blk[None, :] > blk[:, None])
    nxt_blk = jnp.min(jnp.where(later, blk[None, :], n_blocks), axis=1)
    next_e = jnp.where(nxt_blk < n_blocks, block_e[jnp.minimum(nxt_blk, n_blocks - 1)], -1)
    plan = (block_e, n_used, first.astype(jnp.int32), next_e.astype(jnp.int32))
    return dest, pad_end, padded, plan, n_blocks * bm


def kernel(x_prompt, x_sample, cache_diff_k, cache_diff_v, state_ret, cache_mem_k, cache_mem_v, mem_prompt, ln_in_g, ln_in_b, w_in, b_gate, diff_lambda, diff_subln, w_proj_ret, w_proj_diff, w_out, ln1_g, ln1_b, w_mq, w_mk, w_mv, w_mo, ln2_g, ln2_b, router_w, router_b, w1, b1, w2, b2, ln3_g, ln3_b):
    Bp, Lp, D = x_prompt.shape
    Bs, Ls, _ = x_sample.shape
    depth = w_in.shape[0]
    assert depth == 1
    past = cache_diff_k.shape[2]
    n_mem = mem_prompt.shape[1]
    Tp, Ts = Bp * Lp, Bs * Ls
    alpha = (2.0 * depth) ** 0.25
    lambda_init = 0.8 - 0.6 * math.exp(-0.3 * 0)
    row = lambda v: v.reshape(1, -1)

    xp = x_prompt.reshape(Tp, D)
    xs_in = x_sample.reshape(Ts, D)
    kv_blk = 4096 // IN_TN
    P, dk_p, dv_p, dk_s, dv_s = _in_proj(xp, xs_in, row(ln_in_g), row(ln_in_b), w_in[0].astype(BF16),
                                         kv_blk)

    zeros_state = jnp.zeros((Bp,) + state_ret.shape[2:], F32)
    yr_p, s_p = _retention(P, 0, Bp, Lp, 0, zeros_state)
    yr_s, s_s = _retention(P, Tp, Bs, Ls, past, state_ret[0])

    yd_p = _attn_prompt(P, Bp, Lp, diff_lambda[0], row(diff_subln[0]), lambda_init)
    yd_s = _attn_sample(P, Tp, Bs, Ls, cache_diff_k[0], cache_diff_v[0], diff_lambda[0],
                        row(diff_subln[0]), lambda_init)

    h1 = _mix(xp, xs_in, yr_p, yr_s, yd_p, yd_s, P, row(ln_in_g), row(ln_in_b), row(b_gate[0]),
              w_proj_ret[0].astype(BF16), w_proj_diff[0].astype(BF16), w_out[0].astype(BF16),
              row(ln1_g[0]), row(ln1_b[0]), alpha)

    w_mkv = jnp.concatenate([w_mk[0], w_mv[0]], axis=1).astype(BF16)
    mkv = _mem_kv(mem_prompt.reshape(Bp * n_mem, D), w_mkv)
    hm = w_mk.shape[2]
    mk_p = mkv[:, :hm].reshape(Bp, n_mem, hm)
    mv_p = mkv[:, hm:].reshape(Bp, n_mem, hm)
    h2, route, counts = _mem_attn(
        h1, Tp, mk_p, mv_p, Lp, cache_mem_k[0], cache_mem_v[0], Ls, w_mq[0].astype(BF16),
        w_mo[0].astype(BF16),
        row(ln2_g[0]), row(ln2_b[0]), router_w[0], row(router_b[0]), alpha)

    dest, pad_end, padded, plan, n_rows = _moe_plan(route, counts, MOE_BM)
    xs = _moe_dispatch(h2, dest, pad_end, padded, plan[1], n_rows, MOE_BM)
    act = _moe_up(xs, w1[0], b1[0][:, None, :], plan)
    ys = _moe_down(act, w2[0], b2[0][:, None, :], plan)
    out_p, out_s = _moe_combine(ys, dest, h2, route, row(ln3_g[0]), row(ln3_b[0]), Tp, alpha)

    return (
        out_p.reshape(Bp, Lp, D),
        out_s.reshape(Bs, Ls, D),
        dk_p.reshape(1, Bp, Lp, D_HEADS, 128),
        dv_p.reshape(1, Bp, Lp, D_HEADS, 128),
        s_p[None],
        mk_p.reshape(1, Bp, n_mem, M_HEADS, hm // M_HEADS),
        mv_p.reshape(1, Bp, n_mem, M_HEADS, hm // M_HEADS),
        dk_s.reshape(1, Bs, Ls, D_HEADS, 128),
        dv_s.reshape(1, Bs, Ls, D_HEADS, 128),
        s_s[None],
    )
```

```python
import functools
import math

import jax
import jax.numpy as jnp
from jax import lax
from jax.experimental import pallas as pl
from jax.experimental.pallas import tpu as pltpu

F32 = jnp.float32
BF16 = jnp.bfloat16

CHUNK = 64
R_HEADS = 4
D_HEADS = 8
M_HEADS = 4
N_EXPERTS = 32
TOP_K = 4
SWIGLU_ALPHA = 1.702
SWIGLU_LIMIT = 7.0
LN_EPS = 1e-5
RMS_EPS = 1e-5
ROPE_BASE = 10000.0
LOG2E = 1.4426950408889634

LANES = 128
SUBLANES = 8
VMEM_LIMIT = 58 * 1024 * 1024

IN_TM = 512
IN_TN = 2048
RET_CHUNK = 256
ATT_BLK = 512
ATT_UNROLL = 8
ATT_HEADS = 1
MIX_TM = 256
MEM_TM = 512
MOE_BM = 256
MOE_TN1 = 1024
MOE_TN2 = 2048
GATHER_ROWS = 256
COMBINE_TM = 128


def _cparams(sem):
    return pltpu.CompilerParams(dimension_semantics=sem, vmem_limit_bytes=VMEM_LIMIT)


def _layer_norm(x, g, b):
    mu = jnp.mean(x, axis=-1, keepdims=True)
    xc = x - mu
    var = jnp.mean(xc * xc, axis=-1, keepdims=True)
    return xc * lax.rsqrt(var + LN_EPS) * g + b


def _dot(a, b):
    return jnp.dot(a, b, preferred_element_type=F32)


def _dot_nt(a, b):
    return lax.dot_general(a, b, (((1,), (1,)), ((), ())), preferred_element_type=F32)


def _dot_tn(a, b):
    return lax.dot_general(a, b, (((0,), (0,)), ((), ())), preferred_element_type=F32)


def _lo(i, n):
    return jnp.minimum(i, n - 1)


def _hi(i, n):
    return jnp.maximum(i - n, 0)


def _in_proj_kernel(xp_ref, xs_ref, g_ref, b_ref, w_ref, p_ref, dkp_ref, dvp_ref, dks_ref, dvs_ref,
                    h_scr, *, n_p):
    i = pl.program_id(0)
    j = pl.program_id(1)
    half = dkp_ref.shape[1]

    @pl.when(jnp.logical_and(j == 0, i < n_p))
    def _():
        h_scr[...] = _layer_norm(xp_ref[...], g_ref[...], b_ref[...]).astype(BF16)

    @pl.when(jnp.logical_and(j == 0, i >= n_p))
    def _():
        h_scr[...] = _layer_norm(xs_ref[...], g_ref[...], b_ref[...]).astype(BF16)

    acc = _dot(h_scr[...], w_ref[...])
    p_ref[...] = acc.astype(BF16)

    @pl.when(jnp.logical_and(j == 0, i < n_p))
    def _():
        dkp_ref[...] = acc[:, :half]
        dvp_ref[...] = acc[:, half:]

    @pl.when(jnp.logical_and(j == 0, i >= n_p))
    def _():
        dks_ref[...] = acc[:, :half]
        dvs_ref[...] = acc[:, half:]


def _in_proj(xp, xs, g, b, w, kv_blk):
    Tp, D = xp.shape
    Ts = xs.shape[0]
    N = w.shape[1]
    n_p, n_s = Tp // IN_TM, Ts // IN_TM
    half = IN_TN // 2

    def col(j):
        return jnp.where(j == 0, kv_blk, jnp.where(j <= kv_blk, j - 1, j))

    single = pl.Buffered(1)
    kv_p = pl.BlockSpec((IN_TM, half), lambda i, j: (_lo(i, n_p), 0))
    kv_s = pl.BlockSpec((IN_TM, half), lambda i, j: (_hi(i, n_p), 0), pipeline_mode=single)
    return pl.pallas_call(
        functools.partial(_in_proj_kernel, n_p=n_p),
        grid=(n_p + n_s, N // IN_TN),
        in_specs=[
            pl.BlockSpec((IN_TM, D), lambda i, j: (_lo(i, n_p), 0)),
            pl.BlockSpec((IN_TM, D), lambda i, j: (_hi(i, n_p), 0), pipeline_mode=single),
            pl.BlockSpec((1, D), lambda i, j: (0, 0)),
            pl.BlockSpec((1, D), lambda i, j: (0, 0)),
            pl.BlockSpec((D, IN_TN), lambda i, j: (0, col(j))),
        ],
        out_specs=[pl.BlockSpec((IN_TM, IN_TN), lambda i, j: (i, j)), kv_p, kv_p, kv_s, kv_s],
        out_shape=[
            jax.ShapeDtypeStruct((Tp + Ts, N), BF16),
            jax.ShapeDtypeStruct((Tp, half), F32),
            jax.ShapeDtypeStruct((Tp, half), F32),
            jax.ShapeDtypeStruct((Ts, half), F32),
            jax.ShapeDtypeStruct((Ts, half), F32),
        ],
        scratch_shapes=[pltpu.VMEM((IN_TM, D), BF16)],
        compiler_params=_cparams(("arbitrary", "arbitrary")),
        name="in_proj",
    )(xp, xs, g, b, w)


P_DK = 0
P_DV = 8
P_RQ = 16
P_RK = 20
P_RV = 24
P_RG = 32
P_DQ = 40
P_GATE = 48


def _retention_kernel(q_ref, k_ref, v_ref, g_ref, cos_ref, sin_ref, dm_ref, qd_ref, kd_ref,
                      bd_ref, s0_ref, y_ref, sout_ref, s_scr, *, n_chunks, dk, dv):
    c = pl.program_id(1)

    @pl.when(c == 0)
    def _():
        s_scr[...] = s0_ref[0]

    cos = cos_ref[...]
    sin = sin_ref[...]
    k_scale = dk ** -0.5
    for h in range(R_HEADS):
        q = q_ref[:, h * dk:(h + 1) * dk].astype(F32)
        k = k_ref[:, h * dk:(h + 1) * dk].astype(F32)
        v = v_ref[:, h * dv:(h + 1) * dv]
        g = g_ref[:, h * dv:(h + 1) * dv].astype(F32)
        q = q * cos + pltpu.roll(q, dk // 2, 1) * sin
        k = (k * cos + pltpu.roll(k, dk // 2, 1) * sin) * k_scale
        qb = q.astype(BF16)
        kb = k.astype(BF16)
        s = s_scr[h]
        scores = _dot_nt(qb, kb) * dm_ref[h]
        o = _dot(scores.astype(BF16), v) + _dot((q * qd_ref[h]).astype(BF16), s.astype(BF16))
        s_scr[h] = bd_ref[h] * s + _dot_tn((k * kd_ref[h]).astype(BF16), v)
        o = o * lax.rsqrt(jnp.mean(o * o, axis=-1, keepdims=True) + RMS_EPS)
        y_ref[:, h * dv:(h + 1) * dv] = (g * jax.nn.sigmoid(g) * o).astype(BF16)

    @pl.when(c == n_chunks - 1)
    def _():
        sout_ref[0] = s_scr[...]


def _retention(P, row_off, B, L, pos0, S0):
    dk, dv = 128, 256
    C = min(L, RET_CHUNK)
    nc = L // C
    ob = row_off // C
    pos = (pos0 + jnp.arange(L, dtype=jnp.int32)).astype(F32)
    inv = 1.0 / (ROPE_BASE ** jnp.linspace(0.0, 1.0, dk // 2, dtype=F32))
    ang = pos[:, None] * inv[None, :]
    cos = jnp.concatenate([jnp.cos(ang), jnp.cos(ang)], axis=-1)
    sin = jnp.concatenate([-jnp.sin(ang), jnp.sin(ang)], axis=-1)
    log_g = jnp.log1p(-jnp.power(2.0, -5.0 - jnp.arange(R_HEADS, dtype=F32)))
    i = jnp.arange(C, dtype=F32)
    rel = i[:, None] - i[None, :]
    dmask = jnp.where(rel >= 0, jnp.exp(log_g[:, None, None] * jnp.maximum(rel, 0.0)), 0.0)
    q_decay = jnp.exp(log_g[:, None] * (i + 1.0))[..., None]
    k_decay = jnp.exp(log_g[:, None] * (C - 1.0 - i))[..., None]
    b_decay = jnp.exp(log_g * C)[:, None, None]

    def rows(b, c):
        return ob + b * nc + c

    kern = functools.partial(_retention_kernel, n_chunks=nc, dk=dk, dv=dv)
    return pl.pallas_call(
        kern,
        grid=(B, nc),
        in_specs=[
            pl.BlockSpec((C, R_HEADS * dk), lambda b, c: (rows(b, c), P_RQ * LANES // (R_HEADS * dk))),
            pl.BlockSpec((C, R_HEADS * dk), lambda b, c: (rows(b, c), P_RK * LANES // (R_HEADS * dk))),
            pl.BlockSpec((C, R_HEADS * dv), lambda b, c: (rows(b, c), P_RV * LANES // (R_HEADS * dv))),
            pl.BlockSpec((C, R_HEADS * dv), lambda b, c: (rows(b, c), P_RG * LANES // (R_HEADS * dv))),
            pl.BlockSpec((C, dk), lambda b, c: (c, 0)),
            pl.BlockSpec((C, dk), lambda b, c: (c, 0)),
            pl.BlockSpec((R_HEADS, C, C), lambda b, c: (0, 0, 0)),
            pl.BlockSpec((R_HEADS, C, 1), lambda b, c: (0, 0, 0)),
            pl.BlockSpec((R_HEADS, C, 1), lambda b, c: (0, 0, 0)),
            pl.BlockSpec((R_HEADS, 1, 1), lambda b, c: (0, 0, 0)),
            pl.BlockSpec((1, R_HEADS, dk, dv), lambda b, c: (b, 0, 0, 0)),
        ],
        out_specs=[
            pl.BlockSpec((C, R_HEADS * dv), lambda b, c: (b * nc + c, 0)),
            pl.BlockSpec((1, R_HEADS, dk, dv), lambda b, c: (b, 0, 0, 0)),
        ],
        out_shape=[
            jax.ShapeDtypeStruct((B * L, R_HEADS * dv), BF16),
            jax.ShapeDtypeStruct((B, R_HEADS, dk, dv), F32),
        ],
        scratch_shapes=[pltpu.VMEM((R_HEADS, dk, dv), F32)],
        compiler_params=_cparams(("parallel", "arbitrary")),
        name="retention",
    )(P, P, P, P, cos, sin, dmask, q_decay, k_decay, b_decay, S0)


def _diff_lambda(lam_ref, lambda_init):
    lv = lam_ref[...]
    a = jnp.sum(lv[0:1] * lv[1:2], axis=-1, keepdims=True)
    b = jnp.sum(lv[2:3] * lv[3:4], axis=-1, keepdims=True)
    return jnp.exp(a) - jnp.exp(b) + lambda_init


def _lane_tile(x, n):
    return jnp.concatenate([x] * n, axis=1)


def _stack_maps(q, hd):
    lane = lax.broadcasted_iota(jnp.int32, q.shape, 1)
    zero = jnp.zeros_like(q)
    return jnp.concatenate([jnp.where(lane < hd, q, zero), jnp.where(lane < hd, zero, q)], axis=0)


def _diff_finish(acc, l, lam, subln, lambda_init, tq):
    o = acc[:tq] / l[:tq] - lam * (acc[tq:] / l[tq:])
    o = o * lax.rsqrt(jnp.mean(o * o, axis=-1, keepdims=True) + RMS_EPS)
    return (o * subln * (1.0 - lambda_init)).astype(BF16)


def _attn_prompt_kernel(q_ref, k_ref, v_ref, lam_ref, sub_ref, y_ref, qs_scr, ve_scr, m_scr, acc_scr,
                        s_scr, *, blk, hd, lambda_init):
    qi = pl.program_id(2)
    n_heads = qs_scr.shape[0]
    dv = 2 * hd
    seq = v_ref.shape[0]
    head = lambda g: slice(g * dv, (g + 1) * dv)

    @pl.when(qi == 0)
    def _():
        for g in range(n_heads):
            ve_scr[g, :, :dv] = v_ref[:, head(g)]
            ve_scr[g, :, dv:] = jnp.ones((seq, dv), BF16)

    for g in range(n_heads):
        q = q_ref[:, head(g)].astype(F32) * (hd ** -0.5 * LOG2E)
        qs_scr[g] = _stack_maps(q.astype(BF16), hd)
    m_scr[...] = jnp.full_like(m_scr, -jnp.inf)
    acc_scr[...] = jnp.zeros_like(acc_scr)

    def scores(ki, slot):
        off = pl.multiple_of(ki * blk, blk)
        for g in range(n_heads):
            s_scr[g, slot] = _dot_nt(qs_scr[g], k_ref[pl.ds(off, blk), head(g)])

    def step(ki, slot, masked):
        off = pl.multiple_of(ki * blk, blk)
        for g in range(n_heads):
            s = s_scr[g, slot]
            if masked:
                row = lax.broadcasted_iota(jnp.int32, s.shape, 0)
                col = lax.broadcasted_iota(jnp.int32, s.shape, 1)
                qrow = jnp.where(row >= blk, row - blk, row)
                shift = CHUNK.bit_length() - 1
                s = jnp.where((col >> shift) <= (qrow >> shift), s, -1e30)
            m_prev = m_scr[g]
            m_new = jnp.maximum(m_prev, jnp.max(s, axis=-1, keepdims=True))
            alpha = jnp.exp2(m_prev - m_new)
            p = jnp.exp2(s - _lane_tile(m_new, blk // LANES))
            pv = _dot(p.astype(BF16), ve_scr[g, pl.ds(off, blk), :])
            acc_scr[g] = _lane_tile(alpha, 2 * dv // LANES) * acc_scr[g] + pv
            m_scr[g] = m_new

    def run(base, width):
        for u in range(width):
            scores(base + u + 1, (u + 1) % 2)
            step(base + u, u % 2, False)

    def body(j, carry):
        run(ATT_UNROLL * j, ATT_UNROLL)
        return carry

    scores(0, 0)
    n_main = qi // ATT_UNROLL
    lax.fori_loop(0, n_main, body, 0)
    done = n_main * ATT_UNROLL
    width = ATT_UNROLL // 2
    while width >= 2:
        @pl.when(((qi - done) // width) % 2 == 1)
        def _(width=width):
            run(qi - (qi - done) % (2 * width), width)
        width //= 2

    @pl.when(qi % 2 == 1)
    def _():
        scores(qi, 1)
        step(qi - 1, 0, False)
        step(qi, 1, True)

    @pl.when(qi % 2 == 0)
    def _():
        step(qi, 0, True)

    lam = _diff_lambda(lam_ref, lambda_init)
    for g in range(n_heads):
        acc = acc_scr[g]
        o = acc[:blk, :dv] / acc[:blk, dv:] - lam * (acc[blk:, :dv] / acc[blk:, dv:])
        o = o * lax.rsqrt(jnp.mean(o * o, axis=-1, keepdims=True) + RMS_EPS)
        y_ref[:, head(g)] = (o * sub_ref[...] * (1.0 - lambda_init)).astype(BF16)


def _attn_prompt(P, B, S, lam_p, subln, lambda_init):
    hd = 64
    blk = min(ATT_BLK, S)
    nq = S // blk
    g = ATT_HEADS
    w = g * 2 * hd
    kern = functools.partial(_attn_prompt_kernel, blk=blk, hd=hd, lambda_init=lambda_init)
    return pl.pallas_call(
        kern,
        grid=(B, D_HEADS // g, nq),
        in_specs=[
            pl.BlockSpec((blk, w), lambda b, h, i: (b * nq + i, P_DQ // g + h)),
            pl.BlockSpec((S, w), lambda b, h, i: (b, P_DK // g + h)),
            pl.BlockSpec((S, w), lambda b, h, i: (b, P_DV // g + h)),
            pl.BlockSpec((4, hd), lambda b, h, i: (0, 0)),
            pl.BlockSpec((1, 2 * hd), lambda b, h, i: (0, 0)),
        ],
        out_specs=pl.BlockSpec((blk, w), lambda b, h, i: (b * nq + i, h)),
        out_shape=jax.ShapeDtypeStruct((B * S, D_HEADS * 2 * hd), BF16),
        scratch_shapes=[
            pltpu.VMEM((g, 2 * blk, 2 * hd), BF16),
            pltpu.VMEM((g, S, 4 * hd), BF16),
            pltpu.VMEM((g, 2 * blk, LANES), F32),
            pltpu.VMEM((g, 2 * blk, 4 * hd), F32),
            pltpu.VMEM((g, 2, 2 * blk, blk), F32),
        ],
        compiler_params=_cparams(("parallel", "parallel", "arbitrary")),
        name="attn_prompt",
    )(P, P, P, lam_p, subln)


def _attn_sample_kernel(q_ref, kn_ref, vn_ref, kc_ref, vc_ref, lam_ref, sub_ref, y_ref,
                        *, tq, hd, lambda_init):
    lam = _diff_lambda(lam_ref, lambda_init)
    w = 2 * hd
    past = kc_ref.shape[1] // D_HEADS
    for h in range(D_HEADS):
        cols = slice(h * w, (h + 1) * w)
        qs = _stack_maps(q_ref[:, cols] * (hd ** -0.5), hd)
        kc = kc_ref[0, pl.ds(h, past, stride=D_HEADS), :].astype(BF16)
        vc = vc_ref[0, pl.ds(h, past, stride=D_HEADS), :].astype(BF16)
        kn = kn_ref[:, cols]
        vn = vn_ref[:, cols]
        s_c = _dot_nt(qs, kc)
        s_n = _dot_nt(qs, kn)
        m = jnp.maximum(jnp.max(s_c, axis=-1, keepdims=True), jnp.max(s_n, axis=-1, keepdims=True))
        p_c = jnp.exp(s_c - m)
        p_n = jnp.exp(s_n - m)
        l = jnp.sum(p_c, axis=-1, keepdims=True) + jnp.sum(p_n, axis=-1, keepdims=True)
        acc = _dot(p_c.astype(BF16), vc) + _dot(p_n.astype(BF16), vn)
        y_ref[:, cols] = _diff_finish(acc, l, lam, sub_ref[...], lambda_init, tq)


def _attn_sample(P, row_off, B, L, cache_k, cache_v, lam_p, subln, lambda_init):
    hd = 64
    past = cache_k.shape[1]
    ob = row_off // L
    width = D_HEADS * 2 * hd
    kern = functools.partial(_attn_sample_kernel, tq=L, hd=hd, lambda_init=lambda_init)
    cache_k = cache_k.reshape(B, past * D_HEADS, 2 * hd)
    cache_v = cache_v.reshape(B, past * D_HEADS, 2 * hd)
    cache_spec = pl.BlockSpec((1, past * D_HEADS, 2 * hd), lambda b: (b, 0, 0))
    return pl.pallas_call(
        kern,
        grid=(B,),
        in_specs=[
            pl.BlockSpec((L, width), lambda b: (ob + b, P_DQ * LANES // width)),
            pl.BlockSpec((L, width), lambda b: (ob + b, P_DK * LANES // width)),
            pl.BlockSpec((L, width), lambda b: (ob + b, P_DV * LANES // width)),
            cache_spec,
            cache_spec,
            pl.BlockSpec((4, hd), lambda b: (0, 0)),
            pl.BlockSpec((1, 2 * hd), lambda b: (0, 0)),
        ],
        out_specs=pl.BlockSpec((L, width), lambda b: (b, 0)),
        out_shape=jax.ShapeDtypeStruct((B * L, width), BF16),
        compiler_params=_cparams(("parallel",)),
        name="attn_sample",
    )(P, P, P, cache_k, cache_v, lam_p, subln)


def _mix_kernel(xp_ref, xs_ref, yrp_ref, yrs_ref, ydp_ref, yds_ref, gr_ref, gd_ref, lig_ref, lib_ref,
                bg_ref, wpr_ref, wpd_ref, wo_ref, g1_ref, b1_ref, h1_ref, *, alpha, n_p):
    i = pl.program_id(0)
    d = xp_ref.shape[1]

    def compute(x_ref, yr_ref, yd_ref):
        h0 = _layer_norm(x_ref[...], lig_ref[...], lib_ref[...])
        g_ret = jax.nn.sigmoid(gr_ref[...].astype(F32) + bg_ref[:, :d])
        g_diff = jax.nn.sigmoid(gd_ref[...].astype(F32) + bg_ref[:, d:])
        merged = g_ret * _dot(yr_ref[...], wpr_ref[...]) + g_diff * _dot(yd_ref[...], wpd_ref[...])
        mixed = _dot(merged.astype(BF16), wo_ref[...])
        h1_ref[...] = _layer_norm(alpha * h0 + mixed, g1_ref[...], b1_ref[...])

    @pl.when(i < n_p)
    def _():
        compute(xp_ref, yrp_ref, ydp_ref)

    @pl.when(i >= n_p)
    def _():
        compute(xs_ref, yrs_ref, yds_ref)


def _mix(xp, xs, yr_p, yr_s, yd_p, yd_s, P, ln_in_g, ln_in_b, b_gate, wpr, wpd, wo, g1, b1, alpha):
    Tp, D = xp.shape
    Ts = xs.shape[0]
    tm = MIX_TM
    n_p, n_s = Tp // tm, Ts // tm
    gcol = P_GATE * LANES // D
    const = lambda i: (0, 0)
    lo = lambda i: (_lo(i, n_p), 0)
    hi = lambda i: (_hi(i, n_p), 0)
    single = pl.Buffered(1)
    kern = functools.partial(_mix_kernel, alpha=alpha, n_p=n_p)
    return pl.pallas_call(
        kern,
        grid=(n_p + n_s,),
        in_specs=[
            pl.BlockSpec((tm, D), lo),
            pl.BlockSpec((tm, D), hi),
            pl.BlockSpec((tm, yr_p.shape[1]), lo),
            pl.BlockSpec((tm, yr_s.shape[1]), hi),
            pl.BlockSpec((tm, yd_p.shape[1]), lo),
            pl.BlockSpec((tm, yd_s.shape[1]), hi),
            pl.BlockSpec((tm, D), lambda i: (i, gcol)),
            pl.BlockSpec((tm, D), lambda i: (i, gcol + 1)),
            pl.BlockSpec((1, D), const),
            pl.BlockSpec((1, D), const),
            pl.BlockSpec((1, 2 * D), const),
            pl.BlockSpec(wpr.shape, const, pipeline_mode=single),
            pl.BlockSpec(wpd.shape, const, pipeline_mode=single),
            pl.BlockSpec(wo.shape, const, pipeline_mode=single),
            pl.BlockSpec((1, D), const),
            pl.BlockSpec((1, D), const),
        ],
        out_specs=pl.BlockSpec((tm, D), lambda i: (i, 0)),
        out_shape=jax.ShapeDtypeStruct((Tp + Ts, D), F32),
        compiler_params=_cparams(("arbitrary",)),
        name="mix",
    )(xp, xs, yr_p, yr_s, yd_p, yd_s, P, P, ln_in_g, ln_in_b, b_gate, wpr, wpd, wo, g1, b1)


def _mem_kernel(h1_ref, mkp_ref, mvp_ref, mks_ref, mvs_ref, wq_ref, wo_ref, g2_ref, b2_ref, rw_ref,
                rb_ref, h2_ref, route_ref, counts_ref, q_scr, o_scr, cnt_scr, *, n_p, alpha, hd):
    i = pl.program_id(0)
    tm = h1_ref.shape[0]
    h1 = h1_ref[...]
    q_scr[...] = (_dot(h1.astype(BF16), wq_ref[...]) * (hd ** -0.5)).astype(BF16)

    def attend(mk_ref, mv_ref):
        n_sub = mk_ref.shape[0]
        seg = tm // n_sub
        split_heads = mk_ref.shape[2] == hd
        for s in range(n_sub):
            for h in range(M_HEADS):
                if split_heads:
                    rows = pl.ds(h, mk_ref.shape[1] // M_HEADS, stride=M_HEADS)
                    mk = mk_ref[s, rows, :].astype(BF16)
                    mv = mv_ref[s, rows, :].astype(BF16)
                else:
                    mk = mk_ref[s, :, h * hd:(h + 1) * hd].astype(BF16)
                    mv = mv_ref[s, :, h * hd:(h + 1) * hd].astype(BF16)
                qh = q_scr[s * seg:(s + 1) * seg, h * hd:(h + 1) * hd]
                sc = _dot_nt(qh, mk)
                sc = sc - jnp.max(sc, axis=-1, keepdims=True)
                p = jnp.exp(sc)
                p = p / jnp.sum(p, axis=-1, keepdims=True)
                o_scr[s * seg:(s + 1) * seg, h * hd:(h + 1) * hd] = _dot(
                    p.astype(BF16), mv).astype(BF16)

    @pl.when(i < n_p)
    def _():
        attend(mkp_ref, mvp_ref)

    @pl.when(i >= n_p)
    def _():
        attend(mks_ref, mvs_ref)

    h2 = _layer_norm(alpha * h1 + _dot(o_scr[...], wo_ref[...]), g2_ref[...], b2_ref[...])
    h2_ref[...] = h2
    h2b = h2.astype(BF16)

    h_lo = (h2 - h2b.astype(F32)).astype(BF16)
    rw = rw_ref[...]
    rw_hi = rw.astype(BF16)
    rw_lo = (rw - rw_hi.astype(F32)).astype(BF16)
    logits = _dot(h2b, rw_hi) + _dot(h2b, rw_lo) + _dot(h_lo, rw_hi) + rb_ref[...]

    n_e = logits.shape[1]
    eidx = lax.broadcasted_iota(jnp.int32, logits.shape, 1).astype(F32)
    lane = lax.broadcasted_iota(jnp.int32, (tm, LANES), 1)
    route = jnp.zeros((tm, LANES), F32)
    work = logits
    vals = []
    sels = []
    for k in range(TOP_K):
        mx = jnp.max(work, axis=-1, keepdims=True)
        sel = jnp.min(jnp.where(work == mx, eidx, float(n_e)), axis=-1, keepdims=True)
        work = jnp.where(eidx == sel, -jnp.inf, work)
        vals.append(mx)
        sels.append(sel)
        route = jnp.where(lane == TOP_K + k, sel, route)
    ex = [jnp.exp(v - vals[0]) for v in vals]
    den = ex[0] + ex[1] + ex[2] + ex[3]
    for k in range(TOP_K):
        route = jnp.where(lane == k, ex[k] / den, route)

    @pl.when(i == 0)
    def _():
        cnt_scr[...] = jnp.zeros_like(cnt_scr)

    lane_f = lane.astype(F32)
    hit = [lane_f == sels[k] for k in range(TOP_K)]
    cnt = sum(h.astype(F32) for h in hit)
    r_io = lax.broadcasted_iota(jnp.int32, (tm, tm), 0)
    c_io = lax.broadcasted_iota(jnp.int32, (tm, tm), 1)
    ltri = jnp.where(r_io > c_io, 1.0, 0.0).astype(BF16)
    excl = _dot(ltri, cnt.astype(BF16)) + cnt_scr[...]
    for k in range(TOP_K):
        rank = jnp.sum(jnp.where(hit[k], excl, 0.0), axis=-1, keepdims=True)
        route = jnp.where(lane == 2 * TOP_K + k, rank, route)
    route_ref[...] = route
    cnt_scr[...] = cnt_scr[...] + jnp.sum(cnt, axis=0, keepdims=True)
    counts_ref[...] = cnt_scr[...]


def _mem_attn(h1, t_prompt, mem_kp, mem_vp, l_prompt, mem_ks, mem_vs, l_sample, wq, wo, g2, b2, rw, rb,
              alpha):
    T, D = h1.shape
    hd = 128
    tm = MEM_TM
    assert l_prompt % tm == 0 and tm % l_sample == 0
    n_p = t_prompt // tm
    n_s = (T - t_prompt) // tm
    per_b = l_prompt // tm
    sub_s = tm // l_sample
    n_mem = mem_kp.shape[1]
    bp = mem_kp.shape[0]
    const = lambda i: (0, 0)
    single = pl.Buffered(1)
    mem_p = pl.BlockSpec((1, n_mem, M_HEADS * hd), lambda i: (jnp.minimum(i // per_b, bp - 1), 0, 0))
    bs = mem_ks.shape[0]
    mem_ks = mem_ks.reshape(bs, n_mem * M_HEADS, hd)
    mem_vs = mem_vs.reshape(bs, n_mem * M_HEADS, hd)
    mem_s = pl.BlockSpec((sub_s, n_mem * M_HEADS, hd), lambda i: (_hi(i, n_p), 0, 0))
    kern = functools.partial(_mem_kernel, n_p=n_p, alpha=alpha, hd=hd)
    return pl.pallas_call(
        kern,
        grid=(n_p + n_s,),
        in_specs=[
            pl.BlockSpec((tm, D), lambda i: (i, 0)),
            mem_p, mem_p, mem_s, mem_s,
            pl.BlockSpec(wq.shape, const, pipeline_mode=single),
            pl.BlockSpec(wo.shape, const, pipeline_mode=single),
            pl.BlockSpec((1, D), const),
            pl.BlockSpec((1, D), const),
            pl.BlockSpec(rw.shape, const, pipeline_mode=single),
            pl.BlockSpec((1, rw.shape[1]), const),
        ],
        out_specs=[
            pl.BlockSpec((tm, D), lambda i: (i, 0)),
            pl.BlockSpec((tm, LANES), lambda i: (i, 0)),
            pl.BlockSpec((1, LANES), const),
        ],
        out_shape=[
            jax.ShapeDtypeStruct((T, D), F32),
            jax.ShapeDtypeStruct((T, LANES), F32),
            jax.ShapeDtypeStruct((1, LANES), F32),
        ],
        scratch_shapes=[
            pltpu.VMEM((tm, M_HEADS * hd), BF16),
            pltpu.VMEM((tm, M_HEADS * hd), BF16),
            pltpu.VMEM((1, LANES), F32),
        ],
        compiler_params=_cparams(("arbitrary",)),
        name="mem_attn",
    )(h1, mem_kp, mem_vp, mem_ks, mem_vs, wq, wo, g2, b2, rw, rb)


def _mem_kv_kernel(x_ref, w_ref, o_ref):
    o_ref[...] = _dot(x_ref[...].astype(BF16), w_ref[...])


def _mem_kv(mem, w):
    R, D = mem.shape
    N = w.shape[1]
    tm = 256
    return pl.pallas_call(
        _mem_kv_kernel,
        grid=(R // tm,),
        in_specs=[pl.BlockSpec((tm, D), lambda i: (i, 0)), pl.BlockSpec((D, N), lambda i: (0, 0))],
        out_specs=pl.BlockSpec((tm, N), lambda i: (i, 0)),
        out_shape=jax.ShapeDtypeStruct((R, N), F32),
        compiler_params=_cparams(("parallel",)),
        name="mem_kv",
    )(mem, w)


def _row_copy(src_hbm, dst, sem, src_row, dst_row):
    return pltpu.make_async_copy(src_hbm.at[pl.ds(src_row, 1)], dst.at[pl.ds(dst_row, 1)], sem)


def _dispatch_kernel(dest_ref, pend_ref, padded_ref, nu_ref, h_ref, xs_hbm, stage, zeros, sem, zsem,
                     *, tm, bm, n_steps, n_blocks):
    i = pl.program_id(0)
    slot = i % 2

    def zero_block(off):
        return pltpu.make_async_copy(zeros, xs_hbm.at[pl.ds(pl.multiple_of(off, bm), bm)], zsem)

    def zero_fill(act):
        for e in range(N_EXPERTS):
            @pl.when(padded_ref[e] > 0)
            def _(e=e):
                act(zero_block(pend_ref[e] - bm))
        for b in range(n_blocks - N_EXPERTS, n_blocks):
            @pl.when(b >= nu_ref[0])
            def _(b=b):
                act(zero_block(b * bm))

    @pl.when(i == 0)
    def _():
        zeros[...] = jnp.zeros_like(zeros)
        zero_fill(lambda c: c.start())
        zero_fill(lambda c: c.wait())

    def wait(s):
        for _ in range(TOP_K):
            pltpu.make_async_copy(stage.at[s], xs_hbm.at[pl.ds(0, tm)], sem.at[s]).wait()

    @pl.when(i >= 2)
    def _():
        wait(slot)

    stage[slot] = h_ref[...]

    def body(j, carry):
        t0 = pl.multiple_of(j * SUBLANES, SUBLANES)
        for u in range(SUBLANES):
            for k in range(TOP_K):
                row = dest_ref[(i * tm + t0 + u) * TOP_K + k]
                pltpu.make_async_copy(stage.at[slot, pl.ds(t0 + u, 1)], xs_hbm.at[pl.ds(row, 1)],
                                      sem.at[slot]).start(priority=k % 2)
        return carry
    lax.fori_loop(0, tm // SUBLANES, body, 0)

    @pl.when(i == n_steps - 1)
    def _():
        wait(slot)
        if n_steps > 1:
            wait(1 - slot)


def _moe_dispatch(h2, dest, pad_end, padded, n_used, n_rows, bm):
    T, D = h2.shape
    tm = GATHER_ROWS
    n_steps = T // tm
    kern = functools.partial(_dispatch_kernel, tm=tm, bm=bm, n_steps=n_steps, n_blocks=n_rows // bm)
    return pl.pallas_call(
        kern,
        grid_spec=pltpu.PrefetchScalarGridSpec(
            num_scalar_prefetch=4,
            grid=(n_steps,),
            in_specs=[pl.BlockSpec((tm, D), lambda i, d, pe, pd, nu: (i, 0))],
            out_specs=pl.BlockSpec(memory_space=pl.ANY),
            scratch_shapes=[
                pltpu.VMEM((2, tm, D), F32),
                pltpu.VMEM((bm, D), F32),
                pltpu.SemaphoreType.DMA((2,)),
                pltpu.SemaphoreType.DMA(()),
            ],
        ),
        out_shape=jax.ShapeDtypeStruct((n_rows, D), F32),
        compiler_params=_cparams(("arbitrary",)),
        name="moe_dispatch",
    )(dest, pad_end, padded, n_used, h2)


def _expert_weights(be_ref, first_ref, next_ref, cnt_ref, copies, cast, *, p, m, n_pass):
    @pl.when(jnp.logical_and(p == 0, m == 0))
    def _():
        cnt_ref[0] = 0

    @pl.when(first_ref[m] == 1)
    def _():
        seg = cnt_ref[0]
        slot = seg % 2

        @pl.when(seg == 0)
        def _():
            for c in copies(be_ref[0], 0, 0):
                c.start()

        for c in copies(be_ref[m], p, slot):
            c.wait()
        cast(slot)
        ne = next_ref[m]

        @pl.when(ne >= 0)
        def _():
            for c in copies(ne, p, 1 - slot):
                c.start(priority=1)

        @pl.when(jnp.logical_and(ne < 0, p + 1 < n_pass))
        def _():
            for c in copies(be_ref[0], p + 1, 1 - slot):
                c.start(priority=1)

        cnt_ref[0] = seg + 1


def _moe_up_kernel(be_ref, nu_ref, first_ref, next_ref, x_ref, w1_hbm, bg_ref, bl_ref, a_ref,
                   stage, wg_scr, wl_scr, sem, cnt_ref, *, n_pass):
    f = pl.program_id(0)
    m = pl.program_id(1)
    tn = wg_scr.shape[1]
    ff = w1_hbm.shape[2] // 2

    def copies(e, p, slot):
        off = pl.multiple_of(p * tn, tn)
        return [
            pltpu.make_async_copy(w1_hbm.at[e, :, pl.ds(off, tn)], stage.at[slot, 0], sem.at[slot]),
            pltpu.make_async_copy(w1_hbm.at[e, :, pl.ds(ff + off, tn)], stage.at[slot, 1], sem.at[slot]),
        ]

    def cast(slot):
        wg_scr[...] = stage[slot, 0].astype(BF16)
        wl_scr[...] = stage[slot, 1].astype(BF16)

    _expert_weights(be_ref, first_ref, next_ref, cnt_ref, copies, cast, p=f, m=m, n_pass=n_pass)

    @pl.when(m < nu_ref[0])
    def _():
        x = x_ref[...].astype(BF16)
        u_glu = jnp.minimum(_dot(x, wg_scr[...]) + bg_ref[0], SWIGLU_LIMIT)
        u_lin = jnp.clip(_dot(x, wl_scr[...]) + bl_ref[0], -SWIGLU_LIMIT, SWIGLU_LIMIT)
        a = u_glu * jax.nn.sigmoid(SWIGLU_ALPHA * u_glu) * (u_lin + 1.0)
        a_ref[...] = a.astype(BF16)

    @pl.when(m >= nu_ref[0])
    def _():
        a_ref[...] = jnp.zeros_like(a_ref)


def _moe_up(xs, w1, b1, plan):
    n_rows, D = xs.shape
    F = w1.shape[2] // 2
    bm, tn = MOE_BM, MOE_TN1
    nb = n_rows // bm
    nf = F // tn
    return pl.pallas_call(
        functools.partial(_moe_up_kernel, n_pass=nf),
        grid_spec=pltpu.PrefetchScalarGridSpec(
            num_scalar_prefetch=4,
            grid=(nf, nb),
            in_specs=[
                pl.BlockSpec((bm, D), lambda f, m, be, nu, fi, nx: (jnp.minimum(m, nu[0] - 1), 0)),
                pl.BlockSpec(memory_space=pl.ANY),
                pl.BlockSpec((1, 1, tn), lambda f, m, be, nu, fi, nx: (be[m], 0, f)),
                pl.BlockSpec((1, 1, tn), lambda f, m, be, nu, fi, nx: (be[m], 0, nf + f)),
            ],
            out_specs=pl.BlockSpec((bm, tn), lambda f, m, be, nu, fi, nx: (m, f)),
            scratch_shapes=[
                pltpu.VMEM((2, 2, D, tn), F32),
                pltpu.VMEM((D, tn), BF16),
                pltpu.VMEM((D, tn), BF16),
                pltpu.SemaphoreType.DMA((2,)),
                pltpu.SMEM((1,), jnp.int32),
            ],
        ),
        out_shape=jax.ShapeDtypeStruct((n_rows, F), BF16),
        compiler_params=_cparams(("arbitrary", "arbitrary")),
        name="moe_up",
    )(*plan, xs, w1, b1, b1)


def _moe_down_kernel(be_ref, nu_ref, first_ref, next_ref, a_ref, w2_hbm, b_ref, y_ref,
                     stage, w_scr, sem, cnt_ref, *, n_pass):
    d = pl.program_id(0)
    m = pl.program_id(1)
    tn = w_scr.shape[1]

    def copies(e, p, slot):
        off = pl.multiple_of(p * tn, tn)
        return [pltpu.make_async_copy(w2_hbm.at[e, :, pl.ds(off, tn)], stage.at[slot], sem.at[slot])]

    def cast(slot):
        w_scr[...] = stage[slot].astype(BF16)

    _expert_weights(be_ref, first_ref, next_ref, cnt_ref, copies, cast, p=d, m=m, n_pass=n_pass)

    @pl.when(m < nu_ref[0])
    def _():
        y_ref[...] = _dot(a_ref[...], w_scr[...]) + b_ref[0]

    @pl.when(m >= nu_ref[0])
    def _():
        y_ref[...] = jnp.zeros_like(y_ref)


def _moe_down(a, w2, b2, plan):
    n_rows, F = a.shape
    D = w2.shape[2]
    bm, tn = MOE_BM, MOE_TN2
    nb = n_rows // bm
    nd = D // tn
    return pl.pallas_call(
        functools.partial(_moe_down_kernel, n_pass=nd),
        grid_spec=pltpu.PrefetchScalarGridSpec(
            num_scalar_prefetch=4,
            grid=(nd, nb),
            in_specs=[
                pl.BlockSpec((bm, F), lambda d, m, be, nu, fi, nx: (jnp.minimum(m, nu[0] - 1), 0)),
                pl.BlockSpec(memory_space=pl.ANY),
                pl.BlockSpec((1, 1, tn), lambda d, m, be, nu, fi, nx: (be[m], 0, d)),
            ],
            out_specs=pl.BlockSpec((bm, tn), lambda d, m, be, nu, fi, nx: (m, d)),
            scratch_shapes=[
                pltpu.VMEM((2, F, tn), F32),
                pltpu.VMEM((F, tn), BF16),
                pltpu.SemaphoreType.DMA((2,)),
                pltpu.SMEM((1,), jnp.int32),
            ],
        ),
        out_shape=jax.ShapeDtypeStruct((n_rows, D), F32),
        compiler_params=_cparams(("arbitrary", "arbitrary")),
        name="moe_down",
    )(*plan, a, w2, b2)


def _combine_kernel(dest_ref, h2_ref, rt_ref, g_ref, b_ref, ys_hbm, op_ref, os_ref, buf, sem,
                    *, tm, n_steps, n_prompt, alpha):
    i = pl.program_id(0)

    def issue(step, slot):
        def body(j, carry):
            t0 = pl.multiple_of(j * SUBLANES, SUBLANES)
            for u in range(SUBLANES):
                for k in range(TOP_K):
                    row = dest_ref[(step * tm + t0 + u) * TOP_K + k]
                    _row_copy(ys_hbm, buf.at[slot, k], sem.at[slot], row, t0 + u).start(priority=k % 2)
            return carry
        lax.fori_loop(0, tm // SUBLANES, body, 0)

    def wait(slot):
        for k in range(TOP_K):
            pltpu.make_async_copy(ys_hbm.at[pl.ds(0, tm)], buf.at[slot, k], sem.at[slot]).wait()

    slot = i % 2

    @pl.when(i == 0)
    def _():
        issue(0, 0)

    @pl.when(i + 1 < n_steps)
    def _():
        issue(i + 1, 1 - slot)

    wait(slot)
    y = rt_ref[:, 0:1] * buf[slot, 0]
    for k in range(1, TOP_K):
        y = y + rt_ref[:, k:k + 1] * buf[slot, k]
    out = _layer_norm(alpha * h2_ref[...] + y, g_ref[...], b_ref[...])

    @pl.when(i < n_prompt)
    def _():
        op_ref[...] = out

    @pl.when(i >= n_prompt)
    def _():
        os_ref[...] = out


def _moe_combine(ys, dest, h2, route, g3, b3, t_prompt, alpha):
    T, D = h2.shape
    tm = COMBINE_TM
    n_steps = T // tm
    n_prompt = t_prompt // tm
    kern = functools.partial(_combine_kernel, tm=tm, n_steps=n_steps, n_prompt=n_prompt, alpha=alpha)
    return pl.pallas_call(
        kern,
        grid_spec=pltpu.PrefetchScalarGridSpec(
            num_scalar_prefetch=1,
            grid=(n_steps,),
            in_specs=[
                pl.BlockSpec((tm, D), lambda i, d: (i, 0)),
                pl.BlockSpec((tm, LANES), lambda i, d: (i, 0)),
                pl.BlockSpec((1, D), lambda i, d: (0, 0)),
                pl.BlockSpec((1, D), lambda i, d: (0, 0)),
                pl.BlockSpec(memory_space=pl.ANY),
            ],
            out_specs=[
                pl.BlockSpec((tm, D), lambda i, d: (jnp.minimum(i, n_prompt - 1), 0)),
                pl.BlockSpec((tm, D), lambda i, d: (jnp.maximum(i - n_prompt, 0), 0)),
            ],
            scratch_shapes=[pltpu.VMEM((2, TOP_K, tm, D), F32), pltpu.SemaphoreType.DMA((2,))],
        ),
        out_shape=[
            jax.ShapeDtypeStruct((t_prompt, D), F32),
            jax.ShapeDtypeStruct((T - t_prompt, D), F32),
        ],
        compiler_params=_cparams(("arbitrary",)),
        name="moe_combine",
    )(dest, h2, route, g3, b3, ys)


def _moe_plan(route, counts, bm):
    T = route.shape[0]
    top_idx = route[:, TOP_K:2 * TOP_K].astype(jnp.int32)
    rank = route[:, 2 * TOP_K:3 * TOP_K].astype(jnp.int32)
    counts = counts[0, :N_EXPERTS].astype(jnp.int32)
    padded = (counts + bm - 1) // bm * bm
    pad_end = jnp.cumsum(padded)
    pad_start = pad_end - padded
    dest = (pad_start[top_idx] + rank).reshape(-1)
    n_blocks = -(-(T * TOP_K) // bm) + N_EXPERTS
    block_start = jnp.arange(n_blocks, dtype=jnp.int32) * bm
    block_e = jnp.sum((pad_end[None, :] <= block_start[:, None]).astype(jnp.int32), axis=1)
    block_e = jnp.minimum(block_e, N_EXPERTS - 1)
    n_used = (pad_end[-1] // bm).astype(jnp.int32).reshape(1)
    blk = jnp.arange(n_blocks, dtype=jnp.int32)
    prev_e = jnp.concatenate([jnp.full((1,), -1, jnp.int32), block_e[:-1]])
    first = jnp.logical_and(blk < n_used[0], block_e != prev_e)
    later = jnp.logical_and(first[None, :], blk[None, :] > blk[:, None])
    nxt_blk = jnp.min(jnp.where(later, blk[None, :], n_blocks), axis=1)
    next_e = jnp.where(nxt_blk < n_blocks, block_e[jnp.minimum(nxt_blk, n_blocks - 1)], -1)
    plan = (block_e, n_used, first.astype(jnp.int32), next_e.astype(jnp.int32))
    return dest, pad_end, padded, plan, n_blocks * bm


def kernel(x_prompt, x_sample, cache_diff_k, cache_diff_v, state_ret, cache_mem_k, cache_mem_v, mem_prompt, ln_in_g, ln_in_b, w_in, b_gate, diff_lambda, diff_subln, w_proj_ret, w_proj_diff, w_out, ln1_g, ln1_b, w_mq, w_mk, w_mv, w_mo, ln2_g, ln2_b, router_w, router_b, w1, b1, w2, b2, ln3_g, ln3_b):
    Bp, Lp, D = x_prompt.shape
    Bs, Ls, _ = x_sample.shape
    depth = w_in.shape[0]
    assert depth == 1
    past = cache_diff_k.shape[2]
    n_mem = mem_prompt.shape[1]
    Tp, Ts = Bp * Lp, Bs * Ls
    alpha = (2.0 * depth) ** 0.25
    lambda_init = 0.8 - 0.6 * math.exp(-0.3 * 0)
    row = lambda v: v.reshape(1, -1)

    xp = x_prompt.reshape(Tp, D)
    xs_in = x_sample.reshape(Ts, D)
    kv_blk = 4096 // IN_TN
    P, dk_p, dv_p, dk_s, dv_s = _in_proj(xp, xs_in, row(ln_in_g), row(ln_in_b), w_in[0].astype(BF16),
                                         kv_blk)

    zeros_state = jnp.zeros((Bp,) + state_ret.shape[2:], F32)
    yr_p, s_p = _retention(P, 0, Bp, Lp, 0, zeros_state)
    yr_s, s_s = _retention(P, Tp, Bs, Ls, past, state_ret[0])

    yd_p = _attn_prompt(P, Bp, Lp, diff_lambda[0], row(diff_subln[0]), lambda_init)
    yd_s = _attn_sample(P, Tp, Bs, Ls, cache_diff_k[0], cache_diff_v[0], diff_lambda[0],
                        row(diff_subln[0]), lambda_init)

    h1 = _mix(xp, xs_in, yr_p, yr_s, yd_p, yd_s, P, row(ln_in_g), row(ln_in_b), row(b_gate[0]),
              w_proj_ret[0].astype(BF16), w_proj_diff[0].astype(BF16), w_out[0].astype(BF16),
              row(ln1_g[0]), row(ln1_b[0]), alpha)

    w_mkv = jnp.concatenate([w_mk[0], w_mv[0]], axis=1).astype(BF16)
    mkv = _mem_kv(mem_prompt.reshape(Bp * n_mem, D), w_mkv)
    hm = w_mk.shape[2]
    mk_p = mkv[:, :hm].reshape(Bp, n_mem, hm)
    mv_p = mkv[:, hm:].reshape(Bp, n_mem, hm)
    h2, route, counts = _mem_attn(
        h1, Tp, mk_p, mv_p, Lp, cache_mem_k[0], cache_mem_v[0], Ls, w_mq[0].astype(BF16),
        w_mo[0].astype(BF16),
        row(ln2_g[0]), row(ln2_b[0]), router_w[0], row(router_b[0]), alpha)

    dest, pad_end, padded, plan, n_rows = _moe_plan(route, counts, MOE_BM)
    xs = _moe_dispatch(h2, dest, pad_end, padded, plan[1], n_rows, MOE_BM)
    act = _moe_up(xs, w1[0], b1[0][:, None, :], plan)
    ys = _moe_down(act, w2[0], b2[0][:, None, :], plan)
    out_p, out_s = _moe_combine(ys, dest, h2, route, row(ln3_g[0]), row(ln3_b[0]), Tp, alpha)

    return (
        out_p.reshape(Bp, Lp, D),
        out_s.reshape(Bs, Ls, D),
        dk_p.reshape(1, Bp, Lp, D_HEADS, 128),
        dv_p.reshape(1, Bp, Lp, D_HEADS, 128),
        s_p[None],
        mk_p.reshape(1, Bp, n_mem, M_HEADS, hm // M_HEADS),
        mv_p.reshape(1, Bp, n_mem, M_HEADS, hm // M_HEADS),
        dk_s.reshape(1, Bs, Ls, D_HEADS, 128),
        dv_s.reshape(1, Bs, Ls, D_HEADS, 128),
        s_s[None],
    )
```

```python
import functools
import math

import jax
import jax.numpy as jnp
from jax import lax
from jax.experimental import pallas as pl
from jax.experimental.pallas import tpu as pltpu

F32 = jnp.float32
BF16 = jnp.bfloat16

CHUNK = 64
R_HEADS = 4
D_HEADS = 8
M_HEADS = 4
N_EXPERTS = 32
TOP_K = 4
SWIGLU_ALPHA = 1.702
SWIGLU_LIMIT = 7.0
LN_EPS = 1e-5
RMS_EPS = 1e-5
ROPE_BASE = 10000.0
LOG2E = 1.4426950408889634

LANES = 128
SUBLANES = 8
VMEM_LIMIT = 58 * 1024 * 1024

IN_TM = 512
IN_TN = 2048
RET_CHUNK = 256
ATT_BLK = 512
ATT_UNROLL = 8
ATT_HEADS = 1
MIX_TM = 256
MEM_TM = 512
MOE_BM = 256
CAST_ROWS = 256
GATHER_ROWS = 256
COMBINE_TM = 128


def _cparams(sem):
    return pltpu.CompilerParams(dimension_semantics=sem, vmem_limit_bytes=VMEM_LIMIT)


def _layer_norm(x, g, b):
    mu = jnp.mean(x, axis=-1, keepdims=True)
    xc = x - mu
    var = jnp.mean(xc * xc, axis=-1, keepdims=True)
    return xc * lax.rsqrt(var + LN_EPS) * g + b


def _dot(a, b):
    return jnp.dot(a, b, preferred_element_type=F32)


def _dot_nt(a, b):
    return lax.dot_general(a, b, (((1,), (1,)), ((), ())), preferred_element_type=F32)


def _dot_tn(a, b):
    return lax.dot_general(a, b, (((0,), (0,)), ((), ())), preferred_element_type=F32)


def _lo(i, n):
    return jnp.minimum(i, n - 1)


def _hi(i, n):
    return jnp.maximum(i - n, 0)


def _in_proj_kernel(xp_ref, xs_ref, g_ref, b_ref, w_ref, p_ref, dkp_ref, dvp_ref, dks_ref, dvs_ref,
                    h_scr, *, n_p):
    i = pl.program_id(0)
    j = pl.program_id(1)
    half = dkp_ref.shape[1]

    @pl.when(jnp.logical_and(j == 0, i < n_p))
    def _():
        h_scr[...] = _layer_norm(xp_ref[...], g_ref[...], b_ref[...]).astype(BF16)

    @pl.when(jnp.logical_and(j == 0, i >= n_p))
    def _():
        h_scr[...] = _layer_norm(xs_ref[...], g_ref[...], b_ref[...]).astype(BF16)

    acc = _dot(h_scr[...], w_ref[...])
    p_ref[...] = acc.astype(BF16)

    @pl.when(jnp.logical_and(j == 0, i < n_p))
    def _():
        dkp_ref[...] = acc[:, :half]
        dvp_ref[...] = acc[:, half:]

    @pl.when(jnp.logical_and(j == 0, i >= n_p))
    def _():
        dks_ref[...] = acc[:, :half]
        dvs_ref[...] = acc[:, half:]


def _in_proj(xp, xs, g, b, w, kv_blk):
    Tp, D = xp.shape
    Ts = xs.shape[0]
    N = w.shape[1]
    n_p, n_s = Tp // IN_TM, Ts // IN_TM
    half = IN_TN // 2

    def col(j):
        return jnp.where(j == 0, kv_blk, jnp.where(j <= kv_blk, j - 1, j))

    single = pl.Buffered(1)
    kv_p = pl.BlockSpec((IN_TM, half), lambda i, j: (_lo(i, n_p), 0))
    kv_s = pl.BlockSpec((IN_TM, half), lambda i, j: (_hi(i, n_p), 0), pipeline_mode=single)
    return pl.pallas_call(
        functools.partial(_in_proj_kernel, n_p=n_p),
        grid=(n_p + n_s, N // IN_TN),
        in_specs=[
            pl.BlockSpec((IN_TM, D), lambda i, j: (_lo(i, n_p), 0)),
            pl.BlockSpec((IN_TM, D), lambda i, j: (_hi(i, n_p), 0), pipeline_mode=single),
            pl.BlockSpec((1, D), lambda i, j: (0, 0)),
            pl.BlockSpec((1, D), lambda i, j: (0, 0)),
            pl.BlockSpec((D, IN_TN), lambda i, j: (0, col(j))),
        ],
        out_specs=[pl.BlockSpec((IN_TM, IN_TN), lambda i, j: (i, j)), kv_p, kv_p, kv_s, kv_s],
        out_shape=[
            jax.ShapeDtypeStruct((Tp + Ts, N), BF16),
            jax.ShapeDtypeStruct((Tp, half), F32),
            jax.ShapeDtypeStruct((Tp, half), F32),
            jax.ShapeDtypeStruct((Ts, half), F32),
            jax.ShapeDtypeStruct((Ts, half), F32),
        ],
        scratch_shapes=[pltpu.VMEM((IN_TM, D), BF16)],
        compiler_params=_cparams(("arbitrary", "arbitrary")),
        name="in_proj",
    )(xp, xs, g, b, w)


P_DK = 0
P_DV = 8
P_RQ = 16
P_RK = 20
P_RV = 24
P_RG = 32
P_DQ = 40
P_GATE = 48


def _retention_kernel(q_ref, k_ref, v_ref, g_ref, cos_ref, sin_ref, dm_ref, qd_ref, kd_ref,
                      bd_ref, s0_ref, y_ref, sout_ref, s_scr, *, n_chunks, dk, dv):
    c = pl.program_id(1)

    @pl.when(c == 0)
    def _():
        s_scr[...] = s0_ref[0]

    cos = cos_ref[...]
    sin = sin_ref[...]
    k_scale = dk ** -0.5
    for h in range(R_HEADS):
        q = q_ref[:, h * dk:(h + 1) * dk].astype(F32)
        k = k_ref[:, h * dk:(h + 1) * dk].astype(F32)
        v = v_ref[:, h * dv:(h + 1) * dv]
        g = g_ref[:, h * dv:(h + 1) * dv].astype(F32)
        q = q * cos + pltpu.roll(q, dk // 2, 1) * sin
        k = (k * cos + pltpu.roll(k, dk // 2, 1) * sin) * k_scale
        qb = q.astype(BF16)
        kb = k.astype(BF16)
        s = s_scr[h]
        scores = _dot_nt(qb, kb) * dm_ref[h]
        o = _dot(scores.astype(BF16), v) + _dot((q * qd_ref[h]).astype(BF16), s.astype(BF16))
        s_scr[h] = bd_ref[h] * s + _dot_tn((k * kd_ref[h]).astype(BF16), v)
        o = o * lax.rsqrt(jnp.mean(o * o, axis=-1, keepdims=True) + RMS_EPS)
        y_ref[:, h * dv:(h + 1) * dv] = (g * jax.nn.sigmoid(g) * o).astype(BF16)

    @pl.when(c == n_chunks - 1)
    def _():
        sout_ref[0] = s_scr[...]


def _retention(P, row_off, B, L, pos0, S0):
    dk, dv = 128, 256
    C = min(L, RET_CHUNK)
    nc = L // C
    ob = row_off // C
    pos = (pos0 + jnp.arange(L, dtype=jnp.int32)).astype(F32)
    inv = 1.0 / (ROPE_BASE ** jnp.linspace(0.0, 1.0, dk // 2, dtype=F32))
    ang = pos[:, None] * inv[None, :]
    cos = jnp.concatenate([jnp.cos(ang), jnp.cos(ang)], axis=-1)
    sin = jnp.concatenate([-jnp.sin(ang), jnp.sin(ang)], axis=-1)
    log_g = jnp.log1p(-jnp.power(2.0, -5.0 - jnp.arange(R_HEADS, dtype=F32)))
    i = jnp.arange(C, dtype=F32)
    rel = i[:, None] - i[None, :]
    dmask = jnp.where(rel >= 0, jnp.exp(log_g[:, None, None] * jnp.maximum(rel, 0.0)), 0.0)
    q_decay = jnp.exp(log_g[:, None] * (i + 1.0))[..., None]
    k_decay = jnp.exp(log_g[:, None] * (C - 1.0 - i))[..., None]
    b_decay = jnp.exp(log_g * C)[:, None, None]

    def rows(b, c):
        return ob + b * nc + c

    kern = functools.partial(_retention_kernel, n_chunks=nc, dk=dk, dv=dv)
    return pl.pallas_call(
        kern,
        grid=(B, nc),
        in_specs=[
            pl.BlockSpec((C, R_HEADS * dk), lambda b, c: (rows(b, c), P_RQ * LANES // (R_HEADS * dk))),
            pl.BlockSpec((C, R_HEADS * dk), lambda b, c: (rows(b, c), P_RK * LANES // (R_HEADS * dk))),
            pl.BlockSpec((C, R_HEADS * dv), lambda b, c: (rows(b, c), P_RV * LANES // (R_HEADS * dv))),
            pl.BlockSpec((C, R_HEADS * dv), lambda b, c: (rows(b, c), P_RG * LANES // (R_HEADS * dv))),
            pl.BlockSpec((C, dk), lambda b, c: (c, 0)),
            pl.BlockSpec((C, dk), lambda b, c: (c, 0)),
            pl.BlockSpec((R_HEADS, C, C), lambda b, c: (0, 0, 0)),
            pl.BlockSpec((R_HEADS, C, 1), lambda b, c: (0, 0, 0)),
            pl.BlockSpec((R_HEADS, C, 1), lambda b, c: (0, 0, 0)),
            pl.BlockSpec((R_HEADS, 1, 1), lambda b, c: (0, 0, 0)),
            pl.BlockSpec((1, R_HEADS, dk, dv), lambda b, c: (b, 0, 0, 0)),
        ],
        out_specs=[
            pl.BlockSpec((C, R_HEADS * dv), lambda b, c: (b * nc + c, 0)),
            pl.BlockSpec((1, R_HEADS, dk, dv), lambda b, c: (b, 0, 0, 0)),
        ],
        out_shape=[
            jax.ShapeDtypeStruct((B * L, R_HEADS * dv), BF16),
            jax.ShapeDtypeStruct((B, R_HEADS, dk, dv), F32),
        ],
        scratch_shapes=[pltpu.VMEM((R_HEADS, dk, dv), F32)],
        compiler_params=_cparams(("parallel", "arbitrary")),
        name="retention",
    )(P, P, P, P, cos, sin, dmask, q_decay, k_decay, b_decay, S0)


def _diff_lambda(lam_ref, lambda_init):
    lv = lam_ref[...]
    a = jnp.sum(lv[0:1] * lv[1:2], axis=-1, keepdims=True)
    b = jnp.sum(lv[2:3] * lv[3:4], axis=-1, keepdims=True)
    return jnp.exp(a) - jnp.exp(b) + lambda_init


def _lane_tile(x, n):
    return jnp.concatenate([x] * n, axis=1)


def _stack_maps(q, hd):
    lane = lax.broadcasted_iota(jnp.int32, q.shape, 1)
    zero = jnp.zeros_like(q)
    return jnp.concatenate([jnp.where(lane < hd, q, zero), jnp.where(lane < hd, zero, q)], axis=0)


def _diff_finish(acc, l, lam, subln, lambda_init, tq):
    o = acc[:tq] / l[:tq] - lam * (acc[tq:] / l[tq:])
    o = o * lax.rsqrt(jnp.mean(o * o, axis=-1, keepdims=True) + RMS_EPS)
    return (o * subln * (1.0 - lambda_init)).astype(BF16)


def _attn_prompt_kernel(q_ref, k_ref, v_ref, lam_ref, sub_ref, y_ref, qs_scr, ve_scr, m_scr, acc_scr,
                        s_scr, *, blk, hd, lambda_init):
    qi = pl.program_id(2)
    n_heads = qs_scr.shape[0]
    dv = 2 * hd
    seq = v_ref.shape[0]
    head = lambda g: slice(g * dv, (g + 1) * dv)

    @pl.when(qi == 0)
    def _():
        for g in range(n_heads):
            ve_scr[g, :, :dv] = v_ref[:, head(g)]
            ve_scr[g, :, dv:] = jnp.ones((seq, dv), BF16)

    for g in range(n_heads):
        q = q_ref[:, head(g)].astype(F32) * (hd ** -0.5 * LOG2E)
        qs_scr[g] = _stack_maps(q.astype(BF16), hd)
    m_scr[...] = jnp.full_like(m_scr, -jnp.inf)
    acc_scr[...] = jnp.zeros_like(acc_scr)

    def scores(ki, slot):
        off = pl.multiple_of(ki * blk, blk)
        for g in range(n_heads):
            s_scr[g, slot] = _dot_nt(qs_scr[g], k_ref[pl.ds(off, blk), head(g)])

    def step(ki, slot, masked):
        off = pl.multiple_of(ki * blk, blk)
        for g in range(n_heads):
            s = s_scr[g, slot]
            if masked:
                row = lax.broadcasted_iota(jnp.int32, s.shape, 0)
                col = lax.broadcasted_iota(jnp.int32, s.shape, 1)
                qrow = jnp.where(row >= blk, row - blk, row)
                shift = CHUNK.bit_length() - 1
                s = jnp.where((col >> shift) <= (qrow >> shift), s, -1e30)
            m_prev = m_scr[g]
            m_new = jnp.maximum(m_prev, jnp.max(s, axis=-1, keepdims=True))
            alpha = jnp.exp2(m_prev - m_new)
            p = jnp.exp2(s - _lane_tile(m_new, blk // LANES))
            pv = _dot(p.astype(BF16), ve_scr[g, pl.ds(off, blk), :])
            acc_scr[g] = _lane_tile(alpha, 2 * dv // LANES) * acc_scr[g] + pv
            m_scr[g] = m_new

    def run(base, width):
        for u in range(width):
            scores(base + u + 1, (u + 1) % 2)
            step(base + u, u % 2, False)

    def body(j, carry):
        run(ATT_UNROLL * j, ATT_UNROLL)
        return carry

    scores(0, 0)
    n_main = qi // ATT_UNROLL
    lax.fori_loop(0, n_main, body, 0)
    done = n_main * ATT_UNROLL
    width = ATT_UNROLL // 2
    while width >= 2:
        @pl.when(((qi - done) // width) % 2 == 1)
        def _(width=width):
            run(qi - (qi - done) % (2 * width), width)
        width //= 2

    @pl.when(qi % 2 == 1)
    def _():
        scores(qi, 1)
        step(qi - 1, 0, False)
        step(qi, 1, True)

    @pl.when(qi % 2 == 0)
    def _():
        step(qi, 0, True)

    lam = _diff_lambda(lam_ref, lambda_init)
    for g in range(n_heads):
        acc = acc_scr[g]
        o = acc[:blk, :dv] / acc[:blk, dv:] - lam * (acc[blk:, :dv] / acc[blk:, dv:])
        o = o * lax.rsqrt(jnp.mean(o * o, axis=-1, keepdims=True) + RMS_EPS)
        y_ref[:, head(g)] = (o * sub_ref[...] * (1.0 - lambda_init)).astype(BF16)


def _attn_prompt(P, B, S, lam_p, subln, lambda_init):
    hd = 64
    blk = min(ATT_BLK, S)
    nq = S // blk
    g = ATT_HEADS
    w = g * 2 * hd
    kern = functools.partial(_attn_prompt_kernel, blk=blk, hd=hd, lambda_init=lambda_init)
    return pl.pallas_call(
        kern,
        grid=(B, D_HEADS // g, nq),
        in_specs=[
            pl.BlockSpec((blk, w), lambda b, h, i: (b * nq + i, P_DQ // g + h)),
            pl.BlockSpec((S, w), lambda b, h, i: (b, P_DK // g + h)),
            pl.BlockSpec((S, w), lambda b, h, i: (b, P_DV // g + h)),
            pl.BlockSpec((4, hd), lambda b, h, i: (0, 0)),
            pl.BlockSpec((1, 2 * hd), lambda b, h, i: (0, 0)),
        ],
        out_specs=pl.BlockSpec((blk, w), lambda b, h, i: (b * nq + i, h)),
        out_shape=jax.ShapeDtypeStruct((B * S, D_HEADS * 2 * hd), BF16),
        scratch_shapes=[
            pltpu.VMEM((g, 2 * blk, 2 * hd), BF16),
            pltpu.VMEM((g, S, 4 * hd), BF16),
            pltpu.VMEM((g, 2 * blk, LANES), F32),
            pltpu.VMEM((g, 2 * blk, 4 * hd), F32),
            pltpu.VMEM((g, 2, 2 * blk, blk), F32),
        ],
        compiler_params=_cparams(("parallel", "parallel", "arbitrary")),
        name="attn_prompt",
    )(P, P, P, lam_p, subln)


def _attn_sample_kernel(q_ref, kn_ref, vn_ref, kc_ref, vc_ref, lam_ref, sub_ref, y_ref,
                        *, tq, hd, lambda_init):
    lam = _diff_lambda(lam_ref, lambda_init)
    w = 2 * hd
    past = kc_ref.shape[1] // D_HEADS
    for h in range(D_HEADS):
        cols = slice(h * w, (h + 1) * w)
        qs = _stack_maps(q_ref[:, cols] * (hd ** -0.5), hd)
        kc = kc_ref[0, pl.ds(h, past, stride=D_HEADS), :].astype(BF16)
        vc = vc_ref[0, pl.ds(h, past, stride=D_HEADS), :].astype(BF16)
        kn = kn_ref[:, cols]
        vn = vn_ref[:, cols]
        s_c = _dot_nt(qs, kc)
        s_n = _dot_nt(qs, kn)
        m = jnp.maximum(jnp.max(s_c, axis=-1, keepdims=True), jnp.max(s_n, axis=-1, keepdims=True))
        p_c = jnp.exp(s_c - m)
        p_n = jnp.exp(s_n - m)
        l = jnp.sum(p_c, axis=-1, keepdims=True) + jnp.sum(p_n, axis=-1, keepdims=True)
        acc = _dot(p_c.astype(BF16), vc) + _dot(p_n.astype(BF16), vn)
        y_ref[:, cols] = _diff_finish(acc, l, lam, sub_ref[...], lambda_init, tq)


def _attn_sample(P, row_off, B, L, cache_k, cache_v, lam_p, subln, lambda_init):
    hd = 64
    past = cache_k.shape[1]
    ob = row_off // L
    width = D_HEADS * 2 * hd
    kern = functools.partial(_attn_sample_kernel, tq=L, hd=hd, lambda_init=lambda_init)
    cache_k = cache_k.reshape(B, past * D_HEADS, 2 * hd)
    cache_v = cache_v.reshape(B, past * D_HEADS, 2 * hd)
    cache_spec = pl.BlockSpec((1, past * D_HEADS, 2 * hd), lambda b: (b, 0, 0))
    return pl.pallas_call(
        kern,
        grid=(B,),
        in_specs=[
            pl.BlockSpec((L, width), lambda b: (ob + b, P_DQ * LANES // width)),
            pl.BlockSpec((L, width), lambda b: (ob + b, P_DK * LANES // width)),
            pl.BlockSpec((L, width), lambda b: (ob + b, P_DV * LANES // width)),
            cache_spec,
            cache_spec,
            pl.BlockSpec((4, hd), lambda b: (0, 0)),
            pl.BlockSpec((1, 2 * hd), lambda b: (0, 0)),
        ],
        out_specs=pl.BlockSpec((L, width), lambda b: (b, 0)),
        out_shape=jax.ShapeDtypeStruct((B * L, width), BF16),
        compiler_params=_cparams(("parallel",)),
        name="attn_sample",
    )(P, P, P, cache_k, cache_v, lam_p, subln)


def _mix_kernel(xp_ref, xs_ref, yrp_ref, yrs_ref, ydp_ref, yds_ref, gr_ref, gd_ref, lig_ref, lib_ref,
                bg_ref, wpr_ref, wpd_ref, wo_ref, g1_ref, b1_ref, h1_ref, *, alpha, n_p):
    i = pl.program_id(0)
    d = xp_ref.shape[1]

    def compute(x_ref, yr_ref, yd_ref):
        h0 = _layer_norm(x_ref[...], lig_ref[...], lib_ref[...])
        g_ret = jax.nn.sigmoid(gr_ref[...].astype(F32) + bg_ref[:, :d])
        g_diff = jax.nn.sigmoid(gd_ref[...].astype(F32) + bg_ref[:, d:])
        merged = g_ret * _dot(yr_ref[...], wpr_ref[...]) + g_diff * _dot(yd_ref[...], wpd_ref[...])
        mixed = _dot(merged.astype(BF16), wo_ref[...])
        h1_ref[...] = _layer_norm(alpha * h0 + mixed, g1_ref[...], b1_ref[...])

    @pl.when(i < n_p)
    def _():
        compute(xp_ref, yrp_ref, ydp_ref)

    @pl.when(i >= n_p)
    def _():
        compute(xs_ref, yrs_ref, yds_ref)


def _mix(xp, xs, yr_p, yr_s, yd_p, yd_s, P, ln_in_g, ln_in_b, b_gate, wpr, wpd, wo, g1, b1, alpha):
    Tp, D = xp.shape
    Ts = xs.shape[0]
    tm = MIX_TM
    n_p, n_s = Tp // tm, Ts // tm
    gcol = P_GATE * LANES // D
    const = lambda i: (0, 0)
    lo = lambda i: (_lo(i, n_p), 0)
    hi = lambda i: (_hi(i, n_p), 0)
    single = pl.Buffered(1)
    kern = functools.partial(_mix_kernel, alpha=alpha, n_p=n_p)
    return pl.pallas_call(
        kern,
        grid=(n_p + n_s,),
        in_specs=[
            pl.BlockSpec((tm, D), lo),
            pl.BlockSpec((tm, D), hi),
            pl.BlockSpec((tm, yr_p.shape[1]), lo),
            pl.BlockSpec((tm, yr_s.shape[1]), hi),
            pl.BlockSpec((tm, yd_p.shape[1]), lo),
            pl.BlockSpec((tm, yd_s.shape[1]), hi),
            pl.BlockSpec((tm, D), lambda i: (i, gcol)),
            pl.BlockSpec((tm, D), lambda i: (i, gcol + 1)),
            pl.BlockSpec((1, D), const),
            pl.BlockSpec((1, D), const),
            pl.BlockSpec((1, 2 * D), const),
            pl.BlockSpec(wpr.shape, const, pipeline_mode=single),
            pl.BlockSpec(wpd.shape, const, pipeline_mode=single),
            pl.BlockSpec(wo.shape, const, pipeline_mode=single),
            pl.BlockSpec((1, D), const),
            pl.BlockSpec((1, D), const),
        ],
        out_specs=pl.BlockSpec((tm, D), lambda i: (i, 0)),
        out_shape=jax.ShapeDtypeStruct((Tp + Ts, D), F32),
        compiler_params=_cparams(("arbitrary",)),
        name="mix",
    )(xp, xs, yr_p, yr_s, yd_p, yd_s, P, P, ln_in_g, ln_in_b, b_gate, wpr, wpd, wo, g1, b1)


def _mem_kernel(h1_ref, mkp_ref, mvp_ref, mks_ref, mvs_ref, wq_ref, wo_ref, g2_ref, b2_ref, rw_ref,
                rb_ref, h2_ref, route_ref, counts_ref, q_scr, o_scr, cnt_scr, *, n_p, alpha, hd):
    i = pl.program_id(0)
    tm = h1_ref.shape[0]
    h1 = h1_ref[...]
    q_scr[...] = (_dot(h1.astype(BF16), wq_ref[...]) * (hd ** -0.5)).astype(BF16)

    def attend(mk_ref, mv_ref):
        n_sub = mk_ref.shape[0]
        seg = tm // n_sub
        split_heads = mk_ref.shape[2] == hd
        for s in range(n_sub):
            for h in range(M_HEADS):
                if split_heads:
                    rows = pl.ds(h, mk_ref.shape[1] // M_HEADS, stride=M_HEADS)
                    mk = mk_ref[s, rows, :].astype(BF16)
                    mv = mv_ref[s, rows, :].astype(BF16)
                else:
                    mk = mk_ref[s, :, h * hd:(h + 1) * hd].astype(BF16)
                    mv = mv_ref[s, :, h * hd:(h + 1) * hd].astype(BF16)
                qh = q_scr[s * seg:(s + 1) * seg, h * hd:(h + 1) * hd]
                sc = _dot_nt(qh, mk)
                sc = sc - jnp.max(sc, axis=-1, keepdims=True)
                p = jnp.exp(sc)
                p = p / jnp.sum(p, axis=-1, keepdims=True)
                o_scr[s * seg:(s + 1) * seg, h * hd:(h + 1) * hd] = _dot(
                    p.astype(BF16), mv).astype(BF16)

    @pl.when(i < n_p)
    def _():
        attend(mkp_ref, mvp_ref)

    @pl.when(i >= n_p)
    def _():
        attend(mks_ref, mvs_ref)

    h2 = _layer_norm(alpha * h1 + _dot(o_scr[...], wo_ref[...]), g2_ref[...], b2_ref[...])
    h2_ref[...] = h2
    h2b = h2.astype(BF16)

    h_lo = (h2 - h2b.astype(F32)).astype(BF16)
    rw = rw_ref[...]
    rw_hi = rw.astype(BF16)
    rw_lo = (rw - rw_hi.astype(F32)).astype(BF16)
    logits = _dot(h2b, rw_hi) + _dot(h2b, rw_lo) + _dot(h_lo, rw_hi) + rb_ref[...]

    n_e = logits.shape[1]
    eidx = lax.broadcasted_iota(jnp.int32, logits.shape, 1).astype(F32)
    lane = lax.broadcasted_iota(jnp.int32, (tm, LANES), 1)
    route = jnp.zeros((tm, LANES), F32)
    work = logits
    vals = []
    sels = []
    for k in range(TOP_K):
        mx = jnp.max(work, axis=-1, keepdims=True)
        sel = jnp.min(jnp.where(work == mx, eidx, float(n_e)), axis=-1, keepdims=True)
        work = jnp.where(eidx == sel, -jnp.inf, work)
        vals.append(mx)
        sels.append(sel)
        route = jnp.where(lane == TOP_K + k, sel, route)
    ex = [jnp.exp(v - vals[0]) for v in vals]
    den = ex[0] + ex[1] + ex[2] + ex[3]
    for k in range(TOP_K):
        route = jnp.where(lane == k, ex[k] / den, route)

    @pl.when(i == 0)
    def _():
        cnt_scr[...] = jnp.zeros_like(cnt_scr)

    lane_f = lane.astype(F32)
    hit = [lane_f == sels[k] for k in range(TOP_K)]
    cnt = sum(h.astype(F32) for h in hit)
    r_io = lax.broadcasted_iota(jnp.int32, (tm, tm), 0)
    c_io = lax.broadcasted_iota(jnp.int32, (tm, tm), 1)
    ltri = jnp.where(r_io > c_io, 1.0, 0.0).astype(BF16)
    excl = _dot(ltri, cnt.astype(BF16)) + cnt_scr[...]
    for k in range(TOP_K):
        rank = jnp.sum(jnp.where(hit[k], excl, 0.0), axis=-1, keepdims=True)
        route = jnp.where(lane == 2 * TOP_K + k, rank, route)
    route_ref[...] = route
    cnt_scr[...] = cnt_scr[...] + jnp.sum(cnt, axis=0, keepdims=True)
    counts_ref[...] = cnt_scr[...]


def _mem_attn(h1, t_prompt, mem_kp, mem_vp, l_prompt, mem_ks, mem_vs, l_sample, wq, wo, g2, b2, rw, rb,
              alpha):
    T, D = h1.shape
    hd = 128
    tm = MEM_TM
    assert l_prompt % tm == 0 and tm % l_sample == 0
    n_p = t_prompt // tm
    n_s = (T - t_prompt) // tm
    per_b = l_prompt // tm
    sub_s = tm // l_sample
    n_mem = mem_kp.shape[1]
    bp = mem_kp.shape[0]
    const = lambda i: (0, 0)
    single = pl.Buffered(1)
    mem_p = pl.BlockSpec((1, n_mem, M_HEADS * hd), lambda i: (jnp.minimum(i // per_b, bp - 1), 0, 0))
    bs = mem_ks.shape[0]
    mem_ks = mem_ks.reshape(bs, n_mem * M_HEADS, hd)
    mem_vs = mem_vs.reshape(bs, n_mem * M_HEADS, hd)
    mem_s = pl.BlockSpec((sub_s, n_mem * M_HEADS, hd), lambda i: (_hi(i, n_p), 0, 0))
    kern = functools.partial(_mem_kernel, n_p=n_p, alpha=alpha, hd=hd)
    return pl.pallas_call(
        kern,
        grid=(n_p + n_s,),
        in_specs=[
            pl.BlockSpec((tm, D), lambda i: (i, 0)),
            mem_p, mem_p, mem_s, mem_s,
            pl.BlockSpec(wq.shape, const, pipeline_mode=single),
            pl.BlockSpec(wo.shape, const, pipeline_mode=single),
            pl.BlockSpec((1, D), const),
            pl.BlockSpec((1, D), const),
            pl.BlockSpec(rw.shape, const, pipeline_mode=single),
            pl.BlockSpec((1, rw.shape[1]), const),
        ],
        out_specs=[
            pl.BlockSpec((tm, D), lambda i: (i, 0)),
            pl.BlockSpec((tm, LANES), lambda i: (i, 0)),
            pl.BlockSpec((1, LANES), const),
        ],
        out_shape=[
            jax.ShapeDtypeStruct((T, D), F32),
            jax.ShapeDtypeStruct((T, LANES), F32),
            jax.ShapeDtypeStruct((1, LANES), F32),
        ],
        scratch_shapes=[
            pltpu.VMEM((tm, M_HEADS * hd), BF16),
            pltpu.VMEM((tm, M_HEADS * hd), BF16),
            pltpu.VMEM((1, LANES), F32),
        ],
        compiler_params=_cparams(("arbitrary",)),
        name="mem_attn",
    )(h1, mem_kp, mem_vp, mem_ks, mem_vs, wq, wo, g2, b2, rw, rb)


def _mem_kv_kernel(x_ref, w_ref, o_ref):
    o_ref[...] = _dot(x_ref[...].astype(BF16), w_ref[...])


def _mem_kv(mem, w):
    R, D = mem.shape
    N = w.shape[1]
    tm = 256
    return pl.pallas_call(
        _mem_kv_kernel,
        grid=(R // tm,),
        in_specs=[pl.BlockSpec((tm, D), lambda i: (i, 0)), pl.BlockSpec((D, N), lambda i: (0, 0))],
        out_specs=pl.BlockSpec((tm, N), lambda i: (i, 0)),
        out_shape=jax.ShapeDtypeStruct((R, N), F32),
        compiler_params=_cparams(("parallel",)),
        name="mem_kv",
    )(mem, w)


def _row_copy(src_hbm, dst, sem, src_row, dst_row):
    return pltpu.make_async_copy(src_hbm.at[pl.ds(src_row, 1)], dst.at[pl.ds(dst_row, 1)], sem)


def _dispatch_kernel(dest_ref, pend_ref, padded_ref, nu_ref, h_ref, xs_hbm, stage, zeros, sem, zsem,
                     *, tm, bm, n_steps, n_blocks):
    i = pl.program_id(0)
    slot = i % 2

    def zero_block(off):
        return pltpu.make_async_copy(zeros, xs_hbm.at[pl.ds(pl.multiple_of(off, bm), bm)], zsem)

    def zero_fill(act):
        for e in range(N_EXPERTS):
            @pl.when(padded_ref[e] > 0)
            def _(e=e):
                act(zero_block(pend_ref[e] - bm))
        for b in range(n_blocks - N_EXPERTS, n_blocks):
            @pl.when(b >= nu_ref[0])
            def _(b=b):
                act(zero_block(b * bm))

    @pl.when(i == 0)
    def _():
        zeros[...] = jnp.zeros_like(zeros)
        zero_fill(lambda c: c.start())
        zero_fill(lambda c: c.wait())

    def wait(s):
        for _ in range(TOP_K):
            pltpu.make_async_copy(stage.at[s], xs_hbm.at[pl.ds(0, tm)], sem.at[s]).wait()

    @pl.when(i >= 2)
    def _():
        wait(slot)

    stage[slot] = h_ref[...]

    def body(j, carry):
        t0 = pl.multiple_of(j * SUBLANES, SUBLANES)
        for u in range(SUBLANES):
            for k in range(TOP_K):
                row = dest_ref[(i * tm + t0 + u) * TOP_K + k]
                pltpu.make_async_copy(stage.at[slot, pl.ds(t0 + u, 1)], xs_hbm.at[pl.ds(row, 1)],
                                      sem.at[slot]).start(priority=k % 2)
        return carry
    lax.fori_loop(0, tm // SUBLANES, body, 0)

    @pl.when(i == n_steps - 1)
    def _():
        wait(slot)
        if n_steps > 1:
            wait(1 - slot)


def _moe_dispatch(h2, dest, pad_end, padded, n_used, n_rows, bm):
    T, D = h2.shape
    tm = GATHER_ROWS
    n_steps = T // tm
    kern = functools.partial(_dispatch_kernel, tm=tm, bm=bm, n_steps=n_steps, n_blocks=n_rows // bm)
    return pl.pallas_call(
        kern,
        grid_spec=pltpu.PrefetchScalarGridSpec(
            num_scalar_prefetch=4,
            grid=(n_steps,),
            in_specs=[pl.BlockSpec((tm, D), lambda i, d, pe, pd, nu: (i, 0))],
            out_specs=pl.BlockSpec(memory_space=pl.ANY),
            scratch_shapes=[
                pltpu.VMEM((2, tm, D), F32),
                pltpu.VMEM((bm, D), F32),
                pltpu.SemaphoreType.DMA((2,)),
                pltpu.SemaphoreType.DMA(()),
            ],
        ),
        out_shape=jax.ShapeDtypeStruct((n_rows, D), F32),
        compiler_params=_cparams(("arbitrary",)),
        name="moe_dispatch",
    )(dest, pad_end, padded, n_used, h2)


def _cast_rows(src, dst):
    rows = CAST_ROWS

    def body(i, carry):
        r = pl.multiple_of(i * rows, rows)
        dst[pl.ds(r, rows), :] = src[pl.ds(r, rows), :].astype(BF16)
        return carry
    lax.fori_loop(0, src.shape[0] // rows, body, 0)


def _expert_weights(be_ref, first_ref, next_ref, copies, cast, m):
    @pl.when(first_ref[m] == 1)
    def _():
        @pl.when(m == 0)
        def _():
            for c in copies(be_ref[0]):
                c.start()

        for c in copies(be_ref[m]):
            c.wait()
        cast()
        ne = next_ref[m]

        @pl.when(ne >= 0)
        def _():
            for c in copies(ne):
                c.start(priority=1)


def _moe_up_kernel(be_ref, nu_ref, first_ref, next_ref, x_ref, w1_hbm, b_ref, a_ref,
                   stage, w_scr, sem):
    m = pl.program_id(0)
    ff = w_scr.shape[2]

    def copies(e):
        return [pltpu.make_async_copy(w1_hbm.at[e, :, pl.ds(t * ff, ff)], stage.at[t], sem)
                for t in range(2)]

    def cast():
        for t in range(2):
            _cast_rows(stage.at[t], w_scr.at[t])

    _expert_weights(be_ref, first_ref, next_ref, copies, cast, m)

    @pl.when(m < nu_ref[0])
    def _():
        x = x_ref[...].astype(BF16)
        half = ff // 2
        for c in range(2):
            cols = slice(c * half, (c + 1) * half)
            lin_cols = slice(ff + c * half, ff + (c + 1) * half)
            u_glu = jnp.minimum(_dot(x, w_scr[0, :, cols]) + b_ref[0, :, cols], SWIGLU_LIMIT)
            u_lin = jnp.clip(_dot(x, w_scr[1, :, cols]) + b_ref[0, :, lin_cols],
                             -SWIGLU_LIMIT, SWIGLU_LIMIT)
            a = u_glu * jax.nn.sigmoid(SWIGLU_ALPHA * u_glu) * (u_lin + 1.0)
            a_ref[:, cols] = a.astype(BF16)

    @pl.when(m >= nu_ref[0])
    def _():
        a_ref[...] = jnp.zeros_like(a_ref)


def _moe_up(xs, w1, b1, plan):
    n_rows, D = xs.shape
    F = w1.shape[2] // 2
    bm = MOE_BM
    return pl.pallas_call(
        _moe_up_kernel,
        grid_spec=pltpu.PrefetchScalarGridSpec(
            num_scalar_prefetch=4,
            grid=(n_rows // bm,),
            in_specs=[
                pl.BlockSpec((bm, D), lambda m, be, nu, fi, nx: (jnp.minimum(m, nu[0] - 1), 0)),
                pl.BlockSpec(memory_space=pl.ANY),
                pl.BlockSpec((1, 1, 2 * F), lambda m, be, nu, fi, nx: (be[m], 0, 0)),
            ],
            out_specs=pl.BlockSpec((bm, F), lambda m, be, nu, fi, nx: (m, 0)),
            scratch_shapes=[
                pltpu.VMEM((2, D, F), F32),
                pltpu.VMEM((2, D, F), BF16),
                pltpu.SemaphoreType.DMA(()),
            ],
        ),
        out_shape=jax.ShapeDtypeStruct((n_rows, F), BF16),
        compiler_params=_cparams(("arbitrary",)),
        name="moe_up",
    )(*plan, xs, w1, b1)


def _moe_down_kernel(be_ref, nu_ref, first_ref, next_ref, a_ref, w2_hbm, b_ref, y_ref,
                     stage, w_scr, sem):
    m = pl.program_id(0)

    def copies(e):
        return [pltpu.make_async_copy(w2_hbm.at[e], stage, sem)]

    def cast():
        _cast_rows(stage, w_scr)

    _expert_weights(be_ref, first_ref, next_ref, copies, cast, m)

    @pl.when(m < nu_ref[0])
    def _():
        y_ref[...] = _dot(a_ref[...], w_scr[...]) + b_ref[0]

    @pl.when(m >= nu_ref[0])
    def _():
        y_ref[...] = jnp.zeros_like(y_ref)


def _moe_down(a, w2, b2, plan):
    n_rows, F = a.shape
    D = w2.shape[2]
    bm = MOE_BM
    return pl.pallas_call(
        _moe_down_kernel,
        grid_spec=pltpu.PrefetchScalarGridSpec(
            num_scalar_prefetch=4,
            grid=(n_rows // bm,),
            in_specs=[
                pl.BlockSpec((bm, F), lambda m, be, nu, fi, nx: (jnp.minimum(m, nu[0] - 1), 0)),
                pl.BlockSpec(memory_space=pl.ANY),
                pl.BlockSpec((1, 1, D), lambda m, be, nu, fi, nx: (be[m], 0, 0)),
            ],
            out_specs=pl.BlockSpec((bm, D), lambda m, be, nu, fi, nx: (m, 0)),
            scratch_shapes=[
                pltpu.VMEM((F, D), F32),
                pltpu.VMEM((F, D), BF16),
                pltpu.SemaphoreType.DMA(()),
            ],
        ),
        out_shape=jax.ShapeDtypeStruct((n_rows, D), F32),
        compiler_params=_cparams(("arbitrary",)),
        name="moe_down",
    )(*plan, a, w2, b2)


def _combine_kernel(dest_ref, h2_ref, rt_ref, g_ref, b_ref, ys_hbm, op_ref, os_ref, buf, sem,
                    *, tm, n_steps, n_prompt, alpha):
    i = pl.program_id(0)

    def issue(step, slot):
        def body(j, carry):
            t0 = pl.multiple_of(j * SUBLANES, SUBLANES)
            for u in range(SUBLANES):
                for k in range(TOP_K):
                    row = dest_ref[(step * tm + t0 + u) * TOP_K + k]
                    _row_copy(ys_hbm, buf.at[slot, k], sem.at[slot], row, t0 + u).start(priority=k % 2)
            return carry
        lax.fori_loop(0, tm // SUBLANES, body, 0)

    def wait(slot):
        for k in range(TOP_K):
            pltpu.make_async_copy(ys_hbm.at[pl.ds(0, tm)], buf.at[slot, k], sem.at[slot]).wait()

    slot = i % 2

    @pl.when(i == 0)
    def _():
        issue(0, 0)

    @pl.when(i + 1 < n_steps)
    def _():
        issue(i + 1, 1 - slot)

    wait(slot)
    y = rt_ref[:, 0:1] * buf[slot, 0]
    for k in range(1, TOP_K):
        y = y + rt_ref[:, k:k + 1] * buf[slot, k]
    out = _layer_norm(alpha * h2_ref[...] + y, g_ref[...], b_ref[...])

    @pl.when(i < n_prompt)
    def _():
        op_ref[...] = out

    @pl.when(i >= n_prompt)
    def _():
        os_ref[...] = out


def _moe_combine(ys, dest, h2, route, g3, b3, t_prompt, alpha):
    T, D = h2.shape
    tm = COMBINE_TM
    n_steps = T // tm
    n_prompt = t_prompt // tm
    kern = functools.partial(_combine_kernel, tm=tm, n_steps=n_steps, n_prompt=n_prompt, alpha=alpha)
    return pl.pallas_call(
        kern,
        grid_spec=pltpu.PrefetchScalarGridSpec(
            num_scalar_prefetch=1,
            grid=(n_steps,),
            in_specs=[
                pl.BlockSpec((tm, D), lambda i, d: (i, 0)),
                pl.BlockSpec((tm, LANES), lambda i, d: (i, 0)),
                pl.BlockSpec((1, D), lambda i, d: (0, 0)),
                pl.BlockSpec((1, D), lambda i, d: (0, 0)),
                pl.BlockSpec(memory_space=pl.ANY),
            ],
            out_specs=[
                pl.BlockSpec((tm, D), lambda i, d: (jnp.minimum(i, n_prompt - 1), 0)),
                pl.BlockSpec((tm, D), lambda i, d: (jnp.maximum(i - n_prompt, 0), 0)),
            ],
            scratch_shapes=[pltpu.VMEM((2, TOP_K, tm, D), F32), pltpu.SemaphoreType.DMA((2,))],
        ),
        out_shape=[
            jax.ShapeDtypeStruct((t_prompt, D), F32),
            jax.ShapeDtypeStruct((T - t_prompt, D), F32),
        ],
        compiler_params=_cparams(("arbitrary",)),
        name="moe_combine",
    )(dest, h2, route, g3, b3, ys)


def _moe_plan(route, counts, bm):
    T = route.shape[0]
    top_idx = route[:, TOP_K:2 * TOP_K].astype(jnp.int32)
    rank = route[:, 2 * TOP_K:3 * TOP_K].astype(jnp.int32)
    counts = counts[0, :N_EXPERTS].astype(jnp.int32)
    padded = (counts + bm - 1) // bm * bm
    pad_end = jnp.cumsum(padded)
    pad_start = pad_end - padded
    dest = (pad_start[top_idx] + rank).reshape(-1)
    n_blocks = -(-(T * TOP_K) // bm) + N_EXPERTS
    block_start = jnp.arange(n_blocks, dtype=jnp.int32) * bm
    block_e = jnp.sum((pad_end[None, :] <= block_start[:, None]).astype(jnp.int32), axis=1)
    block_e = jnp.minimum(block_e, N_EXPERTS - 1)
    n_used = (pad_end[-1] // bm).astype(jnp.int32).reshape(1)
    blk = jnp.arange(n_blocks, dtype=jnp.int32)
    prev_e = jnp.concatenate([jnp.full((1,), -1, jnp.int32), block_e[:-1]])
    first = jnp.logical_and(blk < n_used[0], block_e != prev_e)
    later = jnp.logical_and(first[None, :], blk[None, :] > blk[:, None])
    nxt_blk = jnp.min(jnp.where(later, blk[None, :], n_blocks), axis=1)
    next_e = jnp.where(nxt_blk < n_blocks, block_e[jnp.minimum(nxt_blk, n_blocks - 1)], -1)
    plan = (block_e, n_used, first.astype(jnp.int32), next_e.astype(jnp.int32))
    return dest, pad_end, padded, plan, n_blocks * bm


def kernel(x_prompt, x_sample, cache_diff_k, cache_diff_v, state_ret, cache_mem_k, cache_mem_v, mem_prompt, ln_in_g, ln_in_b, w_in, b_gate, diff_lambda, diff_subln, w_proj_ret, w_proj_diff, w_out, ln1_g, ln1_b, w_mq, w_mk, w_mv, w_mo, ln2_g, ln2_b, router_w, router_b, w1, b1, w2, b2, ln3_g, ln3_b):
    Bp, Lp, D = x_prompt.shape
    Bs, Ls, _ = x_sample.shape
    depth = w_in.shape[0]
    assert depth == 1
    past = cache_diff_k.shape[2]
    n_mem = mem_prompt.shape[1]
    Tp, Ts = Bp * Lp, Bs * Ls
    alpha = (2.0 * depth) ** 0.25
    lambda_init = 0.8 - 0.6 * math.exp(-0.3 * 0)
    row = lambda v: v.reshape(1, -1)

    xp = x_prompt.reshape(Tp, D)
    xs_in = x_sample.reshape(Ts, D)
    kv_blk = 4096 // IN_TN
    P, dk_p, dv_p, dk_s, dv_s = _in_proj(xp, xs_in, row(ln_in_g), row(ln_in_b), w_in[0].astype(BF16),
                                         kv_blk)

    zeros_state = jnp.zeros((Bp,) + state_ret.shape[2:], F32)
    yr_p, s_p = _retention(P, 0, Bp, Lp, 0, zeros_state)
    yr_s, s_s = _retention(P, Tp, Bs, Ls, past, state_ret[0])

    yd_p = _attn_prompt(P, Bp, Lp, diff_lambda[0], row(diff_subln[0]), lambda_init)
    yd_s = _attn_sample(P, Tp, Bs, Ls, cache_diff_k[0], cache_diff_v[0], diff_lambda[0],
                        row(diff_subln[0]), lambda_init)

    h1 = _mix(xp, xs_in, yr_p, yr_s, yd_p, yd_s, P, row(ln_in_g), row(ln_in_b), row(b_gate[0]),
              w_proj_ret[0].astype(BF16), w_proj_diff[0].astype(BF16), w_out[0].astype(BF16),
              row(ln1_g[0]), row(ln1_b[0]), alpha)

    w_mkv = jnp.concatenate([w_mk[0], w_mv[0]], axis=1).astype(BF16)
    mkv = _mem_kv(mem_prompt.reshape(Bp * n_mem, D), w_mkv)
    hm = w_mk.shape[2]
    mk_p = mkv[:, :hm].reshape(Bp, n_mem, hm)
    mv_p = mkv[:, hm:].reshape(Bp, n_mem, hm)
    h2, route, counts = _mem_attn(
        h1, Tp, mk_p, mv_p, Lp, cache_mem_k[0], cache_mem_v[0], Ls, w_mq[0].astype(BF16),
        w_mo[0].astype(BF16),
        row(ln2_g[0]), row(ln2_b[0]), router_w[0], row(router_b[0]), alpha)

    dest, pad_end, padded, plan, n_rows = _moe_plan(route, counts, MOE_BM)
    xs = _moe_dispatch(h2, dest, pad_end, padded, plan[1], n_rows, MOE_BM)
    act = _moe_up(xs, w1[0], b1[0][:, None, :], plan)
    ys = _moe_down(act, w2[0], b2[0][:, None, :], plan)
    out_p, out_s = _moe_combine(ys, dest, h2, route, row(ln3_g[0]), row(ln3_b[0]), Tp, alpha)

    return (
        out_p.reshape(Bp, Lp, D),
        out_s.reshape(Bs, Ls, D),
        dk_p.reshape(1, Bp, Lp, D_HEADS, 128),
        dv_p.reshape(1, Bp, Lp, D_HEADS, 128),
        s_p[None],
        mk_p.reshape(1, Bp, n_mem, M_HEADS, hm // M_HEADS),
        mv_p.reshape(1, Bp, n_mem, M_HEADS, hm // M_HEADS),
        dk_s.reshape(1, Bs, Ls, D_HEADS, 128),
        dv_s.reshape(1, Bs, Ls, D_HEADS, 128),
        s_s[None],
    )
```

```python
import functools
import math

import jax
import jax.numpy as jnp
from jax import lax
from jax.experimental import pallas as pl
from jax.experimental.pallas import tpu as pltpu

F32 = jnp.float32
BF16 = jnp.bfloat16

CHUNK = 64
R_HEADS = 4
D_HEADS = 8
M_HEADS = 4
N_EXPERTS = 32
TOP_K = 4
SWIGLU_ALPHA = 1.702
SWIGLU_LIMIT = 7.0
LN_EPS = 1e-5
RMS_EPS = 1e-5
ROPE_BASE = 10000.0
LOG2E = 1.4426950408889634

LANES = 128
SUBLANES = 8
VMEM_LIMIT = 58 * 1024 * 1024

IN_TM = 512
IN_TN = 2048
RET_CHUNK = 256
ATT_BLK = 512
ATT_UNROLL = 4
MIX_TM = 256
MEM_TM = 512
MOE_BM = 256
CAST_ROWS = 256
GATHER_ROWS = 512
COMBINE_TM = 256


def _cparams(sem):
    return pltpu.CompilerParams(dimension_semantics=sem, vmem_limit_bytes=VMEM_LIMIT)


def _layer_norm(x, g, b):
    mu = jnp.mean(x, axis=-1, keepdims=True)
    xc = x - mu
    var = jnp.mean(xc * xc, axis=-1, keepdims=True)
    return xc * lax.rsqrt(var + LN_EPS) * g + b


def _dot(a, b):
    return jnp.dot(a, b, preferred_element_type=F32)


def _dot_nt(a, b):
    return lax.dot_general(a, b, (((1,), (1,)), ((), ())), preferred_element_type=F32)


def _dot_tn(a, b):
    return lax.dot_general(a, b, (((0,), (0,)), ((), ())), preferred_element_type=F32)


def _lo(i, n):
    return jnp.minimum(i, n - 1)


def _hi(i, n):
    return jnp.maximum(i - n, 0)


def _in_proj_kernel(xp_ref, xs_ref, g_ref, b_ref, w_ref, p_ref, dkp_ref, dvp_ref, dks_ref, dvs_ref,
                    h_scr, *, n_p):
    i = pl.program_id(0)
    j = pl.program_id(1)
    half = dkp_ref.shape[1]

    @pl.when(jnp.logical_and(j == 0, i < n_p))
    def _():
        h_scr[...] = _layer_norm(xp_ref[...], g_ref[...], b_ref[...]).astype(BF16)

    @pl.when(jnp.logical_and(j == 0, i >= n_p))
    def _():
        h_scr[...] = _layer_norm(xs_ref[...], g_ref[...], b_ref[...]).astype(BF16)

    acc = _dot(h_scr[...], w_ref[...])
    p_ref[...] = acc.astype(BF16)

    @pl.when(jnp.logical_and(j == 0, i < n_p))
    def _():
        dkp_ref[...] = acc[:, :half]
        dvp_ref[...] = acc[:, half:]

    @pl.when(jnp.logical_and(j == 0, i >= n_p))
    def _():
        dks_ref[...] = acc[:, :half]
        dvs_ref[...] = acc[:, half:]


def _in_proj(xp, xs, g, b, w, kv_blk):
    Tp, D = xp.shape
    Ts = xs.shape[0]
    N = w.shape[1]
    n_p, n_s = Tp // IN_TM, Ts // IN_TM
    half = IN_TN // 2

    def col(j):
        return jnp.where(j == 0, kv_blk, jnp.where(j <= kv_blk, j - 1, j))

    single = pl.Buffered(1)
    kv_p = pl.BlockSpec((IN_TM, half), lambda i, j: (_lo(i, n_p), 0))
    kv_s = pl.BlockSpec((IN_TM, half), lambda i, j: (_hi(i, n_p), 0), pipeline_mode=single)
    return pl.pallas_call(
        functools.partial(_in_proj_kernel, n_p=n_p),
        grid=(n_p + n_s, N // IN_TN),
        in_specs=[
            pl.BlockSpec((IN_TM, D), lambda i, j: (_lo(i, n_p), 0)),
            pl.BlockSpec((IN_TM, D), lambda i, j: (_hi(i, n_p), 0), pipeline_mode=single),
            pl.BlockSpec((1, D), lambda i, j: (0, 0)),
            pl.BlockSpec((1, D), lambda i, j: (0, 0)),
            pl.BlockSpec((D, IN_TN), lambda i, j: (0, col(j))),
        ],
        out_specs=[pl.BlockSpec((IN_TM, IN_TN), lambda i, j: (i, j)), kv_p, kv_p, kv_s, kv_s],
        out_shape=[
            jax.ShapeDtypeStruct((Tp + Ts, N), BF16),
            jax.ShapeDtypeStruct((Tp, half), F32),
            jax.ShapeDtypeStruct((Tp, half), F32),
            jax.ShapeDtypeStruct((Ts, half), F32),
            jax.ShapeDtypeStruct((Ts, half), F32),
        ],
        scratch_shapes=[pltpu.VMEM((IN_TM, D), BF16)],
        compiler_params=_cparams(("arbitrary", "arbitrary")),
        name="in_proj",
    )(xp, xs, g, b, w)


P_DK = 0
P_DV = 8
P_RQ = 16
P_RK = 20
P_RV = 24
P_RG = 32
P_DQ = 40
P_GATE = 48


def _retention_kernel(q_ref, k_ref, v_ref, g_ref, cos_ref, sin_ref, dm_ref, qd_ref, kd_ref,
                      bd_ref, s0_ref, y_ref, sout_ref, s_scr, *, n_chunks, dk, dv):
    c = pl.program_id(1)

    @pl.when(c == 0)
    def _():
        s_scr[...] = s0_ref[0]

    cos = cos_ref[...]
    sin = sin_ref[...]
    k_scale = dk ** -0.5
    for h in range(R_HEADS):
        q = q_ref[:, h * dk:(h + 1) * dk].astype(F32)
        k = k_ref[:, h * dk:(h + 1) * dk].astype(F32)
        v = v_ref[:, h * dv:(h + 1) * dv]
        g = g_ref[:, h * dv:(h + 1) * dv].astype(F32)
        q = q * cos + pltpu.roll(q, dk // 2, 1) * sin
        k = (k * cos + pltpu.roll(k, dk // 2, 1) * sin) * k_scale
        qb = q.astype(BF16)
        kb = k.astype(BF16)
        s = s_scr[h]
        scores = _dot_nt(qb, kb) * dm_ref[h]
        o = _dot(scores.astype(BF16), v) + _dot((q * qd_ref[h]).astype(BF16), s.astype(BF16))
        s_scr[h] = bd_ref[h] * s + _dot_tn((k * kd_ref[h]).astype(BF16), v)
        o = o * lax.rsqrt(jnp.mean(o * o, axis=-1, keepdims=True) + RMS_EPS)
        y_ref[:, h * dv:(h + 1) * dv] = (g * jax.nn.sigmoid(g) * o).astype(BF16)

    @pl.when(c == n_chunks - 1)
    def _():
        sout_ref[0] = s_scr[...]


def _retention(P, row_off, B, L, pos0, S0):
    dk, dv = 128, 256
    C = min(L, RET_CHUNK)
    nc = L // C
    ob = row_off // C
    pos = (pos0 + jnp.arange(L, dtype=jnp.int32)).astype(F32)
    inv = 1.0 / (ROPE_BASE ** jnp.linspace(0.0, 1.0, dk // 2, dtype=F32))
    ang = pos[:, None] * inv[None, :]
    cos = jnp.concatenate([jnp.cos(ang), jnp.cos(ang)], axis=-1)
    sin = jnp.concatenate([-jnp.sin(ang), jnp.sin(ang)], axis=-1)
    log_g = jnp.log1p(-jnp.power(2.0, -5.0 - jnp.arange(R_HEADS, dtype=F32)))
    i = jnp.arange(C, dtype=F32)
    rel = i[:, None] - i[None, :]
    dmask = jnp.where(rel >= 0, jnp.exp(log_g[:, None, None] * jnp.maximum(rel, 0.0)), 0.0)
    q_decay = jnp.exp(log_g[:, None] * (i + 1.0))[..., None]
    k_decay = jnp.exp(log_g[:, None] * (C - 1.0 - i))[..., None]
    b_decay = jnp.exp(log_g * C)[:, None, None]

    def rows(b, c):
        return ob + b * nc + c

    kern = functools.partial(_retention_kernel, n_chunks=nc, dk=dk, dv=dv)
    return pl.pallas_call(
        kern,
        grid=(B, nc),
        in_specs=[
            pl.BlockSpec((C, R_HEADS * dk), lambda b, c: (rows(b, c), P_RQ * LANES // (R_HEADS * dk))),
            pl.BlockSpec((C, R_HEADS * dk), lambda b, c: (rows(b, c), P_RK * LANES // (R_HEADS * dk))),
            pl.BlockSpec((C, R_HEADS * dv), lambda b, c: (rows(b, c), P_RV * LANES // (R_HEADS * dv))),
            pl.BlockSpec((C, R_HEADS * dv), lambda b, c: (rows(b, c), P_RG * LANES // (R_HEADS * dv))),
            pl.BlockSpec((C, dk), lambda b, c: (c, 0)),
            pl.BlockSpec((C, dk), lambda b, c: (c, 0)),
            pl.BlockSpec((R_HEADS, C, C), lambda b, c: (0, 0, 0)),
            pl.BlockSpec((R_HEADS, C, 1), lambda b, c: (0, 0, 0)),
            pl.BlockSpec((R_HEADS, C, 1), lambda b, c: (0, 0, 0)),
            pl.BlockSpec((R_HEADS, 1, 1), lambda b, c: (0, 0, 0)),
            pl.BlockSpec((1, R_HEADS, dk, dv), lambda b, c: (b, 0, 0, 0)),
        ],
        out_specs=[
            pl.BlockSpec((C, R_HEADS * dv), lambda b, c: (b * nc + c, 0)),
            pl.BlockSpec((1, R_HEADS, dk, dv), lambda b, c: (b, 0, 0, 0)),
        ],
        out_shape=[
            jax.ShapeDtypeStruct((B * L, R_HEADS * dv), BF16),
            jax.ShapeDtypeStruct((B, R_HEADS, dk, dv), F32),
        ],
        scratch_shapes=[pltpu.VMEM((R_HEADS, dk, dv), F32)],
        compiler_params=_cparams(("parallel", "arbitrary")),
        name="retention",
    )(P, P, P, P, cos, sin, dmask, q_decay, k_decay, b_decay, S0)


def _diff_lambda(lam_ref, lambda_init):
    lv = lam_ref[...]
    a = jnp.sum(lv[0:1] * lv[1:2], axis=-1, keepdims=True)
    b = jnp.sum(lv[2:3] * lv[3:4], axis=-1, keepdims=True)
    return jnp.exp(a) - jnp.exp(b) + lambda_init


def _lane_tile(x, n):
    return jnp.concatenate([x] * n, axis=1)


def _stack_maps(q, hd):
    lane = lax.broadcasted_iota(jnp.int32, q.shape, 1)
    zero = jnp.zeros_like(q)
    return jnp.concatenate([jnp.where(lane < hd, q, zero), jnp.where(lane < hd, zero, q)], axis=0)


def _diff_finish(acc, l, lam, subln, lambda_init, tq):
    o = acc[:tq] / l[:tq] - lam * (acc[tq:] / l[tq:])
    o = o * lax.rsqrt(jnp.mean(o * o, axis=-1, keepdims=True) + RMS_EPS)
    return (o * subln * (1.0 - lambda_init)).astype(BF16)


def _attn_prompt_kernel(q_ref, k_ref, v_ref, lam_ref, sub_ref, y_ref, qs_scr, ve_scr, m_scr, acc_scr,
                        s_scr, *, blk, hd, lambda_init):
    qi = pl.program_id(2)
    dv = 2 * hd
    seq = v_ref.shape[0]
    sub = 2 * blk
    lo, hi, every = slice(0, sub), slice(sub, 2 * sub), slice(0, 2 * sub)

    @pl.when(qi == 0)
    def _():
        ve_scr[:, :dv] = v_ref[...]
        ve_scr[:, dv:] = jnp.ones((seq, dv), BF16)

    for s_idx, rows in enumerate((lo, hi)):
        q = q_ref[s_idx * blk:(s_idx + 1) * blk, :].astype(F32) * (hd ** -0.5 * LOG2E)
        qs_scr[rows] = _stack_maps(q.astype(BF16), hd)
    m_scr[...] = jnp.full_like(m_scr, -jnp.inf)
    acc_scr[...] = jnp.zeros_like(acc_scr)

    def scores(ki, slot, rows=every):
        off = pl.multiple_of(ki * blk, blk)
        s_scr[slot, rows] = _dot_nt(qs_scr[rows], k_ref[pl.ds(off, blk), :])

    def step(ki, slot, rows=every, diag=False):
        off = pl.multiple_of(ki * blk, blk)
        s = s_scr[slot, rows]
        if diag:
            row = lax.broadcasted_iota(jnp.int32, s.shape, 0)
            col = lax.broadcasted_iota(jnp.int32, s.shape, 1)
            shift = CHUNK.bit_length() - 1
            seen = (col >> shift) <= ((row & (blk - 1)) >> shift)
            if s.shape[0] > sub:
                seen = jnp.logical_or(row >= sub, seen)
            s = jnp.where(seen, s, -1e30)
        m_prev = m_scr[rows]
        m_new = jnp.maximum(m_prev, jnp.max(s, axis=-1, keepdims=True))
        alpha = jnp.exp2(m_prev - m_new)
        p = jnp.exp2(s - _lane_tile(m_new, blk // LANES))
        pv = _dot(p.astype(BF16), ve_scr[pl.ds(off, blk), :])
        acc_scr[rows] = _lane_tile(alpha, 2 * dv // LANES) * acc_scr[rows] + pv
        m_scr[rows] = m_new

    def run(base, width):
        for u in range(width):
            scores(base + u + 1, (u + 1) % 2)
            step(base + u, u % 2)

    def body(j, carry):
        run(ATT_UNROLL * j, ATT_UNROLL)
        return carry

    n_open = 2 * qi
    scores(0, 0)
    n_main = n_open // ATT_UNROLL
    lax.fori_loop(0, n_main, body, 0)
    done = n_main * ATT_UNROLL
    width = ATT_UNROLL // 2
    while width >= 2:
        @pl.when(((n_open - done) // width) % 2 == 1)
        def _(width=width):
            run(n_open - (n_open - done) % (2 * width), width)
        width //= 2
    scores(n_open + 1, 1, hi)
    step(n_open, 0, every, diag=True)
    step(n_open + 1, 1, hi, diag=True)

    lam = _diff_lambda(lam_ref, lambda_init)
    for s_idx in range(2):
        a1 = acc_scr[s_idx * sub:s_idx * sub + blk, :]
        a2 = acc_scr[s_idx * sub + blk:(s_idx + 1) * sub, :]
        o = a1[:, :dv] / a1[:, dv:] - lam * (a2[:, :dv] / a2[:, dv:])
        o = o * lax.rsqrt(jnp.mean(o * o, axis=-1, keepdims=True) + RMS_EPS)
        y_ref[s_idx * blk:(s_idx + 1) * blk, :] = (o * sub_ref[...] * (1.0 - lambda_init)).astype(BF16)


def _attn_prompt(P, B, S, lam_p, subln, lambda_init):
    hd = 64
    blk = ATT_BLK
    bq = 2 * blk
    assert S % bq == 0
    nq = S // bq
    w = 2 * hd
    rows = 2 * bq
    kern = functools.partial(_attn_prompt_kernel, blk=blk, hd=hd, lambda_init=lambda_init)
    return pl.pallas_call(
        kern,
        grid=(B, D_HEADS, nq),
        in_specs=[
            pl.BlockSpec((bq, w), lambda b, h, i: (b * nq + i, P_DQ + h)),
            pl.BlockSpec((S, w), lambda b, h, i: (b, P_DK + h)),
            pl.BlockSpec((S, w), lambda b, h, i: (b, P_DV + h)),
            pl.BlockSpec((4, hd), lambda b, h, i: (0, 0)),
            pl.BlockSpec((1, w), lambda b, h, i: (0, 0)),
        ],
        out_specs=pl.BlockSpec((bq, w), lambda b, h, i: (b * nq + i, h)),
        out_shape=jax.ShapeDtypeStruct((B * S, D_HEADS * w), BF16),
        scratch_shapes=[
            pltpu.VMEM((rows, w), BF16),
            pltpu.VMEM((S, 2 * w), BF16),
            pltpu.VMEM((rows, LANES), F32),
            pltpu.VMEM((rows, 2 * w), F32),
            pltpu.VMEM((2, rows, blk), F32),
        ],
        compiler_params=_cparams(("parallel", "parallel", "arbitrary")),
        name="attn_prompt",
    )(P, P, P, lam_p, subln)


def _attn_sample_kernel(q_ref, kn_ref, vn_ref, kc_ref, vc_ref, lam_ref, sub_ref, y_ref,
                        *, tq, hd, lambda_init):
    lam = _diff_lambda(lam_ref, lambda_init)
    w = 2 * hd
    past = kc_ref.shape[1] // D_HEADS
    for h in range(D_HEADS):
        cols = slice(h * w, (h + 1) * w)
        qs = _stack_maps(q_ref[:, cols] * (hd ** -0.5), hd)
        kc = kc_ref[0, pl.ds(h, past, stride=D_HEADS), :].astype(BF16)
        vc = vc_ref[0, pl.ds(h, past, stride=D_HEADS), :].astype(BF16)
        kn = kn_ref[:, cols]
        vn = vn_ref[:, cols]
        s_c = _dot_nt(qs, kc)
        s_n = _dot_nt(qs, kn)
        m = jnp.maximum(jnp.max(s_c, axis=-1, keepdims=True), jnp.max(s_n, axis=-1, keepdims=True))
        p_c = jnp.exp(s_c - m)
        p_n = jnp.exp(s_n - m)
        l = jnp.sum(p_c, axis=-1, keepdims=True) + jnp.sum(p_n, axis=-1, keepdims=True)
        acc = _dot(p_c.astype(BF16), vc) + _dot(p_n.astype(BF16), vn)
        y_ref[:, cols] = _diff_finish(acc, l, lam, sub_ref[...], lambda_init, tq)


def _attn_sample(P, row_off, B, L, cache_k, cache_v, lam_p, subln, lambda_init):
    hd = 64
    past = cache_k.shape[1]
    ob = row_off // L
    width = D_HEADS * 2 * hd
    kern = functools.partial(_attn_sample_kernel, tq=L, hd=hd, lambda_init=lambda_init)
    cache_k = cache_k.reshape(B, past * D_HEADS, 2 * hd)
    cache_v = cache_v.reshape(B, past * D_HEADS, 2 * hd)
    cache_spec = pl.BlockSpec((1, past * D_HEADS, 2 * hd), lambda b: (b, 0, 0))
    return pl.pallas_call(
        kern,
        grid=(B,),
        in_specs=[
            pl.BlockSpec((L, width), lambda b: (ob + b, P_DQ * LANES // width)),
            pl.BlockSpec((L, width), lambda b: (ob + b, P_DK * LANES // width)),
            pl.BlockSpec((L, width), lambda b: (ob + b, P_DV * LANES // width)),
            cache_spec,
            cache_spec,
            pl.BlockSpec((4, hd), lambda b: (0, 0)),
            pl.BlockSpec((1, 2 * hd), lambda b: (0, 0)),
        ],
        out_specs=pl.BlockSpec((L, width), lambda b: (b, 0)),
        out_shape=jax.ShapeDtypeStruct((B * L, width), BF16),
        compiler_params=_cparams(("parallel",)),
        name="attn_sample",
    )(P, P, P, cache_k, cache_v, lam_p, subln)


def _mix_kernel(xp_ref, xs_ref, yrp_ref, yrs_ref, ydp_ref, yds_ref, gr_ref, gd_ref, lig_ref, lib_ref,
                bg_ref, wpr_ref, wpd_ref, wo_ref, g1_ref, b1_ref, h1_ref, *, alpha, n_p):
    i = pl.program_id(0)
    d = xp_ref.shape[1]

    def compute(x_ref, yr_ref, yd_ref):
        h0 = _layer_norm(x_ref[...], lig_ref[...], lib_ref[...])
        g_ret = jax.nn.sigmoid(gr_ref[...].astype(F32) + bg_ref[:, :d])
        g_diff = jax.nn.sigmoid(gd_ref[...].astype(F32) + bg_ref[:, d:])
        merged = g_ret * _dot(yr_ref[...], wpr_ref[...]) + g_diff * _dot(yd_ref[...], wpd_ref[...])
        mixed = _dot(merged.astype(BF16), wo_ref[...])
        h1_ref[...] = _layer_norm(alpha * h0 + mixed, g1_ref[...], b1_ref[...])

    @pl.when(i < n_p)
    def _():
        compute(xp_ref, yrp_ref, ydp_ref)

    @pl.when(i >= n_p)
    def _():
        compute(xs_ref, yrs_ref, yds_ref)


def _mix(xp, xs, yr_p, yr_s, yd_p, yd_s, P, ln_in_g, ln_in_b, b_gate, wpr, wpd, wo, g1, b1, alpha):
    Tp, D = xp.shape
    Ts = xs.shape[0]
    tm = MIX_TM
    n_p, n_s = Tp // tm, Ts // tm
    gcol = P_GATE * LANES // D
    const = lambda i: (0, 0)
    lo = lambda i: (_lo(i, n_p), 0)
    hi = lambda i: (_hi(i, n_p), 0)
    single = pl.Buffered(1)
    kern = functools.partial(_mix_kernel, alpha=alpha, n_p=n_p)
    return pl.pallas_call(
        kern,
        grid=(n_p + n_s,),
        in_specs=[
            pl.BlockSpec((tm, D), lo),
            pl.BlockSpec((tm, D), hi),
            pl.BlockSpec((tm, yr_p.shape[1]), lo),
            pl.BlockSpec((tm, yr_s.shape[1]), hi),
            pl.BlockSpec((tm, yd_p.shape[1]), lo),
            pl.BlockSpec((tm, yd_s.shape[1]), hi),
            pl.BlockSpec((tm, D), lambda i: (i, gcol)),
            pl.BlockSpec((tm, D), lambda i: (i, gcol + 1)),
            pl.BlockSpec((1, D), const),
            pl.BlockSpec((1, D), const),
            pl.BlockSpec((1, 2 * D), const),
            pl.BlockSpec(wpr.shape, const, pipeline_mode=single),
            pl.BlockSpec(wpd.shape, const, pipeline_mode=single),
            pl.BlockSpec(wo.shape, const, pipeline_mode=single),
            pl.BlockSpec((1, D), const),
            pl.BlockSpec((1, D), const),
        ],
        out_specs=pl.BlockSpec((tm, D), lambda i: (i, 0)),
        out_shape=jax.ShapeDtypeStruct((Tp + Ts, D), F32),
        compiler_params=_cparams(("arbitrary",)),
        name="mix",
    )(xp, xs, yr_p, yr_s, yd_p, yd_s, P, P, ln_in_g, ln_in_b, b_gate, wpr, wpd, wo, g1, b1)


def _mem_kernel(h1_ref, mkp_ref, mvp_ref, mks_ref, mvs_ref, wq_ref, wo_ref, g2_ref, b2_ref, rw_ref,
                rb_ref, h2_ref, route_ref, counts_ref, q_scr, o_scr, cnt_scr, *, n_p, alpha, hd):
    i = pl.program_id(0)
    tm = h1_ref.shape[0]
    h1 = h1_ref[...]
    q_scr[...] = (_dot(h1.astype(BF16), wq_ref[...]) * (hd ** -0.5)).astype(BF16)

    def attend(mk_ref, mv_ref):
        n_sub = mk_ref.shape[0]
        seg = tm // n_sub
        split_heads = mk_ref.shape[2] == hd
        for s in range(n_sub):
            for h in range(M_HEADS):
                if split_heads:
                    rows = pl.ds(h, mk_ref.shape[1] // M_HEADS, stride=M_HEADS)
                    mk = mk_ref[s, rows, :].astype(BF16)
                    mv = mv_ref[s, rows, :].astype(BF16)
                else:
                    mk = mk_ref[s, :, h * hd:(h + 1) * hd].astype(BF16)
                    mv = mv_ref[s, :, h * hd:(h + 1) * hd].astype(BF16)
                qh = q_scr[s * seg:(s + 1) * seg, h * hd:(h + 1) * hd]
                sc = _dot_nt(qh, mk)
                sc = sc - jnp.max(sc, axis=-1, keepdims=True)
                p = jnp.exp(sc)
                p = p / jnp.sum(p, axis=-1, keepdims=True)
                o_scr[s * seg:(s + 1) * seg, h * hd:(h + 1) * hd] = _dot(
                    p.astype(BF16), mv).astype(BF16)

    @pl.when(i < n_p)
    def _():
        attend(mkp_ref, mvp_ref)

    @pl.when(i >= n_p)
    def _():
        attend(mks_ref, mvs_ref)

    h2 = _layer_norm(alpha * h1 + _dot(o_scr[...], wo_ref[...]), g2_ref[...], b2_ref[...])
    h2_ref[...] = h2
    h2b = h2.astype(BF16)

    h_lo = (h2 - h2b.astype(F32)).astype(BF16)
    rw = rw_ref[...]
    rw_hi = rw.astype(BF16)
    rw_lo = (rw - rw_hi.astype(F32)).astype(BF16)
    logits = _dot(h2b, rw_hi) + _dot(h2b, rw_lo) + _dot(h_lo, rw_hi) + rb_ref[...]

    n_e = logits.shape[1]
    eidx = lax.broadcasted_iota(jnp.int32, logits.shape, 1).astype(F32)
    lane = lax.broadcasted_iota(jnp.int32, (tm, LANES), 1)
    route = jnp.zeros((tm, LANES), F32)
    work = logits
    vals = []
    sels = []
    for k in range(TOP_K):
        mx = jnp.max(work, axis=-1, keepdims=True)
        sel = jnp.min(jnp.where(work == mx, eidx, float(n_e)), axis=-1, keepdims=True)
        work = jnp.where(eidx == sel, -jnp.inf, work)
        vals.append(mx)
        sels.append(sel)
        route = jnp.where(lane == TOP_K + k, sel, route)
    ex = [jnp.exp(v - vals[0]) for v in vals]
    den = ex[0] + ex[1] + ex[2] + ex[3]
    for k in range(TOP_K):
        route = jnp.where(lane == k, ex[k] / den, route)

    @pl.when(i == 0)
    def _():
        cnt_scr[...] = jnp.zeros_like(cnt_scr)

    lane_f = lane.astype(F32)
    hit = [lane_f == sels[k] for k in range(TOP_K)]
    cnt = sum(h.astype(F32) for h in hit)
    r_io = lax.broadcasted_iota(jnp.int32, (tm, tm), 0)
    c_io = lax.broadcasted_iota(jnp.int32, (tm, tm), 1)
    ltri = jnp.where(r_io > c_io, 1.0, 0.0).astype(BF16)
    excl = _dot(ltri, cnt.astype(BF16)) + cnt_scr[...]
    for k in range(TOP_K):
        rank = jnp.sum(jnp.where(hit[k], excl, 0.0), axis=-1, keepdims=True)
        route = jnp.where(lane == 2 * TOP_K + k, rank, route)
    route_ref[...] = route
    cnt_scr[...] = cnt_scr[...] + jnp.sum(cnt, axis=0, keepdims=True)
    counts_ref[...] = cnt_scr[...]


def _mem_attn(h1, t_prompt, mem_kp, mem_vp, l_prompt, mem_ks, mem_vs, l_sample, wq, wo, g2, b2, rw, rb,
              alpha):
    T, D = h1.shape
    hd = 128
    tm = MEM_TM
    assert l_prompt % tm == 0 and tm % l_sample == 0
    n_p = t_prompt // tm
    n_s = (T - t_prompt) // tm
    per_b = l_prompt // tm
    sub_s = tm // l_sample
    n_mem = mem_kp.shape[1]
    bp = mem_kp.shape[0]
    const = lambda i: (0, 0)
    single = pl.Buffered(1)
    mem_p = pl.BlockSpec((1, n_mem, M_HEADS * hd), lambda i: (jnp.minimum(i // per_b, bp - 1), 0, 0))
    bs = mem_ks.shape[0]
    mem_ks = mem_ks.reshape(bs, n_mem * M_HEADS, hd)
    mem_vs = mem_vs.reshape(bs, n_mem * M_HEADS, hd)
    mem_s = pl.BlockSpec((sub_s, n_mem * M_HEADS, hd), lambda i: (_hi(i, n_p), 0, 0))
    kern = functools.partial(_mem_kernel, n_p=n_p, alpha=alpha, hd=hd)
    return pl.pallas_call(
        kern,
        grid=(n_p + n_s,),
        in_specs=[
            pl.BlockSpec((tm, D), lambda i: (i, 0)),
            mem_p, mem_p, mem_s, mem_s,
            pl.BlockSpec(wq.shape, const, pipeline_mode=single),
            pl.BlockSpec(wo.shape, const, pipeline_mode=single),
            pl.BlockSpec((1, D), const),
            pl.BlockSpec((1, D), const),
            pl.BlockSpec(rw.shape, const, pipeline_mode=single),
            pl.BlockSpec((1, rw.shape[1]), const),
        ],
        out_specs=[
            pl.BlockSpec((tm, D), lambda i: (i, 0)),
            pl.BlockSpec((tm, LANES), lambda i: (i, 0)),
            pl.BlockSpec((1, LANES), const),
        ],
        out_shape=[
            jax.ShapeDtypeStruct((T, D), F32),
            jax.ShapeDtypeStruct((T, LANES), F32),
            jax.ShapeDtypeStruct((1, LANES), F32),
        ],
        scratch_shapes=[
            pltpu.VMEM((tm, M_HEADS * hd), BF16),
            pltpu.VMEM((tm, M_HEADS * hd), BF16),
            pltpu.VMEM((1, LANES), F32),
        ],
        compiler_params=_cparams(("arbitrary",)),
        name="mem_attn",
    )(h1, mem_kp, mem_vp, mem_ks, mem_vs, wq, wo, g2, b2, rw, rb)


def _mem_kv_kernel(x_ref, w_ref, o_ref):
    o_ref[...] = _dot(x_ref[...].astype(BF16), w_ref[...])


def _mem_kv(mem, w):
    R, D = mem.shape
    N = w.shape[1]
    tm = 256
    return pl.pallas_call(
        _mem_kv_kernel,
        grid=(R // tm,),
        in_specs=[pl.BlockSpec((tm, D), lambda i: (i, 0)), pl.BlockSpec((D, N), lambda i: (0, 0))],
        out_specs=pl.BlockSpec((tm, N), lambda i: (i, 0)),
        out_shape=jax.ShapeDtypeStruct((R, N), F32),
        compiler_params=_cparams(("parallel",)),
        name="mem_kv",
    )(mem, w)


def _row_copy(src_hbm, dst, sem, src_row, dst_row):
    return pltpu.make_async_copy(src_hbm.at[pl.ds(src_row, 1)], dst.at[pl.ds(dst_row, 1)], sem)


def _dispatch_kernel(dest_ref, pend_ref, padded_ref, nu_ref, h_ref, xs_hbm, stage, zeros, sem, zsem,
                     *, tm, bm, n_steps, n_blocks):
    i = pl.program_id(0)
    slot = i % 2

    def zero_block(off):
        return pltpu.make_async_copy(zeros, xs_hbm.at[pl.ds(pl.multiple_of(off, bm), bm)], zsem)

    def zero_fill(act):
        for e in range(N_EXPERTS):
            @pl.when(padded_ref[e] > 0)
            def _(e=e):
                act(zero_block(pend_ref[e] - bm))
        for b in range(n_blocks - N_EXPERTS, n_blocks):
            @pl.when(b >= nu_ref[0])
            def _(b=b):
                act(zero_block(b * bm))

    @pl.when(i == 0)
    def _():
        zeros[...] = jnp.zeros_like(zeros)
        zero_fill(lambda c: c.start())
        zero_fill(lambda c: c.wait())

    def wait(s):
        for _ in range(TOP_K):
            pltpu.make_async_copy(stage.at[s], xs_hbm.at[pl.ds(0, tm)], sem.at[s]).wait()

    @pl.when(i >= 2)
    def _():
        wait(slot)

    stage[slot] = h_ref[...]

    def body(j, carry):
        t0 = pl.multiple_of(j * SUBLANES, SUBLANES)
        for u in range(SUBLANES):
            for k in range(TOP_K):
                row = dest_ref[(i * tm + t0 + u) * TOP_K + k]
                pltpu.make_async_copy(stage.at[slot, pl.ds(t0 + u, 1)], xs_hbm.at[pl.ds(row, 1)],
                                      sem.at[slot]).start(priority=k % 2)
        return carry
    lax.fori_loop(0, tm // SUBLANES, body, 0)

    @pl.when(i == n_steps - 1)
    def _():
        wait(slot)
        if n_steps > 1:
            wait(1 - slot)


def _moe_dispatch(h2, dest, pad_end, padded, n_used, n_rows, bm):
    T, D = h2.shape
    tm = GATHER_ROWS
    n_steps = T // tm
    kern = functools.partial(_dispatch_kernel, tm=tm, bm=bm, n_steps=n_steps, n_blocks=n_rows // bm)
    return pl.pallas_call(
        kern,
        grid_spec=pltpu.PrefetchScalarGridSpec(
            num_scalar_prefetch=4,
            grid=(n_steps,),
            in_specs=[pl.BlockSpec((tm, D), lambda i, d, pe, pd, nu: (i, 0))],
            out_specs=pl.BlockSpec(memory_space=pl.ANY),
            scratch_shapes=[
                pltpu.VMEM((2, tm, D), F32),
                pltpu.VMEM((bm, D), F32),
                pltpu.SemaphoreType.DMA((2,)),
                pltpu.SemaphoreType.DMA(()),
            ],
        ),
        out_shape=jax.ShapeDtypeStruct((n_rows, D), F32),
        compiler_params=_cparams(("arbitrary",)),
        name="moe_dispatch",
    )(dest, pad_end, padded, n_used, h2)


def _cast_rows(src, dst):
    rows = CAST_ROWS

    def body(i, carry):
        r = pl.multiple_of(i * rows, rows)
        dst[pl.ds(r, rows), :] = src[pl.ds(r, rows), :].astype(BF16)
        return carry
    lax.fori_loop(0, src.shape[0] // rows, body, 0)


def _expert_weights(be_ref, first_ref, next_ref, copies, cast, m):
    @pl.when(first_ref[m] == 1)
    def _():
        @pl.when(m == 0)
        def _():
            for c in copies(be_ref[0]):
                c.start()

        for c in copies(be_ref[m]):
            c.wait()
        cast()
        ne = next_ref[m]

        @pl.when(ne >= 0)
        def _():
            for c in copies(ne):
                c.start(priority=1)


def _moe_up_kernel(be_ref, nu_ref, first_ref, next_ref, x_ref, w1_hbm, b_ref, a_ref,
                   stage, w_scr, sem):
    m = pl.program_id(0)
    ff = w_scr.shape[2]

    def copies(e):
        return [pltpu.make_async_copy(w1_hbm.at[e, :, pl.ds(t * ff, ff)], stage.at[t], sem)
                for t in range(2)]

    def cast():
        for t in range(2):
            _cast_rows(stage.at[t], w_scr.at[t])

    _expert_weights(be_ref, first_ref, next_ref, copies, cast, m)

    @pl.when(m < nu_ref[0])
    def _():
        x = x_ref[...].astype(BF16)
        half = ff // 2
        for c in range(2):
            cols = slice(c * half, (c + 1) * half)
            lin_cols = slice(ff + c * half, ff + (c + 1) * half)
            u_glu = jnp.minimum(_dot(x, w_scr[0, :, cols]) + b_ref[0, :, cols], SWIGLU_LIMIT)
            u_lin = jnp.clip(_dot(x, w_scr[1, :, cols]) + b_ref[0, :, lin_cols],
                             -SWIGLU_LIMIT, SWIGLU_LIMIT)
            a = u_glu * jax.nn.sigmoid(SWIGLU_ALPHA * u_glu) * (u_lin + 1.0)
            a_ref[:, cols] = a.astype(BF16)

    @pl.when(m >= nu_ref[0])
    def _():
        a_ref[...] = jnp.zeros_like(a_ref)


def _moe_up(xs, w1, b1, plan):
    n_rows, D = xs.shape
    F = w1.shape[2] // 2
    bm = MOE_BM
    return pl.pallas_call(
        _moe_up_kernel,
        grid_spec=pltpu.PrefetchScalarGridSpec(
            num_scalar_prefetch=4,
            grid=(n_rows // bm,),
            in_specs=[
                pl.BlockSpec((bm, D), lambda m, be, nu, fi, nx: (jnp.minimum(m, nu[0] - 1), 0)),
                pl.BlockSpec(memory_space=pl.ANY),
                pl.BlockSpec((1, 1, 2 * F), lambda m, be, nu, fi, nx: (be[m], 0, 0)),
            ],
            out_specs=pl.BlockSpec((bm, F), lambda m, be, nu, fi, nx: (m, 0)),
            scratch_shapes=[
                pltpu.VMEM((2, D, F), F32),
                pltpu.VMEM((2, D, F), BF16),
                pltpu.SemaphoreType.DMA(()),
            ],
        ),
        out_shape=jax.ShapeDtypeStruct((n_rows, F), BF16),
        compiler_params=_cparams(("arbitrary",)),
        name="moe_up",
    )(*plan, xs, w1, b1)


def _moe_down_kernel(be_ref, nu_ref, first_ref, next_ref, a_ref, w2_hbm, b_ref, y_ref,
                     stage, w_scr, sem):
    m = pl.program_id(0)

    def copies(e):
        return [pltpu.make_async_copy(w2_hbm.at[e], stage, sem)]

    def cast():
        _cast_rows(stage, w_scr)

    _expert_weights(be_ref, first_ref, next_ref, copies, cast, m)

    @pl.when(m < nu_ref[0])
    def _():
        y_ref[...] = _dot(a_ref[...], w_scr[...]) + b_ref[0]

    @pl.when(m >= nu_ref[0])
    def _():
        y_ref[...] = jnp.zeros_like(y_ref)


def _moe_down(a, w2, b2, plan):
    n_rows, F = a.shape
    D = w2.shape[2]
    bm = MOE_BM
    return pl.pallas_call(
        _moe_down_kernel,
        grid_spec=pltpu.PrefetchScalarGridSpec(
            num_scalar_prefetch=4,
            grid=(n_rows // bm,),
            in_specs=[
                pl.BlockSpec((bm, F), lambda m, be, nu, fi, nx: (jnp.minimum(m, nu[0] - 1), 0)),
                pl.BlockSpec(memory_space=pl.ANY),
                pl.BlockSpec((1, 1, D), lambda m, be, nu, fi, nx: (be[m], 0, 0)),
            ],
            out_specs=pl.BlockSpec((bm, D), lambda m, be, nu, fi, nx: (m, 0)),
            scratch_shapes=[
                pltpu.VMEM((F, D), F32),
                pltpu.VMEM((F, D), BF16),
                pltpu.SemaphoreType.DMA(()),
            ],
        ),
        out_shape=jax.ShapeDtypeStruct((n_rows, D), F32),
        compiler_params=_cparams(("arbitrary",)),
        name="moe_down",
    )(*plan, a, w2, b2)


def _combine_kernel(dest_ref, h2_ref, rt_ref, g_ref, b_ref, ys_hbm, op_ref, os_ref, buf, sem,
                    *, tm, n_steps, n_prompt, alpha):
    i = pl.program_id(0)

    def issue(step, slot):
        def body(j, carry):
            t0 = pl.multiple_of(j * SUBLANES, SUBLANES)
            for u in range(SUBLANES):
                for k in range(TOP_K):
                    row = dest_ref[(step * tm + t0 + u) * TOP_K + k]
                    _row_copy(ys_hbm, buf.at[slot, k], sem.at[slot], row, t0 + u).start(priority=k % 2)
            return carry
        lax.fori_loop(0, tm // SUBLANES, body, 0)

    def wait(slot):
        for k in range(TOP_K):
            pltpu.make_async_copy(ys_hbm.at[pl.ds(0, tm)], buf.at[slot, k], sem.at[slot]).wait()

    slot = i % 2

    @pl.when(i == 0)
    def _():
        issue(0, 0)

    @pl.when(i + 1 < n_steps)
    def _():
        issue(i + 1, 1 - slot)

    wait(slot)
    y = rt_ref[:, 0:1] * buf[slot, 0]
    for k in range(1, TOP_K):
        y = y + rt_ref[:, k:k + 1] * buf[slot, k]
    out = _layer_norm(alpha * h2_ref[...] + y, g_ref[...], b_ref[...])

    @pl.when(i < n_prompt)
    def _():
        op_ref[...] = out

    @pl.when(i >= n_prompt)
    def _():
        os_ref[...] = out


def _moe_combine(ys, dest, h2, route, g3, b3, t_prompt, alpha):
    T, D = h2.shape
    tm = COMBINE_TM
    n_steps = T // tm
    n_prompt = t_prompt // tm
    kern = functools.partial(_combine_kernel, tm=tm, n_steps=n_steps, n_prompt=n_prompt, alpha=alpha)
    return pl.pallas_call(
        kern,
        grid_spec=pltpu.PrefetchScalarGridSpec(
            num_scalar_prefetch=1,
            grid=(n_steps,),
            in_specs=[
                pl.BlockSpec((tm, D), lambda i, d: (i, 0)),
                pl.BlockSpec((tm, LANES), lambda i, d: (i, 0)),
                pl.BlockSpec((1, D), lambda i, d: (0, 0)),
                pl.BlockSpec((1, D), lambda i, d: (0, 0)),
                pl.BlockSpec(memory_space=pl.ANY),
            ],
            out_specs=[
                pl.BlockSpec((tm, D), lambda i, d: (jnp.minimum(i, n_prompt - 1), 0)),
                pl.BlockSpec((tm, D), lambda i, d: (jnp.maximum(i - n_prompt, 0), 0)),
            ],
            scratch_shapes=[pltpu.VMEM((2, TOP_K, tm, D), F32), pltpu.SemaphoreType.DMA((2,))],
        ),
        out_shape=[
            jax.ShapeDtypeStruct((t_prompt, D), F32),
            jax.ShapeDtypeStruct((T - t_prompt, D), F32),
        ],
        compiler_params=_cparams(("arbitrary",)),
        name="moe_combine",
    )(dest, h2, route, g3, b3, ys)


def _moe_plan(route, counts, bm):
    T = route.shape[0]
    top_idx = route[:, TOP_K:2 * TOP_K].astype(jnp.int32)
    rank = route[:, 2 * TOP_K:3 * TOP_K].astype(jnp.int32)
    counts = counts[0, :N_EXPERTS].astype(jnp.int32)
    padded = (counts + bm - 1) // bm * bm
    pad_end = jnp.cumsum(padded)
    pad_start = pad_end - padded
    dest = (pad_start[top_idx] + rank).reshape(-1)
    n_blocks = -(-(T * TOP_K) // bm) + N_EXPERTS
    block_start = jnp.arange(n_blocks, dtype=jnp.int32) * bm
    block_e = jnp.sum((pad_end[None, :] <= block_start[:, None]).astype(jnp.int32), axis=1)
    block_e = jnp.minimum(block_e, N_EXPERTS - 1)
    n_used = (pad_end[-1] // bm).astype(jnp.int32).reshape(1)
    blk = jnp.arange(n_blocks, dtype=jnp.int32)
    prev_e = jnp.concatenate([jnp.full((1,), -1, jnp.int32), block_e[:-1]])
    first = jnp.logical_and(blk < n_used[0], block_e != prev_e)
    later = jnp.logical_and(first[None, :], blk[None, :] > blk[:, None])
    nxt_blk = jnp.min(jnp.where(later, blk[None, :], n_blocks), axis=1)
    next_e = jnp.where(nxt_blk < n_blocks, block_e[jnp.minimum(nxt_blk, n_blocks - 1)], -1)
    plan = (block_e, n_used, first.astype(jnp.int32), next_e.astype(jnp.int32))
    return dest, pad_end, padded, plan, n_blocks * bm


def kernel(x_prompt, x_sample, cache_diff_k, cache_diff_v, state_ret, cache_mem_k, cache_mem_v, mem_prompt, ln_in_g, ln_in_b, w_in, b_gate, diff_lambda, diff_subln, w_proj_ret, w_proj_diff, w_out, ln1_g, ln1_b, w_mq, w_mk, w_mv, w_mo, ln2_g, ln2_b, router_w, router_b, w1, b1, w2, b2, ln3_g, ln3_b):
    Bp, Lp, D = x_prompt.shape
    Bs, Ls, _ = x_sample.shape
    depth = w_in.shape[0]
    assert depth == 1
    past = cache_diff_k.shape[2]
    n_mem = mem_prompt.shape[1]
    Tp, Ts = Bp * Lp, Bs * Ls
    alpha = (2.0 * depth) ** 0.25
    lambda_init = 0.8 - 0.6 * math.exp(-0.3 * 0)
    row = lambda v: v.reshape(1, -1)

    xp = x_prompt.reshape(Tp, D)
    xs_in = x_sample.reshape(Ts, D)
    kv_blk = 4096 // IN_TN
    P, dk_p, dv_p, dk_s, dv_s = _in_proj(xp, xs_in, row(ln_in_g), row(ln_in_b), w_in[0].astype(BF16),
                                         kv_blk)

    zeros_state = jnp.zeros((Bp,) + state_ret.shape[2:], F32)
    yr_p, s_p = _retention(P, 0, Bp, Lp, 0, zeros_state)
    yr_s, s_s = _retention(P, Tp, Bs, Ls, past, state_ret[0])

    yd_p = _attn_prompt(P, Bp, Lp, diff_lambda[0], row(diff_subln[0]), lambda_init)
    yd_s = _attn_sample(P, Tp, Bs, Ls, cache_diff_k[0], cache_diff_v[0], diff_lambda[0],
                        row(diff_subln[0]), lambda_init)

    h1 = _mix(xp, xs_in, yr_p, yr_s, yd_p, yd_s, P, row(ln_in_g), row(ln_in_b), row(b_gate[0]),
              w_proj_ret[0].astype(BF16), w_proj_diff[0].astype(BF16), w_out[0].astype(BF16),
              row(ln1_g[0]), row(ln1_b[0]), alpha)

    w_mkv = jnp.concatenate([w_mk[0], w_mv[0]], axis=1).astype(BF16)
    mkv = _mem_kv(mem_prompt.reshape(Bp * n_mem, D), w_mkv)
    hm = w_mk.shape[2]
    mk_p = mkv[:, :hm].reshape(Bp, n_mem, hm)
    mv_p = mkv[:, hm:].reshape(Bp, n_mem, hm)
    h2, route, counts = _mem_attn(
        h1, Tp, mk_p, mv_p, Lp, cache_mem_k[0], cache_mem_v[0], Ls, w_mq[0].astype(BF16),
        w_mo[0].astype(BF16),
        row(ln2_g[0]), row(ln2_b[0]), router_w[0], row(router_b[0]), alpha)

    dest, pad_end, padded, plan, n_rows = _moe_plan(route, counts, MOE_BM)
    xs = _moe_dispatch(h2, dest, pad_end, padded, plan[1], n_rows, MOE_BM)
    act = _moe_up(xs, w1[0], b1[0][:, None, :], plan)
    ys = _moe_down(act, w2[0], b2[0][:, None, :], plan)
    out_p, out_s = _moe_combine(ys, dest, h2, route, row(ln3_g[0]), row(ln3_b[0]), Tp, alpha)

    return (
        out_p.reshape(Bp, Lp, D),
        out_s.reshape(Bs, Ls, D),
        dk_p.reshape(1, Bp, Lp, D_HEADS, 128),
        dv_p.reshape(1, Bp, Lp, D_HEADS, 128),
        s_p[None],
        mk_p.reshape(1, Bp, n_mem, M_HEADS, hm // M_HEADS),
        mv_p.reshape(1, Bp, n_mem, M_HEADS, hm // M_HEADS),
        dk_s.reshape(1, Bs, Ls, D_HEADS, 128),
        dv_s.reshape(1, Bs, Ls, D_HEADS, 128),
        s_s[None],
    )
```

```python
import functools
import math

import jax
import jax.numpy as jnp
from jax import lax
from jax.experimental import pallas as pl
from jax.experimental.pallas import tpu as pltpu

F32 = jnp.float32
BF16 = jnp.bfloat16

CHUNK = 64
R_HEADS = 4
D_HEADS = 8
M_HEADS = 4
N_EXPERTS = 32
TOP_K = 4
SWIGLU_ALPHA = 1.702
SWIGLU_LIMIT = 7.0
LN_EPS = 1e-5
RMS_EPS = 1e-5
ROPE_BASE = 10000.0
LOG2E = 1.4426950408889634

LANES = 128
SUBLANES = 8
VMEM_LIMIT = 58 * 1024 * 1024

IN_TM = 512
IN_TN = 2048
RET_CHUNK = 256
ATT_BLK = 512
ATT_UNROLL = 4
MIX_TM = 256
MEM_TM = 512
MOE_BM = 256
CAST_ROWS = 256
GATHER_ROWS = 512
COMBINE_TM = 256


def _cparams(sem):
    return pltpu.CompilerParams(dimension_semantics=sem, vmem_limit_bytes=VMEM_LIMIT)


def _layer_norm(x, g, b):
    mu = jnp.mean(x, axis=-1, keepdims=True)
    xc = x - mu
    var = jnp.mean(xc * xc, axis=-1, keepdims=True)
    return xc * lax.rsqrt(var + LN_EPS) * g + b


def _dot(a, b):
    return jnp.dot(a, b, preferred_element_type=F32)


def _dot_nt(a, b):
    return lax.dot_general(a, b, (((1,), (1,)), ((), ())), preferred_element_type=F32)


def _dot_tn(a, b):
    return lax.dot_general(a, b, (((0,), (0,)), ((), ())), preferred_element_type=F32)


def _lo(i, n):
    return jnp.minimum(i, n - 1)


def _hi(i, n):
    return jnp.maximum(i - n, 0)


def _in_proj_kernel(xp_ref, xs_ref, g_ref, b_ref, w_ref, p_ref, dkp_ref, dvp_ref, dks_ref, dvs_ref,
                    h_scr, *, n_p):
    i = pl.program_id(0)
    j = pl.program_id(1)
    half = dkp_ref.shape[1]

    @pl.when(jnp.logical_and(j == 0, i < n_p))
    def _():
        h_scr[...] = _layer_norm(xp_ref[...], g_ref[...], b_ref[...]).astype(BF16)

    @pl.when(jnp.logical_and(j == 0, i >= n_p))
    def _():
        h_scr[...] = _layer_norm(xs_ref[...], g_ref[...], b_ref[...]).astype(BF16)

    acc = _dot(h_scr[...], w_ref[...])
    p_ref[...] = acc.astype(BF16)

    @pl.when(jnp.logical_and(j == 0, i < n_p))
    def _():
        dkp_ref[...] = acc[:, :half]
        dvp_ref[...] = acc[:, half:]

    @pl.when(jnp.logical_and(j == 0, i >= n_p))
    def _():
        dks_ref[...] = acc[:, :half]
        dvs_ref[...] = acc[:, half:]


def _in_proj(xp, xs, g, b, w, kv_blk):
    Tp, D = xp.shape
    Ts = xs.shape[0]
    N = w.shape[1]
    n_p, n_s = Tp // IN_TM, Ts // IN_TM
    half = IN_TN // 2

    def col(j):
        return jnp.where(j == 0, kv_blk, jnp.where(j <= kv_blk, j - 1, j))

    single = pl.Buffered(1)
    kv_p = pl.BlockSpec((IN_TM, half), lambda i, j: (_lo(i, n_p), 0))
    kv_s = pl.BlockSpec((IN_TM, half), lambda i, j: (_hi(i, n_p), 0), pipeline_mode=single)
    return pl.pallas_call(
        functools.partial(_in_proj_kernel, n_p=n_p),
        grid=(n_p + n_s, N // IN_TN),
        in_specs=[
            pl.BlockSpec((IN_TM, D), lambda i, j: (_lo(i, n_p), 0)),
            pl.BlockSpec((IN_TM, D), lambda i, j: (_hi(i, n_p), 0), pipeline_mode=single),
            pl.BlockSpec((1, D), lambda i, j: (0, 0)),
            pl.BlockSpec((1, D), lambda i, j: (0, 0)),
            pl.BlockSpec((D, IN_TN), lambda i, j: (0, col(j))),
        ],
        out_specs=[pl.BlockSpec((IN_TM, IN_TN), lambda i, j: (i, j)), kv_p, kv_p, kv_s, kv_s],
        out_shape=[
            jax.ShapeDtypeStruct((Tp + Ts, N), BF16),
            jax.ShapeDtypeStruct((Tp, half), F32),
            jax.ShapeDtypeStruct((Tp, half), F32),
            jax.ShapeDtypeStruct((Ts, half), F32),
            jax.ShapeDtypeStruct((Ts, half), F32),
        ],
        scratch_shapes=[pltpu.VMEM((IN_TM, D), BF16)],
        compiler_params=_cparams(("arbitrary", "arbitrary")),
        name="in_proj",
    )(xp, xs, g, b, w)


P_DK = 0
P_DV = 8
P_RQ = 16
P_RK = 20
P_RV = 24
P_RG = 32
P_DQ = 40
P_GATE = 48


def _retention_kernel(q_ref, k_ref, v_ref, g_ref, cos_ref, sin_ref, dm_ref, qd_ref, kd_ref,
                      bd_ref, s0_ref, y_ref, sout_ref, s_scr, *, n_chunks, dk, dv):
    c = pl.program_id(1)

    @pl.when(c == 0)
    def _():
        s_scr[...] = s0_ref[0]

    cos = cos_ref[...]
    sin = sin_ref[...]
    k_scale = dk ** -0.5
    for h in range(R_HEADS):
        q = q_ref[:, h * dk:(h + 1) * dk].astype(F32)
        k = k_ref[:, h * dk:(h + 1) * dk].astype(F32)
        v = v_ref[:, h * dv:(h + 1) * dv]
        g = g_ref[:, h * dv:(h + 1) * dv].astype(F32)
        q = q * cos + pltpu.roll(q, dk // 2, 1) * sin
        k = (k * cos + pltpu.roll(k, dk // 2, 1) * sin) * k_scale
        qb = q.astype(BF16)
        kb = k.astype(BF16)
        s = s_scr[h]
        scores = _dot_nt(qb, kb) * dm_ref[h]
        o = _dot(scores.astype(BF16), v) + _dot((q * qd_ref[h]).astype(BF16), s.astype(BF16))
        s_scr[h] = bd_ref[h] * s + _dot_tn((k * kd_ref[h]).astype(BF16), v)
        o = o * lax.rsqrt(jnp.mean(o * o, axis=-1, keepdims=True) + RMS_EPS)
        y_ref[:, h * dv:(h + 1) * dv] = (g * jax.nn.sigmoid(g) * o).astype(BF16)

    @pl.when(c == n_chunks - 1)
    def _():
        sout_ref[0] = s_scr[...]


def _retention(P, row_off, B, L, pos0, S0):
    dk, dv = 128, 256
    C = min(L, RET_CHUNK)
    nc = L // C
    ob = row_off // C
    pos = (pos0 + jnp.arange(L, dtype=jnp.int32)).astype(F32)
    inv = 1.0 / (ROPE_BASE ** jnp.linspace(0.0, 1.0, dk // 2, dtype=F32))
    ang = pos[:, None] * inv[None, :]
    cos = jnp.concatenate([jnp.cos(ang), jnp.cos(ang)], axis=-1)
    sin = jnp.concatenate([-jnp.sin(ang), jnp.sin(ang)], axis=-1)
    log_g = jnp.log1p(-jnp.power(2.0, -5.0 - jnp.arange(R_HEADS, dtype=F32)))
    i = jnp.arange(C, dtype=F32)
    rel = i[:, None] - i[None, :]
    dmask = jnp.where(rel >= 0, jnp.exp(log_g[:, None, None] * jnp.maximum(rel, 0.0)), 0.0)
    q_decay = jnp.exp(log_g[:, None] * (i + 1.0))[..., None]
    k_decay = jnp.exp(log_g[:, None] * (C - 1.0 - i))[..., None]
    b_decay = jnp.exp(log_g * C)[:, None, None]

    def rows(b, c):
        return ob + b * nc + c

    kern = functools.partial(_retention_kernel, n_chunks=nc, dk=dk, dv=dv)
    return pl.pallas_call(
        kern,
        grid=(B, nc),
        in_specs=[
            pl.BlockSpec((C, R_HEADS * dk), lambda b, c: (rows(b, c), P_RQ * LANES // (R_HEADS * dk))),
            pl.BlockSpec((C, R_HEADS * dk), lambda b, c: (rows(b, c), P_RK * LANES // (R_HEADS * dk))),
            pl.BlockSpec((C, R_HEADS * dv), lambda b, c: (rows(b, c), P_RV * LANES // (R_HEADS * dv))),
            pl.BlockSpec((C, R_HEADS * dv), lambda b, c: (rows(b, c), P_RG * LANES // (R_HEADS * dv))),
            pl.BlockSpec((C, dk), lambda b, c: (c, 0)),
            pl.BlockSpec((C, dk), lambda b, c: (c, 0)),
            pl.BlockSpec((R_HEADS, C, C), lambda b, c: (0, 0, 0)),
            pl.BlockSpec((R_HEADS, C, 1), lambda b, c: (0, 0, 0)),
            pl.BlockSpec((R_HEADS, C, 1), lambda b, c: (0, 0, 0)),
            pl.BlockSpec((R_HEADS, 1, 1), lambda b, c: (0, 0, 0)),
            pl.BlockSpec((1, R_HEADS, dk, dv), lambda b, c: (b, 0, 0, 0)),
        ],
        out_specs=[
            pl.BlockSpec((C, R_HEADS * dv), lambda b, c: (b * nc + c, 0)),
            pl.BlockSpec((1, R_HEADS, dk, dv), lambda b, c: (b, 0, 0, 0)),
        ],
        out_shape=[
            jax.ShapeDtypeStruct((B * L, R_HEADS * dv), BF16),
            jax.ShapeDtypeStruct((B, R_HEADS, dk, dv), F32),
        ],
        scratch_shapes=[pltpu.VMEM((R_HEADS, dk, dv), F32)],
        compiler_params=_cparams(("parallel", "arbitrary")),
        name="retention",
    )(P, P, P, P, cos, sin, dmask, q_decay, k_decay, b_decay, S0)


def _diff_lambda(lam_ref, lambda_init):
    lv = lam_ref[...]
    a = jnp.sum(lv[0:1] * lv[1:2], axis=-1, keepdims=True)
    b = jnp.sum(lv[2:3] * lv[3:4], axis=-1, keepdims=True)
    return jnp.exp(a) - jnp.exp(b) + lambda_init


def _lane_tile(x, n):
    return jnp.concatenate([x] * n, axis=1)


def _stack_maps(q, hd):
    lane = lax.broadcasted_iota(jnp.int32, q.shape, 1)
    zero = jnp.zeros_like(q)
    return jnp.concatenate([jnp.where(lane < hd, q, zero), jnp.where(lane < hd, zero, q)], axis=0)


def _diff_finish(acc, l, lam, subln, lambda_init, tq):
    o = acc[:tq] / l[:tq] - lam * (acc[tq:] / l[tq:])
    o = o * lax.rsqrt(jnp.mean(o * o, axis=-1, keepdims=True) + RMS_EPS)
    return (o * subln * (1.0 - lambda_init)).astype(BF16)


def _attn_prompt_kernel(q_ref, k_ref, v_ref, lam_ref, sub_ref, y_ref, qs_scr, ve_scr, m_scr, acc_scr,
                        s_scr, *, blk, hd, lambda_init):
    qi = pl.program_id(2)
    dv = 2 * hd
    seq = v_ref.shape[0]
    sub = 2 * blk
    lo, hi, every = slice(0, sub), slice(sub, 2 * sub), slice(0, 2 * sub)

    @pl.when(qi == 0)
    def _():
        ve_scr[:, :dv] = v_ref[...]
        ve_scr[:, dv:] = jnp.ones((seq, dv), BF16)

    for s_idx, rows in enumerate((lo, hi)):
        q = q_ref[s_idx * blk:(s_idx + 1) * blk, :].astype(F32) * (hd ** -0.5 * LOG2E)
        qs_scr[rows] = _stack_maps(q.astype(BF16), hd)
    m_scr[...] = jnp.full_like(m_scr, -jnp.inf)
    acc_scr[...] = jnp.zeros_like(acc_scr)

    def scores(ki, slot, rows=every):
        off = pl.multiple_of(ki * blk, blk)
        s_scr[slot, rows] = _dot_nt(qs_scr[rows], k_ref[pl.ds(off, blk), :])

    def step(ki, slot, rows=every, diag=False):
        off = pl.multiple_of(ki * blk, blk)
        s = s_scr[slot, rows]
        if diag:
            row = lax.broadcasted_iota(jnp.int32, s.shape, 0)
            col = lax.broadcasted_iota(jnp.int32, s.shape, 1)
            shift = CHUNK.bit_length() - 1
            seen = (col >> shift) <= ((row & (blk - 1)) >> shift)
            if s.shape[0] > sub:
                seen = jnp.logical_or(row >= sub, seen)
            s = jnp.where(seen, s, -1e30)
        m_prev = m_scr[rows]
        m_new = jnp.maximum(m_prev, jnp.max(s, axis=-1, keepdims=True))
        alpha = jnp.exp2(m_prev - m_new)
        p = jnp.exp2(s - _lane_tile(m_new, blk // LANES))
        pv = _dot(p.astype(BF16), ve_scr[pl.ds(off, blk), :])
        acc_scr[rows] = _lane_tile(alpha, 2 * dv // LANES) * acc_scr[rows] + pv
        m_scr[rows] = m_new

    def run(base, width):
        for u in range(width):
            scores(base + u + 1, (u + 1) % 2)
            step(base + u, u % 2)

    def body(j, carry):
        run(ATT_UNROLL * j, ATT_UNROLL)
        return carry

    n_open = 2 * qi
    scores(0, 0)
    n_main = n_open // ATT_UNROLL
    lax.fori_loop(0, n_main, body, 0)
    done = n_main * ATT_UNROLL
    width = ATT_UNROLL // 2
    while width >= 2:
        @pl.when(((n_open - done) // width) % 2 == 1)
        def _(width=width):
            run(n_open - (n_open - done) % (2 * width), width)
        width //= 2
    scores(n_open + 1, 1, hi)
    step(n_open, 0, every, diag=True)
    step(n_open + 1, 1, hi, diag=True)

    lam = _diff_lambda(lam_ref, lambda_init)
    for s_idx in range(2):
        a1 = acc_scr[s_idx * sub:s_idx * sub + blk, :]
        a2 = acc_scr[s_idx * sub + blk:(s_idx + 1) * sub, :]
        o = a1[:, :dv] / a1[:, dv:] - lam * (a2[:, :dv] / a2[:, dv:])
        o = o * lax.rsqrt(jnp.mean(o * o, axis=-1, keepdims=True) + RMS_EPS)
        y_ref[s_idx * blk:(s_idx + 1) * blk, :] = (o * sub_ref[...] * (1.0 - lambda_init)).astype(BF16)


def _attn_prompt(P, B, S, lam_p, subln, lambda_init):
    hd = 64
    blk = ATT_BLK
    bq = 2 * blk
    assert S % bq == 0
    nq = S // bq
    w = 2 * hd
    rows = 2 * bq
    kern = functools.partial(_attn_prompt_kernel, blk=blk, hd=hd, lambda_init=lambda_init)
    return pl.pallas_call(
        kern,
        grid=(B, D_HEADS, nq),
        in_specs=[
            pl.BlockSpec((bq, w), lambda b, h, i: (b * nq + i, P_DQ + h)),
            pl.BlockSpec((S, w), lambda b, h, i: (b, P_DK + h)),
            pl.BlockSpec((S, w), lambda b, h, i: (b, P_DV + h)),
            pl.BlockSpec((4, hd), lambda b, h, i: (0, 0)),
            pl.BlockSpec((1, w), lambda b, h, i: (0, 0)),
        ],
        out_specs=pl.BlockSpec((bq, w), lambda b, h, i: (b * nq + i, h)),
        out_shape=jax.ShapeDtypeStruct((B * S, D_HEADS * w), BF16),
        scratch_shapes=[
            pltpu.VMEM((rows, w), BF16),
            pltpu.VMEM((S, 2 * w), BF16),
            pltpu.VMEM((rows, LANES), F32),
            pltpu.VMEM((rows, 2 * w), F32),
            pltpu.VMEM((2, rows, blk), F32),
        ],
        compiler_params=_cparams(("parallel", "parallel", "arbitrary")),
        name="attn_prompt",
    )(P, P, P, lam_p, subln)


def _attn_sample_kernel(q_ref, kn_ref, vn_ref, kc_ref, vc_ref, lam_ref, sub_ref, y_ref,
                        *, tq, hd, lambda_init):
    lam = _diff_lambda(lam_ref, lambda_init)
    w = 2 * hd
    past = kc_ref.shape[1] // D_HEADS
    for h in range(D_HEADS):
        cols = slice(h * w, (h + 1) * w)
        qs = _stack_maps(q_ref[:, cols] * (hd ** -0.5), hd)
        kc = kc_ref[0, pl.ds(h, past, stride=D_HEADS), :].astype(BF16)
        vc = vc_ref[0, pl.ds(h, past, stride=D_HEADS), :].astype(BF16)
        kn = kn_ref[:, cols]
        vn = vn_ref[:, cols]
        s_c = _dot_nt(qs, kc)
        s_n = _dot_nt(qs, kn)
        m = jnp.maximum(jnp.max(s_c, axis=-1, keepdims=True), jnp.max(s_n, axis=-1, keepdims=True))
        p_c = jnp.exp(s_c - m)
        p_n = jnp.exp(s_n - m)
        l = jnp.sum(p_c, axis=-1, keepdims=True) + jnp.sum(p_n, axis=-1, keepdims=True)
        acc = _dot(p_c.astype(BF16), vc) + _dot(p_n.astype(BF16), vn)
        y_ref[:, cols] = _diff_finish(acc, l, lam, sub_ref[...], lambda_init, tq)


def _attn_sample(P, row_off, B, L, cache_k, cache_v, lam_p, subln, lambda_init):
    hd = 64
    past = cache_k.shape[1]
    ob = row_off // L
    width = D_HEADS * 2 * hd
    kern = functools.partial(_attn_sample_kernel, tq=L, hd=hd, lambda_init=lambda_init)
    cache_k = cache_k.reshape(B, past * D_HEADS, 2 * hd)
    cache_v = cache_v.reshape(B, past * D_HEADS, 2 * hd)
    cache_spec = pl.BlockSpec((1, past * D_HEADS, 2 * hd), lambda b: (b, 0, 0))
    return pl.pallas_call(
        kern,
        grid=(B,),
        in_specs=[
            pl.BlockSpec((L, width), lambda b: (ob + b, P_DQ * LANES // width)),
            pl.BlockSpec((L, width), lambda b: (ob + b, P_DK * LANES // width)),
            pl.BlockSpec((L, width), lambda b: (ob + b, P_DV * LANES // width)),
            cache_spec,
            cache_spec,
            pl.BlockSpec((4, hd), lambda b: (0, 0)),
            pl.BlockSpec((1, 2 * hd), lambda b: (0, 0)),
        ],
        out_specs=pl.BlockSpec((L, width), lambda b: (b, 0)),
        out_shape=jax.ShapeDtypeStruct((B * L, width), BF16),
        compiler_params=_cparams(("parallel",)),
        name="attn_sample",
    )(P, P, P, cache_k, cache_v, lam_p, subln)


def _mix_kernel(xp_ref, xs_ref, yrp_ref, yrs_ref, ydp_ref, yds_ref, gr_ref, gd_ref, lig_ref, lib_ref,
                bg_ref, wpr_ref, wpd_ref, wo_ref, g1_ref, b1_ref, h1_ref, *, alpha, n_p):
    i = pl.program_id(0)
    d = xp_ref.shape[1]

    def compute(x_ref, yr_ref, yd_ref):
        h0 = _layer_norm(x_ref[...], lig_ref[...], lib_ref[...])
        g_ret = jax.nn.sigmoid(gr_ref[...].astype(F32) + bg_ref[:, :d])
        g_diff = jax.nn.sigmoid(gd_ref[...].astype(F32) + bg_ref[:, d:])
        merged = g_ret * _dot(yr_ref[...], wpr_ref[...]) + g_diff * _dot(yd_ref[...], wpd_ref[...])
        mixed = _dot(merged.astype(BF16), wo_ref[...])
        h1_ref[...] = _layer_norm(alpha * h0 + mixed, g1_ref[...], b1_ref[...])

    @pl.when(i < n_p)
    def _():
        compute(xp_ref, yrp_ref, ydp_ref)

    @pl.when(i >= n_p)
    def _():
        compute(xs_ref, yrs_ref, yds_ref)


def _mix(xp, xs, yr_p, yr_s, yd_p, yd_s, P, ln_in_g, ln_in_b, b_gate, wpr, wpd, wo, g1, b1, alpha):
    Tp, D = xp.shape
    Ts = xs.shape[0]
    tm = MIX_TM
    n_p, n_s = Tp // tm, Ts // tm
    gcol = P_GATE * LANES // D
    const = lambda i: (0, 0)
    lo = lambda i: (_lo(i, n_p), 0)
    hi = lambda i: (_hi(i, n_p), 0)
    single = pl.Buffered(1)
    kern = functools.partial(_mix_kernel, alpha=alpha, n_p=n_p)
    return pl.pallas_call(
        kern,
        grid=(n_p + n_s,),
        in_specs=[
            pl.BlockSpec((tm, D), lo),
            pl.BlockSpec((tm, D), hi),
            pl.BlockSpec((tm, yr_p.shape[1]), lo),
            pl.BlockSpec((tm, yr_s.shape[1]), hi),
            pl.BlockSpec((tm, yd_p.shape[1]), lo),
            pl.BlockSpec((tm, yd_s.shape[1]), hi),
            pl.BlockSpec((tm, D), lambda i: (i, gcol)),
            pl.BlockSpec((tm, D), lambda i: (i, gcol + 1)),
            pl.BlockSpec((1, D), const),
            pl.BlockSpec((1, D), const),
            pl.BlockSpec((1, 2 * D), const),
            pl.BlockSpec(wpr.shape, const, pipeline_mode=single),
            pl.BlockSpec(wpd.shape, const, pipeline_mode=single),
            pl.BlockSpec(wo.shape, const, pipeline_mode=single),
            pl.BlockSpec((1, D), const),
            pl.BlockSpec((1, D), const),
        ],
        out_specs=pl.BlockSpec((tm, D), lambda i: (i, 0)),
        out_shape=jax.ShapeDtypeStruct((Tp + Ts, D), F32),
        compiler_params=_cparams(("arbitrary",)),
        name="mix",
    )(xp, xs, yr_p, yr_s, yd_p, yd_s, P, P, ln_in_g, ln_in_b, b_gate, wpr, wpd, wo, g1, b1)


def _mem_kernel(h1_ref, mkp_ref, mvp_ref, mks_ref, mvs_ref, wq_ref, wo_ref, g2_ref, b2_ref, rw_ref,
                rb_ref, h2_ref, route_ref, counts_ref, q_scr, o_scr, cnt_scr, *, n_p, alpha, hd):
    i = pl.program_id(0)
    tm = h1_ref.shape[0]
    h1 = h1_ref[...]
    q_scr[...] = (_dot(h1.astype(BF16), wq_ref[...]) * (hd ** -0.5)).astype(BF16)

    def attend(mk_ref, mv_ref):
        n_sub = mk_ref.shape[0]
        seg = tm // n_sub
        split_heads = mk_ref.shape[2] == hd
        for s in range(n_sub):
            for h in range(M_HEADS):
                if split_heads:
                    rows = pl.ds(h, mk_ref.shape[1] // M_HEADS, stride=M_HEADS)
                    mk = mk_ref[s, rows, :].astype(BF16)
                    mv = mv_ref[s, rows, :].astype(BF16)
                else:
                    mk = mk_ref[s, :, h * hd:(h + 1) * hd].astype(BF16)
                    mv = mv_ref[s, :, h * hd:(h + 1) * hd].astype(BF16)
                qh = q_scr[s * seg:(s + 1) * seg, h * hd:(h + 1) * hd]
                sc = _dot_nt(qh, mk)
                sc = sc - jnp.max(sc, axis=-1, keepdims=True)
                p = jnp.exp(sc)
                p = p / jnp.sum(p, axis=-1, keepdims=True)
                o_scr[s * seg:(s + 1) * seg, h * hd:(h + 1) * hd] = _dot(
                    p.astype(BF16), mv).astype(BF16)

    @pl.when(i < n_p)
    def _():
        attend(mkp_ref, mvp_ref)

    @pl.when(i >= n_p)
    def _():
        attend(mks_ref, mvs_ref)

    h2 = _layer_norm(alpha * h1 + _dot(o_scr[...], wo_ref[...]), g2_ref[...], b2_ref[...])
    h2_ref[...] = h2
    h2b = h2.astype(BF16)

    h_lo = (h2 - h2b.astype(F32)).astype(BF16)
    rw = rw_ref[...]
    rw_hi = rw.astype(BF16)
    rw_lo = (rw - rw_hi.astype(F32)).astype(BF16)
    logits = _dot(h2b, rw_hi) + _dot(h2b, rw_lo) + _dot(h_lo, rw_hi) + rb_ref[...]

    n_e = logits.shape[1]
    eidx = lax.broadcasted_iota(jnp.int32, logits.shape, 1).astype(F32)
    lane = lax.broadcasted_iota(jnp.int32, (tm, LANES), 1)
    route = jnp.zeros((tm, LANES), F32)
    work = logits
    vals = []
    sels = []
    for k in range(TOP_K):
        mx = jnp.max(work, axis=-1, keepdims=True)
        sel = jnp.min(jnp.where(work == mx, eidx, float(n_e)), axis=-1, keepdims=True)
        work = jnp.where(eidx == sel, -jnp.inf, work)
        vals.append(mx)
        sels.append(sel)
        route = jnp.where(lane == TOP_K + k, sel, route)
    ex = [jnp.exp(v - vals[0]) for v in vals]
    den = ex[0] + ex[1] + ex[2] + ex[3]
    for k in range(TOP_K):
        route = jnp.where(lane == k, ex[k] / den, route)

    @pl.when(i == 0)
    def _():
        cnt_scr[...] = jnp.zeros_like(cnt_scr)

    lane_f = lane.astype(F32)
    hit = [lane_f == sels[k] for k in range(TOP_K)]
    cnt = sum(h.astype(F32) for h in hit)
    r_io = lax.broadcasted_iota(jnp.int32, (tm, tm), 0)
    c_io = lax.broadcasted_iota(jnp.int32, (tm, tm), 1)
    ltri = jnp.where(r_io > c_io, 1.0, 0.0).astype(BF16)
    excl = _dot(ltri, cnt.astype(BF16)) + cnt_scr[...]
    for k in range(TOP_K):
        rank = jnp.sum(jnp.where(hit[k], excl, 0.0), axis=-1, keepdims=True)
        route = jnp.where(lane == 2 * TOP_K + k, rank, route)
    route_ref[...] = route
    cnt_scr[...] = cnt_scr[...] + jnp.sum(cnt, axis=0, keepdims=True)
    counts_ref[...] = cnt_scr[...]


def _mem_attn(h1, t_prompt, mem_kp, mem_vp, l_prompt, mem_ks, mem_vs, l_sample, wq, wo, g2, b2, rw, rb,
              alpha):
    T, D = h1.shape
    hd = 128
    tm = MEM_TM
    assert l_prompt % tm == 0 and tm % l_sample == 0
    n_p = t_prompt // tm
    n_s = (T - t_prompt) // tm
    per_b = l_prompt // tm
    sub_s = tm // l_sample
    n_mem = mem_kp.shape[1]
    bp = mem_kp.shape[0]
    const = lambda i: (0, 0)
    single = pl.Buffered(1)
    mem_p = pl.BlockSpec((1, n_mem, M_HEADS * hd), lambda i: (jnp.minimum(i // per_b, bp - 1), 0, 0))
    bs = mem_ks.shape[0]
    mem_ks = mem_ks.reshape(bs, n_mem * M_HEADS, hd)
    mem_vs = mem_vs.reshape(bs, n_mem * M_HEADS, hd)
    mem_s = pl.BlockSpec((sub_s, n_mem * M_HEADS, hd), lambda i: (_hi(i, n_p), 0, 0))
    kern = functools.partial(_mem_kernel, n_p=n_p, alpha=alpha, hd=hd)
    return pl.pallas_call(
        kern,
        grid=(n_p + n_s,),
        in_specs=[
            pl.BlockSpec((tm, D), lambda i: (i, 0)),
            mem_p, mem_p, mem_s, mem_s,
            pl.BlockSpec(wq.shape, const, pipeline_mode=single),
            pl.BlockSpec(wo.shape, const, pipeline_mode=single),
            pl.BlockSpec((1, D), const),
            pl.BlockSpec((1, D), const),
            pl.BlockSpec(rw.shape, const, pipeline_mode=single),
            pl.BlockSpec((1, rw.shape[1]), const),
        ],
        out_specs=[
            pl.BlockSpec((tm, D), lambda i: (i, 0)),
            pl.BlockSpec((tm, LANES), lambda i: (i, 0)),
            pl.BlockSpec((1, LANES), const),
        ],
        out_shape=[
            jax.ShapeDtypeStruct((T, D), F32),
            jax.ShapeDtypeStruct((T, LANES), F32),
            jax.ShapeDtypeStruct((1, LANES), F32),
        ],
        scratch_shapes=[
            pltpu.VMEM((tm, M_HEADS * hd), BF16),
            pltpu.VMEM((tm, M_HEADS * hd), BF16),
            pltpu.VMEM((1, LANES), F32),
        ],
        compiler_params=_cparams(("arbitrary",)),
        name="mem_attn",
    )(h1, mem_kp, mem_vp, mem_ks, mem_vs, wq, wo, g2, b2, rw, rb)


def _mem_kv_kernel(x_ref, w_ref, o_ref):
    o_ref[...] = _dot(x_ref[...].astype(BF16), w_ref[...])


def _mem_kv(mem, w):
    R, D = mem.shape
    N = w.shape[1]
    tm = 256
    return pl.pallas_call(
        _mem_kv_kernel,
        grid=(R // tm,),
        in_specs=[pl.BlockSpec((tm, D), lambda i: (i, 0)), pl.BlockSpec((D, N), lambda i: (0, 0))],
        out_specs=pl.BlockSpec((tm, N), lambda i: (i, 0)),
        out_shape=jax.ShapeDtypeStruct((R, N), F32),
        compiler_params=_cparams(("parallel",)),
        name="mem_kv",
    )(mem, w)


def _row_copy(src_hbm, dst, sem, src_row, dst_row):
    return pltpu.make_async_copy(src_hbm.at[pl.ds(src_row, 1)], dst.at[pl.ds(dst_row, 1)], sem)


def _rows_to_slabs(x, dst):
    n = x.shape[0]
    n_slab = x.shape[1] // LANES
    for s in range(n_slab):
        dst[pl.ds(s, n, stride=n_slab), :] = x[:, s * LANES:(s + 1) * LANES]


def _slabs_to_rows(src, n_slab):
    n = src.shape[0] // n_slab
    return jnp.concatenate([src[pl.ds(s, n, stride=n_slab), :] for s in range(n_slab)], axis=1)


def _dispatch_kernel(dest_ref, pend_ref, padded_ref, nu_ref, h_ref, xs_hbm, stage, zeros, sem, zsem,
                     *, tm, bm, n_steps, n_blocks):
    i = pl.program_id(0)
    slot = i % 2

    n_slab = h_ref.shape[1] // LANES

    def zero_block(off):
        off = pl.multiple_of(off * n_slab, bm * n_slab)
        return pltpu.make_async_copy(zeros, xs_hbm.at[pl.ds(off, bm * n_slab)], zsem)

    def zero_fill(act):
        for e in range(N_EXPERTS):
            @pl.when(padded_ref[e] > 0)
            def _(e=e):
                act(zero_block(pend_ref[e] - bm))
        for b in range(n_blocks - N_EXPERTS, n_blocks):
            @pl.when(b >= nu_ref[0])
            def _(b=b):
                act(zero_block(b * bm))

    @pl.when(i == 0)
    def _():
        zeros[...] = jnp.zeros_like(zeros)
        zero_fill(lambda c: c.start())
        zero_fill(lambda c: c.wait())

    def wait(s):
        for _ in range(TOP_K):
            pltpu.make_async_copy(stage.at[s], xs_hbm.at[pl.ds(0, tm * n_slab)], sem.at[s]).wait()

    @pl.when(i >= 2)
    def _():
        wait(slot)

    _rows_to_slabs(h_ref[...], stage.at[slot])

    def body(j, carry):
        t0 = j * SUBLANES
        for u in range(SUBLANES):
            src = stage.at[slot, pl.ds(pl.multiple_of((t0 + u) * n_slab, n_slab), n_slab)]
            for k in range(TOP_K):
                row = dest_ref[(i * tm + t0 + u) * TOP_K + k]
                dst = xs_hbm.at[pl.ds(pl.multiple_of(row * n_slab, n_slab), n_slab)]
                pltpu.make_async_copy(src, dst, sem.at[slot]).start(priority=k % 2)
        return carry
    lax.fori_loop(0, tm // SUBLANES, body, 0)

    @pl.when(i == n_steps - 1)
    def _():
        wait(slot)
        if n_steps > 1:
            wait(1 - slot)


def _moe_dispatch(h2, dest, pad_end, padded, n_used, n_rows, bm):
    T, D = h2.shape
    tm = GATHER_ROWS
    n_steps = T // tm
    n_slab = D // LANES
    kern = functools.partial(_dispatch_kernel, tm=tm, bm=bm, n_steps=n_steps, n_blocks=n_rows // bm)
    return pl.pallas_call(
        kern,
        grid_spec=pltpu.PrefetchScalarGridSpec(
            num_scalar_prefetch=4,
            grid=(n_steps,),
            in_specs=[pl.BlockSpec((tm, D), lambda i, d, pe, pd, nu: (i, 0))],
            out_specs=pl.BlockSpec(memory_space=pl.ANY),
            scratch_shapes=[
                pltpu.VMEM((2, tm * n_slab, LANES), F32),
                pltpu.VMEM((bm * n_slab, LANES), F32),
                pltpu.SemaphoreType.DMA((2,)),
                pltpu.SemaphoreType.DMA(()),
            ],
        ),
        out_shape=jax.ShapeDtypeStruct((n_rows * n_slab, LANES), F32),
        compiler_params=_cparams(("arbitrary",)),
        name="moe_dispatch",
    )(dest, pad_end, padded, n_used, h2)


def _cast_rows(src, dst):
    rows = CAST_ROWS

    def body(i, carry):
        r = pl.multiple_of(i * rows, rows)
        dst[pl.ds(r, rows), :] = src[pl.ds(r, rows), :].astype(BF16)
        return carry
    lax.fori_loop(0, src.shape[0] // rows, body, 0)


def _expert_weights(be_ref, first_ref, next_ref, copies, cast, m):
    @pl.when(first_ref[m] == 1)
    def _():
        @pl.when(m == 0)
        def _():
            for c in copies(be_ref[0]):
                c.start()

        for c in copies(be_ref[m]):
            c.wait()
        cast()
        ne = next_ref[m]

        @pl.when(ne >= 0)
        def _():
            for c in copies(ne):
                c.start(priority=1)


def _moe_up_kernel(be_ref, nu_ref, first_ref, next_ref, x_ref, w1_hbm, b_ref, a_ref,
                   stage, w_scr, sem):
    m = pl.program_id(0)
    ff = w_scr.shape[2]

    def copies(e):
        return [pltpu.make_async_copy(w1_hbm.at[e, :, pl.ds(t * ff, ff)], stage.at[t], sem)
                for t in range(2)]

    def cast():
        for t in range(2):
            _cast_rows(stage.at[t], w_scr.at[t])

    _expert_weights(be_ref, first_ref, next_ref, copies, cast, m)

    @pl.when(m < nu_ref[0])
    def _():
        x = _slabs_to_rows(x_ref, w_scr.shape[1] // LANES).astype(BF16)
        half = ff // 2
        for c in range(2):
            cols = slice(c * half, (c + 1) * half)
            lin_cols = slice(ff + c * half, ff + (c + 1) * half)
            u_glu = jnp.minimum(_dot(x, w_scr[0, :, cols]) + b_ref[0, :, cols], SWIGLU_LIMIT)
            u_lin = jnp.clip(_dot(x, w_scr[1, :, cols]) + b_ref[0, :, lin_cols],
                             -SWIGLU_LIMIT, SWIGLU_LIMIT)
            a = u_glu * jax.nn.sigmoid(SWIGLU_ALPHA * u_glu) * (u_lin + 1.0)
            a_ref[:, cols] = a.astype(BF16)

    @pl.when(m >= nu_ref[0])
    def _():
        a_ref[...] = jnp.zeros_like(a_ref)


def _moe_up(xs, w1, b1, plan):
    D = w1.shape[1]
    n_slab = D // LANES
    n_rows = xs.shape[0] // n_slab
    F = w1.shape[2] // 2
    bm = MOE_BM
    return pl.pallas_call(
        _moe_up_kernel,
        grid_spec=pltpu.PrefetchScalarGridSpec(
            num_scalar_prefetch=4,
            grid=(n_rows // bm,),
            in_specs=[
                pl.BlockSpec((bm * n_slab, LANES),
                             lambda m, be, nu, fi, nx: (jnp.minimum(m, nu[0] - 1), 0)),
                pl.BlockSpec(memory_space=pl.ANY),
                pl.BlockSpec((1, 1, 2 * F), lambda m, be, nu, fi, nx: (be[m], 0, 0)),
            ],
            out_specs=pl.BlockSpec((bm, F), lambda m, be, nu, fi, nx: (m, 0)),
            scratch_shapes=[
                pltpu.VMEM((2, D, F), F32),
                pltpu.VMEM((2, D, F), BF16),
                pltpu.SemaphoreType.DMA(()),
            ],
        ),
        out_shape=jax.ShapeDtypeStruct((n_rows, F), BF16),
        compiler_params=_cparams(("arbitrary",)),
        name="moe_up",
    )(*plan, xs, w1, b1)


def _moe_down_kernel(be_ref, nu_ref, first_ref, next_ref, a_ref, w2_hbm, b_ref, y_ref,
                     stage, w_scr, sem):
    m = pl.program_id(0)

    def copies(e):
        return [pltpu.make_async_copy(w2_hbm.at[e], stage, sem)]

    def cast():
        _cast_rows(stage, w_scr)

    _expert_weights(be_ref, first_ref, next_ref, copies, cast, m)

    @pl.when(m < nu_ref[0])
    def _():
        y_ref[...] = _dot(a_ref[...], w_scr[...]) + b_ref[0]

    @pl.when(m >= nu_ref[0])
    def _():
        y_ref[...] = jnp.zeros_like(y_ref)


def _moe_down(a, w2, b2, plan):
    n_rows, F = a.shape
    D = w2.shape[2]
    bm = MOE_BM
    return pl.pallas_call(
        _moe_down_kernel,
        grid_spec=pltpu.PrefetchScalarGridSpec(
            num_scalar_prefetch=4,
            grid=(n_rows // bm,),
            in_specs=[
                pl.BlockSpec((bm, F), lambda m, be, nu, fi, nx: (jnp.minimum(m, nu[0] - 1), 0)),
                pl.BlockSpec(memory_space=pl.ANY),
                pl.BlockSpec((1, 1, D), lambda m, be, nu, fi, nx: (be[m], 0, 0)),
            ],
            out_specs=pl.BlockSpec((bm, D), lambda m, be, nu, fi, nx: (m, 0)),
            scratch_shapes=[
                pltpu.VMEM((F, D), F32),
                pltpu.VMEM((F, D), BF16),
                pltpu.SemaphoreType.DMA(()),
            ],
        ),
        out_shape=jax.ShapeDtypeStruct((n_rows, D), F32),
        compiler_params=_cparams(("arbitrary",)),
        name="moe_down",
    )(*plan, a, w2, b2)


def _combine_kernel(dest_ref, h2_ref, rt_ref, g_ref, b_ref, ys_hbm, op_ref, os_ref, buf, sem,
                    *, tm, n_steps, n_prompt, alpha):
    i = pl.program_id(0)

    def issue(step, slot):
        def body(j, carry):
            t0 = pl.multiple_of(j * SUBLANES, SUBLANES)
            for u in range(SUBLANES):
                for k in range(TOP_K):
                    row = dest_ref[(step * tm + t0 + u) * TOP_K + k]
                    _row_copy(ys_hbm, buf.at[slot, k], sem.at[slot], row, t0 + u).start(priority=k % 2)
            return carry
        lax.fori_loop(0, tm // SUBLANES, body, 0)

    def wait(slot):
        for k in range(TOP_K):
            pltpu.make_async_copy(ys_hbm.at[pl.ds(0, tm)], buf.at[slot, k], sem.at[slot]).wait()

    slot = i % 2

    @pl.when(i == 0)
    def _():
        issue(0, 0)

    @pl.when(i + 1 < n_steps)
    def _():
        issue(i + 1, 1 - slot)

    wait(slot)
    y = rt_ref[:, 0:1] * buf[slot, 0]
    for k in range(1, TOP_K):
        y = y + rt_ref[:, k:k + 1] * buf[slot, k]
    out = _layer_norm(alpha * h2_ref[...] + y, g_ref[...], b_ref[...])

    @pl.when(i < n_prompt)
    def _():
        op_ref[...] = out

    @pl.when(i >= n_prompt)
    def _():
        os_ref[...] = out


def _moe_combine(ys, dest, h2, route, g3, b3, t_prompt, alpha):
    T, D = h2.shape
    tm = COMBINE_TM
    n_steps = T // tm
    n_prompt = t_prompt // tm
    kern = functools.partial(_combine_kernel, tm=tm, n_steps=n_steps, n_prompt=n_prompt, alpha=alpha)
    return pl.pallas_call(
        kern,
        grid_spec=pltpu.PrefetchScalarGridSpec(
            num_scalar_prefetch=1,
            grid=(n_steps,),
            in_specs=[
                pl.BlockSpec((tm, D), lambda i, d: (i, 0)),
                pl.BlockSpec((tm, LANES), lambda i, d: (i, 0)),
                pl.BlockSpec((1, D), lambda i, d: (0, 0)),
                pl.BlockSpec((1, D), lambda i, d: (0, 0)),
                pl.BlockSpec(memory_space=pl.ANY),
            ],
            out_specs=[
                pl.BlockSpec((tm, D), lambda i, d: (jnp.minimum(i, n_prompt - 1), 0)),
                pl.BlockSpec((tm, D), lambda i, d: (jnp.maximum(i - n_prompt, 0), 0)),
            ],
            scratch_shapes=[pltpu.VMEM((2, TOP_K, tm, D), F32), pltpu.SemaphoreType.DMA((2,))],
        ),
        out_shape=[
            jax.ShapeDtypeStruct((t_prompt, D), F32),
            jax.ShapeDtypeStruct((T - t_prompt, D), F32),
        ],
        compiler_params=_cparams(("arbitrary",)),
        name="moe_combine",
    )(dest, h2, route, g3, b3, ys)


def _moe_plan(route, counts, bm):
    T = route.shape[0]
    top_idx = route[:, TOP_K:2 * TOP_K].astype(jnp.int32)
    rank = route[:, 2 * TOP_K:3 * TOP_K].astype(jnp.int32)
    counts = counts[0, :N_EXPERTS].astype(jnp.int32)
    padded = (counts + bm - 1) // bm * bm
    pad_end = jnp.cumsum(padded)
    pad_start = pad_end - padded
    dest = (pad_start[top_idx] + rank).reshape(-1)
    n_blocks = -(-(T * TOP_K) // bm) + N_EXPERTS
    block_start = jnp.arange(n_blocks, dtype=jnp.int32) * bm
    block_e = jnp.sum((pad_end[None, :] <= block_start[:, None]).astype(jnp.int32), axis=1)
    block_e = jnp.minimum(block_e, N_EXPERTS - 1)
    n_used = (pad_end[-1] // bm).astype(jnp.int32).reshape(1)
    blk = jnp.arange(n_blocks, dtype=jnp.int32)
    prev_e = jnp.concatenate([jnp.full((1,), -1, jnp.int32), block_e[:-1]])
    first = jnp.logical_and(blk < n_used[0], block_e != prev_e)
    later = jnp.logical_and(first[None, :], blk[None, :] > blk[:, None])
    nxt_blk = jnp.min(jnp.where(later, blk[None, :], n_blocks), axis=1)
    next_e = jnp.where(nxt_blk < n_blocks, block_e[jnp.minimum(nxt_blk, n_blocks - 1)], -1)
    plan = (block_e, n_used, first.astype(jnp.int32), next_e.astype(jnp.int32))
    return dest, pad_end, padded, plan, n_blocks * bm


def kernel(x_prompt, x_sample, cache_diff_k, cache_diff_v, state_ret, cache_mem_k, cache_mem_v, mem_prompt, ln_in_g, ln_in_b, w_in, b_gate, diff_lambda, diff_subln, w_proj_ret, w_proj_diff, w_out, ln1_g, ln1_b, w_mq, w_mk, w_mv, w_mo, ln2_g, ln2_b, router_w, router_b, w1, b1, w2, b2, ln3_g, ln3_b):
    Bp, Lp, D = x_prompt.shape
    Bs, Ls, _ = x_sample.shape
    depth = w_in.shape[0]
    assert depth == 1
    past = cache_diff_k.shape[2]
    n_mem = mem_prompt.shape[1]
    Tp, Ts = Bp * Lp, Bs * Ls
    alpha = (2.0 * depth) ** 0.25
    lambda_init = 0.8 - 0.6 * math.exp(-0.3 * 0)
    row = lambda v: v.reshape(1, -1)

    xp = x_prompt.reshape(Tp, D)
    xs_in = x_sample.reshape(Ts, D)
    kv_blk = 4096 // IN_TN
    P, dk_p, dv_p, dk_s, dv_s = _in_proj(xp, xs_in, row(ln_in_g), row(ln_in_b), w_in[0].astype(BF16),
                                         kv_blk)

    zeros_state = jnp.zeros((Bp,) + state_ret.shape[2:], F32)
    yr_p, s_p = _retention(P, 0, Bp, Lp, 0, zeros_state)
    yr_s, s_s = _retention(P, Tp, Bs, Ls, past, state_ret[0])

    yd_p = _attn_prompt(P, Bp, Lp, diff_lambda[0], row(diff_subln[0]), lambda_init)
    yd_s = _attn_sample(P, Tp, Bs, Ls, cache_diff_k[0], cache_diff_v[0], diff_lambda[0],
                        row(diff_subln[0]), lambda_init)

    h1 = _mix(xp, xs_in, yr_p, yr_s, yd_p, yd_s, P, row(ln_in_g), row(ln_in_b), row(b_gate[0]),
              w_proj_ret[0].astype(BF16), w_proj_diff[0].astype(BF16), w_out[0].astype(BF16),
              row(ln1_g[0]), row(ln1_b[0]), alpha)

    w_mkv = jnp.concatenate([w_mk[0], w_mv[0]], axis=1).astype(BF16)
    mkv = _mem_kv(mem_prompt.reshape(Bp * n_mem, D), w_mkv)
    hm = w_mk.shape[2]
    mk_p = mkv[:, :hm].reshape(Bp, n_mem, hm)
    mv_p = mkv[:, hm:].reshape(Bp, n_mem, hm)
    h2, route, counts = _mem_attn(
        h1, Tp, mk_p, mv_p, Lp, cache_mem_k[0], cache_mem_v[0], Ls, w_mq[0].astype(BF16),
        w_mo[0].astype(BF16),
        row(ln2_g[0]), row(ln2_b[0]), router_w[0], row(router_b[0]), alpha)

    dest, pad_end, padded, plan, n_rows = _moe_plan(route, counts, MOE_BM)
    xs = _moe_dispatch(h2, dest, pad_end, padded, plan[1], n_rows, MOE_BM)
    act = _moe_up(xs, w1[0], b1[0][:, None, :], plan)
    ys = _moe_down(act, w2[0], b2[0][:, None, :], plan)
    out_p, out_s = _moe_combine(ys, dest, h2, route, row(ln3_g[0]), row(ln3_b[0]), Tp, alpha)

    return (
        out_p.reshape(Bp, Lp, D),
        out_s.reshape(Bs, Ls, D),
        dk_p.reshape(1, Bp, Lp, D_HEADS, 128),
        dv_p.reshape(1, Bp, Lp, D_HEADS, 128),
        s_p[None],
        mk_p.reshape(1, Bp, n_mem, M_HEADS, hm // M_HEADS),
        mv_p.reshape(1, Bp, n_mem, M_HEADS, hm // M_HEADS),
        dk_s.reshape(1, Bs, Ls, D_HEADS, 128),
        dv_s.reshape(1, Bs, Ls, D_HEADS, 128),
        s_s[None],
    )
```

```python
import functools
import math

import jax
import jax.numpy as jnp
from jax import lax
from jax.experimental import pallas as pl
from jax.experimental.pallas import tpu as pltpu

F32 = jnp.float32
BF16 = jnp.bfloat16

CHUNK = 64
R_HEADS = 4
D_HEADS = 8
M_HEADS = 4
N_EXPERTS = 32
TOP_K = 4
SWIGLU_ALPHA = 1.702
SWIGLU_LIMIT = 7.0
LN_EPS = 1e-5
RMS_EPS = 1e-5
ROPE_BASE = 10000.0
LOG2E = 1.4426950408889634

LANES = 128
SUBLANES = 8
VMEM_LIMIT = 58 * 1024 * 1024

IN_TM = 512
IN_TN = 2048
RET_CHUNK = 256
ATT_BLK = 512
ATT_UNROLL = 4
MIX_TM = 256
MEM_TM = 512
MOE_BM = 256
CAST_ROWS = 256
GATHER_ROWS = 512
COMBINE_TM = 256


def _cparams(sem):
    return pltpu.CompilerParams(dimension_semantics=sem, vmem_limit_bytes=VMEM_LIMIT)


def _layer_norm(x, g, b):
    mu = jnp.mean(x, axis=-1, keepdims=True)
    xc = x - mu
    var = jnp.mean(xc * xc, axis=-1, keepdims=True)
    return xc * lax.rsqrt(var + LN_EPS) * g + b


def _dot(a, b):
    return jnp.dot(a, b, preferred_element_type=F32)


def _dot_nt(a, b):
    return lax.dot_general(a, b, (((1,), (1,)), ((), ())), preferred_element_type=F32)


def _dot_tn(a, b):
    return lax.dot_general(a, b, (((0,), (0,)), ((), ())), preferred_element_type=F32)


def _lo(i, n):
    return jnp.minimum(i, n - 1)


def _hi(i, n):
    return jnp.maximum(i - n, 0)


def _in_proj_kernel(xp_ref, xs_ref, g_ref, b_ref, w_ref, p_ref, dkp_ref, dvp_ref, dks_ref, dvs_ref,
                    h_scr, *, n_p):
    i = pl.program_id(0)
    j = pl.program_id(1)
    half = dkp_ref.shape[1]

    @pl.when(jnp.logical_and(j == 0, i < n_p))
    def _():
        h_scr[...] = _layer_norm(xp_ref[...], g_ref[...], b_ref[...]).astype(BF16)

    @pl.when(jnp.logical_and(j == 0, i >= n_p))
    def _():
        h_scr[...] = _layer_norm(xs_ref[...], g_ref[...], b_ref[...]).astype(BF16)

    acc = _dot(h_scr[...], w_ref[...])
    p_ref[...] = acc.astype(BF16)

    @pl.when(jnp.logical_and(j == 0, i < n_p))
    def _():
        dkp_ref[...] = acc[:, :half]
        dvp_ref[...] = acc[:, half:]

    @pl.when(jnp.logical_and(j == 0, i >= n_p))
    def _():
        dks_ref[...] = acc[:, :half]
        dvs_ref[...] = acc[:, half:]


def _in_proj(xp, xs, g, b, w, kv_blk):
    Tp, D = xp.shape
    Ts = xs.shape[0]
    N = w.shape[1]
    n_p, n_s = Tp // IN_TM, Ts // IN_TM
    half = IN_TN // 2

    def col(j):
        return jnp.where(j == 0, kv_blk, jnp.where(j <= kv_blk, j - 1, j))

    single = pl.Buffered(1)
    kv_p = pl.BlockSpec((IN_TM, half), lambda i, j: (_lo(i, n_p), 0))
    kv_s = pl.BlockSpec((IN_TM, half), lambda i, j: (_hi(i, n_p), 0), pipeline_mode=single)
    return pl.pallas_call(
        functools.partial(_in_proj_kernel, n_p=n_p),
        grid=(n_p + n_s, N // IN_TN),
        in_specs=[
            pl.BlockSpec((IN_TM, D), lambda i, j: (_lo(i, n_p), 0)),
            pl.BlockSpec((IN_TM, D), lambda i, j: (_hi(i, n_p), 0), pipeline_mode=single),
            pl.BlockSpec((1, D), lambda i, j: (0, 0)),
            pl.BlockSpec((1, D), lambda i, j: (0, 0)),
            pl.BlockSpec((D, IN_TN), lambda i, j: (0, col(j))),
        ],
        out_specs=[pl.BlockSpec((IN_TM, IN_TN), lambda i, j: (i, j)), kv_p, kv_p, kv_s, kv_s],
        out_shape=[
            jax.ShapeDtypeStruct((Tp + Ts, N), BF16),
            jax.ShapeDtypeStruct((Tp, half), F32),
            jax.ShapeDtypeStruct((Tp, half), F32),
            jax.ShapeDtypeStruct((Ts, half), F32),
            jax.ShapeDtypeStruct((Ts, half), F32),
        ],
        scratch_shapes=[pltpu.VMEM((IN_TM, D), BF16)],
        compiler_params=_cparams(("arbitrary", "arbitrary")),
        name="in_proj",
    )(xp, xs, g, b, w)


P_DK = 0
P_DV = 8
P_RQ = 16
P_RK = 20
P_RV = 24
P_RG = 32
P_DQ = 40
P_GATE = 48


def _retention_kernel(q_ref, k_ref, v_ref, g_ref, cos_ref, sin_ref, dm_ref, qd_ref, kd_ref,
                      bd_ref, s0_ref, y_ref, sout_ref, s_scr, *, n_chunks, dk, dv):
    c = pl.program_id(1)

    @pl.when(c == 0)
    def _():
        s_scr[...] = s0_ref[0]

    cos = cos_ref[...]
    sin = sin_ref[...]
    k_scale = dk ** -0.5
    for h in range(R_HEADS):
        q = q_ref[:, h * dk:(h + 1) * dk].astype(F32)
        k = k_ref[:, h * dk:(h + 1) * dk].astype(F32)
        v = v_ref[:, h * dv:(h + 1) * dv]
        g = g_ref[:, h * dv:(h + 1) * dv].astype(F32)
        q = q * cos + pltpu.roll(q, dk // 2, 1) * sin
        k = (k * cos + pltpu.roll(k, dk // 2, 1) * sin) * k_scale
        qb = q.astype(BF16)
        kb = k.astype(BF16)
        s = s_scr[h]
        scores = _dot_nt(qb, kb) * dm_ref[h]
        o = _dot(scores.astype(BF16), v) + _dot((q * qd_ref[h]).astype(BF16), s.astype(BF16))
        s_scr[h] = bd_ref[h] * s + _dot_tn((k * kd_ref[h]).astype(BF16), v)
        o = o * lax.rsqrt(jnp.mean(o * o, axis=-1, keepdims=True) + RMS_EPS)
        y_ref[:, h * dv:(h + 1) * dv] = (g * jax.nn.sigmoid(g) * o).astype(BF16)

    @pl.when(c == n_chunks - 1)
    def _():
        sout_ref[0] = s_scr[...]


def _retention(P, row_off, B, L, pos0, S0):
    dk, dv = 128, 256
    C = min(L, RET_CHUNK)
    nc = L // C
    ob = row_off // C
    pos = (pos0 + jnp.arange(L, dtype=jnp.int32)).astype(F32)
    inv = 1.0 / (ROPE_BASE ** jnp.linspace(0.0, 1.0, dk // 2, dtype=F32))
    ang = pos[:, None] * inv[None, :]
    cos = jnp.concatenate([jnp.cos(ang), jnp.cos(ang)], axis=-1)
    sin = jnp.concatenate([-jnp.sin(ang), jnp.sin(ang)], axis=-1)
    log_g = jnp.log1p(-jnp.power(2.0, -5.0 - jnp.arange(R_HEADS, dtype=F32)))
    i = jnp.arange(C, dtype=F32)
    rel = i[:, None] - i[None, :]
    dmask = jnp.where(rel >= 0, jnp.exp(log_g[:, None, None] * jnp.maximum(rel, 0.0)), 0.0)
    q_decay = jnp.exp(log_g[:, None] * (i + 1.0))[..., None]
    k_decay = jnp.exp(log_g[:, None] * (C - 1.0 - i))[..., None]
    b_decay = jnp.exp(log_g * C)[:, None, None]

    def rows(b, c):
        return ob + b * nc + c

    kern = functools.partial(_retention_kernel, n_chunks=nc, dk=dk, dv=dv)
    return pl.pallas_call(
        kern,
        grid=(B, nc),
        in_specs=[
            pl.BlockSpec((C, R_HEADS * dk), lambda b, c: (rows(b, c), P_RQ * LANES // (R_HEADS * dk))),
            pl.BlockSpec((C, R_HEADS * dk), lambda b, c: (rows(b, c), P_RK * LANES // (R_HEADS * dk))),
            pl.BlockSpec((C, R_HEADS * dv), lambda b, c: (rows(b, c), P_RV * LANES // (R_HEADS * dv))),
            pl.BlockSpec((C, R_HEADS * dv), lambda b, c: (rows(b, c), P_RG * LANES // (R_HEADS * dv))),
            pl.BlockSpec((C, dk), lambda b, c: (c, 0)),
            pl.BlockSpec((C, dk), lambda b, c: (c, 0)),
            pl.BlockSpec((R_HEADS, C, C), lambda b, c: (0, 0, 0)),
            pl.BlockSpec((R_HEADS, C, 1), lambda b, c: (0, 0, 0)),
            pl.BlockSpec((R_HEADS, C, 1), lambda b, c: (0, 0, 0)),
            pl.BlockSpec((R_HEADS, 1, 1), lambda b, c: (0, 0, 0)),
            pl.BlockSpec((1, R_HEADS, dk, dv), lambda b, c: (b, 0, 0, 0)),
        ],
        out_specs=[
            pl.BlockSpec((C, R_HEADS * dv), lambda b, c: (b * nc + c, 0)),
            pl.BlockSpec((1, R_HEADS, dk, dv), lambda b, c: (b, 0, 0, 0)),
        ],
        out_shape=[
            jax.ShapeDtypeStruct((B * L, R_HEADS * dv), BF16),
            jax.ShapeDtypeStruct((B, R_HEADS, dk, dv), F32),
        ],
        scratch_shapes=[pltpu.VMEM((R_HEADS, dk, dv), F32)],
        compiler_params=_cparams(("parallel", "arbitrary")),
        name="retention",
    )(P, P, P, P, cos, sin, dmask, q_decay, k_decay, b_decay, S0)


def _diff_lambda(lam_ref, lambda_init):
    lv = lam_ref[...]
    a = jnp.sum(lv[0:1] * lv[1:2], axis=-1, keepdims=True)
    b = jnp.sum(lv[2:3] * lv[3:4], axis=-1, keepdims=True)
    return jnp.exp(a) - jnp.exp(b) + lambda_init


def _lane_tile(x, n):
    return jnp.concatenate([x] * n, axis=1)


def _stack_maps(q, hd):
    lane = lax.broadcasted_iota(jnp.int32, q.shape, 1)
    zero = jnp.zeros_like(q)
    return jnp.concatenate([jnp.where(lane < hd, q, zero), jnp.where(lane < hd, zero, q)], axis=0)


def _diff_finish(acc, l, lam, subln, lambda_init, tq):
    o = acc[:tq] / l[:tq] - lam * (acc[tq:] / l[tq:])
    o = o * lax.rsqrt(jnp.mean(o * o, axis=-1, keepdims=True) + RMS_EPS)
    return (o * subln * (1.0 - lambda_init)).astype(BF16)


def _attn_prompt_kernel(q_ref, k_ref, v_ref, lam_ref, sub_ref, y_ref, qs_scr, ve_scr, m_scr, acc_scr,
                        s_scr, *, blk, hd, lambda_init):
    qi = pl.program_id(2)
    dv = 2 * hd
    seq = v_ref.shape[0]
    sub = 2 * blk
    lo, hi, every = slice(0, sub), slice(sub, 2 * sub), slice(0, 2 * sub)

    @pl.when(qi == 0)
    def _():
        ve_scr[:, :dv] = v_ref[...]
        ve_scr[:, dv:] = jnp.ones((seq, dv), BF16)

    for s_idx, rows in enumerate((lo, hi)):
        q = q_ref[s_idx * blk:(s_idx + 1) * blk, :].astype(F32) * (hd ** -0.5 * LOG2E)
        qs_scr[rows] = _stack_maps(q.astype(BF16), hd)
    m_scr[...] = jnp.full_like(m_scr, -jnp.inf)
    acc_scr[...] = jnp.zeros_like(acc_scr)

    def scores(ki, slot, rows=every):
        off = pl.multiple_of(ki * blk, blk)
        s_scr[slot, rows] = _dot_nt(qs_scr[rows], k_ref[pl.ds(off, blk), :])

    def step(ki, slot, rows=every, diag=False):
        off = pl.multiple_of(ki * blk, blk)
        s = s_scr[slot, rows]
        if diag:
            row = lax.broadcasted_iota(jnp.int32, s.shape, 0)
            col = lax.broadcasted_iota(jnp.int32, s.shape, 1)
            shift = CHUNK.bit_length() - 1
            seen = (col >> shift) <= ((row & (blk - 1)) >> shift)
            if s.shape[0] > sub:
                seen = jnp.logical_or(row >= sub, seen)
            s = jnp.where(seen, s, -1e30)
        m_prev = m_scr[rows]
        m_new = jnp.maximum(m_prev, jnp.max(s, axis=-1, keepdims=True))
        alpha = jnp.exp2(m_prev - m_new)
        p = jnp.exp2(s - _lane_tile(m_new, blk // LANES))
        pv = _dot(p.astype(BF16), ve_scr[pl.ds(off, blk), :])
        acc_scr[rows] = _lane_tile(alpha, 2 * dv // LANES) * acc_scr[rows] + pv
        m_scr[rows] = m_new

    def run(base, width):
        for u in range(width):
            scores(base + u + 1, (u + 1) % 2)
            step(base + u, u % 2)

    def body(j, carry):
        run(ATT_UNROLL * j, ATT_UNROLL)
        return carry

    n_open = 2 * qi
    scores(0, 0)
    n_main = n_open // ATT_UNROLL
    lax.fori_loop(0, n_main, body, 0)
    done = n_main * ATT_UNROLL
    width = ATT_UNROLL // 2
    while width >= 2:
        @pl.when(((n_open - done) // width) % 2 == 1)
        def _(width=width):
            run(n_open - (n_open - done) % (2 * width), width)
        width //= 2
    scores(n_open + 1, 1, hi)
    step(n_open, 0, every, diag=True)
    step(n_open + 1, 1, hi, diag=True)

    lam = _diff_lambda(lam_ref, lambda_init)
    for s_idx in range(2):
        a1 = acc_scr[s_idx * sub:s_idx * sub + blk, :]
        a2 = acc_scr[s_idx * sub + blk:(s_idx + 1) * sub, :]
        o = a1[:, :dv] / a1[:, dv:] - lam * (a2[:, :dv] / a2[:, dv:])
        o = o * lax.rsqrt(jnp.mean(o * o, axis=-1, keepdims=True) + RMS_EPS)
        y_ref[s_idx * blk:(s_idx + 1) * blk, :] = (o * sub_ref[...] * (1.0 - lambda_init)).astype(BF16)


def _attn_prompt(P, B, S, lam_p, subln, lambda_init):
    hd = 64
    blk = ATT_BLK
    bq = 2 * blk
    assert S % bq == 0
    nq = S // bq
    w = 2 * hd
    rows = 2 * bq
    kern = functools.partial(_attn_prompt_kernel, blk=blk, hd=hd, lambda_init=lambda_init)
    return pl.pallas_call(
        kern,
        grid=(B, D_HEADS, nq),
        in_specs=[
            pl.BlockSpec((bq, w), lambda b, h, i: (b * nq + i, P_DQ + h)),
            pl.BlockSpec((S, w), lambda b, h, i: (b, P_DK + h)),
            pl.BlockSpec((S, w), lambda b, h, i: (b, P_DV + h)),
            pl.BlockSpec((4, hd), lambda b, h, i: (0, 0)),
            pl.BlockSpec((1, w), lambda b, h, i: (0, 0)),
        ],
        out_specs=pl.BlockSpec((bq, w), lambda b, h, i: (b * nq + i, h)),
        out_shape=jax.ShapeDtypeStruct((B * S, D_HEADS * w), BF16),
        scratch_shapes=[
            pltpu.VMEM((rows, w), BF16),
            pltpu.VMEM((S, 2 * w), BF16),
            pltpu.VMEM((rows, LANES), F32),
            pltpu.VMEM((rows, 2 * w), F32),
            pltpu.VMEM((2, rows, blk), F32),
        ],
        compiler_params=_cparams(("parallel", "parallel", "arbitrary")),
        name="attn_prompt",
    )(P, P, P, lam_p, subln)


def _attn_sample_kernel(q_ref, kn_ref, vn_ref, kc_ref, vc_ref, lam_ref, sub_ref, y_ref,
                        *, tq, hd, lambda_init):
    lam = _diff_lambda(lam_ref, lambda_init)
    w = 2 * hd
    past = kc_ref.shape[1] // D_HEADS
    for h in range(D_HEADS):
        cols = slice(h * w, (h + 1) * w)
        qs = _stack_maps(q_ref[:, cols] * (hd ** -0.5), hd)
        kc = kc_ref[0, pl.ds(h, past, stride=D_HEADS), :].astype(BF16)
        vc = vc_ref[0, pl.ds(h, past, stride=D_HEADS), :].astype(BF16)
        kn = kn_ref[:, cols]
        vn = vn_ref[:, cols]
        s_c = _dot_nt(qs, kc)
        s_n = _dot_nt(qs, kn)
        m = jnp.maximum(jnp.max(s_c, axis=-1, keepdims=True), jnp.max(s_n, axis=-1, keepdims=True))
        p_c = jnp.exp(s_c - m)
        p_n = jnp.exp(s_n - m)
        l = jnp.sum(p_c, axis=-1, keepdims=True) + jnp.sum(p_n, axis=-1, keepdims=True)
        acc = _dot(p_c.astype(BF16), vc) + _dot(p_n.astype(BF16), vn)
        y_ref[:, cols] = _diff_finish(acc, l, lam, sub_ref[...], lambda_init, tq)


def _attn_sample(P, row_off, B, L, cache_k, cache_v, lam_p, subln, lambda_init):
    hd = 64
    past = cache_k.shape[1]
    ob = row_off // L
    width = D_HEADS * 2 * hd
    kern = functools.partial(_attn_sample_kernel, tq=L, hd=hd, lambda_init=lambda_init)
    cache_k = cache_k.reshape(B, past * D_HEADS, 2 * hd)
    cache_v = cache_v.reshape(B, past * D_HEADS, 2 * hd)
    cache_spec = pl.BlockSpec((1, past * D_HEADS, 2 * hd), lambda b: (b, 0, 0))
    return pl.pallas_call(
        kern,
        grid=(B,),
        in_specs=[
            pl.BlockSpec((L, width), lambda b: (ob + b, P_DQ * LANES // width)),
            pl.BlockSpec((L, width), lambda b: (ob + b, P_DK * LANES // width)),
            pl.BlockSpec((L, width), lambda b: (ob + b, P_DV * LANES // width)),
            cache_spec,
            cache_spec,
            pl.BlockSpec((4, hd), lambda b: (0, 0)),
            pl.BlockSpec((1, 2 * hd), lambda b: (0, 0)),
        ],
        out_specs=pl.BlockSpec((L, width), lambda b: (b, 0)),
        out_shape=jax.ShapeDtypeStruct((B * L, width), BF16),
        compiler_params=_cparams(("parallel",)),
        name="attn_sample",
    )(P, P, P, cache_k, cache_v, lam_p, subln)


def _mix_kernel(xp_ref, xs_ref, yrp_ref, yrs_ref, ydp_ref, yds_ref, gr_ref, gd_ref, lig_ref, lib_ref,
                bg_ref, wpr_ref, wpd_ref, wo_ref, g1_ref, b1_ref, h1_ref, *, alpha, n_p):
    i = pl.program_id(0)
    d = xp_ref.shape[1]

    def compute(x_ref, yr_ref, yd_ref):
        h0 = _layer_norm(x_ref[...], lig_ref[...], lib_ref[...])
        g_ret = jax.nn.sigmoid(gr_ref[...].astype(F32) + bg_ref[:, :d])
        g_diff = jax.nn.sigmoid(gd_ref[...].astype(F32) + bg_ref[:, d:])
        merged = g_ret * _dot(yr_ref[...], wpr_ref[...]) + g_diff * _dot(yd_ref[...], wpd_ref[...])
        mixed = _dot(merged.astype(BF16), wo_ref[...])
        h1_ref[...] = _layer_norm(alpha * h0 + mixed, g1_ref[...], b1_ref[...])

    @pl.when(i < n_p)
    def _():
        compute(xp_ref, yrp_ref, ydp_ref)

    @pl.when(i >= n_p)
    def _():
        compute(xs_ref, yrs_ref, yds_ref)


def _mix(xp, xs, yr_p, yr_s, yd_p, yd_s, P, ln_in_g, ln_in_b, b_gate, wpr, wpd, wo, g1, b1, alpha):
    Tp, D = xp.shape
    Ts = xs.shape[0]
    tm = MIX_TM
    n_p, n_s = Tp // tm, Ts // tm
    gcol = P_GATE * LANES // D
    const = lambda i: (0, 0)
    lo = lambda i: (_lo(i, n_p), 0)
    hi = lambda i: (_hi(i, n_p), 0)
    single = pl.Buffered(1)
    kern = functools.partial(_mix_kernel, alpha=alpha, n_p=n_p)
    return pl.pallas_call(
        kern,
        grid=(n_p + n_s,),
        in_specs=[
            pl.BlockSpec((tm, D), lo),
            pl.BlockSpec((tm, D), hi),
            pl.BlockSpec((tm, yr_p.shape[1]), lo),
            pl.BlockSpec((tm, yr_s.shape[1]), hi),
            pl.BlockSpec((tm, yd_p.shape[1]), lo),
            pl.BlockSpec((tm, yd_s.shape[1]), hi),
            pl.BlockSpec((tm, D), lambda i: (i, gcol)),
            pl.BlockSpec((tm, D), lambda i: (i, gcol + 1)),
            pl.BlockSpec((1, D), const),
            pl.BlockSpec((1, D), const),
            pl.BlockSpec((1, 2 * D), const),
            pl.BlockSpec(wpr.shape, const, pipeline_mode=single),
            pl.BlockSpec(wpd.shape, const, pipeline_mode=single),
            pl.BlockSpec(wo.shape, const, pipeline_mode=single),
            pl.BlockSpec((1, D), const),
            pl.BlockSpec((1, D), const),
        ],
        out_specs=pl.BlockSpec((tm, D), lambda i: (i, 0)),
        out_shape=jax.ShapeDtypeStruct((Tp + Ts, D), F32),
        compiler_params=_cparams(("arbitrary",)),
        name="mix",
    )(xp, xs, yr_p, yr_s, yd_p, yd_s, P, P, ln_in_g, ln_in_b, b_gate, wpr, wpd, wo, g1, b1)


def _mem_kernel(h1_ref, mkp_ref, mvp_ref, mks_ref, mvs_ref, wq_ref, wo_ref, g2_ref, b2_ref, rw_ref,
                rb_ref, h2_ref, route_ref, counts_ref, q_scr, o_scr, cnt_scr, *, n_p, alpha, hd):
    i = pl.program_id(0)
    tm = h1_ref.shape[0]
    h1 = h1_ref[...]
    q_scr[...] = (_dot(h1.astype(BF16), wq_ref[...]) * (hd ** -0.5)).astype(BF16)

    def attend(mk_ref, mv_ref):
        n_sub = mk_ref.shape[0]
        seg = tm // n_sub
        split_heads = mk_ref.shape[2] == hd
        for s in range(n_sub):
            for h in range(M_HEADS):
                if split_heads:
                    rows = pl.ds(h, mk_ref.shape[1] // M_HEADS, stride=M_HEADS)
                    mk = mk_ref[s, rows, :].astype(BF16)
                    mv = mv_ref[s, rows, :].astype(BF16)
                else:
                    mk = mk_ref[s, :, h * hd:(h + 1) * hd].astype(BF16)
                    mv = mv_ref[s, :, h * hd:(h + 1) * hd].astype(BF16)
                qh = q_scr[s * seg:(s + 1) * seg, h * hd:(h + 1) * hd]
                sc = _dot_nt(qh, mk)
                sc = sc - jnp.max(sc, axis=-1, keepdims=True)
                p = jnp.exp(sc)
                p = p / jnp.sum(p, axis=-1, keepdims=True)
                o_scr[s * seg:(s + 1) * seg, h * hd:(h + 1) * hd] = _dot(
                    p.astype(BF16), mv).astype(BF16)

    @pl.when(i < n_p)
    def _():
        attend(mkp_ref, mvp_ref)

    @pl.when(i >= n_p)
    def _():
        attend(mks_ref, mvs_ref)

    h2 = _layer_norm(alpha * h1 + _dot(o_scr[...], wo_ref[...]), g2_ref[...], b2_ref[...])
    h2_ref[...] = h2
    h2b = h2.astype(BF16)

    h_lo = (h2 - h2b.astype(F32)).astype(BF16)
    rw = rw_ref[...]
    rw_hi = rw.astype(BF16)
    rw_lo = (rw - rw_hi.astype(F32)).astype(BF16)
    logits = _dot(h2b, rw_hi) + _dot(h2b, rw_lo) + _dot(h_lo, rw_hi) + rb_ref[...]

    n_e = logits.shape[1]
    eidx = lax.broadcasted_iota(jnp.int32, logits.shape, 1).astype(F32)
    lane = lax.broadcasted_iota(jnp.int32, (tm, LANES), 1)
    route = jnp.zeros((tm, LANES), F32)
    work = logits
    vals = []
    sels = []
    for k in range(TOP_K):
        mx = jnp.max(work, axis=-1, keepdims=True)
        sel = jnp.min(jnp.where(work == mx, eidx, float(n_e)), axis=-1, keepdims=True)
        work = jnp.where(eidx == sel, -jnp.inf, work)
        vals.append(mx)
        sels.append(sel)
        route = jnp.where(lane == TOP_K + k, sel, route)
    ex = [jnp.exp(v - vals[0]) for v in vals]
    den = ex[0] + ex[1] + ex[2] + ex[3]
    for k in range(TOP_K):
        route = jnp.where(lane == k, ex[k] / den, route)

    @pl.when(i == 0)
    def _():
        cnt_scr[...] = jnp.zeros_like(cnt_scr)

    lane_f = lane.astype(F32)
    hit = [lane_f == sels[k] for k in range(TOP_K)]
    cnt = sum(h.astype(F32) for h in hit)
    r_io = lax.broadcasted_iota(jnp.int32, (tm, tm), 0)
    c_io = lax.broadcasted_iota(jnp.int32, (tm, tm), 1)
    ltri = jnp.where(r_io > c_io, 1.0, 0.0).astype(BF16)
    excl = _dot(ltri, cnt.astype(BF16)) + cnt_scr[...]
    for k in range(TOP_K):
        rank = jnp.sum(jnp.where(hit[k], excl, 0.0), axis=-1, keepdims=True)
        route = jnp.where(lane == 2 * TOP_K + k, rank, route)
    route_ref[...] = route
    cnt_scr[...] = cnt_scr[...] + jnp.sum(cnt, axis=0, keepdims=True)
    counts_ref[...] = cnt_scr[...]


def _mem_attn(h1, t_prompt, mem_kp, mem_vp, l_prompt, mem_ks, mem_vs, l_sample, wq, wo, g2, b2, rw, rb,
              alpha):
    T, D = h1.shape
    hd = 128
    tm = MEM_TM
    assert l_prompt % tm == 0 and tm % l_sample == 0
    n_p = t_prompt // tm
    n_s = (T - t_prompt) // tm
    per_b = l_prompt // tm
    sub_s = tm // l_sample
    n_mem = mem_kp.shape[1]
    bp = mem_kp.shape[0]
    const = lambda i: (0, 0)
    single = pl.Buffered(1)
    mem_p = pl.BlockSpec((1, n_mem, M_HEADS * hd), lambda i: (jnp.minimum(i // per_b, bp - 1), 0, 0))
    bs = mem_ks.shape[0]
    mem_ks = mem_ks.reshape(bs, n_mem * M_HEADS, hd)
    mem_vs = mem_vs.reshape(bs, n_mem * M_HEADS, hd)
    mem_s = pl.BlockSpec((sub_s, n_mem * M_HEADS, hd), lambda i: (_hi(i, n_p), 0, 0))
    kern = functools.partial(_mem_kernel, n_p=n_p, alpha=alpha, hd=hd)
    return pl.pallas_call(
        kern,
        grid=(n_p + n_s,),
        in_specs=[
            pl.BlockSpec((tm, D), lambda i: (i, 0)),
            mem_p, mem_p, mem_s, mem_s,
            pl.BlockSpec(wq.shape, const, pipeline_mode=single),
            pl.BlockSpec(wo.shape, const, pipeline_mode=single),
            pl.BlockSpec((1, D), const),
            pl.BlockSpec((1, D), const),
            pl.BlockSpec(rw.shape, const, pipeline_mode=single),
            pl.BlockSpec((1, rw.shape[1]), const),
        ],
        out_specs=[
            pl.BlockSpec((tm, D), lambda i: (i, 0)),
            pl.BlockSpec((tm, LANES), lambda i: (i, 0)),
            pl.BlockSpec((1, LANES), const),
        ],
        out_shape=[
            jax.ShapeDtypeStruct((T, D), F32),
            jax.ShapeDtypeStruct((T, LANES), F32),
            jax.ShapeDtypeStruct((1, LANES), F32),
        ],
        scratch_shapes=[
            pltpu.VMEM((tm, M_HEADS * hd), BF16),
            pltpu.VMEM((tm, M_HEADS * hd), BF16),
            pltpu.VMEM((1, LANES), F32),
        ],
        compiler_params=_cparams(("arbitrary",)),
        name="mem_attn",
    )(h1, mem_kp, mem_vp, mem_ks, mem_vs, wq, wo, g2, b2, rw, rb)


def _mem_kv_kernel(x_ref, w_ref, o_ref):
    o_ref[...] = _dot(x_ref[...].astype(BF16), w_ref[...])


def _mem_kv(mem, w):
    R, D = mem.shape
    N = w.shape[1]
    tm = 256
    return pl.pallas_call(
        _mem_kv_kernel,
        grid=(R // tm,),
        in_specs=[pl.BlockSpec((tm, D), lambda i: (i, 0)), pl.BlockSpec((D, N), lambda i: (0, 0))],
        out_specs=pl.BlockSpec((tm, N), lambda i: (i, 0)),
        out_shape=jax.ShapeDtypeStruct((R, N), F32),
        compiler_params=_cparams(("parallel",)),
        name="mem_kv",
    )(mem, w)


def _row_copy(src_hbm, dst, sem, src_row, dst_row):
    return pltpu.make_async_copy(src_hbm.at[pl.ds(src_row, 1)], dst.at[pl.ds(dst_row, 1)], sem)


def _rows_to_slabs(x, dst):
    n = x.shape[0]
    n_slab = x.shape[1] // LANES
    for s in range(n_slab):
        dst[pl.ds(s, n, stride=n_slab), :] = x[:, s * LANES:(s + 1) * LANES]


def _slabs_to_rows(src, n_slab):
    n = src.shape[0] // n_slab
    return jnp.concatenate([src[pl.ds(s, n, stride=n_slab), :] for s in range(n_slab)], axis=1)


def _dispatch_kernel(dest_ref, pend_ref, padded_ref, nu_ref, h_ref, xs_hbm, stage, zeros, sem, zsem,
                     *, tm, bm, n_steps, n_blocks):
    i = pl.program_id(0)
    slot = i % 2

    n_slab = h_ref.shape[1] // LANES

    def zero_block(off):
        off = pl.multiple_of(off * n_slab, bm * n_slab)
        return pltpu.make_async_copy(zeros, xs_hbm.at[pl.ds(off, bm * n_slab)], zsem)

    def zero_fill(act):
        for e in range(N_EXPERTS):
            @pl.when(padded_ref[e] > 0)
            def _(e=e):
                act(zero_block(pend_ref[e] - bm))
        for b in range(n_blocks - N_EXPERTS, n_blocks):
            @pl.when(b >= nu_ref[0])
            def _(b=b):
                act(zero_block(b * bm))

    @pl.when(i == 0)
    def _():
        zeros[...] = jnp.zeros_like(zeros)
        zero_fill(lambda c: c.start())
        zero_fill(lambda c: c.wait())

    def wait(s):
        for _ in range(TOP_K):
            pltpu.make_async_copy(stage.at[s], xs_hbm.at[pl.ds(0, tm * n_slab)], sem.at[s]).wait()

    @pl.when(i >= 2)
    def _():
        wait(slot)

    _rows_to_slabs(h_ref[...], stage.at[slot])

    def body(j, carry):
        t0 = j * SUBLANES
        for u in range(SUBLANES):
            src = stage.at[slot, pl.ds(pl.multiple_of((t0 + u) * n_slab, n_slab), n_slab)]
            for k in range(TOP_K):
                row = dest_ref[(i * tm + t0 + u) * TOP_K + k]
                dst = xs_hbm.at[pl.ds(pl.multiple_of(row * n_slab, n_slab), n_slab)]
                pltpu.make_async_copy(src, dst, sem.at[slot]).start(priority=k % 2)
        return carry
    lax.fori_loop(0, tm // SUBLANES, body, 0)

    @pl.when(i == n_steps - 1)
    def _():
        wait(slot)
        if n_steps > 1:
            wait(1 - slot)


def _moe_dispatch(h2, dest, pad_end, padded, n_used, n_rows, bm):
    T, D = h2.shape
    tm = GATHER_ROWS
    n_steps = T // tm
    n_slab = D // LANES
    kern = functools.partial(_dispatch_kernel, tm=tm, bm=bm, n_steps=n_steps, n_blocks=n_rows // bm)
    return pl.pallas_call(
        kern,
        grid_spec=pltpu.PrefetchScalarGridSpec(
            num_scalar_prefetch=4,
            grid=(n_steps,),
            in_specs=[pl.BlockSpec((tm, D), lambda i, d, pe, pd, nu: (i, 0))],
            out_specs=pl.BlockSpec(memory_space=pl.ANY),
            scratch_shapes=[
                pltpu.VMEM((2, tm * n_slab, LANES), F32),
                pltpu.VMEM((bm * n_slab, LANES), F32),
                pltpu.SemaphoreType.DMA((2,)),
                pltpu.SemaphoreType.DMA(()),
            ],
        ),
        out_shape=jax.ShapeDtypeStruct((n_rows * n_slab, LANES), F32),
        compiler_params=_cparams(("arbitrary",)),
        name="moe_dispatch",
    )(dest, pad_end, padded, n_used, h2)


def _cast_rows(src, dst):
    rows = CAST_ROWS

    def body(i, carry):
        r = pl.multiple_of(i * rows, rows)
        dst[pl.ds(r, rows), :] = src[pl.ds(r, rows), :].astype(BF16)
        return carry
    lax.fori_loop(0, src.shape[0] // rows, body, 0)


def _expert_weights(be_ref, first_ref, next_ref, copies, cast, m):
    @pl.when(first_ref[m] == 1)
    def _():
        @pl.when(m == 0)
        def _():
            for c in copies(be_ref[0]):
                c.start()

        for c in copies(be_ref[m]):
            c.wait()
        cast()
        ne = next_ref[m]

        @pl.when(ne >= 0)
        def _():
            for c in copies(ne):
                c.start(priority=1)


def _moe_up_kernel(be_ref, nu_ref, first_ref, next_ref, x_ref, w1_hbm, b_ref, a_ref,
                   stage, w_scr, sem):
    m = pl.program_id(0)
    ff = w_scr.shape[2]

    def copies(e):
        return [pltpu.make_async_copy(w1_hbm.at[e, :, pl.ds(t * ff, ff)], stage.at[t], sem)
                for t in range(2)]

    def cast():
        for t in range(2):
            _cast_rows(stage.at[t], w_scr.at[t])

    _expert_weights(be_ref, first_ref, next_ref, copies, cast, m)

    @pl.when(m < nu_ref[0])
    def _():
        x = _slabs_to_rows(x_ref, w_scr.shape[1] // LANES).astype(BF16)
        half = ff // 2
        for c in range(2):
            cols = slice(c * half, (c + 1) * half)
            lin_cols = slice(ff + c * half, ff + (c + 1) * half)
            u_glu = jnp.minimum(_dot(x, w_scr[0, :, cols]) + b_ref[0, :, cols], SWIGLU_LIMIT)
            u_lin = jnp.clip(_dot(x, w_scr[1, :, cols]) + b_ref[0, :, lin_cols],
                             -SWIGLU_LIMIT, SWIGLU_LIMIT)
            a = u_glu * jax.nn.sigmoid(SWIGLU_ALPHA * u_glu) * (u_lin + 1.0)
            a_ref[:, cols] = a.astype(BF16)

    @pl.when(m >= nu_ref[0])
    def _():
        a_ref[...] = jnp.zeros_like(a_ref)


def _moe_up(xs, w1, b1, plan):
    D = w1.shape[1]
    n_slab = D // LANES
    n_rows = xs.shape[0] // n_slab
    F = w1.shape[2] // 2
    bm = MOE_BM
    return pl.pallas_call(
        _moe_up_kernel,
        grid_spec=pltpu.PrefetchScalarGridSpec(
            num_scalar_prefetch=4,
            grid=(n_rows // bm,),
            in_specs=[
                pl.BlockSpec((bm * n_slab, LANES),
                             lambda m, be, nu, fi, nx: (jnp.minimum(m, nu[0] - 1), 0)),
                pl.BlockSpec(memory_space=pl.ANY),
                pl.BlockSpec((1, 1, 2 * F), lambda m, be, nu, fi, nx: (be[m], 0, 0)),
            ],
            out_specs=pl.BlockSpec((bm, F), lambda m, be, nu, fi, nx: (m, 0)),
            scratch_shapes=[
                pltpu.VMEM((2, D, F), F32),
                pltpu.VMEM((2, D, F), BF16),
                pltpu.SemaphoreType.DMA(()),
            ],
        ),
        out_shape=jax.ShapeDtypeStruct((n_rows, F), BF16),
        compiler_params=_cparams(("arbitrary",)),
        name="moe_up",
    )(*plan, xs, w1, b1)


def _moe_down_kernel(be_ref, nu_ref, first_ref, next_ref, a_ref, w2_hbm, b_ref, y_ref,
                     stage, w_scr, sem):
    m = pl.program_id(0)

    def copies(e):
        return [pltpu.make_async_copy(w2_hbm.at[e], stage, sem)]

    def cast():
        _cast_rows(stage, w_scr)

    _expert_weights(be_ref, first_ref, next_ref, copies, cast, m)

    @pl.when(m < nu_ref[0])
    def _():
        y_ref[...] = _dot(a_ref[...], w_scr[...]) + b_ref[0]

    @pl.when(m >= nu_ref[0])
    def _():
        y_ref[...] = jnp.zeros_like(y_ref)


def _moe_down(a, w2, b2, plan):
    n_rows, F = a.shape
    D = w2.shape[2]
    bm = MOE_BM
    return pl.pallas_call(
        _moe_down_kernel,
        grid_spec=pltpu.PrefetchScalarGridSpec(
            num_scalar_prefetch=4,
            grid=(n_rows // bm,),
            in_specs=[
                pl.BlockSpec((bm, F), lambda m, be, nu, fi, nx: (jnp.minimum(m, nu[0] - 1), 0)),
                pl.BlockSpec(memory_space=pl.ANY),
                pl.BlockSpec((1, 1, D), lambda m, be, nu, fi, nx: (be[m], 0, 0)),
            ],
            out_specs=pl.BlockSpec((bm, D), lambda m, be, nu, fi, nx: (m, 0)),
            scratch_shapes=[
                pltpu.VMEM((F, D), F32),
                pltpu.VMEM((F, D), BF16),
                pltpu.SemaphoreType.DMA(()),
            ],
        ),
        out_shape=jax.ShapeDtypeStruct((n_rows, D), F32),
        compiler_params=_cparams(("arbitrary",)),
        name="moe_down",
    )(*plan, a, w2, b2)


def _combine_kernel(dest_ref, h2_ref, rt_ref, g_ref, b_ref, ys_hbm, op_ref, os_ref, buf, sem,
                    *, tm, n_steps, n_prompt, alpha):
    i = pl.program_id(0)

    def issue(step, slot):
        def body(j, carry):
            t0 = pl.multiple_of(j * SUBLANES, SUBLANES)
            for u in range(SUBLANES):
                for k in range(TOP_K):
                    row = dest_ref[(step * tm + t0 + u) * TOP_K + k]
                    _row_copy(ys_hbm, buf.at[slot, k], sem.at[slot], row, t0 + u).start(priority=k % 2)
            return carry
        lax.fori_loop(0, tm // SUBLANES, body, 0)

    def wait(slot):
        for k in range(TOP_K):
            pltpu.make_async_copy(ys_hbm.at[pl.ds(0, tm)], buf.at[slot, k], sem.at[slot]).wait()

    slot = i % 2

    @pl.when(i == 0)
    def _():
        issue(0, 0)

    @pl.when(i + 1 < n_steps)
    def _():
        issue(i + 1, 1 - slot)

    wait(slot)
    y = rt_ref[:, 0:1] * buf[slot, 0]
    for k in range(1, TOP_K):
        y = y + rt_ref[:, k:k + 1] * buf[slot, k]
    out = _layer_norm(alpha * h2_ref[...] + y, g_ref[...], b_ref[...])

    @pl.when(i < n_prompt)
    def _():
        op_ref[...] = out

    @pl.when(i >= n_prompt)
    def _():
        os_ref[...] = out


def _moe_combine(ys, dest, h2, route, g3, b3, t_prompt, alpha):
    T, D = h2.shape
    tm = COMBINE_TM
    n_steps = T // tm
    n_prompt = t_prompt // tm
    kern = functools.partial(_combine_kernel, tm=tm, n_steps=n_steps, n_prompt=n_prompt, alpha=alpha)
    return pl.pallas_call(
        kern,
        grid_spec=pltpu.PrefetchScalarGridSpec(
            num_scalar_prefetch=1,
            grid=(n_steps,),
            in_specs=[
                pl.BlockSpec((tm, D), lambda i, d: (i, 0)),
                pl.BlockSpec((tm, LANES), lambda i, d: (i, 0)),
                pl.BlockSpec((1, D), lambda i, d: (0, 0)),
                pl.BlockSpec((1, D), lambda i, d: (0, 0)),
                pl.BlockSpec(memory_space=pl.ANY),
            ],
            out_specs=[
                pl.BlockSpec((tm, D), lambda i, d: (jnp.minimum(i, n_prompt - 1), 0)),
                pl.BlockSpec((tm, D), lambda i, d: (jnp.maximum(i - n_prompt, 0), 0)),
            ],
            scratch_shapes=[pltpu.VMEM((2, TOP_K, tm, D), F32), pltpu.SemaphoreType.DMA((2,))],
        ),
        out_shape=[
            jax.ShapeDtypeStruct((t_prompt, D), F32),
            jax.ShapeDtypeStruct((T - t_prompt, D), F32),
        ],
        compiler_params=_cparams(("arbitrary",)),
        name="moe_combine",
    )(dest, h2, route, g3, b3, ys)


def _moe_plan(route, counts, bm):
    T = route.shape[0]
    top_idx = route[:, TOP_K:2 * TOP_K].astype(jnp.int32)
    rank = route[:, 2 * TOP_K:3 * TOP_K].astype(jnp.int32)
    counts = counts[0, :N_EXPERTS].astype(jnp.int32)
    padded = (counts + bm - 1) // bm * bm
    pad_end = jnp.cumsum(padded)
    pad_start = pad_end - padded
    experts = jnp.arange(N_EXPERTS, dtype=jnp.int32)
    start_of = jnp.sum(jnp.where(top_idx[..., None] == experts, pad_start, 0), axis=-1)
    dest = (start_of + rank).reshape(-1)
    n_blocks = -(-(T * TOP_K) // bm) + N_EXPERTS
    block_start = jnp.arange(n_blocks, dtype=jnp.int32) * bm
    block_e = jnp.sum((pad_end[None, :] <= block_start[:, None]).astype(jnp.int32), axis=1)
    block_e = jnp.minimum(block_e, N_EXPERTS - 1)
    n_used = (pad_end[-1] // bm).astype(jnp.int32).reshape(1)
    blk = jnp.arange(n_blocks, dtype=jnp.int32)
    prev_e = jnp.concatenate([jnp.full((1,), -1, jnp.int32), block_e[:-1]])
    first = jnp.logical_and(blk < n_used[0], block_e != prev_e)
    later = jnp.logical_and(first[None, :], blk[None, :] > blk[:, None])
    nxt_blk = jnp.min(jnp.where(later, blk[None, :], n_blocks), axis=1)
    next_e = jnp.where(nxt_blk < n_blocks, block_e[jnp.minimum(nxt_blk, n_blocks - 1)], -1)
    plan = (block_e, n_used, first.astype(jnp.int32), next_e.astype(jnp.int32))
    return dest, pad_end, padded, plan, n_blocks * bm


def kernel(x_prompt, x_sample, cache_diff_k, cache_diff_v, state_ret, cache_mem_k, cache_mem_v, mem_prompt, ln_in_g, ln_in_b, w_in, b_gate, diff_lambda, diff_subln, w_proj_ret, w_proj_diff, w_out, ln1_g, ln1_b, w_mq, w_mk, w_mv, w_mo, ln2_g, ln2_b, router_w, router_b, w1, b1, w2, b2, ln3_g, ln3_b):
    Bp, Lp, D = x_prompt.shape
    Bs, Ls, _ = x_sample.shape
    depth = w_in.shape[0]
    assert depth == 1
    past = cache_diff_k.shape[2]
    n_mem = mem_prompt.shape[1]
    Tp, Ts = Bp * Lp, Bs * Ls
    alpha = (2.0 * depth) ** 0.25
    lambda_init = 0.8 - 0.6 * math.exp(-0.3 * 0)
    row = lambda v: v.reshape(1, -1)

    xp = x_prompt.reshape(Tp, D)
    xs_in = x_sample.reshape(Ts, D)
    kv_blk = 4096 // IN_TN
    P, dk_p, dv_p, dk_s, dv_s = _in_proj(xp, xs_in, row(ln_in_g), row(ln_in_b), w_in[0].astype(BF16),
                                         kv_blk)

    zeros_state = jnp.zeros((Bp,) + state_ret.shape[2:], F32)
    yr_p, s_p = _retention(P, 0, Bp, Lp, 0, zeros_state)
    yr_s, s_s = _retention(P, Tp, Bs, Ls, past, state_ret[0])

    yd_p = _attn_prompt(P, Bp, Lp, diff_lambda[0], row(diff_subln[0]), lambda_init)
    yd_s = _attn_sample(P, Tp, Bs, Ls, cache_diff_k[0], cache_diff_v[0], diff_lambda[0],
                        row(diff_subln[0]), lambda_init)

    h1 = _mix(xp, xs_in, yr_p, yr_s, yd_p, yd_s, P, row(ln_in_g), row(ln_in_b), row(b_gate[0]),
              w_proj_ret[0].astype(BF16), w_proj_diff[0].astype(BF16), w_out[0].astype(BF16),
              row(ln1_g[0]), row(ln1_b[0]), alpha)

    w_mkv = jnp.concatenate([w_mk[0], w_mv[0]], axis=1).astype(BF16)
    mkv = _mem_kv(mem_prompt.reshape(Bp * n_mem, D), w_mkv)
    hm = w_mk.shape[2]
    mk_p = mkv[:, :hm].reshape(Bp, n_mem, hm)
    mv_p = mkv[:, hm:].reshape(Bp, n_mem, hm)
    h2, route, counts = _mem_attn(
        h1, Tp, mk_p, mv_p, Lp, cache_mem_k[0], cache_mem_v[0], Ls, w_mq[0].astype(BF16),
        w_mo[0].astype(BF16),
        row(ln2_g[0]), row(ln2_b[0]), router_w[0], row(router_b[0]), alpha)

    dest, pad_end, padded, plan, n_rows = _moe_plan(route, counts, MOE_BM)
    xs = _moe_dispatch(h2, dest, pad_end, padded, plan[1], n_rows, MOE_BM)
    act = _moe_up(xs, w1[0], b1[0][:, None, :], plan)
    ys = _moe_down(act, w2[0], b2[0][:, None, :], plan)
    out_p, out_s = _moe_combine(ys, dest, h2, route, row(ln3_g[0]), row(ln3_b[0]), Tp, alpha)

    return (
        out_p.reshape(Bp, Lp, D),
        out_s.reshape(Bs, Ls, D),
        dk_p.reshape(1, Bp, Lp, D_HEADS, 128),
        dv_p.reshape(1, Bp, Lp, D_HEADS, 128),
        s_p[None],
        mk_p.reshape(1, Bp, n_mem, M_HEADS, hm // M_HEADS),
        mv_p.reshape(1, Bp, n_mem, M_HEADS, hm // M_HEADS),
        dk_s.reshape(1, Bs, Ls, D_HEADS, 128),
        dv_s.reshape(1, Bs, Ls, D_HEADS, 128),
        s_s[None],
    )
```

```python
import functools
import math

import jax
import jax.numpy as jnp
from jax import lax
from jax.experimental import pallas as pl
from jax.experimental.pallas import tpu as pltpu

F32 = jnp.float32
BF16 = jnp.bfloat16

CHUNK = 64
R_HEADS = 4
D_HEADS = 8
M_HEADS = 4
N_EXPERTS = 32
TOP_K = 4
SWIGLU_ALPHA = 1.702
SWIGLU_LIMIT = 7.0
LN_EPS = 1e-5
RMS_EPS = 1e-5
ROPE_BASE = 10000.0
LOG2E = 1.4426950408889634

LANES = 128
SUBLANES = 8
VMEM_LIMIT = 58 * 1024 * 1024

IN_TM = 512
IN_TN = 2048
RET_CHUNK = 256
ATT_BLK = 512
ATT_UNROLL = 4
MIX_TM = 256
MEM_TM = 512
MOE_BM = 256
CAST_ROWS = 256
STAGE_PIECES = 4
GATHER_ROWS = 512
COMBINE_TM = 256


def _cparams(sem):
    return pltpu.CompilerParams(dimension_semantics=sem, vmem_limit_bytes=VMEM_LIMIT)


def _layer_norm(x, g, b):
    mu = jnp.mean(x, axis=-1, keepdims=True)
    xc = x - mu
    var = jnp.mean(xc * xc, axis=-1, keepdims=True)
    return xc * lax.rsqrt(var + LN_EPS) * g + b


def _dot(a, b):
    return jnp.dot(a, b, preferred_element_type=F32)


def _dot_nt(a, b):
    return lax.dot_general(a, b, (((1,), (1,)), ((), ())), preferred_element_type=F32)


def _dot_tn(a, b):
    return lax.dot_general(a, b, (((0,), (0,)), ((), ())), preferred_element_type=F32)


def _lo(i, n):
    return jnp.minimum(i, n - 1)


def _hi(i, n):
    return jnp.maximum(i - n, 0)


def _in_proj_kernel(xp_ref, xs_ref, g_ref, b_ref, w_ref, p_ref, dkp_ref, dvp_ref, dks_ref, dvs_ref,
                    h_scr, *, n_p):
    i = pl.program_id(0)
    j = pl.program_id(1)
    half = dkp_ref.shape[1]

    @pl.when(jnp.logical_and(j == 0, i < n_p))
    def _():
        h_scr[...] = _layer_norm(xp_ref[...], g_ref[...], b_ref[...]).astype(BF16)

    @pl.when(jnp.logical_and(j == 0, i >= n_p))
    def _():
        h_scr[...] = _layer_norm(xs_ref[...], g_ref[...], b_ref[...]).astype(BF16)

    acc = _dot(h_scr[...], w_ref[...])
    p_ref[...] = acc.astype(BF16)

    @pl.when(jnp.logical_and(j == 0, i < n_p))
    def _():
        dkp_ref[...] = acc[:, :half]
        dvp_ref[...] = acc[:, half:]

    @pl.when(jnp.logical_and(j == 0, i >= n_p))
    def _():
        dks_ref[...] = acc[:, :half]
        dvs_ref[...] = acc[:, half:]


def _in_proj(xp, xs, g, b, w, kv_blk):
    Tp, D = xp.shape
    Ts = xs.shape[0]
    N = w.shape[1]
    n_p, n_s = Tp // IN_TM, Ts // IN_TM
    half = IN_TN // 2

    def col(j):
        return jnp.where(j == 0, kv_blk, jnp.where(j <= kv_blk, j - 1, j))

    single = pl.Buffered(1)
    kv_p = pl.BlockSpec((IN_TM, half), lambda i, j: (_lo(i, n_p), 0))
    kv_s = pl.BlockSpec((IN_TM, half), lambda i, j: (_hi(i, n_p), 0), pipeline_mode=single)
    return pl.pallas_call(
        functools.partial(_in_proj_kernel, n_p=n_p),
        grid=(n_p + n_s, N // IN_TN),
        in_specs=[
            pl.BlockSpec((IN_TM, D), lambda i, j: (_lo(i, n_p), 0)),
            pl.BlockSpec((IN_TM, D), lambda i, j: (_hi(i, n_p), 0), pipeline_mode=single),
            pl.BlockSpec((1, D), lambda i, j: (0, 0)),
            pl.BlockSpec((1, D), lambda i, j: (0, 0)),
            pl.BlockSpec((D, IN_TN), lambda i, j: (0, col(j))),
        ],
        out_specs=[pl.BlockSpec((IN_TM, IN_TN), lambda i, j: (i, j)), kv_p, kv_p, kv_s, kv_s],
        out_shape=[
            jax.ShapeDtypeStruct((Tp + Ts, N), BF16),
            jax.ShapeDtypeStruct((Tp, half), F32),
            jax.ShapeDtypeStruct((Tp, half), F32),
            jax.ShapeDtypeStruct((Ts, half), F32),
            jax.ShapeDtypeStruct((Ts, half), F32),
        ],
        scratch_shapes=[pltpu.VMEM((IN_TM, D), BF16)],
        compiler_params=_cparams(("arbitrary", "arbitrary")),
        name="in_proj",
    )(xp, xs, g, b, w)


P_DK = 0
P_DV = 8
P_RQ = 16
P_RK = 20
P_RV = 24
P_RG = 32
P_DQ = 40
P_GATE = 48


def _retention_kernel(q_ref, k_ref, v_ref, g_ref, cos_ref, sin_ref, dm_ref, qd_ref, kd_ref,
                      bd_ref, s0_ref, y_ref, sout_ref, s_scr, *, n_chunks, dk, dv):
    c = pl.program_id(1)

    @pl.when(c == 0)
    def _():
        s_scr[...] = s0_ref[0]

    cos = cos_ref[...]
    sin = sin_ref[...]
    k_scale = dk ** -0.5
    for h in range(R_HEADS):
        q = q_ref[:, h * dk:(h + 1) * dk].astype(F32)
        k = k_ref[:, h * dk:(h + 1) * dk].astype(F32)
        v = v_ref[:, h * dv:(h + 1) * dv]
        g = g_ref[:, h * dv:(h + 1) * dv].astype(F32)
        q = q * cos + pltpu.roll(q, dk // 2, 1) * sin
        k = (k * cos + pltpu.roll(k, dk // 2, 1) * sin) * k_scale
        qb = q.astype(BF16)
        kb = k.astype(BF16)
        s = s_scr[h]
        scores = _dot_nt(qb, kb) * dm_ref[h]
        o = _dot(scores.astype(BF16), v) + _dot((q * qd_ref[h]).astype(BF16), s.astype(BF16))
        s_scr[h] = bd_ref[h] * s + _dot_tn((k * kd_ref[h]).astype(BF16), v)
        o = o * lax.rsqrt(jnp.mean(o * o, axis=-1, keepdims=True) + RMS_EPS)
        y_ref[:, h * dv:(h + 1) * dv] = (g * jax.nn.sigmoid(g) * o).astype(BF16)

    @pl.when(c == n_chunks - 1)
    def _():
        sout_ref[0] = s_scr[...]


def _retention(P, row_off, B, L, pos0, S0):
    dk, dv = 128, 256
    C = min(L, RET_CHUNK)
    nc = L // C
    ob = row_off // C
    pos = (pos0 + jnp.arange(L, dtype=jnp.int32)).astype(F32)
    inv = 1.0 / (ROPE_BASE ** jnp.linspace(0.0, 1.0, dk // 2, dtype=F32))
    ang = pos[:, None] * inv[None, :]
    cos = jnp.concatenate([jnp.cos(ang), jnp.cos(ang)], axis=-1)
    sin = jnp.concatenate([-jnp.sin(ang), jnp.sin(ang)], axis=-1)
    log_g = jnp.log1p(-jnp.power(2.0, -5.0 - jnp.arange(R_HEADS, dtype=F32)))
    i = jnp.arange(C, dtype=F32)
    rel = i[:, None] - i[None, :]
    dmask = jnp.where(rel >= 0, jnp.exp(log_g[:, None, None] * jnp.maximum(rel, 0.0)), 0.0)
    q_decay = jnp.exp(log_g[:, None] * (i + 1.0))[..., None]
    k_decay = jnp.exp(log_g[:, None] * (C - 1.0 - i))[..., None]
    b_decay = jnp.exp(log_g * C)[:, None, None]

    def rows(b, c):
        return ob + b * nc + c

    kern = functools.partial(_retention_kernel, n_chunks=nc, dk=dk, dv=dv)
    return pl.pallas_call(
        kern,
        grid=(B, nc),
        in_specs=[
            pl.BlockSpec((C, R_HEADS * dk), lambda b, c: (rows(b, c), P_RQ * LANES // (R_HEADS * dk))),
            pl.BlockSpec((C, R_HEADS * dk), lambda b, c: (rows(b, c), P_RK * LANES // (R_HEADS * dk))),
            pl.BlockSpec((C, R_HEADS * dv), lambda b, c: (rows(b, c), P_RV * LANES // (R_HEADS * dv))),
            pl.BlockSpec((C, R_HEADS * dv), lambda b, c: (rows(b, c), P_RG * LANES // (R_HEADS * dv))),
            pl.BlockSpec((C, dk), lambda b, c: (c, 0)),
            pl.BlockSpec((C, dk), lambda b, c: (c, 0)),
            pl.BlockSpec((R_HEADS, C, C), lambda b, c: (0, 0, 0)),
            pl.BlockSpec((R_HEADS, C, 1), lambda b, c: (0, 0, 0)),
            pl.BlockSpec((R_HEADS, C, 1), lambda b, c: (0, 0, 0)),
            pl.BlockSpec((R_HEADS, 1, 1), lambda b, c: (0, 0, 0)),
            pl.BlockSpec((1, R_HEADS, dk, dv), lambda b, c: (b, 0, 0, 0)),
        ],
        out_specs=[
            pl.BlockSpec((C, R_HEADS * dv), lambda b, c: (b * nc + c, 0)),
            pl.BlockSpec((1, R_HEADS, dk, dv), lambda b, c: (b, 0, 0, 0)),
        ],
        out_shape=[
            jax.ShapeDtypeStruct((B * L, R_HEADS * dv), BF16),
            jax.ShapeDtypeStruct((B, R_HEADS, dk, dv), F32),
        ],
        scratch_shapes=[pltpu.VMEM((R_HEADS, dk, dv), F32)],
        compiler_params=_cparams(("parallel", "arbitrary")),
        name="retention",
    )(P, P, P, P, cos, sin, dmask, q_decay, k_decay, b_decay, S0)


def _diff_lambda(lam_ref, lambda_init):
    lv = lam_ref[...]
    a = jnp.sum(lv[0:1] * lv[1:2], axis=-1, keepdims=True)
    b = jnp.sum(lv[2:3] * lv[3:4], axis=-1, keepdims=True)
    return jnp.exp(a) - jnp.exp(b) + lambda_init


def _lane_tile(x, n):
    return jnp.concatenate([x] * n, axis=1)


def _stack_maps(q, hd):
    lane = lax.broadcasted_iota(jnp.int32, q.shape, 1)
    zero = jnp.zeros_like(q)
    return jnp.concatenate([jnp.where(lane < hd, q, zero), jnp.where(lane < hd, zero, q)], axis=0)


def _diff_finish(acc, l, lam, subln, lambda_init, tq):
    o = acc[:tq] / l[:tq] - lam * (acc[tq:] / l[tq:])
    o = o * lax.rsqrt(jnp.mean(o * o, axis=-1, keepdims=True) + RMS_EPS)
    return (o * subln * (1.0 - lambda_init)).astype(BF16)


def _attn_prompt_kernel(q_ref, k_ref, v_ref, lam_ref, sub_ref, y_ref, qs_scr, ve_scr, m_scr, acc_scr,
                        s_scr, *, blk, hd, lambda_init):
    qi = pl.program_id(2)
    dv = 2 * hd
    seq = v_ref.shape[0]
    sub = 2 * blk
    lo, hi, every = slice(0, sub), slice(sub, 2 * sub), slice(0, 2 * sub)

    @pl.when(qi == 0)
    def _():
        ve_scr[:, :dv] = v_ref[...]
        ve_scr[:, dv:] = jnp.ones((seq, dv), BF16)

    for s_idx, rows in enumerate((lo, hi)):
        q = q_ref[s_idx * blk:(s_idx + 1) * blk, :].astype(F32) * (hd ** -0.5 * LOG2E)
        qs_scr[rows] = _stack_maps(q.astype(BF16), hd)
    m_scr[...] = jnp.full_like(m_scr, -jnp.inf)
    acc_scr[...] = jnp.zeros_like(acc_scr)

    def scores(ki, slot, rows=every):
        off = pl.multiple_of(ki * blk, blk)
        s_scr[slot, rows] = _dot_nt(qs_scr[rows], k_ref[pl.ds(off, blk), :])

    def step(ki, slot, rows=every, diag=False):
        off = pl.multiple_of(ki * blk, blk)
        s = s_scr[slot, rows]
        if diag:
            row = lax.broadcasted_iota(jnp.int32, s.shape, 0)
            col = lax.broadcasted_iota(jnp.int32, s.shape, 1)
            shift = CHUNK.bit_length() - 1
            seen = (col >> shift) <= ((row & (blk - 1)) >> shift)
            if s.shape[0] > sub:
                seen = jnp.logical_or(row >= sub, seen)
            s = jnp.where(seen, s, -1e30)
        m_prev = m_scr[rows]
        m_new = jnp.maximum(m_prev, jnp.max(s, axis=-1, keepdims=True))
        alpha = jnp.exp2(m_prev - m_new)
        p = jnp.exp2(s - _lane_tile(m_new, blk // LANES))
        pv = _dot(p.astype(BF16), ve_scr[pl.ds(off, blk), :])
        acc_scr[rows] = _lane_tile(alpha, 2 * dv // LANES) * acc_scr[rows] + pv
        m_scr[rows] = m_new

    def run(base, width):
        for u in range(width):
            scores(base + u + 1, (u + 1) % 2)
            step(base + u, u % 2)

    def body(j, carry):
        run(ATT_UNROLL * j, ATT_UNROLL)
        return carry

    n_open = 2 * qi
    scores(0, 0)
    n_main = n_open // ATT_UNROLL
    lax.fori_loop(0, n_main, body, 0)
    done = n_main * ATT_UNROLL
    width = ATT_UNROLL // 2
    while width >= 2:
        @pl.when(((n_open - done) // width) % 2 == 1)
        def _(width=width):
            run(n_open - (n_open - done) % (2 * width), width)
        width //= 2
    scores(n_open + 1, 1, hi)
    step(n_open, 0, every, diag=True)
    step(n_open + 1, 1, hi, diag=True)

    lam = _diff_lambda(lam_ref, lambda_init)
    for s_idx in range(2):
        a1 = acc_scr[s_idx * sub:s_idx * sub + blk, :]
        a2 = acc_scr[s_idx * sub + blk:(s_idx + 1) * sub, :]
        o = a1[:, :dv] / a1[:, dv:] - lam * (a2[:, :dv] / a2[:, dv:])
        o = o * lax.rsqrt(jnp.mean(o * o, axis=-1, keepdims=True) + RMS_EPS)
        y_ref[s_idx * blk:(s_idx + 1) * blk, :] = (o * sub_ref[...] * (1.0 - lambda_init)).astype(BF16)


def _attn_prompt(P, B, S, lam_p, subln, lambda_init):
    hd = 64
    blk = ATT_BLK
    bq = 2 * blk
    assert S % bq == 0
    nq = S // bq
    w = 2 * hd
    rows = 2 * bq
    kern = functools.partial(_attn_prompt_kernel, blk=blk, hd=hd, lambda_init=lambda_init)
    return pl.pallas_call(
        kern,
        grid=(B, D_HEADS, nq),
        in_specs=[
            pl.BlockSpec((bq, w), lambda b, h, i: (b * nq + i, P_DQ + h)),
            pl.BlockSpec((S, w), lambda b, h, i: (b, P_DK + h)),
            pl.BlockSpec((S, w), lambda b, h, i: (b, P_DV + h)),
            pl.BlockSpec((4, hd), lambda b, h, i: (0, 0)),
            pl.BlockSpec((1, w), lambda b, h, i: (0, 0)),
        ],
        out_specs=pl.BlockSpec((bq, w), lambda b, h, i: (b * nq + i, h)),
        out_shape=jax.ShapeDtypeStruct((B * S, D_HEADS * w), BF16),
        scratch_shapes=[
            pltpu.VMEM((rows, w), BF16),
            pltpu.VMEM((S, 2 * w), BF16),
            pltpu.VMEM((rows, LANES), F32),
            pltpu.VMEM((rows, 2 * w), F32),
            pltpu.VMEM((2, rows, blk), F32),
        ],
        compiler_params=_cparams(("parallel", "parallel", "arbitrary")),
        name="attn_prompt",
    )(P, P, P, lam_p, subln)


def _attn_sample_kernel(q_ref, kn_ref, vn_ref, kc_ref, vc_ref, lam_ref, sub_ref, y_ref,
                        *, tq, hd, lambda_init):
    lam = _diff_lambda(lam_ref, lambda_init)
    w = 2 * hd
    past = kc_ref.shape[1] // D_HEADS
    for h in range(D_HEADS):
        cols = slice(h * w, (h + 1) * w)
        qs = _stack_maps(q_ref[:, cols] * (hd ** -0.5), hd)
        kc = kc_ref[0, pl.ds(h, past, stride=D_HEADS), :].astype(BF16)
        vc = vc_ref[0, pl.ds(h, past, stride=D_HEADS), :].astype(BF16)
        kn = kn_ref[:, cols]
        vn = vn_ref[:, cols]
        s_c = _dot_nt(qs, kc)
        s_n = _dot_nt(qs, kn)
        m = jnp.maximum(jnp.max(s_c, axis=-1, keepdims=True), jnp.max(s_n, axis=-1, keepdims=True))
        p_c = jnp.exp(s_c - m)
        p_n = jnp.exp(s_n - m)
        l = jnp.sum(p_c, axis=-1, keepdims=True) + jnp.sum(p_n, axis=-1, keepdims=True)
        acc = _dot(p_c.astype(BF16), vc) + _dot(p_n.astype(BF16), vn)
        y_ref[:, cols] = _diff_finish(acc, l, lam, sub_ref[...], lambda_init, tq)


def _attn_sample(P, row_off, B, L, cache_k, cache_v, lam_p, subln, lambda_init):
    hd = 64
    past = cache_k.shape[1]
    ob = row_off // L
    width = D_HEADS * 2 * hd
    kern = functools.partial(_attn_sample_kernel, tq=L, hd=hd, lambda_init=lambda_init)
    cache_k = cache_k.reshape(B, past * D_HEADS, 2 * hd)
    cache_v = cache_v.reshape(B, past * D_HEADS, 2 * hd)
    cache_spec = pl.BlockSpec((1, past * D_HEADS, 2 * hd), lambda b: (b, 0, 0))
    return pl.pallas_call(
        kern,
        grid=(B,),
        in_specs=[
            pl.BlockSpec((L, width), lambda b: (ob + b, P_DQ * LANES // width)),
            pl.BlockSpec((L, width), lambda b: (ob + b, P_DK * LANES // width)),
            pl.BlockSpec((L, width), lambda b: (ob + b, P_DV * LANES // width)),
            cache_spec,
            cache_spec,
            pl.BlockSpec((4, hd), lambda b: (0, 0)),
            pl.BlockSpec((1, 2 * hd), lambda b: (0, 0)),
        ],
        out_specs=pl.BlockSpec((L, width), lambda b: (b, 0)),
        out_shape=jax.ShapeDtypeStruct((B * L, width), BF16),
        compiler_params=_cparams(("parallel",)),
        name="attn_sample",
    )(P, P, P, cache_k, cache_v, lam_p, subln)


def _mix_kernel(xp_ref, xs_ref, yrp_ref, yrs_ref, ydp_ref, yds_ref, gr_ref, gd_ref, lig_ref, lib_ref,
                bg_ref, wpr_ref, wpd_ref, wo_ref, g1_ref, b1_ref, h1_ref, *, alpha, n_p):
    i = pl.program_id(0)
    d = xp_ref.shape[1]

    def compute(x_ref, yr_ref, yd_ref):
        h0 = _layer_norm(x_ref[...], lig_ref[...], lib_ref[...])
        g_ret = jax.nn.sigmoid(gr_ref[...].astype(F32) + bg_ref[:, :d])
        g_diff = jax.nn.sigmoid(gd_ref[...].astype(F32) + bg_ref[:, d:])
        merged = g_ret * _dot(yr_ref[...], wpr_ref[...]) + g_diff * _dot(yd_ref[...], wpd_ref[...])
        mixed = _dot(merged.astype(BF16), wo_ref[...])
        h1_ref[...] = _layer_norm(alpha * h0 + mixed, g1_ref[...], b1_ref[...])

    @pl.when(i < n_p)
    def _():
        compute(xp_ref, yrp_ref, ydp_ref)

    @pl.when(i >= n_p)
    def _():
        compute(xs_ref, yrs_ref, yds_ref)


def _mix(xp, xs, yr_p, yr_s, yd_p, yd_s, P, ln_in_g, ln_in_b, b_gate, wpr, wpd, wo, g1, b1, alpha):
    Tp, D = xp.shape
    Ts = xs.shape[0]
    tm = MIX_TM
    n_p, n_s = Tp // tm, Ts // tm
    gcol = P_GATE * LANES // D
    const = lambda i: (0, 0)
    lo = lambda i: (_lo(i, n_p), 0)
    hi = lambda i: (_hi(i, n_p), 0)
    single = pl.Buffered(1)
    kern = functools.partial(_mix_kernel, alpha=alpha, n_p=n_p)
    return pl.pallas_call(
        kern,
        grid=(n_p + n_s,),
        in_specs=[
            pl.BlockSpec((tm, D), lo),
            pl.BlockSpec((tm, D), hi),
            pl.BlockSpec((tm, yr_p.shape[1]), lo),
            pl.BlockSpec((tm, yr_s.shape[1]), hi),
            pl.BlockSpec((tm, yd_p.shape[1]), lo),
            pl.BlockSpec((tm, yd_s.shape[1]), hi),
            pl.BlockSpec((tm, D), lambda i: (i, gcol)),
            pl.BlockSpec((tm, D), lambda i: (i, gcol + 1)),
            pl.BlockSpec((1, D), const),
            pl.BlockSpec((1, D), const),
            pl.BlockSpec((1, 2 * D), const),
            pl.BlockSpec(wpr.shape, const, pipeline_mode=single),
            pl.BlockSpec(wpd.shape, const, pipeline_mode=single),
            pl.BlockSpec(wo.shape, const, pipeline_mode=single),
            pl.BlockSpec((1, D), const),
            pl.BlockSpec((1, D), const),
        ],
        out_specs=pl.BlockSpec((tm, D), lambda i: (i, 0)),
        out_shape=jax.ShapeDtypeStruct((Tp + Ts, D), F32),
        compiler_params=_cparams(("arbitrary",)),
        name="mix",
    )(xp, xs, yr_p, yr_s, yd_p, yd_s, P, P, ln_in_g, ln_in_b, b_gate, wpr, wpd, wo, g1, b1)


def _mem_kernel(h1_ref, mkp_ref, mvp_ref, mks_ref, mvs_ref, wq_ref, wo_ref, g2_ref, b2_ref, rw_ref,
                rb_ref, h2_ref, route_ref, counts_ref, q_scr, o_scr, cnt_scr, *, n_p, alpha, hd):
    i = pl.program_id(0)
    tm = h1_ref.shape[0]
    h1 = h1_ref[...]
    q_scr[...] = (_dot(h1.astype(BF16), wq_ref[...]) * (hd ** -0.5)).astype(BF16)

    def attend(mk_ref, mv_ref):
        n_sub = mk_ref.shape[0]
        seg = tm // n_sub
        split_heads = mk_ref.shape[2] == hd
        for s in range(n_sub):
            for h in range(M_HEADS):
                if split_heads:
                    rows = pl.ds(h, mk_ref.shape[1] // M_HEADS, stride=M_HEADS)
                    mk = mk_ref[s, rows, :].astype(BF16)
                    mv = mv_ref[s, rows, :].astype(BF16)
                else:
                    mk = mk_ref[s, :, h * hd:(h + 1) * hd].astype(BF16)
                    mv = mv_ref[s, :, h * hd:(h + 1) * hd].astype(BF16)
                qh = q_scr[s * seg:(s + 1) * seg, h * hd:(h + 1) * hd]
                sc = _dot_nt(qh, mk)
                sc = sc - jnp.max(sc, axis=-1, keepdims=True)
                p = jnp.exp(sc)
                p = p / jnp.sum(p, axis=-1, keepdims=True)
                o_scr[s * seg:(s + 1) * seg, h * hd:(h + 1) * hd] = _dot(
                    p.astype(BF16), mv).astype(BF16)

    @pl.when(i < n_p)
    def _():
        attend(mkp_ref, mvp_ref)

    @pl.when(i >= n_p)
    def _():
        attend(mks_ref, mvs_ref)

    h2 = _layer_norm(alpha * h1 + _dot(o_scr[...], wo_ref[...]), g2_ref[...], b2_ref[...])
    h2_ref[...] = h2
    h2b = h2.astype(BF16)

    h_lo = (h2 - h2b.astype(F32)).astype(BF16)
    rw = rw_ref[...]
    rw_hi = rw.astype(BF16)
    rw_lo = (rw - rw_hi.astype(F32)).astype(BF16)
    logits = _dot(h2b, rw_hi) + _dot(h2b, rw_lo) + _dot(h_lo, rw_hi) + rb_ref[...]

    n_e = logits.shape[1]
    eidx = lax.broadcasted_iota(jnp.int32, logits.shape, 1).astype(F32)
    lane = lax.broadcasted_iota(jnp.int32, (tm, LANES), 1)
    route = jnp.zeros((tm, LANES), F32)
    work = logits
    vals = []
    sels = []
    for k in range(TOP_K):
        mx = jnp.max(work, axis=-1, keepdims=True)
        sel = jnp.min(jnp.where(work == mx, eidx, float(n_e)), axis=-1, keepdims=True)
        work = jnp.where(eidx == sel, -jnp.inf, work)
        vals.append(mx)
        sels.append(sel)
        route = jnp.where(lane == TOP_K + k, sel, route)
    ex = [jnp.exp(v - vals[0]) for v in vals]
    den = ex[0] + ex[1] + ex[2] + ex[3]
    for k in range(TOP_K):
        route = jnp.where(lane == k, ex[k] / den, route)

    @pl.when(i == 0)
    def _():
        cnt_scr[...] = jnp.zeros_like(cnt_scr)

    lane_f = lane.astype(F32)
    hit = [lane_f == sels[k] for k in range(TOP_K)]
    cnt = sum(h.astype(F32) for h in hit)
    r_io = lax.broadcasted_iota(jnp.int32, (tm, tm), 0)
    c_io = lax.broadcasted_iota(jnp.int32, (tm, tm), 1)
    ltri = jnp.where(r_io > c_io, 1.0, 0.0).astype(BF16)
    excl = _dot(ltri, cnt.astype(BF16)) + cnt_scr[...]
    for k in range(TOP_K):
        rank = jnp.sum(jnp.where(hit[k], excl, 0.0), axis=-1, keepdims=True)
        route = jnp.where(lane == 2 * TOP_K + k, rank, route)
    route_ref[...] = route
    cnt_scr[...] = cnt_scr[...] + jnp.sum(cnt, axis=0, keepdims=True)
    counts_ref[...] = cnt_scr[...]


def _mem_attn(h1, t_prompt, mem_kp, mem_vp, l_prompt, mem_ks, mem_vs, l_sample, wq, wo, g2, b2, rw, rb,
              alpha):
    T, D = h1.shape
    hd = 128
    tm = MEM_TM
    assert l_prompt % tm == 0 and tm % l_sample == 0
    n_p = t_prompt // tm
    n_s = (T - t_prompt) // tm
    per_b = l_prompt // tm
    sub_s = tm // l_sample
    n_mem = mem_kp.shape[1]
    bp = mem_kp.shape[0]
    const = lambda i: (0, 0)
    single = pl.Buffered(1)
    mem_p = pl.BlockSpec((1, n_mem, M_HEADS * hd), lambda i: (jnp.minimum(i // per_b, bp - 1), 0, 0))
    bs = mem_ks.shape[0]
    mem_ks = mem_ks.reshape(bs, n_mem * M_HEADS, hd)
    mem_vs = mem_vs.reshape(bs, n_mem * M_HEADS, hd)
    mem_s = pl.BlockSpec((sub_s, n_mem * M_HEADS, hd), lambda i: (_hi(i, n_p), 0, 0))
    kern = functools.partial(_mem_kernel, n_p=n_p, alpha=alpha, hd=hd)
    return pl.pallas_call(
        kern,
        grid=(n_p + n_s,),
        in_specs=[
            pl.BlockSpec((tm, D), lambda i: (i, 0)),
            mem_p, mem_p, mem_s, mem_s,
            pl.BlockSpec(wq.shape, const, pipeline_mode=single),
            pl.BlockSpec(wo.shape, const, pipeline_mode=single),
            pl.BlockSpec((1, D), const),
            pl.BlockSpec((1, D), const),
            pl.BlockSpec(rw.shape, const, pipeline_mode=single),
            pl.BlockSpec((1, rw.shape[1]), const),
        ],
        out_specs=[
            pl.BlockSpec((tm, D), lambda i: (i, 0)),
            pl.BlockSpec((tm, LANES), lambda i: (i, 0)),
            pl.BlockSpec((1, LANES), const),
        ],
        out_shape=[
            jax.ShapeDtypeStruct((T, D), F32),
            jax.ShapeDtypeStruct((T, LANES), F32),
            jax.ShapeDtypeStruct((1, LANES), F32),
        ],
        scratch_shapes=[
            pltpu.VMEM((tm, M_HEADS * hd), BF16),
            pltpu.VMEM((tm, M_HEADS * hd), BF16),
            pltpu.VMEM((1, LANES), F32),
        ],
        compiler_params=_cparams(("arbitrary",)),
        name="mem_attn",
    )(h1, mem_kp, mem_vp, mem_ks, mem_vs, wq, wo, g2, b2, rw, rb)


def _mem_kv_kernel(x_ref, w_ref, o_ref):
    o_ref[...] = _dot(x_ref[...].astype(BF16), w_ref[...])


def _mem_kv(mem, w):
    R, D = mem.shape
    N = w.shape[1]
    tm = 256
    return pl.pallas_call(
        _mem_kv_kernel,
        grid=(R // tm,),
        in_specs=[pl.BlockSpec((tm, D), lambda i: (i, 0)), pl.BlockSpec((D, N), lambda i: (0, 0))],
        out_specs=pl.BlockSpec((tm, N), lambda i: (i, 0)),
        out_shape=jax.ShapeDtypeStruct((R, N), F32),
        compiler_params=_cparams(("parallel",)),
        name="mem_kv",
    )(mem, w)


def _row_copy(src_hbm, dst, sem, src_row, dst_row):
    return pltpu.make_async_copy(src_hbm.at[pl.ds(src_row, 1)], dst.at[pl.ds(dst_row, 1)], sem)


def _rows_to_slabs(x, dst):
    n = x.shape[0]
    n_slab = x.shape[1] // LANES
    for s in range(n_slab):
        dst[pl.ds(s, n, stride=n_slab), :] = x[:, s * LANES:(s + 1) * LANES]


def _slabs_to_rows(src, n_slab):
    n = src.shape[0] // n_slab
    return jnp.concatenate([src[pl.ds(s, n, stride=n_slab), :] for s in range(n_slab)], axis=1)


def _dispatch_kernel(dest_ref, pend_ref, padded_ref, nu_ref, h_ref, xs_hbm, stage, zeros, sem, zsem,
                     *, tm, bm, n_steps, n_blocks):
    i = pl.program_id(0)
    slot = i % 2

    n_slab = h_ref.shape[1] // LANES

    def zero_block(off):
        off = pl.multiple_of(off * n_slab, bm * n_slab)
        return pltpu.make_async_copy(zeros, xs_hbm.at[pl.ds(off, bm * n_slab)], zsem)

    def zero_fill(act):
        for e in range(N_EXPERTS):
            @pl.when(padded_ref[e] > 0)
            def _(e=e):
                act(zero_block(pend_ref[e] - bm))
        for b in range(n_blocks - N_EXPERTS, n_blocks):
            @pl.when(b >= nu_ref[0])
            def _(b=b):
                act(zero_block(b * bm))

    @pl.when(i == 0)
    def _():
        zeros[...] = jnp.zeros_like(zeros)
        zero_fill(lambda c: c.start())
        zero_fill(lambda c: c.wait())

    def wait(s):
        for _ in range(TOP_K):
            pltpu.make_async_copy(stage.at[s], xs_hbm.at[pl.ds(0, tm * n_slab)], sem.at[s]).wait()

    @pl.when(i >= 2)
    def _():
        wait(slot)

    _rows_to_slabs(h_ref[...], stage.at[slot])

    def body(j, carry):
        t0 = j * SUBLANES
        for u in range(SUBLANES):
            src = stage.at[slot, pl.ds(pl.multiple_of((t0 + u) * n_slab, n_slab), n_slab)]
            for k in range(TOP_K):
                row = dest_ref[(i * tm + t0 + u) * TOP_K + k]
                dst = xs_hbm.at[pl.ds(pl.multiple_of(row * n_slab, n_slab), n_slab)]
                pltpu.make_async_copy(src, dst, sem.at[slot]).start(priority=k % 2)
        return carry
    lax.fori_loop(0, tm // SUBLANES, body, 0)

    @pl.when(i == n_steps - 1)
    def _():
        wait(slot)
        if n_steps > 1:
            wait(1 - slot)


def _moe_dispatch(h2, dest, pad_end, padded, n_used, n_rows, bm):
    T, D = h2.shape
    tm = GATHER_ROWS
    n_steps = T // tm
    n_slab = D // LANES
    kern = functools.partial(_dispatch_kernel, tm=tm, bm=bm, n_steps=n_steps, n_blocks=n_rows // bm)
    return pl.pallas_call(
        kern,
        grid_spec=pltpu.PrefetchScalarGridSpec(
            num_scalar_prefetch=4,
            grid=(n_steps,),
            in_specs=[pl.BlockSpec((tm, D), lambda i, d, pe, pd, nu: (i, 0))],
            out_specs=pl.BlockSpec(memory_space=pl.ANY),
            scratch_shapes=[
                pltpu.VMEM((2, tm * n_slab, LANES), F32),
                pltpu.VMEM((bm * n_slab, LANES), F32),
                pltpu.SemaphoreType.DMA((2,)),
                pltpu.SemaphoreType.DMA(()),
            ],
        ),
        out_shape=jax.ShapeDtypeStruct((n_rows * n_slab, LANES), F32),
        compiler_params=_cparams(("arbitrary",)),
        name="moe_dispatch",
    )(dest, pad_end, padded, n_used, h2)


def _cast_rows(src, dst):
    rows = CAST_ROWS

    def body(i, carry):
        r = pl.multiple_of(i * rows, rows)
        dst[pl.ds(r, rows), :] = src[pl.ds(r, rows), :].astype(BF16)
        return carry
    lax.fori_loop(0, src.shape[0] // rows, body, 0)


def _expert_weights(be_ref, first_ref, next_ref, pos_ref, len_ref, copies, cast, m):
    @pl.when(first_ref[m] == 1)
    def _():
        @pl.when(m == 0)
        def _():
            for c in copies(be_ref[0]):
                c.start()

        for c in copies(be_ref[m]):
            c.wait()
        cast()

    ne = next_ref[m]
    pos = pos_ref[m]
    last = pos == len_ref[m] - 1
    for i, c in enumerate(copies(jnp.maximum(ne, 0))):
        due = jnp.logical_or(pos == i, jnp.logical_and(last, pos < i))

        @pl.when(jnp.logical_and(ne >= 0, due))
        def _(c=c):
            c.start(priority=1)


def _moe_up_kernel(be_ref, nu_ref, first_ref, next_ref, pos_ref, len_ref, x_ref, w1_hbm, b_ref, a_ref,
                   stage, w_scr, sem):
    m = pl.program_id(0)
    ff = w_scr.shape[2]
    rows = w_scr.shape[1] // STAGE_PIECES

    def copies(e):
        return [pltpu.make_async_copy(w1_hbm.at[e, pl.ds(r * rows, rows), pl.ds(t * ff, ff)],
                                      stage.at[t, pl.ds(r * rows, rows)], sem)
                for t in range(2) for r in range(STAGE_PIECES)]

    def cast():
        for t in range(2):
            _cast_rows(stage.at[t], w_scr.at[t])

    _expert_weights(be_ref, first_ref, next_ref, pos_ref, len_ref, copies, cast, m)

    @pl.when(m < nu_ref[0])
    def _():
        x = _slabs_to_rows(x_ref, w_scr.shape[1] // LANES).astype(BF16)
        half = ff // 2
        for c in range(2):
            cols = slice(c * half, (c + 1) * half)
            lin_cols = slice(ff + c * half, ff + (c + 1) * half)
            u_glu = jnp.minimum(_dot(x, w_scr[0, :, cols]) + b_ref[0, :, cols], SWIGLU_LIMIT)
            u_lin = jnp.clip(_dot(x, w_scr[1, :, cols]) + b_ref[0, :, lin_cols],
                             -SWIGLU_LIMIT, SWIGLU_LIMIT)
            a = u_glu * jax.nn.sigmoid(SWIGLU_ALPHA * u_glu) * (u_lin + 1.0)
            a_ref[:, cols] = a.astype(BF16)

    @pl.when(m >= nu_ref[0])
    def _():
        a_ref[...] = jnp.zeros_like(a_ref)


def _moe_up(xs, w1, b1, plan):
    D = w1.shape[1]
    n_slab = D // LANES
    n_rows = xs.shape[0] // n_slab
    F = w1.shape[2] // 2
    bm = MOE_BM
    return pl.pallas_call(
        _moe_up_kernel,
        grid_spec=pltpu.PrefetchScalarGridSpec(
            num_scalar_prefetch=len(plan),
            grid=(n_rows // bm,),
            in_specs=[
                pl.BlockSpec((bm * n_slab, LANES), lambda m, be, nu, *_: (jnp.minimum(m, nu[0] - 1), 0)),
                pl.BlockSpec(memory_space=pl.ANY),
                pl.BlockSpec((1, 1, 2 * F), lambda m, be, *_: (be[m], 0, 0)),
            ],
            out_specs=pl.BlockSpec((bm, F), lambda m, *_: (m, 0)),
            scratch_shapes=[
                pltpu.VMEM((2, D, F), F32),
                pltpu.VMEM((2, D, F), BF16),
                pltpu.SemaphoreType.DMA(()),
            ],
        ),
        out_shape=jax.ShapeDtypeStruct((n_rows, F), BF16),
        compiler_params=_cparams(("arbitrary",)),
        name="moe_up",
    )(*plan, xs, w1, b1)


def _moe_down_kernel(be_ref, nu_ref, first_ref, next_ref, pos_ref, len_ref, a_ref, w2_hbm, b_ref, y_ref,
                     stage, w_scr, sem):
    m = pl.program_id(0)
    rows = w_scr.shape[0] // STAGE_PIECES

    def copies(e):
        return [pltpu.make_async_copy(w2_hbm.at[e, pl.ds(r * rows, rows)],
                                      stage.at[pl.ds(r * rows, rows)], sem)
                for r in range(STAGE_PIECES)]

    def cast():
        _cast_rows(stage, w_scr)

    _expert_weights(be_ref, first_ref, next_ref, pos_ref, len_ref, copies, cast, m)

    @pl.when(m < nu_ref[0])
    def _():
        y_ref[...] = _dot(a_ref[...], w_scr[...]) + b_ref[0]

    @pl.when(m >= nu_ref[0])
    def _():
        y_ref[...] = jnp.zeros_like(y_ref)


def _moe_down(a, w2, b2, plan):
    n_rows, F = a.shape
    D = w2.shape[2]
    bm = MOE_BM
    return pl.pallas_call(
        _moe_down_kernel,
        grid_spec=pltpu.PrefetchScalarGridSpec(
            num_scalar_prefetch=len(plan),
            grid=(n_rows // bm,),
            in_specs=[
                pl.BlockSpec((bm, F), lambda m, be, nu, *_: (jnp.minimum(m, nu[0] - 1), 0)),
                pl.BlockSpec(memory_space=pl.ANY),
                pl.BlockSpec((1, 1, D), lambda m, be, *_: (be[m], 0, 0)),
            ],
            out_specs=pl.BlockSpec((bm, D), lambda m, *_: (m, 0)),
            scratch_shapes=[
                pltpu.VMEM((F, D), F32),
                pltpu.VMEM((F, D), BF16),
                pltpu.SemaphoreType.DMA(()),
            ],
        ),
        out_shape=jax.ShapeDtypeStruct((n_rows, D), F32),
        compiler_params=_cparams(("arbitrary",)),
        name="moe_down",
    )(*plan, a, w2, b2)


def _combine_kernel(dest_ref, h2_ref, rt_ref, g_ref, b_ref, ys_hbm, op_ref, os_ref, buf, sem,
                    *, tm, n_steps, n_prompt, alpha):
    i = pl.program_id(0)

    def issue(step, slot):
        def body(j, carry):
            t0 = pl.multiple_of(j * SUBLANES, SUBLANES)
            for u in range(SUBLANES):
                for k in range(TOP_K):
                    row = dest_ref[(step * tm + t0 + u) * TOP_K + k]
                    _row_copy(ys_hbm, buf.at[slot, k], sem.at[slot], row, t0 + u).start(priority=k % 2)
            return carry
        lax.fori_loop(0, tm // SUBLANES, body, 0)

    def wait(slot):
        for k in range(TOP_K):
            pltpu.make_async_copy(ys_hbm.at[pl.ds(0, tm)], buf.at[slot, k], sem.at[slot]).wait()

    slot = i % 2

    @pl.when(i == 0)
    def _():
        issue(0, 0)

    @pl.when(i + 1 < n_steps)
    def _():
        issue(i + 1, 1 - slot)

    wait(slot)
    y = rt_ref[:, 0:1] * buf[slot, 0]
    for k in range(1, TOP_K):
        y = y + rt_ref[:, k:k + 1] * buf[slot, k]
    out = _layer_norm(alpha * h2_ref[...] + y, g_ref[...], b_ref[...])

    @pl.when(i < n_prompt)
    def _():
        op_ref[...] = out

    @pl.when(i >= n_prompt)
    def _():
        os_ref[...] = out


def _moe_combine(ys, dest, h2, route, g3, b3, t_prompt, alpha):
    T, D = h2.shape
    tm = COMBINE_TM
    n_steps = T // tm
    n_prompt = t_prompt // tm
    kern = functools.partial(_combine_kernel, tm=tm, n_steps=n_steps, n_prompt=n_prompt, alpha=alpha)
    return pl.pallas_call(
        kern,
        grid_spec=pltpu.PrefetchScalarGridSpec(
            num_scalar_prefetch=1,
            grid=(n_steps,),
            in_specs=[
                pl.BlockSpec((tm, D), lambda i, d: (i, 0)),
                pl.BlockSpec((tm, LANES), lambda i, d: (i, 0)),
                pl.BlockSpec((1, D), lambda i, d: (0, 0)),
                pl.BlockSpec((1, D), lambda i, d: (0, 0)),
                pl.BlockSpec(memory_space=pl.ANY),
            ],
            out_specs=[
                pl.BlockSpec((tm, D), lambda i, d: (jnp.minimum(i, n_prompt - 1), 0)),
                pl.BlockSpec((tm, D), lambda i, d: (jnp.maximum(i - n_prompt, 0), 0)),
            ],
            scratch_shapes=[pltpu.VMEM((2, TOP_K, tm, D), F32), pltpu.SemaphoreType.DMA((2,))],
        ),
        out_shape=[
            jax.ShapeDtypeStruct((t_prompt, D), F32),
            jax.ShapeDtypeStruct((T - t_prompt, D), F32),
        ],
        compiler_params=_cparams(("arbitrary",)),
        name="moe_combine",
    )(dest, h2, route, g3, b3, ys)


def _moe_plan(route, counts, bm):
    T = route.shape[0]
    top_idx = route[:, TOP_K:2 * TOP_K].astype(jnp.int32)
    rank = route[:, 2 * TOP_K:3 * TOP_K].astype(jnp.int32)
    counts = counts[0, :N_EXPERTS].astype(jnp.int32)
    padded = (counts + bm - 1) // bm * bm
    pad_end = jnp.cumsum(padded)
    pad_start = pad_end - padded
    experts = jnp.arange(N_EXPERTS, dtype=jnp.int32)
    start_of = jnp.sum(jnp.where(top_idx[..., None] == experts, pad_start, 0), axis=-1)
    dest = (start_of + rank).reshape(-1)
    n_blocks = -(-(T * TOP_K) // bm) + N_EXPERTS
    block_start = jnp.arange(n_blocks, dtype=jnp.int32) * bm
    block_e = jnp.sum((pad_end[None, :] <= block_start[:, None]).astype(jnp.int32), axis=1)
    block_e = jnp.minimum(block_e, N_EXPERTS - 1)
    n_used = (pad_end[-1] // bm).astype(jnp.int32).reshape(1)
    blk = jnp.arange(n_blocks, dtype=jnp.int32)
    prev_e = jnp.concatenate([jnp.full((1,), -1, jnp.int32), block_e[:-1]])
    first = jnp.logical_and(blk < n_used[0], block_e != prev_e)
    later = jnp.logical_and(first[None, :], blk[None, :] > blk[:, None])
    nxt_blk = jnp.min(jnp.where(later, blk[None, :], n_blocks), axis=1)
    next_e = jnp.where(nxt_blk < n_blocks, block_e[jnp.minimum(nxt_blk, n_blocks - 1)], -1)
    earlier = jnp.logical_and(first[None, :], blk[None, :] <= blk[:, None])
    seg_start = jnp.max(jnp.where(earlier, blk[None, :], 0), axis=1)
    seg_len = jnp.minimum(nxt_blk, n_used[0]) - seg_start
    plan = (block_e, n_used, first.astype(jnp.int32), next_e.astype(jnp.int32),
            (blk - seg_start).astype(jnp.int32), seg_len.astype(jnp.int32))
    return dest, pad_end, padded, plan, n_blocks * bm


def kernel(x_prompt, x_sample, cache_diff_k, cache_diff_v, state_ret, cache_mem_k, cache_mem_v, mem_prompt, ln_in_g, ln_in_b, w_in, b_gate, diff_lambda, diff_subln, w_proj_ret, w_proj_diff, w_out, ln1_g, ln1_b, w_mq, w_mk, w_mv, w_mo, ln2_g, ln2_b, router_w, router_b, w1, b1, w2, b2, ln3_g, ln3_b):
    Bp, Lp, D = x_prompt.shape
    Bs, Ls, _ = x_sample.shape
    depth = w_in.shape[0]
    assert depth == 1
    past = cache_diff_k.shape[2]
    n_mem = mem_prompt.shape[1]
    Tp, Ts = Bp * Lp, Bs * Ls
    alpha = (2.0 * depth) ** 0.25
    lambda_init = 0.8 - 0.6 * math.exp(-0.3 * 0)
    row = lambda v: v.reshape(1, -1)

    xp = x_prompt.reshape(Tp, D)
    xs_in = x_sample.reshape(Ts, D)
    kv_blk = 4096 // IN_TN
    P, dk_p, dv_p, dk_s, dv_s = _in_proj(xp, xs_in, row(ln_in_g), row(ln_in_b), w_in[0].astype(BF16),
                                         kv_blk)

    zeros_state = jnp.zeros((Bp,) + state_ret.shape[2:], F32)
    yr_p, s_p = _retention(P, 0, Bp, Lp, 0, zeros_state)
    yr_s, s_s = _retention(P, Tp, Bs, Ls, past, state_ret[0])

    yd_p = _attn_prompt(P, Bp, Lp, diff_lambda[0], row(diff_subln[0]), lambda_init)
    yd_s = _attn_sample(P, Tp, Bs, Ls, cache_diff_k[0], cache_diff_v[0], diff_lambda[0],
                        row(diff_subln[0]), lambda_init)

    h1 = _mix(xp, xs_in, yr_p, yr_s, yd_p, yd_s, P, row(ln_in_g), row(ln_in_b), row(b_gate[0]),
              w_proj_ret[0].astype(BF16), w_proj_diff[0].astype(BF16), w_out[0].astype(BF16),
              row(ln1_g[0]), row(ln1_b[0]), alpha)

    w_mkv = jnp.concatenate([w_mk[0], w_mv[0]], axis=1).astype(BF16)
    mkv = _mem_kv(mem_prompt.reshape(Bp * n_mem, D), w_mkv)
    hm = w_mk.shape[2]
    mk_p = mkv[:, :hm].reshape(Bp, n_mem, hm)
    mv_p = mkv[:, hm:].reshape(Bp, n_mem, hm)
    h2, route, counts = _mem_attn(
        h1, Tp, mk_p, mv_p, Lp, cache_mem_k[0], cache_mem_v[0], Ls, w_mq[0].astype(BF16),
        w_mo[0].astype(BF16),
        row(ln2_g[0]), row(ln2_b[0]), router_w[0], row(router_b[0]), alpha)

    dest, pad_end, padded, plan, n_rows = _moe_plan(route, counts, MOE_BM)
    xs = _moe_dispatch(h2, dest, pad_end, padded, plan[1], n_rows, MOE_BM)
    act = _moe_up(xs, w1[0], b1[0][:, None, :], plan)
    ys = _moe_down(act, w2[0], b2[0][:, None, :], plan)
    out_p, out_s = _moe_combine(ys, dest, h2, route, row(ln3_g[0]), row(ln3_b[0]), Tp, alpha)

    return (
        out_p.reshape(Bp, Lp, D),
        out_s.reshape(Bs, Ls, D),
        dk_p.reshape(1, Bp, Lp, D_HEADS, 128),
        dv_p.reshape(1, Bp, Lp, D_HEADS, 128),
        s_p[None],
        mk_p.reshape(1, Bp, n_mem, M_HEADS, hm // M_HEADS),
        mv_p.reshape(1, Bp, n_mem, M_HEADS, hm // M_HEADS),
        dk_s.reshape(1, Bs, Ls, D_HEADS, 128),
        dv_s.reshape(1, Bs, Ls, D_HEADS, 128),
        s_s[None],
    )
```

```python
import functools
import math

import jax
import jax.numpy as jnp
from jax import lax
from jax.experimental import pallas as pl
from jax.experimental.pallas import tpu as pltpu

F32 = jnp.float32
BF16 = jnp.bfloat16

CHUNK = 64
R_HEADS = 4
D_HEADS = 8
M_HEADS = 4
N_EXPERTS = 32
TOP_K = 4
SWIGLU_ALPHA = 1.702
SWIGLU_LIMIT = 7.0
LN_EPS = 1e-5
RMS_EPS = 1e-5
ROPE_BASE = 10000.0
LOG2E = 1.4426950408889634

LANES = 128
SUBLANES = 8
VMEM_LIMIT = 58 * 1024 * 1024

IN_TM = 512
IN_TN = 2048
RET_CHUNK = 256
ATT_BLK = 512
ATT_UNROLL = 4
MIX_TM = 256
MEM_TM = 512
MOE_BM = 256
CAST_ROWS = 256
GATHER_ROWS = 512
COMBINE_TM = 256


def _cparams(sem):
    return pltpu.CompilerParams(dimension_semantics=sem, vmem_limit_bytes=VMEM_LIMIT)


def _layer_norm(x, g, b):
    mu = jnp.mean(x, axis=-1, keepdims=True)
    xc = x - mu
    var = jnp.mean(xc * xc, axis=-1, keepdims=True)
    return xc * lax.rsqrt(var + LN_EPS) * g + b


def _dot(a, b):
    return jnp.dot(a, b, preferred_element_type=F32)


def _dot_nt(a, b):
    return lax.dot_general(a, b, (((1,), (1,)), ((), ())), preferred_element_type=F32)


def _dot_tn(a, b):
    return lax.dot_general(a, b, (((0,), (0,)), ((), ())), preferred_element_type=F32)


def _lo(i, n):
    return jnp.minimum(i, n - 1)


def _hi(i, n):
    return jnp.maximum(i - n, 0)


def _in_proj_kernel(xp_ref, xs_ref, g_ref, b_ref, w_ref, p_ref, dkp_ref, dvp_ref, dks_ref, dvs_ref,
                    h_scr, *, n_p):
    i = pl.program_id(0)
    j = pl.program_id(1)
    half = dkp_ref.shape[1]

    @pl.when(jnp.logical_and(j == 0, i < n_p))
    def _():
        h_scr[...] = _layer_norm(xp_ref[...], g_ref[...], b_ref[...]).astype(BF16)

    @pl.when(jnp.logical_and(j == 0, i >= n_p))
    def _():
        h_scr[...] = _layer_norm(xs_ref[...], g_ref[...], b_ref[...]).astype(BF16)

    acc = _dot(h_scr[...], w_ref[...])
    p_ref[...] = acc.astype(BF16)

    @pl.when(jnp.logical_and(j == 0, i < n_p))
    def _():
        dkp_ref[...] = acc[:, :half]
        dvp_ref[...] = acc[:, half:]

    @pl.when(jnp.logical_and(j == 0, i >= n_p))
    def _():
        dks_ref[...] = acc[:, :half]
        dvs_ref[...] = acc[:, half:]


def _in_proj(xp, xs, g, b, w, kv_blk):
    Tp, D = xp.shape
    Ts = xs.shape[0]
    N = w.shape[1]
    n_p, n_s = Tp // IN_TM, Ts // IN_TM
    half = IN_TN // 2

    def col(j):
        return jnp.where(j == 0, kv_blk, jnp.where(j <= kv_blk, j - 1, j))

    single = pl.Buffered(1)
    kv_p = pl.BlockSpec((IN_TM, half), lambda i, j: (_lo(i, n_p), 0))
    kv_s = pl.BlockSpec((IN_TM, half), lambda i, j: (_hi(i, n_p), 0), pipeline_mode=single)
    return pl.pallas_call(
        functools.partial(_in_proj_kernel, n_p=n_p),
        grid=(n_p + n_s, N // IN_TN),
        in_specs=[
            pl.BlockSpec((IN_TM, D), lambda i, j: (_lo(i, n_p), 0)),
            pl.BlockSpec((IN_TM, D), lambda i, j: (_hi(i, n_p), 0), pipeline_mode=single),
            pl.BlockSpec((1, D), lambda i, j: (0, 0)),
            pl.BlockSpec((1, D), lambda i, j: (0, 0)),
            pl.BlockSpec((D, IN_TN), lambda i, j: (0, col(j))),
        ],
        out_specs=[pl.BlockSpec((IN_TM, IN_TN), lambda i, j: (i, j)), kv_p, kv_p, kv_s, kv_s],
        out_shape=[
            jax.ShapeDtypeStruct((Tp + Ts, N), BF16),
            jax.ShapeDtypeStruct((Tp, half), F32),
            jax.ShapeDtypeStruct((Tp, half), F32),
            jax.ShapeDtypeStruct((Ts, half), F32),
            jax.ShapeDtypeStruct((Ts, half), F32),
        ],
        scratch_shapes=[pltpu.VMEM((IN_TM, D), BF16)],
        compiler_params=_cparams(("arbitrary", "arbitrary")),
        name="in_proj",
    )(xp, xs, g, b, w)


P_DK = 0
P_DV = 8
P_RQ = 16
P_RK = 20
P_RV = 24
P_RG = 32
P_DQ = 40
P_GATE = 48


def _retention_kernel(q_ref, k_ref, v_ref, g_ref, cos_ref, sin_ref, dm_ref, qd_ref, kd_ref,
                      bd_ref, s0_ref, y_ref, sout_ref, s_scr, *, n_chunks, dk, dv):
    c = pl.program_id(1)

    @pl.when(c == 0)
    def _():
        s_scr[...] = s0_ref[0]

    cos = cos_ref[...]
    sin = sin_ref[...]
    k_scale = dk ** -0.5
    for h in range(R_HEADS):
        q = q_ref[:, h * dk:(h + 1) * dk].astype(F32)
        k = k_ref[:, h * dk:(h + 1) * dk].astype(F32)
        v = v_ref[:, h * dv:(h + 1) * dv]
        g = g_ref[:, h * dv:(h + 1) * dv].astype(F32)
        q = q * cos + pltpu.roll(q, dk // 2, 1) * sin
        k = (k * cos + pltpu.roll(k, dk // 2, 1) * sin) * k_scale
        qb = q.astype(BF16)
        kb = k.astype(BF16)
        s = s_scr[h]
        scores = _dot_nt(qb, kb) * dm_ref[h]
        o = _dot(scores.astype(BF16), v) + _dot((q * qd_ref[h]).astype(BF16), s.astype(BF16))
        s_scr[h] = bd_ref[h] * s + _dot_tn((k * kd_ref[h]).astype(BF16), v)
        o = o * lax.rsqrt(jnp.mean(o * o, axis=-1, keepdims=True) + RMS_EPS)
        y_ref[:, h * dv:(h + 1) * dv] = (g * jax.nn.sigmoid(g) * o).astype(BF16)

    @pl.when(c == n_chunks - 1)
    def _():
        sout_ref[0] = s_scr[...]


def _retention(P, row_off, B, L, pos0, S0):
    dk, dv = 128, 256
    C = min(L, RET_CHUNK)
    nc = L // C
    ob = row_off // C
    pos = (pos0 + jnp.arange(L, dtype=jnp.int32)).astype(F32)
    inv = 1.0 / (ROPE_BASE ** jnp.linspace(0.0, 1.0, dk // 2, dtype=F32))
    ang = pos[:, None] * inv[None, :]
    cos = jnp.concatenate([jnp.cos(ang), jnp.cos(ang)], axis=-1)
    sin = jnp.concatenate([-jnp.sin(ang), jnp.sin(ang)], axis=-1)
    log_g = jnp.log1p(-jnp.power(2.0, -5.0 - jnp.arange(R_HEADS, dtype=F32)))
    i = jnp.arange(C, dtype=F32)
    rel = i[:, None] - i[None, :]
    dmask = jnp.where(rel >= 0, jnp.exp(log_g[:, None, None] * jnp.maximum(rel, 0.0)), 0.0)
    q_decay = jnp.exp(log_g[:, None] * (i + 1.0))[..., None]
    k_decay = jnp.exp(log_g[:, None] * (C - 1.0 - i))[..., None]
    b_decay = jnp.exp(log_g * C)[:, None, None]

    def rows(b, c):
        return ob + b * nc + c

    kern = functools.partial(_retention_kernel, n_chunks=nc, dk=dk, dv=dv)
    return pl.pallas_call(
        kern,
        grid=(B, nc),
        in_specs=[
            pl.BlockSpec((C, R_HEADS * dk), lambda b, c: (rows(b, c), P_RQ * LANES // (R_HEADS * dk))),
            pl.BlockSpec((C, R_HEADS * dk), lambda b, c: (rows(b, c), P_RK * LANES // (R_HEADS * dk))),
            pl.BlockSpec((C, R_HEADS * dv), lambda b, c: (rows(b, c), P_RV * LANES // (R_HEADS * dv))),
            pl.BlockSpec((C, R_HEADS * dv), lambda b, c: (rows(b, c), P_RG * LANES // (R_HEADS * dv))),
            pl.BlockSpec((C, dk), lambda b, c: (c, 0)),
            pl.BlockSpec((C, dk), lambda b, c: (c, 0)),
            pl.BlockSpec((R_HEADS, C, C), lambda b, c: (0, 0, 0)),
            pl.BlockSpec((R_HEADS, C, 1), lambda b, c: (0, 0, 0)),
            pl.BlockSpec((R_HEADS, C, 1), lambda b, c: (0, 0, 0)),
            pl.BlockSpec((R_HEADS, 1, 1), lambda b, c: (0, 0, 0)),
            pl.BlockSpec((1, R_HEADS, dk, dv), lambda b, c: (b, 0, 0, 0)),
        ],
        out_specs=[
            pl.BlockSpec((C, R_HEADS * dv), lambda b, c: (b * nc + c, 0)),
            pl.BlockSpec((1, R_HEADS, dk, dv), lambda b, c: (b, 0, 0, 0)),
        ],
        out_shape=[
            jax.ShapeDtypeStruct((B * L, R_HEADS * dv), BF16),
            jax.ShapeDtypeStruct((B, R_HEADS, dk, dv), F32),
        ],
        scratch_shapes=[pltpu.VMEM((R_HEADS, dk, dv), F32)],
        compiler_params=_cparams(("parallel", "arbitrary")),
        name="retention",
    )(P, P, P, P, cos, sin, dmask, q_decay, k_decay, b_decay, S0)


def _diff_lambda(lam_ref, lambda_init):
    lv = lam_ref[...]
    a = jnp.sum(lv[0:1] * lv[1:2], axis=-1, keepdims=True)
    b = jnp.sum(lv[2:3] * lv[3:4], axis=-1, keepdims=True)
    return jnp.exp(a) - jnp.exp(b) + lambda_init


def _lane_tile(x, n):
    return jnp.concatenate([x] * n, axis=1)


def _stack_maps(q, hd):
    lane = lax.broadcasted_iota(jnp.int32, q.shape, 1)
    zero = jnp.zeros_like(q)
    return jnp.concatenate([jnp.where(lane < hd, q, zero), jnp.where(lane < hd, zero, q)], axis=0)


def _diff_finish(acc, l, lam, subln, lambda_init, tq):
    o = acc[:tq] / l[:tq] - lam * (acc[tq:] / l[tq:])
    o = o * lax.rsqrt(jnp.mean(o * o, axis=-1, keepdims=True) + RMS_EPS)
    return (o * subln * (1.0 - lambda_init)).astype(BF16)


def _attn_prompt_kernel(q_ref, k_ref, v_ref, lam_ref, sub_ref, y_ref, qs_scr, ve_scr, m_scr, acc_scr,
                        s_scr, *, blk, hd, lambda_init):
    qi = pl.program_id(2)
    dv = 2 * hd
    seq = v_ref.shape[0]
    sub = 2 * blk
    lo, hi, every = slice(0, sub), slice(sub, 2 * sub), slice(0, 2 * sub)

    @pl.when(qi == 0)
    def _():
        ve_scr[:, :dv] = v_ref[...]
        ve_scr[:, dv:] = jnp.ones((seq, dv), BF16)

    for s_idx, rows in enumerate((lo, hi)):
        q = q_ref[s_idx * blk:(s_idx + 1) * blk, :].astype(F32) * (hd ** -0.5 * LOG2E)
        qs_scr[rows] = _stack_maps(q.astype(BF16), hd)
    m_scr[...] = jnp.full_like(m_scr, -jnp.inf)
    acc_scr[...] = jnp.zeros_like(acc_scr)

    def scores(ki, slot, rows=every):
        off = pl.multiple_of(ki * blk, blk)
        s_scr[slot, rows] = _dot_nt(qs_scr[rows], k_ref[pl.ds(off, blk), :])

    def step(ki, slot, rows=every, diag=False):
        off = pl.multiple_of(ki * blk, blk)
        s = s_scr[slot, rows]
        if diag:
            row = lax.broadcasted_iota(jnp.int32, s.shape, 0)
            col = lax.broadcasted_iota(jnp.int32, s.shape, 1)
            shift = CHUNK.bit_length() - 1
            seen = (col >> shift) <= ((row & (blk - 1)) >> shift)
            if s.shape[0] > sub:
                seen = jnp.logical_or(row >= sub, seen)
            s = jnp.where(seen, s, -1e30)
        m_prev = m_scr[rows]
        m_new = jnp.maximum(m_prev, jnp.max(s, axis=-1, keepdims=True))
        alpha = jnp.exp2(m_prev - m_new)
        p = jnp.exp2(s - _lane_tile(m_new, blk // LANES))
        pv = _dot(p.astype(BF16), ve_scr[pl.ds(off, blk), :])
        acc_scr[rows] = _lane_tile(alpha, 2 * dv // LANES) * acc_scr[rows] + pv
        m_scr[rows] = m_new

    def run(base, width):
        for u in range(width):
            scores(base + u + 1, (u + 1) % 2)
            step(base + u, u % 2)

    def body(j, carry):
        run(ATT_UNROLL * j, ATT_UNROLL)
        return carry

    n_open = 2 * qi
    scores(0, 0)
    n_main = n_open // ATT_UNROLL
    lax.fori_loop(0, n_main, body, 0)
    done = n_main * ATT_UNROLL
    width = ATT_UNROLL // 2
    while width >= 2:
        @pl.when(((n_open - done) // width) % 2 == 1)
        def _(width=width):
            run(n_open - (n_open - done) % (2 * width), width)
        width //= 2
    scores(n_open + 1, 1, hi)
    step(n_open, 0, every, diag=True)
    step(n_open + 1, 1, hi, diag=True)

    lam = _diff_lambda(lam_ref, lambda_init)
    for s_idx in range(2):
        a1 = acc_scr[s_idx * sub:s_idx * sub + blk, :]
        a2 = acc_scr[s_idx * sub + blk:(s_idx + 1) * sub, :]
        o = a1[:, :dv] / a1[:, dv:] - lam * (a2[:, :dv] / a2[:, dv:])
        o = o * lax.rsqrt(jnp.mean(o * o, axis=-1, keepdims=True) + RMS_EPS)
        y_ref[s_idx * blk:(s_idx + 1) * blk, :] = (o * sub_ref[...] * (1.0 - lambda_init)).astype(BF16)


def _attn_prompt(P, B, S, lam_p, subln, lambda_init):
    hd = 64
    blk = ATT_BLK
    bq = 2 * blk
    assert S % bq == 0
    nq = S // bq
    w = 2 * hd
    rows = 2 * bq
    kern = functools.partial(_attn_prompt_kernel, blk=blk, hd=hd, lambda_init=lambda_init)
    return pl.pallas_call(
        kern,
        grid=(B, D_HEADS, nq),
        in_specs=[
            pl.BlockSpec((bq, w), lambda b, h, i: (b * nq + i, P_DQ + h)),
            pl.BlockSpec((S, w), lambda b, h, i: (b, P_DK + h)),
            pl.BlockSpec((S, w), lambda b, h, i: (b, P_DV + h)),
            pl.BlockSpec((4, hd), lambda b, h, i: (0, 0)),
            pl.BlockSpec((1, w), lambda b, h, i: (0, 0)),
        ],
        out_specs=pl.BlockSpec((bq, w), lambda b, h, i: (b * nq + i, h)),
        out_shape=jax.ShapeDtypeStruct((B * S, D_HEADS * w), BF16),
        scratch_shapes=[
            pltpu.VMEM((rows, w), BF16),
            pltpu.VMEM((S, 2 * w), BF16),
            pltpu.VMEM((rows, LANES), F32),
            pltpu.VMEM((rows, 2 * w), F32),
            pltpu.VMEM((2, rows, blk), F32),
        ],
        compiler_params=_cparams(("parallel", "parallel", "arbitrary")),
        name="attn_prompt",
    )(P, P, P, lam_p, subln)


def _attn_sample_kernel(q_ref, kn_ref, vn_ref, kc_ref, vc_ref, lam_ref, sub_ref, y_ref,
                        *, tq, hd, lambda_init):
    lam = _diff_lambda(lam_ref, lambda_init)
    w = 2 * hd
    past = kc_ref.shape[1] // D_HEADS
    for h in range(D_HEADS):
        cols = slice(h * w, (h + 1) * w)
        qs = _stack_maps(q_ref[:, cols] * (hd ** -0.5), hd)
        kc = kc_ref[0, pl.ds(h, past, stride=D_HEADS), :].astype(BF16)
        vc = vc_ref[0, pl.ds(h, past, stride=D_HEADS), :].astype(BF16)
        kn = kn_ref[:, cols]
        vn = vn_ref[:, cols]
        s_c = _dot_nt(qs, kc)
        s_n = _dot_nt(qs, kn)
        m = jnp.maximum(jnp.max(s_c, axis=-1, keepdims=True), jnp.max(s_n, axis=-1, keepdims=True))
        p_c = jnp.exp(s_c - m)
        p_n = jnp.exp(s_n - m)
        l = jnp.sum(p_c, axis=-1, keepdims=True) + jnp.sum(p_n, axis=-1, keepdims=True)
        acc = _dot(p_c.astype(BF16), vc) + _dot(p_n.astype(BF16), vn)
        y_ref[:, cols] = _diff_finish(acc, l, lam, sub_ref[...], lambda_init, tq)


def _attn_sample(P, row_off, B, L, cache_k, cache_v, lam_p, subln, lambda_init):
    hd = 64
    past = cache_k.shape[1]
    ob = row_off // L
    width = D_HEADS * 2 * hd
    kern = functools.partial(_attn_sample_kernel, tq=L, hd=hd, lambda_init=lambda_init)
    cache_k = cache_k.reshape(B, past * D_HEADS, 2 * hd)
    cache_v = cache_v.reshape(B, past * D_HEADS, 2 * hd)
    cache_spec = pl.BlockSpec((1, past * D_HEADS, 2 * hd), lambda b: (b, 0, 0))
    return pl.pallas_call(
        kern,
        grid=(B,),
        in_specs=[
            pl.BlockSpec((L, width), lambda b: (ob + b, P_DQ * LANES // width)),
            pl.BlockSpec((L, width), lambda b: (ob + b, P_DK * LANES // width)),
            pl.BlockSpec((L, width), lambda b: (ob + b, P_DV * LANES // width)),
            cache_spec,
            cache_spec,
            pl.BlockSpec((4, hd), lambda b: (0, 0)),
            pl.BlockSpec((1, 2 * hd), lambda b: (0, 0)),
        ],
        out_specs=pl.BlockSpec((L, width), lambda b: (b, 0)),
        out_shape=jax.ShapeDtypeStruct((B * L, width), BF16),
        compiler_params=_cparams(("parallel",)),
        name="attn_sample",
    )(P, P, P, cache_k, cache_v, lam_p, subln)


def _mix_kernel(xp_ref, xs_ref, yrp_ref, yrs_ref, ydp_ref, yds_ref, gr_ref, gd_ref, lig_ref, lib_ref,
                bg_ref, wpr_ref, wpd_ref, wo_ref, g1_ref, b1_ref, h1_ref, *, alpha, n_p):
    i = pl.program_id(0)
    d = xp_ref.shape[1]

    def compute(x_ref, yr_ref, yd_ref):
        h0 = _layer_norm(x_ref[...], lig_ref[...], lib_ref[...])
        g_ret = jax.nn.sigmoid(gr_ref[...].astype(F32) + bg_ref[:, :d])
        g_diff = jax.nn.sigmoid(gd_ref[...].astype(F32) + bg_ref[:, d:])
        merged = g_ret * _dot(yr_ref[...], wpr_ref[...]) + g_diff * _dot(yd_ref[...], wpd_ref[...])
        mixed = _dot(merged.astype(BF16), wo_ref[...])
        h1_ref[...] = _layer_norm(alpha * h0 + mixed, g1_ref[...], b1_ref[...])

    @pl.when(i < n_p)
    def _():
        compute(xp_ref, yrp_ref, ydp_ref)

    @pl.when(i >= n_p)
    def _():
        compute(xs_ref, yrs_ref, yds_ref)


def _mix(xp, xs, yr_p, yr_s, yd_p, yd_s, P, ln_in_g, ln_in_b, b_gate, wpr, wpd, wo, g1, b1, alpha):
    Tp, D = xp.shape
    Ts = xs.shape[0]
    tm = MIX_TM
    n_p, n_s = Tp // tm, Ts // tm
    gcol = P_GATE * LANES // D
    const = lambda i: (0, 0)
    lo = lambda i: (_lo(i, n_p), 0)
    hi = lambda i: (_hi(i, n_p), 0)
    single = pl.Buffered(1)
    kern = functools.partial(_mix_kernel, alpha=alpha, n_p=n_p)
    return pl.pallas_call(
        kern,
        grid=(n_p + n_s,),
        in_specs=[
            pl.BlockSpec((tm, D), lo),
            pl.BlockSpec((tm, D), hi),
            pl.BlockSpec((tm, yr_p.shape[1]), lo),
            pl.BlockSpec((tm, yr_s.shape[1]), hi),
            pl.BlockSpec((tm, yd_p.shape[1]), lo),
            pl.BlockSpec((tm, yd_s.shape[1]), hi),
            pl.BlockSpec((tm, D), lambda i: (i, gcol)),
            pl.BlockSpec((tm, D), lambda i: (i, gcol + 1)),
            pl.BlockSpec((1, D), const),
            pl.BlockSpec((1, D), const),
            pl.BlockSpec((1, 2 * D), const),
            pl.BlockSpec(wpr.shape, const, pipeline_mode=single),
            pl.BlockSpec(wpd.shape, const, pipeline_mode=single),
            pl.BlockSpec(wo.shape, const, pipeline_mode=single),
            pl.BlockSpec((1, D), const),
            pl.BlockSpec((1, D), const),
        ],
        out_specs=pl.BlockSpec((tm, D), lambda i: (i, 0)),
        out_shape=jax.ShapeDtypeStruct((Tp + Ts, D), F32),
        compiler_params=_cparams(("arbitrary",)),
        name="mix",
    )(xp, xs, yr_p, yr_s, yd_p, yd_s, P, P, ln_in_g, ln_in_b, b_gate, wpr, wpd, wo, g1, b1)


def _mem_kernel(h1_ref, mkp_ref, mvp_ref, mks_ref, mvs_ref, wq_ref, wo_ref, g2_ref, b2_ref, rw_ref,
                rb_ref, h2_ref, route_ref, counts_ref, q_scr, o_scr, cnt_scr, *, n_p, alpha, hd):
    i = pl.program_id(0)
    tm = h1_ref.shape[0]
    h1 = h1_ref[...]
    q_scr[...] = (_dot(h1.astype(BF16), wq_ref[...]) * (hd ** -0.5)).astype(BF16)

    def attend(mk_ref, mv_ref):
        n_sub = mk_ref.shape[0]
        seg = tm // n_sub
        split_heads = mk_ref.shape[2] == hd
        for s in range(n_sub):
            for h in range(M_HEADS):
                if split_heads:
                    rows = pl.ds(h, mk_ref.shape[1] // M_HEADS, stride=M_HEADS)
                    mk = mk_ref[s, rows, :].astype(BF16)
                    mv = mv_ref[s, rows, :].astype(BF16)
                else:
                    mk = mk_ref[s, :, h * hd:(h + 1) * hd].astype(BF16)
                    mv = mv_ref[s, :, h * hd:(h + 1) * hd].astype(BF16)
                qh = q_scr[s * seg:(s + 1) * seg, h * hd:(h + 1) * hd]
                sc = _dot_nt(qh, mk)
                sc = sc - jnp.max(sc, axis=-1, keepdims=True)
                p = jnp.exp(sc)
                p = p / jnp.sum(p, axis=-1, keepdims=True)
                o_scr[s * seg:(s + 1) * seg, h * hd:(h + 1) * hd] = _dot(
                    p.astype(BF16), mv).astype(BF16)

    @pl.when(i < n_p)
    def _():
        attend(mkp_ref, mvp_ref)

    @pl.when(i >= n_p)
    def _():
        attend(mks_ref, mvs_ref)

    h2 = _layer_norm(alpha * h1 + _dot(o_scr[...], wo_ref[...]), g2_ref[...], b2_ref[...])
    h2_ref[...] = h2
    h2b = h2.astype(BF16)

    h_lo = (h2 - h2b.astype(F32)).astype(BF16)
    rw = rw_ref[...]
    rw_hi = rw.astype(BF16)
    rw_lo = (rw - rw_hi.astype(F32)).astype(BF16)
    logits = _dot(h2b, rw_hi) + _dot(h2b, rw_lo) + _dot(h_lo, rw_hi) + rb_ref[...]

    n_e = logits.shape[1]
    eidx = lax.broadcasted_iota(jnp.int32, logits.shape, 1).astype(F32)
    lane = lax.broadcasted_iota(jnp.int32, (tm, LANES), 1)
    route = jnp.zeros((tm, LANES), F32)
    work = logits
    vals = []
    sels = []
    for k in range(TOP_K):
        mx = jnp.max(work, axis=-1, keepdims=True)
        sel = jnp.min(jnp.where(work == mx, eidx, float(n_e)), axis=-1, keepdims=True)
        work = jnp.where(eidx == sel, -jnp.inf, work)
        vals.append(mx)
        sels.append(sel)
        route = jnp.where(lane == TOP_K + k, sel, route)
    ex = [jnp.exp(v - vals[0]) for v in vals]
    den = ex[0] + ex[1] + ex[2] + ex[3]
    for k in range(TOP_K):
        route = jnp.where(lane == k, ex[k] / den, route)

    @pl.when(i == 0)
    def _():
        cnt_scr[...] = jnp.zeros_like(cnt_scr)

    lane_f = lane.astype(F32)
    hit = [lane_f == sels[k] for k in range(TOP_K)]
    cnt = sum(h.astype(F32) for h in hit)
    r_io = lax.broadcasted_iota(jnp.int32, (tm, tm), 0)
    c_io = lax.broadcasted_iota(jnp.int32, (tm, tm), 1)
    ltri = jnp.where(r_io > c_io, 1.0, 0.0).astype(BF16)
    excl = _dot(ltri, cnt.astype(BF16)) + cnt_scr[...]
    for k in range(TOP_K):
        rank = jnp.sum(jnp.where(hit[k], excl, 0.0), axis=-1, keepdims=True)
        route = jnp.where(lane == 2 * TOP_K + k, rank, route)
    route_ref[...] = route
    cnt_scr[...] = cnt_scr[...] + jnp.sum(cnt, axis=0, keepdims=True)
    counts_ref[...] = cnt_scr[...]


def _mem_attn(h1, t_prompt, mem_kp, mem_vp, l_prompt, mem_ks, mem_vs, l_sample, wq, wo, g2, b2, rw, rb,
              alpha):
    T, D = h1.shape
    hd = 128
    tm = MEM_TM
    assert l_prompt % tm == 0 and tm % l_sample == 0
    n_p = t_prompt // tm
    n_s = (T - t_prompt) // tm
    per_b = l_prompt // tm
    sub_s = tm // l_sample
    n_mem = mem_kp.shape[1]
    bp = mem_kp.shape[0]
    const = lambda i: (0, 0)
    single = pl.Buffered(1)
    mem_p = pl.BlockSpec((1, n_mem, M_HEADS * hd), lambda i: (jnp.minimum(i // per_b, bp - 1), 0, 0))
    bs = mem_ks.shape[0]
    mem_ks = mem_ks.reshape(bs, n_mem * M_HEADS, hd)
    mem_vs = mem_vs.reshape(bs, n_mem * M_HEADS, hd)
    mem_s = pl.BlockSpec((sub_s, n_mem * M_HEADS, hd), lambda i: (_hi(i, n_p), 0, 0))
    kern = functools.partial(_mem_kernel, n_p=n_p, alpha=alpha, hd=hd)
    return pl.pallas_call(
        kern,
        grid=(n_p + n_s,),
        in_specs=[
            pl.BlockSpec((tm, D), lambda i: (i, 0)),
            mem_p, mem_p, mem_s, mem_s,
            pl.BlockSpec(wq.shape, const, pipeline_mode=single),
            pl.BlockSpec(wo.shape, const, pipeline_mode=single),
            pl.BlockSpec((1, D), const),
            pl.BlockSpec((1, D), const),
            pl.BlockSpec(rw.shape, const, pipeline_mode=single),
            pl.BlockSpec((1, rw.shape[1]), const),
        ],
        out_specs=[
            pl.BlockSpec((tm, D), lambda i: (i, 0)),
            pl.BlockSpec((tm, LANES), lambda i: (i, 0)),
            pl.BlockSpec((1, LANES), const),
        ],
        out_shape=[
            jax.ShapeDtypeStruct((T, D), F32),
            jax.ShapeDtypeStruct((T, LANES), F32),
            jax.ShapeDtypeStruct((1, LANES), F32),
        ],
        scratch_shapes=[
            pltpu.VMEM((tm, M_HEADS * hd), BF16),
            pltpu.VMEM((tm, M_HEADS * hd), BF16),
            pltpu.VMEM((1, LANES), F32),
        ],
        compiler_params=_cparams(("arbitrary",)),
        name="mem_attn",
    )(h1, mem_kp, mem_vp, mem_ks, mem_vs, wq, wo, g2, b2, rw, rb)


def _mem_kv_kernel(x_ref, w_ref, o_ref):
    o_ref[...] = _dot(x_ref[...].astype(BF16), w_ref[...])


def _mem_kv(mem, w):
    R, D = mem.shape
    N = w.shape[1]
    tm = 256
    return pl.pallas_call(
        _mem_kv_kernel,
        grid=(R // tm,),
        in_specs=[pl.BlockSpec((tm, D), lambda i: (i, 0)), pl.BlockSpec((D, N), lambda i: (0, 0))],
        out_specs=pl.BlockSpec((tm, N), lambda i: (i, 0)),
        out_shape=jax.ShapeDtypeStruct((R, N), F32),
        compiler_params=_cparams(("parallel",)),
        name="mem_kv",
    )(mem, w)


def _row_copy(src_hbm, dst, sem, src_row, dst_row):
    return pltpu.make_async_copy(src_hbm.at[pl.ds(src_row, 1)], dst.at[pl.ds(dst_row, 1)], sem)


def _rows_to_slabs(x, dst):
    n = x.shape[0]
    n_slab = x.shape[1] // LANES
    for s in range(n_slab):
        dst[pl.ds(s, n, stride=n_slab), :] = x[:, s * LANES:(s + 1) * LANES]


def _slabs_to_rows(src, n_slab):
    n = src.shape[0] // n_slab
    return jnp.concatenate([src[pl.ds(s, n, stride=n_slab), :] for s in range(n_slab)], axis=1)


def _dispatch_kernel(dest_ref, pend_ref, padded_ref, nu_ref, h_ref, xs_hbm, stage, zeros, sem, zsem,
                     *, tm, bm, n_steps, n_blocks):
    i = pl.program_id(0)
    slot = i % 2

    n_slab = h_ref.shape[1] // LANES

    def zero_block(off):
        off = pl.multiple_of(off * n_slab, bm * n_slab)
        return pltpu.make_async_copy(zeros, xs_hbm.at[pl.ds(off, bm * n_slab)], zsem)

    def zero_fill(act):
        for e in range(N_EXPERTS):
            @pl.when(padded_ref[e] > 0)
            def _(e=e):
                act(zero_block(pend_ref[e] - bm))
        for b in range(n_blocks - N_EXPERTS, n_blocks):
            @pl.when(b >= nu_ref[0])
            def _(b=b):
                act(zero_block(b * bm))

    @pl.when(i == 0)
    def _():
        zeros[...] = jnp.zeros_like(zeros)
        zero_fill(lambda c: c.start())
        zero_fill(lambda c: c.wait())

    def wait(s):
        for _ in range(TOP_K):
            pltpu.make_async_copy(stage.at[s], xs_hbm.at[pl.ds(0, tm * n_slab)], sem.at[s]).wait()

    @pl.when(i >= 2)
    def _():
        wait(slot)

    _rows_to_slabs(h_ref[...], stage.at[slot])

    def body(j, carry):
        t0 = j * SUBLANES
        for u in range(SUBLANES):
            src = stage.at[slot, pl.ds(pl.multiple_of((t0 + u) * n_slab, n_slab), n_slab)]
            for k in range(TOP_K):
                row = dest_ref[(i * tm + t0 + u) * TOP_K + k]
                dst = xs_hbm.at[pl.ds(pl.multiple_of(row * n_slab, n_slab), n_slab)]
                pltpu.make_async_copy(src, dst, sem.at[slot]).start(priority=k % 2)
        return carry
    lax.fori_loop(0, tm // SUBLANES, body, 0)

    @pl.when(i == n_steps - 1)
    def _():
        wait(slot)
        if n_steps > 1:
            wait(1 - slot)


def _moe_dispatch(h2, dest, pad_end, padded, n_used, n_rows, bm):
    T, D = h2.shape
    tm = GATHER_ROWS
    n_steps = T // tm
    n_slab = D // LANES
    kern = functools.partial(_dispatch_kernel, tm=tm, bm=bm, n_steps=n_steps, n_blocks=n_rows // bm)
    return pl.pallas_call(
        kern,
        grid_spec=pltpu.PrefetchScalarGridSpec(
            num_scalar_prefetch=4,
            grid=(n_steps,),
            in_specs=[pl.BlockSpec((tm, D), lambda i, d, pe, pd, nu: (i, 0))],
            out_specs=pl.BlockSpec(memory_space=pl.ANY),
            scratch_shapes=[
                pltpu.VMEM((2, tm * n_slab, LANES), F32),
                pltpu.VMEM((bm * n_slab, LANES), F32),
                pltpu.SemaphoreType.DMA((2,)),
                pltpu.SemaphoreType.DMA(()),
            ],
        ),
        out_shape=jax.ShapeDtypeStruct((n_rows * n_slab, LANES), F32),
        compiler_params=_cparams(("arbitrary",)),
        name="moe_dispatch",
    )(dest, pad_end, padded, n_used, h2)


def _cast_rows(src, dst):
    rows = CAST_ROWS

    def body(i, carry):
        r = pl.multiple_of(i * rows, rows)
        dst[pl.ds(r, rows), :] = src[pl.ds(r, rows), :].astype(BF16)
        return carry
    lax.fori_loop(0, src.shape[0] // rows, body, 0)


def _expert_weights(be_ref, first_ref, next_ref, copies, cast, m):
    @pl.when(first_ref[m] == 1)
    def _():
        @pl.when(m == 0)
        def _():
            for c in copies(be_ref[0]):
                c.start()

        for c in copies(be_ref[m]):
            c.wait()
        cast()
        ne = next_ref[m]

        @pl.when(ne >= 0)
        def _():
            for c in copies(ne):
                c.start()


def _moe_up_kernel(be_ref, nu_ref, first_ref, next_ref, x_ref, w1_hbm, b_ref, a_ref,
                   stage, w_scr, sem):
    m = pl.program_id(0)
    ff = w_scr.shape[2]

    def copies(e):
        return [pltpu.make_async_copy(w1_hbm.at[e, :, pl.ds(t * ff, ff)], stage.at[t], sem)
                for t in range(2)]

    def cast():
        for t in range(2):
            _cast_rows(stage.at[t], w_scr.at[t])

    _expert_weights(be_ref, first_ref, next_ref, copies, cast, m)

    @pl.when(m < nu_ref[0])
    def _():
        x = _slabs_to_rows(x_ref, w_scr.shape[1] // LANES).astype(BF16)
        half = ff // 2
        for c in range(2):
            cols = slice(c * half, (c + 1) * half)
            lin_cols = slice(ff + c * half, ff + (c + 1) * half)
            u_glu = jnp.minimum(_dot(x, w_scr[0, :, cols]) + b_ref[0, :, cols], SWIGLU_LIMIT)
            u_lin = jnp.clip(_dot(x, w_scr[1, :, cols]) + b_ref[0, :, lin_cols],
                             -SWIGLU_LIMIT, SWIGLU_LIMIT)
            a = u_glu * jax.nn.sigmoid(SWIGLU_ALPHA * u_glu) * (u_lin + 1.0)
            a_ref[:, cols] = a.astype(BF16)

    @pl.when(m >= nu_ref[0])
    def _():
        a_ref[...] = jnp.zeros_like(a_ref)


def _moe_up(xs, w1, b1, plan):
    D = w1.shape[1]
    n_slab = D // LANES
    n_rows = xs.shape[0] // n_slab
    F = w1.shape[2] // 2
    bm = MOE_BM
    return pl.pallas_call(
        _moe_up_kernel,
        grid_spec=pltpu.PrefetchScalarGridSpec(
            num_scalar_prefetch=4,
            grid=(n_rows // bm,),
            in_specs=[
                pl.BlockSpec((bm * n_slab, LANES),
                             lambda m, be, nu, fi, nx: (jnp.minimum(m, nu[0] - 1), 0)),
                pl.BlockSpec(memory_space=pl.ANY),
                pl.BlockSpec((1, 1, 2 * F), lambda m, be, nu, fi, nx: (be[m], 0, 0)),
            ],
            out_specs=pl.BlockSpec((bm, F), lambda m, be, nu, fi, nx: (m, 0)),
            scratch_shapes=[
                pltpu.VMEM((2, D, F), F32),
                pltpu.VMEM((2, D, F), BF16),
                pltpu.SemaphoreType.DMA(()),
            ],
        ),
        out_shape=jax.ShapeDtypeStruct((n_rows, F), BF16),
        compiler_params=_cparams(("arbitrary",)),
        name="moe_up",
    )(*plan, xs, w1, b1)


def _moe_down_kernel(be_ref, nu_ref, first_ref, next_ref, a_ref, w2_hbm, b_ref, y_ref,
                     stage, w_scr, sem):
    m = pl.program_id(0)

    def copies(e):
        return [pltpu.make_async_copy(w2_hbm.at[e], stage, sem)]

    def cast():
        _cast_rows(stage, w_scr)

    _expert_weights(be_ref, first_ref, next_ref, copies, cast, m)

    @pl.when(m < nu_ref[0])
    def _():
        y_ref[...] = _dot(a_ref[...], w_scr[...]) + b_ref[0]

    @pl.when(m >= nu_ref[0])
    def _():
        y_ref[...] = jnp.zeros_like(y_ref)


def _moe_down(a, w2, b2, plan):
    n_rows, F = a.shape
    D = w2.shape[2]
    bm = MOE_BM
    return pl.pallas_call(
        _moe_down_kernel,
        grid_spec=pltpu.PrefetchScalarGridSpec(
            num_scalar_prefetch=4,
            grid=(n_rows // bm,),
            in_specs=[
                pl.BlockSpec((bm, F), lambda m, be, nu, fi, nx: (jnp.minimum(m, nu[0] - 1), 0)),
                pl.BlockSpec(memory_space=pl.ANY),
                pl.BlockSpec((1, 1, D), lambda m, be, nu, fi, nx: (be[m], 0, 0)),
            ],
            out_specs=pl.BlockSpec((bm, D), lambda m, be, nu, fi, nx: (m, 0)),
            scratch_shapes=[
                pltpu.VMEM((F, D), F32),
                pltpu.VMEM((F, D), BF16),
                pltpu.SemaphoreType.DMA(()),
            ],
        ),
        out_shape=jax.ShapeDtypeStruct((n_rows, D), F32),
        compiler_params=_cparams(("arbitrary",)),
        name="moe_down",
    )(*plan, a, w2, b2)


def _combine_kernel(dest_ref, h2_ref, rt_ref, g_ref, b_ref, ys_hbm, op_ref, os_ref, buf, sem,
                    *, tm, n_steps, n_prompt, alpha):
    i = pl.program_id(0)

    def issue(step, slot):
        def body(j, carry):
            t0 = pl.multiple_of(j * SUBLANES, SUBLANES)
            for u in range(SUBLANES):
                for k in range(TOP_K):
                    row = dest_ref[(step * tm + t0 + u) * TOP_K + k]
                    _row_copy(ys_hbm, buf.at[slot, k], sem.at[slot], row, t0 + u).start(priority=k % 2)
            return carry
        lax.fori_loop(0, tm // SUBLANES, body, 0)

    def wait(slot):
        for k in range(TOP_K):
            pltpu.make_async_copy(ys_hbm.at[pl.ds(0, tm)], buf.at[slot, k], sem.at[slot]).wait()

    slot = i % 2

    @pl.when(i == 0)
    def _():
        issue(0, 0)

    @pl.when(i + 1 < n_steps)
    def _():
        issue(i + 1, 1 - slot)

    wait(slot)
    y = rt_ref[:, 0:1] * buf[slot, 0]
    for k in range(1, TOP_K):
        y = y + rt_ref[:, k:k + 1] * buf[slot, k]
    out = _layer_norm(alpha * h2_ref[...] + y, g_ref[...], b_ref[...])

    @pl.when(i < n_prompt)
    def _():
        op_ref[...] = out

    @pl.when(i >= n_prompt)
    def _():
        os_ref[...] = out


def _moe_combine(ys, dest, h2, route, g3, b3, t_prompt, alpha):
    T, D = h2.shape
    tm = COMBINE_TM
    n_steps = T // tm
    n_prompt = t_prompt // tm
    kern = functools.partial(_combine_kernel, tm=tm, n_steps=n_steps, n_prompt=n_prompt, alpha=alpha)
    return pl.pallas_call(
        kern,
        grid_spec=pltpu.PrefetchScalarGridSpec(
            num_scalar_prefetch=1,
            grid=(n_steps,),
            in_specs=[
                pl.BlockSpec((tm, D), lambda i, d: (i, 0)),
                pl.BlockSpec((tm, LANES), lambda i, d: (i, 0)),
                pl.BlockSpec((1, D), lambda i, d: (0, 0)),
                pl.BlockSpec((1, D), lambda i, d: (0, 0)),
                pl.BlockSpec(memory_space=pl.ANY),
            ],
            out_specs=[
                pl.BlockSpec((tm, D), lambda i, d: (jnp.minimum(i, n_prompt - 1), 0)),
                pl.BlockSpec((tm, D), lambda i, d: (jnp.maximum(i - n_prompt, 0), 0)),
            ],
            scratch_shapes=[pltpu.VMEM((2, TOP_K, tm, D), F32), pltpu.SemaphoreType.DMA((2,))],
        ),
        out_shape=[
            jax.ShapeDtypeStruct((t_prompt, D), F32),
            jax.ShapeDtypeStruct((T - t_prompt, D), F32),
        ],
        compiler_params=_cparams(("arbitrary",)),
        name="moe_combine",
    )(dest, h2, route, g3, b3, ys)


def _moe_plan(route, counts, bm):
    T = route.shape[0]
    top_idx = route[:, TOP_K:2 * TOP_K].astype(jnp.int32)
    rank = route[:, 2 * TOP_K:3 * TOP_K].astype(jnp.int32)
    counts = counts[0, :N_EXPERTS].astype(jnp.int32)
    padded = (counts + bm - 1) // bm * bm
    pad_end = jnp.cumsum(padded)
    pad_start = pad_end - padded
    experts = jnp.arange(N_EXPERTS, dtype=jnp.int32)
    start_of = jnp.sum(jnp.where(top_idx[..., None] == experts, pad_start, 0), axis=-1)
    dest = (start_of + rank).reshape(-1)
    n_blocks = -(-(T * TOP_K) // bm) + N_EXPERTS
    block_start = jnp.arange(n_blocks, dtype=jnp.int32) * bm
    block_e = jnp.sum((pad_end[None, :] <= block_start[:, None]).astype(jnp.int32), axis=1)
    block_e = jnp.minimum(block_e, N_EXPERTS - 1)
    n_used = (pad_end[-1] // bm).astype(jnp.int32).reshape(1)
    blk = jnp.arange(n_blocks, dtype=jnp.int32)
    prev_e = jnp.concatenate([jnp.full((1,), -1, jnp.int32), block_e[:-1]])
    first = jnp.logical_and(blk < n_used[0], block_e != prev_e)
    later = jnp.logical_and(first[None, :], blk[None, :] > blk[:, None])
    nxt_blk = jnp.min(jnp.where(later, blk[None, :], n_blocks), axis=1)
    next_e = jnp.where(nxt_blk < n_blocks, block_e[jnp.minimum(nxt_blk, n_blocks - 1)], -1)
    plan = (block_e, n_used, first.astype(jnp.int32), next_e.astype(jnp.int32))
    return dest, pad_end, padded, plan, n_blocks * bm


def kernel(x_prompt, x_sample, cache_diff_k, cache_diff_v, state_ret, cache_mem_k, cache_mem_v, mem_prompt, ln_in_g, ln_in_b, w_in, b_gate, diff_lambda, diff_subln, w_proj_ret, w_proj_diff, w_out, ln1_g, ln1_b, w_mq, w_mk, w_mv, w_mo, ln2_g, ln2_b, router_w, router_b, w1, b1, w2, b2, ln3_g, ln3_b):
    Bp, Lp, D = x_prompt.shape
    Bs, Ls, _ = x_sample.shape
    depth = w_in.shape[0]
    assert depth == 1
    past = cache_diff_k.shape[2]
    n_mem = mem_prompt.shape[1]
    Tp, Ts = Bp * Lp, Bs * Ls
    alpha = (2.0 * depth) ** 0.25
    lambda_init = 0.8 - 0.6 * math.exp(-0.3 * 0)
    row = lambda v: v.reshape(1, -1)

    xp = x_prompt.reshape(Tp, D)
    xs_in = x_sample.reshape(Ts, D)
    kv_blk = 4096 // IN_TN
    P, dk_p, dv_p, dk_s, dv_s = _in_proj(xp, xs_in, row(ln_in_g), row(ln_in_b), w_in[0].astype(BF16),
                                         kv_blk)

    zeros_state = jnp.zeros((Bp,) + state_ret.shape[2:], F32)
    yr_p, s_p = _retention(P, 0, Bp, Lp, 0, zeros_state)
    yr_s, s_s = _retention(P, Tp, Bs, Ls, past, state_ret[0])

    yd_p = _attn_prompt(P, Bp, Lp, diff_lambda[0], row(diff_subln[0]), lambda_init)
    yd_s = _attn_sample(P, Tp, Bs, Ls, cache_diff_k[0], cache_diff_v[0], diff_lambda[0],
                        row(diff_subln[0]), lambda_init)

    h1 = _mix(xp, xs_in, yr_p, yr_s, yd_p, yd_s, P, row(ln_in_g), row(ln_in_b), row(b_gate[0]),
              w_proj_ret[0].astype(BF16), w_proj_diff[0].astype(BF16), w_out[0].astype(BF16),
              row(ln1_g[0]), row(ln1_b[0]), alpha)

    w_mkv = jnp.concatenate([w_mk[0], w_mv[0]], axis=1).astype(BF16)
    mkv = _mem_kv(mem_prompt.reshape(Bp * n_mem, D), w_mkv)
    hm = w_mk.shape[2]
    mk_p = mkv[:, :hm].reshape(Bp, n_mem, hm)
    mv_p = mkv[:, hm:].reshape(Bp, n_mem, hm)
    h2, route, counts = _mem_attn(
        h1, Tp, mk_p, mv_p, Lp, cache_mem_k[0], cache_mem_v[0], Ls, w_mq[0].astype(BF16),
        w_mo[0].astype(BF16),
        row(ln2_g[0]), row(ln2_b[0]), router_w[0], row(router_b[0]), alpha)

    dest, pad_end, padded, plan, n_rows = _moe_plan(route, counts, MOE_BM)
    xs = _moe_dispatch(h2, dest, pad_end, padded, plan[1], n_rows, MOE_BM)
    act = _moe_up(xs, w1[0], b1[0][:, None, :], plan)
    ys = _moe_down(act, w2[0], b2[0][:, None, :], plan)
    out_p, out_s = _moe_combine(ys, dest, h2, route, row(ln3_g[0]), row(ln3_b[0]), Tp, alpha)

    return (
        out_p.reshape(Bp, Lp, D),
        out_s.reshape(Bs, Ls, D),
        dk_p.reshape(1, Bp, Lp, D_HEADS, 128),
        dv_p.reshape(1, Bp, Lp, D_HEADS, 128),
        s_p[None],
        mk_p.reshape(1, Bp, n_mem, M_HEADS, hm // M_HEADS),
        mv_p.reshape(1, Bp, n_mem, M_HEADS, hm // M_HEADS),
        dk_s.reshape(1, Bs, Ls, D_HEADS, 128),
        dv_s.reshape(1, Bs, Ls, D_HEADS, 128),
        s_s[None],
    )
```

```python
import functools
import math

import jax
import jax.numpy as jnp
from jax import lax
from jax.experimental import pallas as pl
from jax.experimental.pallas import tpu as pltpu

F32 = jnp.float32
BF16 = jnp.bfloat16

CHUNK = 64
R_HEADS = 4
D_HEADS = 8
M_HEADS = 4
N_EXPERTS = 32
TOP_K = 4
SWIGLU_ALPHA = 1.702
SWIGLU_LIMIT = 7.0
LN_EPS = 1e-5
RMS_EPS = 1e-5
ROPE_BASE = 10000.0
LOG2E = 1.4426950408889634

LANES = 128
SUBLANES = 8
VMEM_LIMIT = 58 * 1024 * 1024

IN_TM = 512
IN_TN = 2048
RET_CHUNK = 256
ATT_BLK = 512
ATT_UNROLL = 4
MIX_TM = 256
MIX_PIECES = 2
MEM_TM = 512
MOE_BM = 256
CAST_ROWS = 256
GATHER_ROWS = 512
COMBINE_TM = 256


def _cparams(sem):
    return pltpu.CompilerParams(dimension_semantics=sem, vmem_limit_bytes=VMEM_LIMIT)


def _layer_norm(x, g, b):
    mu = jnp.mean(x, axis=-1, keepdims=True)
    xc = x - mu
    var = jnp.mean(xc * xc, axis=-1, keepdims=True)
    return xc * lax.rsqrt(var + LN_EPS) * g + b


def _dot(a, b):
    return jnp.dot(a, b, preferred_element_type=F32)


def _dot_nt(a, b):
    return lax.dot_general(a, b, (((1,), (1,)), ((), ())), preferred_element_type=F32)


def _dot_tn(a, b):
    return lax.dot_general(a, b, (((0,), (0,)), ((), ())), preferred_element_type=F32)


def _lo(i, n):
    return jnp.minimum(i, n - 1)


def _hi(i, n):
    return jnp.maximum(i - n, 0)


def _in_proj_kernel(xp_ref, xs_ref, g_ref, b_ref, w_ref, p_ref, dkp_ref, dvp_ref, dks_ref, dvs_ref,
                    h_scr, *, n_p):
    i = pl.program_id(0)
    j = pl.program_id(1)
    half = dkp_ref.shape[1]

    @pl.when(jnp.logical_and(j == 0, i < n_p))
    def _():
        h_scr[...] = _layer_norm(xp_ref[...], g_ref[...], b_ref[...]).astype(BF16)

    @pl.when(jnp.logical_and(j == 0, i >= n_p))
    def _():
        h_scr[...] = _layer_norm(xs_ref[...], g_ref[...], b_ref[...]).astype(BF16)

    acc = _dot(h_scr[...], w_ref[...])
    p_ref[...] = acc.astype(BF16)

    @pl.when(jnp.logical_and(j == 0, i < n_p))
    def _():
        dkp_ref[...] = acc[:, :half]
        dvp_ref[...] = acc[:, half:]

    @pl.when(jnp.logical_and(j == 0, i >= n_p))
    def _():
        dks_ref[...] = acc[:, :half]
        dvs_ref[...] = acc[:, half:]


def _in_proj(xp, xs, g, b, w, kv_blk):
    Tp, D = xp.shape
    Ts = xs.shape[0]
    N = w.shape[1]
    n_p, n_s = Tp // IN_TM, Ts // IN_TM
    half = IN_TN // 2

    def col(j):
        return jnp.where(j == 0, kv_blk, jnp.where(j <= kv_blk, j - 1, j))

    single = pl.Buffered(1)
    kv_p = pl.BlockSpec((IN_TM, half), lambda i, j: (_lo(i, n_p), 0))
    kv_s = pl.BlockSpec((IN_TM, half), lambda i, j: (_hi(i, n_p), 0), pipeline_mode=single)
    return pl.pallas_call(
        functools.partial(_in_proj_kernel, n_p=n_p),
        grid=(n_p + n_s, N // IN_TN),
        in_specs=[
            pl.BlockSpec((IN_TM, D), lambda i, j: (_lo(i, n_p), 0)),
            pl.BlockSpec((IN_TM, D), lambda i, j: (_hi(i, n_p), 0), pipeline_mode=single),
            pl.BlockSpec((1, D), lambda i, j: (0, 0)),
            pl.BlockSpec((1, D), lambda i, j: (0, 0)),
            pl.BlockSpec((D, IN_TN), lambda i, j: (0, col(j))),
        ],
        out_specs=[pl.BlockSpec((IN_TM, IN_TN), lambda i, j: (i, j)), kv_p, kv_p, kv_s, kv_s],
        out_shape=[
            jax.ShapeDtypeStruct((Tp + Ts, N), BF16),
            jax.ShapeDtypeStruct((Tp, half), F32),
            jax.ShapeDtypeStruct((Tp, half), F32),
            jax.ShapeDtypeStruct((Ts, half), F32),
            jax.ShapeDtypeStruct((Ts, half), F32),
        ],
        scratch_shapes=[pltpu.VMEM((IN_TM, D), BF16)],
        compiler_params=_cparams(("arbitrary", "arbitrary")),
        name="in_proj",
    )(xp, xs, g, b, w)


P_DK = 0
P_DV = 8
P_RQ = 16
P_RK = 20
P_RV = 24
P_RG = 32
P_DQ = 40
P_GATE = 48


def _retention_kernel(q_ref, k_ref, v_ref, g_ref, cos_ref, sin_ref, dm_ref, qd_ref, kd_ref,
                      bd_ref, s0_ref, y_ref, sout_ref, s_scr, *, n_chunks, dk, dv):
    c = pl.program_id(1)

    @pl.when(c == 0)
    def _():
        s_scr[...] = s0_ref[0]

    cos = cos_ref[...]
    sin = sin_ref[...]
    k_scale = dk ** -0.5
    for h in range(R_HEADS):
        q = q_ref[:, h * dk:(h + 1) * dk].astype(F32)
        k = k_ref[:, h * dk:(h + 1) * dk].astype(F32)
        v = v_ref[:, h * dv:(h + 1) * dv]
        g = g_ref[:, h * dv:(h + 1) * dv].astype(F32)
        q = q * cos + pltpu.roll(q, dk // 2, 1) * sin
        k = (k * cos + pltpu.roll(k, dk // 2, 1) * sin) * k_scale
        qb = q.astype(BF16)
        kb = k.astype(BF16)
        s = s_scr[h]
        scores = _dot_nt(qb, kb) * dm_ref[h]
        o = _dot(scores.astype(BF16), v) + _dot((q * qd_ref[h]).astype(BF16), s.astype(BF16))
        s_scr[h] = bd_ref[h] * s + _dot_tn((k * kd_ref[h]).astype(BF16), v)
        o = o * lax.rsqrt(jnp.mean(o * o, axis=-1, keepdims=True) + RMS_EPS)
        y_ref[:, h * dv:(h + 1) * dv] = (g * jax.nn.sigmoid(g) * o).astype(BF16)

    @pl.when(c == n_chunks - 1)
    def _():
        sout_ref[0] = s_scr[...]


def _retention(P, row_off, B, L, pos0, S0):
    dk, dv = 128, 256
    C = min(L, RET_CHUNK)
    nc = L // C
    ob = row_off // C
    pos = (pos0 + jnp.arange(L, dtype=jnp.int32)).astype(F32)
    inv = 1.0 / (ROPE_BASE ** jnp.linspace(0.0, 1.0, dk // 2, dtype=F32))
    ang = pos[:, None] * inv[None, :]
    cos = jnp.concatenate([jnp.cos(ang), jnp.cos(ang)], axis=-1)
    sin = jnp.concatenate([-jnp.sin(ang), jnp.sin(ang)], axis=-1)
    log_g = jnp.log1p(-jnp.power(2.0, -5.0 - jnp.arange(R_HEADS, dtype=F32)))
    i = jnp.arange(C, dtype=F32)
    rel = i[:, None] - i[None, :]
    dmask = jnp.where(rel >= 0, jnp.exp(log_g[:, None, None] * jnp.maximum(rel, 0.0)), 0.0)
    q_decay = jnp.exp(log_g[:, None] * (i + 1.0))[..., None]
    k_decay = jnp.exp(log_g[:, None] * (C - 1.0 - i))[..., None]
    b_decay = jnp.exp(log_g * C)[:, None, None]

    def rows(b, c):
        return ob + b * nc + c

    kern = functools.partial(_retention_kernel, n_chunks=nc, dk=dk, dv=dv)
    return pl.pallas_call(
        kern,
        grid=(B, nc),
        in_specs=[
            pl.BlockSpec((C, R_HEADS * dk), lambda b, c: (rows(b, c), P_RQ * LANES // (R_HEADS * dk))),
            pl.BlockSpec((C, R_HEADS * dk), lambda b, c: (rows(b, c), P_RK * LANES // (R_HEADS * dk))),
            pl.BlockSpec((C, R_HEADS * dv), lambda b, c: (rows(b, c), P_RV * LANES // (R_HEADS * dv))),
            pl.BlockSpec((C, R_HEADS * dv), lambda b, c: (rows(b, c), P_RG * LANES // (R_HEADS * dv))),
            pl.BlockSpec((C, dk), lambda b, c: (c, 0)),
            pl.BlockSpec((C, dk), lambda b, c: (c, 0)),
            pl.BlockSpec((R_HEADS, C, C), lambda b, c: (0, 0, 0)),
            pl.BlockSpec((R_HEADS, C, 1), lambda b, c: (0, 0, 0)),
            pl.BlockSpec((R_HEADS, C, 1), lambda b, c: (0, 0, 0)),
            pl.BlockSpec((R_HEADS, 1, 1), lambda b, c: (0, 0, 0)),
            pl.BlockSpec((1, R_HEADS, dk, dv), lambda b, c: (b, 0, 0, 0)),
        ],
        out_specs=[
            pl.BlockSpec((C, R_HEADS * dv), lambda b, c: (b * nc + c, 0)),
            pl.BlockSpec((1, R_HEADS, dk, dv), lambda b, c: (b, 0, 0, 0)),
        ],
        out_shape=[
            jax.ShapeDtypeStruct((B * L, R_HEADS * dv), BF16),
            jax.ShapeDtypeStruct((B, R_HEADS, dk, dv), F32),
        ],
        scratch_shapes=[pltpu.VMEM((R_HEADS, dk, dv), F32)],
        compiler_params=_cparams(("parallel", "arbitrary")),
        name="retention",
    )(P, P, P, P, cos, sin, dmask, q_decay, k_decay, b_decay, S0)


def _diff_lambda(lam_ref, lambda_init):
    lv = lam_ref[...]
    a = jnp.sum(lv[0:1] * lv[1:2], axis=-1, keepdims=True)
    b = jnp.sum(lv[2:3] * lv[3:4], axis=-1, keepdims=True)
    return jnp.exp(a) - jnp.exp(b) + lambda_init


def _lane_tile(x, n):
    return jnp.concatenate([x] * n, axis=1)


def _stack_maps(q, hd):
    lane = lax.broadcasted_iota(jnp.int32, q.shape, 1)
    zero = jnp.zeros_like(q)
    return jnp.concatenate([jnp.where(lane < hd, q, zero), jnp.where(lane < hd, zero, q)], axis=0)


def _diff_finish(acc, l, lam, subln, lambda_init, tq):
    o = acc[:tq] / l[:tq] - lam * (acc[tq:] / l[tq:])
    o = o * lax.rsqrt(jnp.mean(o * o, axis=-1, keepdims=True) + RMS_EPS)
    return (o * subln * (1.0 - lambda_init)).astype(BF16)


def _attn_prompt_kernel(q_ref, k_ref, v_ref, lam_ref, sub_ref, y_ref, qs_scr, ve_scr, m_scr, acc_scr,
                        s_scr, *, blk, hd, lambda_init):
    qi = pl.program_id(2)
    dv = 2 * hd
    seq = v_ref.shape[0]
    sub = 2 * blk
    lo, hi, every = slice(0, sub), slice(sub, 2 * sub), slice(0, 2 * sub)

    @pl.when(qi == 0)
    def _():
        ve_scr[:, :dv] = v_ref[...]
        ve_scr[:, dv:] = jnp.ones((seq, dv), BF16)

    for s_idx, rows in enumerate((lo, hi)):
        q = q_ref[s_idx * blk:(s_idx + 1) * blk, :].astype(F32) * (hd ** -0.5 * LOG2E)
        qs_scr[rows] = _stack_maps(q.astype(BF16), hd)
    m_scr[...] = jnp.full_like(m_scr, -jnp.inf)
    acc_scr[...] = jnp.zeros_like(acc_scr)

    def scores(ki, slot, rows=every):
        off = pl.multiple_of(ki * blk, blk)
        s_scr[slot, rows] = _dot_nt(qs_scr[rows], k_ref[pl.ds(off, blk), :])

    def step(ki, slot, rows=every, diag=False):
        off = pl.multiple_of(ki * blk, blk)
        s = s_scr[slot, rows]
        if diag:
            row = lax.broadcasted_iota(jnp.int32, s.shape, 0)
            col = lax.broadcasted_iota(jnp.int32, s.shape, 1)
            shift = CHUNK.bit_length() - 1
            seen = (col >> shift) <= ((row & (blk - 1)) >> shift)
            if s.shape[0] > sub:
                seen = jnp.logical_or(row >= sub, seen)
            s = jnp.where(seen, s, -1e30)
        m_prev = m_scr[rows]
        m_new = jnp.maximum(m_prev, jnp.max(s, axis=-1, keepdims=True))
        alpha = jnp.exp2(m_prev - m_new)
        p = jnp.exp2(s - _lane_tile(m_new, blk // LANES))
        pv = _dot(p.astype(BF16), ve_scr[pl.ds(off, blk), :])
        acc_scr[rows] = _lane_tile(alpha, 2 * dv // LANES) * acc_scr[rows] + pv
        m_scr[rows] = m_new

    def run(base, width):
        for u in range(width):
            scores(base + u + 1, (u + 1) % 2)
            step(base + u, u % 2)

    def body(j, carry):
        run(ATT_UNROLL * j, ATT_UNROLL)
        return carry

    n_open = 2 * qi
    scores(0, 0)
    n_main = n_open // ATT_UNROLL
    lax.fori_loop(0, n_main, body, 0)
    done = n_main * ATT_UNROLL
    width = ATT_UNROLL // 2
    while width >= 2:
        @pl.when(((n_open - done) // width) % 2 == 1)
        def _(width=width):
            run(n_open - (n_open - done) % (2 * width), width)
        width //= 2
    scores(n_open + 1, 1, hi)
    step(n_open, 0, every, diag=True)
    step(n_open + 1, 1, hi, diag=True)

    lam = _diff_lambda(lam_ref, lambda_init)
    for s_idx in range(2):
        a1 = acc_scr[s_idx * sub:s_idx * sub + blk, :]
        a2 = acc_scr[s_idx * sub + blk:(s_idx + 1) * sub, :]
        o = a1[:, :dv] / a1[:, dv:] - lam * (a2[:, :dv] / a2[:, dv:])
        o = o * lax.rsqrt(jnp.mean(o * o, axis=-1, keepdims=True) + RMS_EPS)
        y_ref[s_idx * blk:(s_idx + 1) * blk, :] = (o * sub_ref[...] * (1.0 - lambda_init)).astype(BF16)


def _attn_prompt(P, B, S, lam_p, subln, lambda_init):
    hd = 64
    blk = ATT_BLK
    bq = 2 * blk
    assert S % bq == 0
    nq = S // bq
    w = 2 * hd
    rows = 2 * bq
    kern = functools.partial(_attn_prompt_kernel, blk=blk, hd=hd, lambda_init=lambda_init)
    return pl.pallas_call(
        kern,
        grid=(B, D_HEADS, nq),
        in_specs=[
            pl.BlockSpec((bq, w), lambda b, h, i: (b * nq + i, P_DQ + h)),
            pl.BlockSpec((S, w), lambda b, h, i: (b, P_DK + h)),
            pl.BlockSpec((S, w), lambda b, h, i: (b, P_DV + h)),
            pl.BlockSpec((4, hd), lambda b, h, i: (0, 0)),
            pl.BlockSpec((1, w), lambda b, h, i: (0, 0)),
        ],
        out_specs=pl.BlockSpec((bq, w), lambda b, h, i: (b * nq + i, h)),
        out_shape=jax.ShapeDtypeStruct((B * S, D_HEADS * w), BF16),
        scratch_shapes=[
            pltpu.VMEM((rows, w), BF16),
            pltpu.VMEM((S, 2 * w), BF16),
            pltpu.VMEM((rows, LANES), F32),
            pltpu.VMEM((rows, 2 * w), F32),
            pltpu.VMEM((2, rows, blk), F32),
        ],
        compiler_params=_cparams(("parallel", "parallel", "arbitrary")),
        name="attn_prompt",
    )(P, P, P, lam_p, subln)


def _attn_sample_kernel(q_ref, kn_ref, vn_ref, kc_ref, vc_ref, lam_ref, sub_ref, y_ref,
                        *, tq, hd, lambda_init):
    lam = _diff_lambda(lam_ref, lambda_init)
    w = 2 * hd
    past = kc_ref.shape[1] // D_HEADS
    for h in range(D_HEADS):
        cols = slice(h * w, (h + 1) * w)
        qs = _stack_maps(q_ref[:, cols] * (hd ** -0.5), hd)
        kc = kc_ref[0, pl.ds(h, past, stride=D_HEADS), :].astype(BF16)
        vc = vc_ref[0, pl.ds(h, past, stride=D_HEADS), :].astype(BF16)
        kn = kn_ref[:, cols]
        vn = vn_ref[:, cols]
        s_c = _dot_nt(qs, kc)
        s_n = _dot_nt(qs, kn)
        m = jnp.maximum(jnp.max(s_c, axis=-1, keepdims=True), jnp.max(s_n, axis=-1, keepdims=True))
        p_c = jnp.exp(s_c - m)
        p_n = jnp.exp(s_n - m)
        l = jnp.sum(p_c, axis=-1, keepdims=True) + jnp.sum(p_n, axis=-1, keepdims=True)
        acc = _dot(p_c.astype(BF16), vc) + _dot(p_n.astype(BF16), vn)
        y_ref[:, cols] = _diff_finish(acc, l, lam, sub_ref[...], lambda_init, tq)


def _attn_sample(P, row_off, B, L, cache_k, cache_v, lam_p, subln, lambda_init):
    hd = 64
    past = cache_k.shape[1]
    ob = row_off // L
    width = D_HEADS * 2 * hd
    kern = functools.partial(_attn_sample_kernel, tq=L, hd=hd, lambda_init=lambda_init)
    cache_k = cache_k.reshape(B, past * D_HEADS, 2 * hd)
    cache_v = cache_v.reshape(B, past * D_HEADS, 2 * hd)
    cache_spec = pl.BlockSpec((1, past * D_HEADS, 2 * hd), lambda b: (b, 0, 0))
    return pl.pallas_call(
        kern,
        grid=(B,),
        in_specs=[
            pl.BlockSpec((L, width), lambda b: (ob + b, P_DQ * LANES // width)),
            pl.BlockSpec((L, width), lambda b: (ob + b, P_DK * LANES // width)),
            pl.BlockSpec((L, width), lambda b: (ob + b, P_DV * LANES // width)),
            cache_spec,
            cache_spec,
            pl.BlockSpec((4, hd), lambda b: (0, 0)),
            pl.BlockSpec((1, 2 * hd), lambda b: (0, 0)),
        ],
        out_specs=pl.BlockSpec((L, width), lambda b: (b, 0)),
        out_shape=jax.ShapeDtypeStruct((B * L, width), BF16),
        compiler_params=_cparams(("parallel",)),
        name="attn_sample",
    )(P, P, P, cache_k, cache_v, lam_p, subln)


def _mix_kernel(xp_ref, xs_ref, yrp_ref, yrs_ref, ydp_ref, yds_ref, gr_ref, gd_ref, lig_ref, lib_ref,
                bg_ref, wpr_ref, wpd_ref, wo_ref, g1_ref, b1_ref, h1_ref, *, alpha, n_p):
    i = pl.program_id(0)
    d = xp_ref.shape[1]

    def compute(x_ref, yr_ref, yd_ref):
        mixed = alpha * _layer_norm(x_ref[...], lig_ref[...], lib_ref[...])
        piece = d // MIX_PIECES
        for c in range(MIX_PIECES):
            cols = slice(c * piece, (c + 1) * piece)
            g_ret = jax.nn.sigmoid(gr_ref[:, cols].astype(F32) + bg_ref[:, cols])
            g_diff = jax.nn.sigmoid(gd_ref[:, cols].astype(F32)
                                    + bg_ref[:, d + c * piece:d + (c + 1) * piece])
            merged = (g_ret * _dot(yr_ref[...], wpr_ref[:, cols])
                      + g_diff * _dot(yd_ref[...], wpd_ref[:, cols]))
            mixed = mixed + _dot(merged.astype(BF16), wo_ref[cols, :])
        h1_ref[...] = _layer_norm(mixed, g1_ref[...], b1_ref[...])

    @pl.when(i < n_p)
    def _():
        compute(xp_ref, yrp_ref, ydp_ref)

    @pl.when(i >= n_p)
    def _():
        compute(xs_ref, yrs_ref, yds_ref)


def _mix(xp, xs, yr_p, yr_s, yd_p, yd_s, P, ln_in_g, ln_in_b, b_gate, wpr, wpd, wo, g1, b1, alpha):
    Tp, D = xp.shape
    Ts = xs.shape[0]
    tm = MIX_TM
    n_p, n_s = Tp // tm, Ts // tm
    gcol = P_GATE * LANES // D
    const = lambda i: (0, 0)
    lo = lambda i: (_lo(i, n_p), 0)
    hi = lambda i: (_hi(i, n_p), 0)
    single = pl.Buffered(1)
    kern = functools.partial(_mix_kernel, alpha=alpha, n_p=n_p)
    return pl.pallas_call(
        kern,
        grid=(n_p + n_s,),
        in_specs=[
            pl.BlockSpec((tm, D), lo),
            pl.BlockSpec((tm, D), hi),
            pl.BlockSpec((tm, yr_p.shape[1]), lo),
            pl.BlockSpec((tm, yr_s.shape[1]), hi),
            pl.BlockSpec((tm, yd_p.shape[1]), lo),
            pl.BlockSpec((tm, yd_s.shape[1]), hi),
            pl.BlockSpec((tm, D), lambda i: (i, gcol)),
            pl.BlockSpec((tm, D), lambda i: (i, gcol + 1)),
            pl.BlockSpec((1, D), const),
            pl.BlockSpec((1, D), const),
            pl.BlockSpec((1, 2 * D), const),
            pl.BlockSpec(wpr.shape, const, pipeline_mode=single),
            pl.BlockSpec(wpd.shape, const, pipeline_mode=single),
            pl.BlockSpec(wo.shape, const, pipeline_mode=single),
            pl.BlockSpec((1, D), const),
            pl.BlockSpec((1, D), const),
        ],
        out_specs=pl.BlockSpec((tm, D), lambda i: (i, 0)),
        out_shape=jax.ShapeDtypeStruct((Tp + Ts, D), F32),
        compiler_params=_cparams(("arbitrary",)),
        name="mix",
    )(xp, xs, yr_p, yr_s, yd_p, yd_s, P, P, ln_in_g, ln_in_b, b_gate, wpr, wpd, wo, g1, b1)


def _mem_kernel(h1_ref, mkp_ref, mvp_ref, mks_ref, mvs_ref, wq_ref, wo_ref, g2_ref, b2_ref, rw_ref,
                rb_ref, h2_ref, route_ref, counts_ref, q_scr, o_scr, cnt_scr, *, n_p, alpha, hd):
    i = pl.program_id(0)
    tm = h1_ref.shape[0]
    h1 = h1_ref[...]
    q_scr[...] = (_dot(h1.astype(BF16), wq_ref[...]) * (hd ** -0.5)).astype(BF16)

    def attend(mk_ref, mv_ref):
        n_sub = mk_ref.shape[0]
        seg = tm // n_sub
        split_heads = mk_ref.shape[2] == hd
        for s in range(n_sub):
            for h in range(M_HEADS):
                if split_heads:
                    rows = pl.ds(h, mk_ref.shape[1] // M_HEADS, stride=M_HEADS)
                    mk = mk_ref[s, rows, :].astype(BF16)
                    mv = mv_ref[s, rows, :].astype(BF16)
                else:
                    mk = mk_ref[s, :, h * hd:(h + 1) * hd].astype(BF16)
                    mv = mv_ref[s, :, h * hd:(h + 1) * hd].astype(BF16)
                qh = q_scr[s * seg:(s + 1) * seg, h * hd:(h + 1) * hd]
                sc = _dot_nt(qh, mk)
                sc = sc - jnp.max(sc, axis=-1, keepdims=True)
                p = jnp.exp(sc)
                p = p / jnp.sum(p, axis=-1, keepdims=True)
                o_scr[s * seg:(s + 1) * seg, h * hd:(h + 1) * hd] = _dot(
                    p.astype(BF16), mv).astype(BF16)

    @pl.when(i < n_p)
    def _():
        attend(mkp_ref, mvp_ref)

    @pl.when(i >= n_p)
    def _():
        attend(mks_ref, mvs_ref)

    h2 = _layer_norm(alpha * h1 + _dot(o_scr[...], wo_ref[...]), g2_ref[...], b2_ref[...])
    h2_ref[...] = h2
    h2b = h2.astype(BF16)

    h_lo = (h2 - h2b.astype(F32)).astype(BF16)
    rw = rw_ref[...]
    rw_hi = rw.astype(BF16)
    rw_lo = (rw - rw_hi.astype(F32)).astype(BF16)
    logits = _dot(h2b, rw_hi) + _dot(h2b, rw_lo) + _dot(h_lo, rw_hi) + rb_ref[...]

    n_e = logits.shape[1]
    eidx = lax.broadcasted_iota(jnp.int32, logits.shape, 1).astype(F32)
    lane = lax.broadcasted_iota(jnp.int32, (tm, LANES), 1)
    route = jnp.zeros((tm, LANES), F32)
    work = logits
    vals = []
    sels = []
    for k in range(TOP_K):
        mx = jnp.max(work, axis=-1, keepdims=True)
        sel = jnp.min(jnp.where(work == mx, eidx, float(n_e)), axis=-1, keepdims=True)
        work = jnp.where(eidx == sel, -jnp.inf, work)
        vals.append(mx)
        sels.append(sel)
        route = jnp.where(lane == TOP_K + k, sel, route)
    ex = [jnp.exp(v - vals[0]) for v in vals]
    den = ex[0] + ex[1] + ex[2] + ex[3]
    for k in range(TOP_K):
        route = jnp.where(lane == k, ex[k] / den, route)

    @pl.when(i == 0)
    def _():
        cnt_scr[...] = jnp.zeros_like(cnt_scr)

    lane_f = lane.astype(F32)
    hit = [lane_f == sels[k] for k in range(TOP_K)]
    cnt = sum(h.astype(F32) for h in hit)
    r_io = lax.broadcasted_iota(jnp.int32, (tm, tm), 0)
    c_io = lax.broadcasted_iota(jnp.int32, (tm, tm), 1)
    ltri = jnp.where(r_io > c_io, 1.0, 0.0).astype(BF16)
    excl = _dot(ltri, cnt.astype(BF16)) + cnt_scr[...]
    for k in range(TOP_K):
        rank = jnp.sum(jnp.where(hit[k], excl, 0.0), axis=-1, keepdims=True)
        route = jnp.where(lane == 2 * TOP_K + k, rank, route)
    route_ref[...] = route
    cnt_scr[...] = cnt_scr[...] + jnp.sum(cnt, axis=0, keepdims=True)
    counts_ref[...] = cnt_scr[...]


def _mem_attn(h1, t_prompt, mem_kp, mem_vp, l_prompt, mem_ks, mem_vs, l_sample, wq, wo, g2, b2, rw, rb,
              alpha):
    T, D = h1.shape
    hd = 128
    tm = MEM_TM
    assert l_prompt % tm == 0 and tm % l_sample == 0
    n_p = t_prompt // tm
    n_s = (T - t_prompt) // tm
    per_b = l_prompt // tm
    sub_s = tm // l_sample
    n_mem = mem_kp.shape[1]
    bp = mem_kp.shape[0]
    const = lambda i: (0, 0)
    single = pl.Buffered(1)
    mem_p = pl.BlockSpec((1, n_mem, M_HEADS * hd), lambda i: (jnp.minimum(i // per_b, bp - 1), 0, 0))
    bs = mem_ks.shape[0]
    mem_ks = mem_ks.reshape(bs, n_mem * M_HEADS, hd)
    mem_vs = mem_vs.reshape(bs, n_mem * M_HEADS, hd)
    mem_s = pl.BlockSpec((sub_s, n_mem * M_HEADS, hd), lambda i: (_hi(i, n_p), 0, 0))
    kern = functools.partial(_mem_kernel, n_p=n_p, alpha=alpha, hd=hd)
    return pl.pallas_call(
        kern,
        grid=(n_p + n_s,),
        in_specs=[
            pl.BlockSpec((tm, D), lambda i: (i, 0)),
            mem_p, mem_p, mem_s, mem_s,
            pl.BlockSpec(wq.shape, const, pipeline_mode=single),
            pl.BlockSpec(wo.shape, const, pipeline_mode=single),
            pl.BlockSpec((1, D), const),
            pl.BlockSpec((1, D), const),
            pl.BlockSpec(rw.shape, const, pipeline_mode=single),
            pl.BlockSpec((1, rw.shape[1]), const),
        ],
        out_specs=[
            pl.BlockSpec((tm, D), lambda i: (i, 0)),
            pl.BlockSpec((tm, LANES), lambda i: (i, 0)),
            pl.BlockSpec((1, LANES), const),
        ],
        out_shape=[
            jax.ShapeDtypeStruct((T, D), F32),
            jax.ShapeDtypeStruct((T, LANES), F32),
            jax.ShapeDtypeStruct((1, LANES), F32),
        ],
        scratch_shapes=[
            pltpu.VMEM((tm, M_HEADS * hd), BF16),
            pltpu.VMEM((tm, M_HEADS * hd), BF16),
            pltpu.VMEM((1, LANES), F32),
        ],
        compiler_params=_cparams(("arbitrary",)),
        name="mem_attn",
    )(h1, mem_kp, mem_vp, mem_ks, mem_vs, wq, wo, g2, b2, rw, rb)


def _mem_kv_kernel(x_ref, w_ref, o_ref):
    o_ref[...] = _dot(x_ref[...].astype(BF16), w_ref[...])


def _mem_kv(mem, w):
    R, D = mem.shape
    N = w.shape[1]
    tm = 256
    return pl.pallas_call(
        _mem_kv_kernel,
        grid=(R // tm,),
        in_specs=[pl.BlockSpec((tm, D), lambda i: (i, 0)), pl.BlockSpec((D, N), lambda i: (0, 0))],
        out_specs=pl.BlockSpec((tm, N), lambda i: (i, 0)),
        out_shape=jax.ShapeDtypeStruct((R, N), F32),
        compiler_params=_cparams(("parallel",)),
        name="mem_kv",
    )(mem, w)


def _row_copy(src_hbm, dst, sem, src_row, dst_row):
    return pltpu.make_async_copy(src_hbm.at[pl.ds(src_row, 1)], dst.at[pl.ds(dst_row, 1)], sem)


def _rows_to_slabs(x, dst):
    n = x.shape[0]
    n_slab = x.shape[1] // LANES
    for s in range(n_slab):
        dst[pl.ds(s, n, stride=n_slab), :] = x[:, s * LANES:(s + 1) * LANES]


def _slabs_to_rows(src, n_slab):
    n = src.shape[0] // n_slab
    return jnp.concatenate([src[pl.ds(s, n, stride=n_slab), :] for s in range(n_slab)], axis=1)


def _dispatch_kernel(dest_ref, pend_ref, padded_ref, nu_ref, h_ref, xs_hbm, stage, zeros, sem, zsem,
                     *, tm, bm, n_steps, n_blocks):
    i = pl.program_id(0)
    slot = i % 2

    n_slab = h_ref.shape[1] // LANES

    def zero_block(off):
        off = pl.multiple_of(off * n_slab, bm * n_slab)
        return pltpu.make_async_copy(zeros, xs_hbm.at[pl.ds(off, bm * n_slab)], zsem)

    def zero_fill(act):
        for e in range(N_EXPERTS):
            @pl.when(padded_ref[e] > 0)
            def _(e=e):
                act(zero_block(pend_ref[e] - bm))
        for b in range(n_blocks - N_EXPERTS, n_blocks):
            @pl.when(b >= nu_ref[0])
            def _(b=b):
                act(zero_block(b * bm))

    @pl.when(i == 0)
    def _():
        zeros[...] = jnp.zeros_like(zeros)
        zero_fill(lambda c: c.start())
        zero_fill(lambda c: c.wait())

    def wait(s):
        for _ in range(TOP_K):
            pltpu.make_async_copy(stage.at[s], xs_hbm.at[pl.ds(0, tm * n_slab)], sem.at[s]).wait()

    @pl.when(i >= 2)
    def _():
        wait(slot)

    _rows_to_slabs(h_ref[...], stage.at[slot])

    def body(j, carry):
        t0 = j * SUBLANES
        for u in range(SUBLANES):
            src = stage.at[slot, pl.ds(pl.multiple_of((t0 + u) * n_slab, n_slab), n_slab)]
            for k in range(TOP_K):
                row = dest_ref[(i * tm + t0 + u) * TOP_K + k]
                dst = xs_hbm.at[pl.ds(pl.multiple_of(row * n_slab, n_slab), n_slab)]
                pltpu.make_async_copy(src, dst, sem.at[slot]).start(priority=k % 2)
        return carry
    lax.fori_loop(0, tm // SUBLANES, body, 0)

    @pl.when(i == n_steps - 1)
    def _():
        wait(slot)
        if n_steps > 1:
            wait(1 - slot)


def _moe_dispatch(h2, dest, pad_end, padded, n_used, n_rows, bm):
    T, D = h2.shape
    tm = GATHER_ROWS
    n_steps = T // tm
    n_slab = D // LANES
    kern = functools.partial(_dispatch_kernel, tm=tm, bm=bm, n_steps=n_steps, n_blocks=n_rows // bm)
    return pl.pallas_call(
        kern,
        grid_spec=pltpu.PrefetchScalarGridSpec(
            num_scalar_prefetch=4,
            grid=(n_steps,),
            in_specs=[pl.BlockSpec((tm, D), lambda i, d, pe, pd, nu: (i, 0))],
            out_specs=pl.BlockSpec(memory_space=pl.ANY),
            scratch_shapes=[
                pltpu.VMEM((2, tm * n_slab, LANES), F32),
                pltpu.VMEM((bm * n_slab, LANES), F32),
                pltpu.SemaphoreType.DMA((2,)),
                pltpu.SemaphoreType.DMA(()),
            ],
        ),
        out_shape=jax.ShapeDtypeStruct((n_rows * n_slab, LANES), F32),
        compiler_params=_cparams(("arbitrary",)),
        name="moe_dispatch",
    )(dest, pad_end, padded, n_used, h2)


def _cast_rows(src, dst):
    rows = CAST_ROWS

    def body(i, carry):
        r = pl.multiple_of(i * rows, rows)
        dst[pl.ds(r, rows), :] = src[pl.ds(r, rows), :].astype(BF16)
        return carry
    lax.fori_loop(0, src.shape[0] // rows, body, 0)


def _expert_weights(be_ref, first_ref, next_ref, copies, cast, m):
    @pl.when(first_ref[m] == 1)
    def _():
        @pl.when(m == 0)
        def _():
            for c in copies(be_ref[0]):
                c.start()

        for c in copies(be_ref[m]):
            c.wait()
        cast()
        ne = next_ref[m]

        @pl.when(ne >= 0)
        def _():
            for c in copies(ne):
                c.start(priority=1)


def _moe_up_kernel(be_ref, nu_ref, first_ref, next_ref, x_ref, w1_hbm, b_ref, a_ref,
                   stage, w_scr, sem):
    m = pl.program_id(0)
    ff = w_scr.shape[2]

    def copies(e):
        return [pltpu.make_async_copy(w1_hbm.at[e, :, pl.ds(t * ff, ff)], stage.at[t], sem)
                for t in range(2)]

    def cast():
        for t in range(2):
            _cast_rows(stage.at[t], w_scr.at[t])

    _expert_weights(be_ref, first_ref, next_ref, copies, cast, m)

    @pl.when(m < nu_ref[0])
    def _():
        x = _slabs_to_rows(x_ref, w_scr.shape[1] // LANES).astype(BF16)
        half = ff // 2
        for c in range(2):
            cols = slice(c * half, (c + 1) * half)
            lin_cols = slice(ff + c * half, ff + (c + 1) * half)
            u_glu = jnp.minimum(_dot(x, w_scr[0, :, cols]) + b_ref[0, :, cols], SWIGLU_LIMIT)
            u_lin = jnp.clip(_dot(x, w_scr[1, :, cols]) + b_ref[0, :, lin_cols],
                             -SWIGLU_LIMIT, SWIGLU_LIMIT)
            a = u_glu * jax.nn.sigmoid(SWIGLU_ALPHA * u_glu) * (u_lin + 1.0)
            a_ref[:, cols] = a.astype(BF16)

    @pl.when(m >= nu_ref[0])
    def _():
        a_ref[...] = jnp.zeros_like(a_ref)


def _moe_up(xs, w1, b1, plan):
    D = w1.shape[1]
    n_slab = D // LANES
    n_rows = xs.shape[0] // n_slab
    F = w1.shape[2] // 2
    bm = MOE_BM
    return pl.pallas_call(
        _moe_up_kernel,
        grid_spec=pltpu.PrefetchScalarGridSpec(
            num_scalar_prefetch=4,
            grid=(n_rows // bm,),
            in_specs=[
                pl.BlockSpec((bm * n_slab, LANES),
                             lambda m, be, nu, fi, nx: (jnp.minimum(m, nu[0] - 1), 0)),
                pl.BlockSpec(memory_space=pl.ANY),
                pl.BlockSpec((1, 1, 2 * F), lambda m, be, nu, fi, nx: (be[m], 0, 0)),
            ],
            out_specs=pl.BlockSpec((bm, F), lambda m, be, nu, fi, nx: (m, 0)),
            scratch_shapes=[
                pltpu.VMEM((2, D, F), F32),
                pltpu.VMEM((2, D, F), BF16),
                pltpu.SemaphoreType.DMA(()),
            ],
        ),
        out_shape=jax.ShapeDtypeStruct((n_rows, F), BF16),
        compiler_params=_cparams(("arbitrary",)),
        name="moe_up",
    )(*plan, xs, w1, b1)


def _moe_down_kernel(be_ref, nu_ref, first_ref, next_ref, a_ref, w2_hbm, b_ref, y_ref,
                     stage, w_scr, sem):
    m = pl.program_id(0)

    def copies(e):
        return [pltpu.make_async_copy(w2_hbm.at[e], stage, sem)]

    def cast():
        _cast_rows(stage, w_scr)

    _expert_weights(be_ref, first_ref, next_ref, copies, cast, m)

    @pl.when(m < nu_ref[0])
    def _():
        y_ref[...] = _dot(a_ref[...], w_scr[...]) + b_ref[0]

    @pl.when(m >= nu_ref[0])
    def _():
        y_ref[...] = jnp.zeros_like(y_ref)


def _moe_down(a, w2, b2, plan):
    n_rows, F = a.shape
    D = w2.shape[2]
    bm = MOE_BM
    return pl.pallas_call(
        _moe_down_kernel,
        grid_spec=pltpu.PrefetchScalarGridSpec(
            num_scalar_prefetch=4,
            grid=(n_rows // bm,),
            in_specs=[
                pl.BlockSpec((bm, F), lambda m, be, nu, fi, nx: (jnp.minimum(m, nu[0] - 1), 0)),
                pl.BlockSpec(memory_space=pl.ANY),
                pl.BlockSpec((1, 1, D), lambda m, be, nu, fi, nx: (be[m], 0, 0)),
            ],
            out_specs=pl.BlockSpec((bm, D), lambda m, be, nu, fi, nx: (m, 0)),
            scratch_shapes=[
                pltpu.VMEM((F, D), F32),
                pltpu.VMEM((F, D), BF16),
                pltpu.SemaphoreType.DMA(()),
            ],
        ),
        out_shape=jax.ShapeDtypeStruct((n_rows, D), F32),
        compiler_params=_cparams(("arbitrary",)),
        name="moe_down",
    )(*plan, a, w2, b2)


def _combine_kernel(dest_ref, h2_ref, rt_ref, g_ref, b_ref, ys_hbm, op_ref, os_ref, buf, sem,
                    *, tm, n_steps, n_prompt, alpha):
    i = pl.program_id(0)

    def issue(step, slot):
        def body(j, carry):
            t0 = pl.multiple_of(j * SUBLANES, SUBLANES)
            for u in range(SUBLANES):
                for k in range(TOP_K):
                    row = dest_ref[(step * tm + t0 + u) * TOP_K + k]
                    _row_copy(ys_hbm, buf.at[slot, k], sem.at[slot], row, t0 + u).start(priority=k % 2)
            return carry
        lax.fori_loop(0, tm // SUBLANES, body, 0)

    def wait(slot):
        for k in range(TOP_K):
            pltpu.make_async_copy(ys_hbm.at[pl.ds(0, tm)], buf.at[slot, k], sem.at[slot]).wait()

    slot = i % 2

    @pl.when(i == 0)
    def _():
        issue(0, 0)

    @pl.when(i + 1 < n_steps)
    def _():
        issue(i + 1, 1 - slot)

    wait(slot)
    y = rt_ref[:, 0:1] * buf[slot, 0]
    for k in range(1, TOP_K):
        y = y + rt_ref[:, k:k + 1] * buf[slot, k]
    out = _layer_norm(alpha * h2_ref[...] + y, g_ref[...], b_ref[...])

    @pl.when(i < n_prompt)
    def _():
        op_ref[...] = out

    @pl.when(i >= n_prompt)
    def _():
        os_ref[...] = out


def _moe_combine(ys, dest, h2, route, g3, b3, t_prompt, alpha):
    T, D = h2.shape
    tm = COMBINE_TM
    n_steps = T // tm
    n_prompt = t_prompt // tm
    kern = functools.partial(_combine_kernel, tm=tm, n_steps=n_steps, n_prompt=n_prompt, alpha=alpha)
    return pl.pallas_call(
        kern,
        grid_spec=pltpu.PrefetchScalarGridSpec(
            num_scalar_prefetch=1,
            grid=(n_steps,),
            in_specs=[
                pl.BlockSpec((tm, D), lambda i, d: (i, 0)),
                pl.BlockSpec((tm, LANES), lambda i, d: (i, 0)),
                pl.BlockSpec((1, D), lambda i, d: (0, 0)),
                pl.BlockSpec((1, D), lambda i, d: (0, 0)),
                pl.BlockSpec(memory_space=pl.ANY),
            ],
            out_specs=[
                pl.BlockSpec((tm, D), lambda i, d: (jnp.minimum(i, n_prompt - 1), 0)),
                pl.BlockSpec((tm, D), lambda i, d: (jnp.maximum(i - n_prompt, 0), 0)),
            ],
            scratch_shapes=[pltpu.VMEM((2, TOP_K, tm, D), F32), pltpu.SemaphoreType.DMA((2,))],
        ),
        out_shape=[
            jax.ShapeDtypeStruct((t_prompt, D), F32),
            jax.ShapeDtypeStruct((T - t_prompt, D), F32),
        ],
        compiler_params=_cparams(("arbitrary",)),
        name="moe_combine",
    )(dest, h2, route, g3, b3, ys)


def _moe_plan(route, counts, bm):
    T = route.shape[0]
    top_idx = route[:, TOP_K:2 * TOP_K].astype(jnp.int32)
    rank = route[:, 2 * TOP_K:3 * TOP_K].astype(jnp.int32)
    counts = counts[0, :N_EXPERTS].astype(jnp.int32)
    padded = (counts + bm - 1) // bm * bm
    pad_end = jnp.cumsum(padded)
    pad_start = pad_end - padded
    experts = jnp.arange(N_EXPERTS, dtype=jnp.int32)
    start_of = jnp.sum(jnp.where(top_idx[..., None] == experts, pad_start, 0), axis=-1)
    dest = (start_of + rank).reshape(-1)
    n_blocks = -(-(T * TOP_K) // bm) + N_EXPERTS
    block_start = jnp.arange(n_blocks, dtype=jnp.int32) * bm
    block_e = jnp.sum((pad_end[None, :] <= block_start[:, None]).astype(jnp.int32), axis=1)
    block_e = jnp.minimum(block_e, N_EXPERTS - 1)
    n_used = (pad_end[-1] // bm).astype(jnp.int32).reshape(1)
    blk = jnp.arange(n_blocks, dtype=jnp.int32)
    prev_e = jnp.concatenate([jnp.full((1,), -1, jnp.int32), block_e[:-1]])
    first = jnp.logical_and(blk < n_used[0], block_e != prev_e)
    later = jnp.logical_and(first[None, :], blk[None, :] > blk[:, None])
    nxt_blk = jnp.min(jnp.where(later, blk[None, :], n_blocks), axis=1)
    next_e = jnp.where(nxt_blk < n_blocks, block_e[jnp.minimum(nxt_blk, n_blocks - 1)], -1)
    plan = (block_e, n_used, first.astype(jnp.int32), next_e.astype(jnp.int32))
    return dest, pad_end, padded, plan, n_blocks * bm


def kernel(x_prompt, x_sample, cache_diff_k, cache_diff_v, state_ret, cache_mem_k, cache_mem_v, mem_prompt, ln_in_g, ln_in_b, w_in, b_gate, diff_lambda, diff_subln, w_proj_ret, w_proj_diff, w_out, ln1_g, ln1_b, w_mq, w_mk, w_mv, w_mo, ln2_g, ln2_b, router_w, router_b, w1, b1, w2, b2, ln3_g, ln3_b):
    Bp, Lp, D = x_prompt.shape
    Bs, Ls, _ = x_sample.shape
    depth = w_in.shape[0]
    assert depth == 1
    past = cache_diff_k.shape[2]
    n_mem = mem_prompt.shape[1]
    Tp, Ts = Bp * Lp, Bs * Ls
    alpha = (2.0 * depth) ** 0.25
    lambda_init = 0.8 - 0.6 * math.exp(-0.3 * 0)
    row = lambda v: v.reshape(1, -1)

    xp = x_prompt.reshape(Tp, D)
    xs_in = x_sample.reshape(Ts, D)
    kv_blk = 4096 // IN_TN
    P, dk_p, dv_p, dk_s, dv_s = _in_proj(xp, xs_in, row(ln_in_g), row(ln_in_b), w_in[0].astype(BF16),
                                         kv_blk)

    zeros_state = jnp.zeros((Bp,) + state_ret.shape[2:], F32)
    yr_p, s_p = _retention(P, 0, Bp, Lp, 0, zeros_state)
    yr_s, s_s = _retention(P, Tp, Bs, Ls, past, state_ret[0])

    yd_p = _attn_prompt(P, Bp, Lp, diff_lambda[0], row(diff_subln[0]), lambda_init)
    yd_s = _attn_sample(P, Tp, Bs, Ls, cache_diff_k[0], cache_diff_v[0], diff_lambda[0],
                        row(diff_subln[0]), lambda_init)

    h1 = _mix(xp, xs_in, yr_p, yr_s, yd_p, yd_s, P, row(ln_in_g), row(ln_in_b), row(b_gate[0]),
              w_proj_ret[0].astype(BF16), w_proj_diff[0].astype(BF16), w_out[0].astype(BF16),
              row(ln1_g[0]), row(ln1_b[0]), alpha)

    w_mkv = jnp.concatenate([w_mk[0], w_mv[0]], axis=1).astype(BF16)
    mkv = _mem_kv(mem_prompt.reshape(Bp * n_mem, D), w_mkv)
    hm = w_mk.shape[2]
    mk_p = mkv[:, :hm].reshape(Bp, n_mem, hm)
    mv_p = mkv[:, hm:].reshape(Bp, n_mem, hm)
    h2, route, counts = _mem_attn(
        h1, Tp, mk_p, mv_p, Lp, cache_mem_k[0], cache_mem_v[0], Ls, w_mq[0].astype(BF16),
        w_mo[0].astype(BF16),
        row(ln2_g[0]), row(ln2_b[0]), router_w[0], row(router_b[0]), alpha)

    dest, pad_end, padded, plan, n_rows = _moe_plan(route, counts, MOE_BM)
    xs = _moe_dispatch(h2, dest, pad_end, padded, plan[1], n_rows, MOE_BM)
    act = _moe_up(xs, w1[0], b1[0][:, None, :], plan)
    ys = _moe_down(act, w2[0], b2[0][:, None, :], plan)
    out_p, out_s = _moe_combine(ys, dest, h2, route, row(ln3_g[0]), row(ln3_b[0]), Tp, alpha)

    return (
        out_p.reshape(Bp, Lp, D),
        out_s.reshape(Bs, Ls, D),
        dk_p.reshape(1, Bp, Lp, D_HEADS, 128),
        dv_p.reshape(1, Bp, Lp, D_HEADS, 128),
        s_p[None],
        mk_p.reshape(1, Bp, n_mem, M_HEADS, hm // M_HEADS),
        mv_p.reshape(1, Bp, n_mem, M_HEADS, hm // M_HEADS),
        dk_s.reshape(1, Bs, Ls, D_HEADS, 128),
        dv_s.reshape(1, Bs, Ls, D_HEADS, 128),
        s_s[None],
    )
```
